```python
import math, functools
import jax, jax.numpy as jnp
from jax import lax
import numpy as np

D_MODEL = 1024
BATCH = 2
SEQ = 8192
DEPTH = 1
DEC_BATCH = 128
DEC_SEQ = 4
PAST_LEN = 16384
PAGE_SIZE = 128

HEAD_DIM = 64
N_HEADS = D_MODEL // HEAD_DIM
N_KV_HEADS = 4
GQA_GROUP = N_HEADS // N_KV_HEADS
WINDOW = 128
BLOCK = WINDOW
D_ATTN = N_HEADS * HEAD_DIM
D_KV = N_KV_HEADS * HEAD_DIM
D_SSM = D_MODEL
SSM_CH = 16
SSM_GROUPS = D_SSM // SSM_CH
SSM_STATE = 64
D_FF = 4 * D_MODEL
RMS_EPS = 1e-5
DT_MIN = 0.001
DT_MAX = 0.1
D_IN = D_ATTN + 2 * D_KV + D_SSM + 2 * D_MODEL

kernel_name = "hybrid_swa_sink_s5_decoder_step"


def rmsnorm(x, g):
    xf = x.astype(jnp.float32)
    y = xf * lax.rsqrt(jnp.mean(xf * xf, axis=-1, keepdims=True) + RMS_EPS)
    return (y * g.astype(jnp.float32)).astype(x.dtype)


def sink_attention(q, k, v, valid, sinks):
    s = jnp.einsum('...qkgd,...skd->...kgqs', q, k,
                   preferred_element_type=jnp.float32) * (HEAD_DIM ** -0.5)
    s = jnp.where(valid, s, -jnp.inf)
    sink = sinks.astype(jnp.float32).reshape(N_KV_HEADS, GQA_GROUP, 1, 1)
    m = jnp.maximum(jnp.max(s, axis=-1, keepdims=True), sink)
    p = jnp.exp(s - m)
    denom = jnp.sum(p, axis=-1, keepdims=True) + jnp.exp(sink - m)
    return jnp.einsum('...kgqs,...skd->...qkgd', (p / denom).astype(v.dtype), v)


def prompt_attention(q, k, v, sinks):
    b, l = q.shape[0], q.shape[1]
    nb = l // BLOCK
    qb = q.reshape(b, nb, BLOCK, N_KV_HEADS, GQA_GROUP, HEAD_DIM)
    kb = k.reshape(b, nb, BLOCK, N_KV_HEADS, HEAD_DIM)
    vb = v.reshape(b, nb, BLOCK, N_KV_HEADS, HEAD_DIM)
    pad = jnp.zeros_like(kb[:, :1])
    k2 = jnp.concatenate([jnp.concatenate([pad, kb[:, :-1]], axis=1), kb], axis=2)
    v2 = jnp.concatenate([jnp.concatenate([pad, vb[:, :-1]], axis=1), vb], axis=2)
    qi = jnp.arange(BLOCK)[:, None]
    si = jnp.arange(2 * BLOCK)[None, :]
    diff = qi + BLOCK - si
    band = (diff >= 0) & (diff < WINDOW)
    has_prev = (jnp.arange(nb) > 0)[:, None, None] | (si >= BLOCK)[None]
    valid = (band[None] & has_prev)[:, None, None]
    o = sink_attention(qb, k2, v2, valid, sinks)
    return (o.reshape(b, l, D_ATTN), k[:, -WINDOW:], v[:, -WINDOW:])


def sample_attention(q, k, v, sinks, cache_k, cache_v):
    b, t = q.shape[0], q.shape[1]
    w = cache_k.shape[1]
    kk = jnp.concatenate([cache_k, k], axis=1)
    vv = jnp.concatenate([cache_v, v], axis=1)
    qpos = PAST_LEN + jnp.arange(t)
    kpos = PAST_LEN - w + jnp.arange(w + t)
    diff = qpos[:, None] - kpos[None, :]
    valid = (diff >= 0) & (diff < WINDOW)
    qg = q.reshape(b, t, N_KV_HEADS, GQA_GROUP, HEAD_DIM)
    o = sink_attention(qg, kk, vv, valid, sinks)
    return (o.reshape(b, t, D_ATTN), kk[:, -WINDOW:], vv[:, -WINDOW:])


def _cplx_affine_combine(e1, e2):
    a1r, a1i, b1r, b1i = e1
    a2r, a2i, b2r, b2i = e2
    return (a2r * a1r - a2i * a1i,
            a2r * a1i + a2i * a1r,
            a2r * b1r - a2i * b1i + b2r,
            a2r * b1i + a2i * b1r + b2i)


def s5_discretize(lam_re, lam_im, log_dt, b_re, b_im):
    dt = jnp.exp(log_dt)[:, None]
    decay = jnp.exp(lam_re * dt)
    ab_re = decay * jnp.cos(lam_im * dt)
    ab_im = decay * jnp.sin(lam_im * dt)
    nr, ni = ab_re - 1.0, ab_im
    den = lam_re * lam_re + lam_im * lam_im
    f_re = ((nr * lam_re + ni * lam_im) / den)[..., None]
    f_im = ((ni * lam_re - nr * lam_im) / den)[..., None]
    bb_re = f_re * b_re - f_im * b_im
    bb_im = f_re * b_im + f_im * b_re
    return ab_re, ab_im, bb_re, bb_im


def s5_branch(u, h0_re, h0_im, lam_re, lam_im, log_dt, b_re, b_im, c_re, c_im, d_skip):
    f32 = jnp.float32
    b, l = u.shape[0], u.shape[1]
    uf = u.astype(f32)
    ug = uf.reshape(b, l, SSM_GROUPS, SSM_CH)
    ab_re, ab_im, bb_re, bb_im = s5_discretize(
        lam_re.astype(f32), lam_im.astype(f32), log_dt.astype(f32), b_re.astype(f32), b_im.astype(f32))
    bu_re = jnp.einsum('blgc,gpc->blgp', ug, bb_re)
    bu_im = jnp.einsum('blgc,gpc->blgp', ug, bb_im)
    h0r, h0i = h0_re.astype(f32), h0_im.astype(f32)
    first_re = ab_re * h0r - ab_im * h0i + bu_re[:, 0]
    first_im = ab_re * h0i + ab_im * h0r + bu_im[:, 0]
    bu_re = bu_re.at[:, 0].set(first_re)
    bu_im = bu_im.at[:, 0].set(first_im)
    a_re = jnp.broadcast_to(ab_re, bu_re.shape)
    a_im = jnp.broadcast_to(ab_im, bu_im.shape)
    _, _, h_re, h_im = lax.associative_scan(_cplx_affine_combine, (a_re, a_im, bu_re, bu_im), axis=1)
    y = (jnp.einsum('blgp,gcp->blgc', h_re, c_re.astype(f32))
         - jnp.einsum('blgp,gcp->blgc', h_im, c_im.astype(f32)))
    y = y.reshape(b, l, D_SSM) + d_skip.astype(f32) * uf
    return y.astype(u.dtype), h_re[:, -1], h_im[:, -1]


def hybrid_layer(x, attn_fn, h0_re, h0_im, g_mix, w_in, sinks, w_attn_o, lam_re, lam_im, log_dt,
                 b_re, b_im, c_re, c_im, d_skip, w_glu, w_out, g_ffn, w_up, w_down):
    b, l = x.shape[0], x.shape[1]
    h = rmsnorm(x, g_mix)
    proj = h @ w_in
    q, k, v, u, gate_logits = jnp.split(
        proj, [D_ATTN, D_ATTN + D_KV, D_ATTN + 2 * D_KV, D_ATTN + 2 * D_KV + D_SSM], axis=-1)
    q = q.reshape(b, l, N_HEADS, HEAD_DIM)
    k = k.reshape(b, l, N_KV_HEADS, HEAD_DIM)
    v = v.reshape(b, l, N_KV_HEADS, HEAD_DIM)
    attn, k_buf, v_buf = attn_fn(q, k, v, sinks)
    a_out = attn @ w_attn_o
    y_ssm, hT_re, hT_im = s5_branch(u, h0_re, h0_im, lam_re, lam_im, log_dt,
                                    b_re, b_im, c_re, c_im, d_skip)
    glu = jax.nn.gelu(y_ssm) @ w_glu
    s_out = glu[..., :D_MODEL] * jax.nn.sigmoid(glu[..., D_MODEL:])
    merged = (jax.nn.sigmoid(gate_logits[..., :D_MODEL]) * a_out
              + jax.nn.sigmoid(gate_logits[..., D_MODEL:]) * s_out)
    x = x + merged @ w_out
    h2 = rmsnorm(x, g_ffn)
    x = x + jnp.square(jax.nn.relu(h2 @ w_up)) @ w_down
    return x, k_buf, v_buf, hT_re, hT_im


def setup_inputs(seed: int = 0) -> dict:
    key = jax.random.key(seed)
    ks = jax.random.split(key, 24)
    f32 = jnp.float32
    nrm = lambda k, shape, scale: jax.random.normal(k, shape, f32) * scale
    lam_im_base = jnp.broadcast_to(math.pi * jnp.arange(SSM_STATE, dtype=f32), (DEPTH, SSM_GROUPS, SSM_STATE))
    return {
        "x_prompt": nrm(ks[0], (BATCH, SEQ, D_MODEL), 1.0),
        "x_sample": nrm(ks[1], (DEC_BATCH, DEC_SEQ, D_MODEL), 1.0),
        "cache_k": nrm(ks[2], (DEPTH, DEC_BATCH, WINDOW, N_KV_HEADS, HEAD_DIM), 1.0),
        "cache_v": nrm(ks[3], (DEPTH, DEC_BATCH, WINDOW, N_KV_HEADS, HEAD_DIM), 1.0),
        "state_ssm_re": nrm(ks[4], (DEPTH, DEC_BATCH, SSM_GROUPS, SSM_STATE), 0.5),
        "state_ssm_im": nrm(ks[5], (DEPTH, DEC_BATCH, SSM_GROUPS, SSM_STATE), 0.5),
        "g_mix": 1.0 + nrm(ks[6], (DEPTH, D_MODEL), 0.1),
        "w_in": nrm(ks[7], (DEPTH, D_MODEL, D_IN), D_MODEL ** -0.5),
        "attn_sinks": nrm(ks[8], (DEPTH, N_HEADS), 0.5),
        "w_attn_o": nrm(ks[9], (DEPTH, D_ATTN, D_MODEL), D_ATTN ** -0.5),
        "ssm_lambda_re": -0.5 + nrm(ks[10], (DEPTH, SSM_GROUPS, SSM_STATE), 0.01),
        "ssm_lambda_im": lam_im_base + nrm(ks[11], (DEPTH, SSM_GROUPS, SSM_STATE), 0.01),
        "ssm_log_dt": jax.random.uniform(ks[12], (DEPTH, SSM_GROUPS), f32,
                                         math.log(DT_MIN), math.log(DT_MAX)),
        "ssm_b_re": nrm(ks[13], (DEPTH, SSM_GROUPS, SSM_STATE, SSM_CH), (2 * SSM_CH) ** -0.5),
        "ssm_b_im": nrm(ks[14], (DEPTH, SSM_GROUPS, SSM_STATE, SSM_CH), (2 * SSM_CH) ** -0.5),
        "ssm_c_re": nrm(ks[15], (DEPTH, SSM_GROUPS, SSM_CH, SSM_STATE), (2 * SSM_STATE) ** -0.5),
        "ssm_c_im": nrm(ks[16], (DEPTH, SSM_GROUPS, SSM_CH, SSM_STATE), (2 * SSM_STATE) ** -0.5),
        "ssm_d": nrm(ks[17], (DEPTH, D_SSM), 1.0),
        "w_glu": nrm(ks[18], (DEPTH, D_SSM, 2 * D_MODEL), D_SSM ** -0.5),
        "w_out": nrm(ks[19], (DEPTH, D_MODEL, D_MODEL), D_MODEL ** -0.5),
        "g_ffn": 1.0 + nrm(ks[20], (DEPTH, D_MODEL), 0.1),
        "w_up": nrm(ks[21], (DEPTH, D_MODEL, D_FF), D_MODEL ** -0.5),
        "w_down": nrm(ks[22], (DEPTH, D_FF, D_MODEL), D_FF ** -0.5),
        "g_final": 1.0 + nrm(ks[23], (D_MODEL,), 0.1),
    }


def reference(x_prompt, x_sample, cache_k, cache_v, state_ssm_re, state_ssm_im, g_mix, w_in,
              attn_sinks, w_attn_o, ssm_lambda_re, ssm_lambda_im, ssm_log_dt, ssm_b_re, ssm_b_im,
              ssm_c_re, ssm_c_im, ssm_d, w_glu, w_out, g_ffn, w_up, w_down, g_final):
    yp, ys = x_prompt, x_sample
    kp_l, vp_l, hpr_l, hpi_l = [], [], [], []
    ksl, vsl, hsr_l, hsi_l = [], [], [], []
    h0_zero = jnp.zeros((x_prompt.shape[0], SSM_GROUPS, SSM_STATE), jnp.float32)
    for l in range(DEPTH):
        params = (g_mix[l], w_in[l], attn_sinks[l], w_attn_o[l], ssm_lambda_re[l], ssm_lambda_im[l],
                  ssm_log_dt[l], ssm_b_re[l], ssm_b_im[l], ssm_c_re[l], ssm_c_im[l], ssm_d[l],
                  w_glu[l], w_out[l], g_ffn[l], w_up[l], w_down[l])
        yp, kp, vp, hpr, hpi = hybrid_layer(yp, prompt_attention, h0_zero, h0_zero, *params)
        samp_fn = functools.partial(sample_attention, cache_k=cache_k[l], cache_v=cache_v[l])
        ys, kq, vq, hsr, hsi = hybrid_layer(ys, samp_fn, state_ssm_re[l], state_ssm_im[l], *params)
        kp_l.append(kp); vp_l.append(vp); hpr_l.append(hpr); hpi_l.append(hpi)
        ksl.append(kq); vsl.append(vq); hsr_l.append(hsr); hsi_l.append(hsi)
    y_prompt = rmsnorm(yp, g_final)
    y_sample = rmsnorm(ys, g_final)
    k_prompt = jnp.stack(kp_l, axis=0)
    v_prompt = jnp.stack(vp_l, axis=0)
    ssm_re_prompt = jnp.stack(hpr_l, axis=0)
    ssm_im_prompt = jnp.stack(hpi_l, axis=0)
    k_sample = jnp.stack(ksl, axis=0)
    v_sample = jnp.stack(vsl, axis=0)
    ssm_re_sample = jnp.stack(hsr_l, axis=0)
    ssm_im_sample = jnp.stack(hsi_l, axis=0)
    return (y_prompt, y_sample, k_prompt, v_prompt, ssm_re_prompt, ssm_im_prompt,
            k_sample, v_sample, ssm_re_sample, ssm_im_sample)
```

```python
import functools
import math

import jax
import jax.numpy as jnp
from jax import lax
from jax.experimental import pallas as pl
from jax.experimental.pallas import tpu as pltpu

F32 = jnp.float32
BF16 = jnp.bfloat16

D_MODEL = 1024
HEAD_DIM = 64
N_HEADS = 16
N_KV_HEADS = 4
WINDOW = 128
D_KV = N_KV_HEADS * HEAD_DIM
SSM_CH = 16
SSM_GROUPS = 64
SSM_STATE = 64
D_FF = 4 * D_MODEL
RMS_EPS = 1e-5

LANES = 128
SUBLANES = 8
N_LANE_TILES = D_MODEL // LANES
GROUPS_PER_TILE = LANES // SSM_CH
STATE_PER_TILE = GROUPS_PER_TILE * SSM_STATE
VMEM_LIMIT = 56 * 1024 * 1024

_Q0, _KV0, _U0, _GA0, _GS0, _END = 0, 1024, 1536, 2560, 3584, 4608


def _rmsnorm(x, g):
    return x * lax.rsqrt(jnp.mean(x * x, axis=-1, keepdims=True) + RMS_EPS) * g


def _params(*sem):
    return pltpu.CompilerParams(dimension_semantics=sem, vmem_limit_bytes=VMEM_LIMIT)


def _const_spec(shape):
    nd = len(shape)
    return pl.BlockSpec(shape, lambda *_: (0,) * nd, pipeline_mode=pl.Buffered(1))


def _proj_kernel(x_ref, g_ref, w_ref, q_ref, kv_ref, u_ref, ga_ref, gs_ref):
    h = _rmsnorm(x_ref[...], g_ref[...]).astype(BF16)

    def seg(lo, hi):
        return jnp.dot(h, w_ref[:, lo:hi], preferred_element_type=F32)

    q_ref[...] = seg(_Q0, _KV0).astype(BF16)
    kv_ref[...] = seg(_KV0, _U0)
    u_ref[...] = seg(_U0, _GA0).astype(BF16)
    ga_ref[...] = jax.nn.sigmoid(seg(_GA0, _GS0)).astype(BF16)
    gs_ref[...] = jax.nn.sigmoid(seg(_GS0, _END)).astype(BF16)


def _proj(x, g_mix, w_in, tm):
    n = x.shape[0]
    row = lambda w: pl.BlockSpec((tm, w), lambda i: (i, 0))
    return pl.pallas_call(
        _proj_kernel,
        grid=(n // tm,),
        in_specs=[row(D_MODEL), _const_spec((1, D_MODEL)), _const_spec(w_in.shape)],
        out_specs=[row(D_MODEL), row(2 * D_KV), row(D_MODEL), row(D_MODEL), row(D_MODEL)],
        out_shape=[jax.ShapeDtypeStruct((n, D_MODEL), BF16),
                   jax.ShapeDtypeStruct((n, 2 * D_KV), F32),
                   jax.ShapeDtypeStruct((n, D_MODEL), BF16),
                   jax.ShapeDtypeStruct((n, D_MODEL), BF16),
                   jax.ShapeDtypeStruct((n, D_MODEL), BF16)],
        compiler_params=_params("arbitrary"),
        name="proj",
    )(x, g_mix, w_in)


def _dup_heads(tile):
    lo = lax.broadcasted_iota(jnp.int32, tile.shape, tile.ndim - 1) < HEAD_DIM
    rolled = pltpu.roll(tile, HEAD_DIM, tile.ndim - 1)
    return (jnp.where(lo, tile, rolled).astype(BF16), jnp.where(lo, rolled, tile).astype(BF16))


def _sink_softmax(s, valid, sink):
    s = jnp.where(valid, s * (HEAD_DIM ** -0.5), -jnp.inf)
    m = jnp.maximum(jnp.max(s, axis=-1, keepdims=True), sink)
    p = jnp.exp(s - m)
    denom = jnp.sum(p, axis=-1, keepdims=True) + jnp.exp(sink - m)
    return (p / denom).astype(BF16)


def _attn_prompt_kernel(sinks_ref, q_ref, kvc_ref, kvp_ref, ga_ref, wo_ref, out_ref,
                        kd_scr, vd_scr, attn_scr, *, tq):
    i = pl.program_id(1)
    kv_full = jnp.concatenate([kvp_ref[...], kvc_ref[...]], axis=0)
    for t in range(2):
        ke, ko = _dup_heads(kv_full[:, t * LANES:(t + 1) * LANES])
        ve, vo = _dup_heads(kv_full[:, D_KV + t * LANES:D_KV + (t + 1) * LANES])
        kd_scr[2 * t], kd_scr[2 * t + 1] = ke, ko
        vd_scr[2 * t], vd_scr[2 * t + 1] = ve, vo

    qi = lax.broadcasted_iota(jnp.int32, (WINDOW, 2 * WINDOW), 0)
    si = lax.broadcasted_iota(jnp.int32, (WINDOW, 2 * WINDOW), 1)
    band = (si > qi) & (si <= qi + WINDOW)
    in_block = si >= WINDOW
    lo = lax.broadcasted_iota(jnp.int32, (WINDOW, LANES), 1) < HEAD_DIM

    def block(jb, carry):
        q0 = pl.multiple_of(jb * WINDOW, WINDOW)
        has_prev = (i > 0) | (jb > 0)
        valid = band & (in_block | has_prev)
        for j in range(N_KV_HEADS):
            kd = kd_scr[j, pl.ds(q0, 2 * WINDOW), :]
            vd = vd_scr[j, pl.ds(q0, 2 * WINDOW), :]
            for r in range(2):
                c0 = j * 2 * LANES + r * LANES
                qp = q_ref[pl.ds(q0, WINDOW), c0:c0 + LANES]
                outs = []
                for par in range(2):
                    head = 4 * j + 2 * r + par
                    qm = jnp.where(lo if par == 0 else ~lo, qp, jnp.zeros_like(qp))
                    s = lax.dot_general(qm, kd, (((1,), (1,)), ((), ())), preferred_element_type=F32)
                    p = _sink_softmax(s, valid, sinks_ref[head])
                    outs.append(jnp.dot(p, vd, preferred_element_type=F32))
                attn_scr[pl.ds(q0, WINDOW), c0:c0 + LANES] = jnp.where(lo, outs[0], outs[1]).astype(BF16)
        return carry

    lax.fori_loop(0, tq // WINDOW, block, 0)
    a_out = jnp.dot(attn_scr[...], wo_ref[...], preferred_element_type=F32)
    out_ref[...] = ga_ref[...].astype(F32) * a_out


def _attn_prompt(sinks, q, kv, ga, wo, batch, seq, tq):
    nq = seq // tq
    bpt = tq // WINDOW
    row = lambda w: pl.BlockSpec((tq, w), lambda b, i: (b * nq + i, 0))
    prev = pl.BlockSpec((WINDOW, 2 * D_KV),
                        lambda b, i: (jnp.maximum((b * nq + i) * bpt - 1, 0), 0))
    return pl.pallas_call(
        functools.partial(_attn_prompt_kernel, tq=tq),
        grid=(batch, nq),
        in_specs=[pl.BlockSpec(memory_space=pltpu.SMEM), row(D_MODEL), row(2 * D_KV), prev,
                  row(D_MODEL), _const_spec(wo.shape)],
        out_specs=row(D_MODEL),
        out_shape=jax.ShapeDtypeStruct((batch * seq, D_MODEL), F32),
        scratch_shapes=[pltpu.VMEM((N_KV_HEADS, WINDOW + tq, LANES), BF16),
                        pltpu.VMEM((N_KV_HEADS, WINDOW + tq, LANES), BF16),
                        pltpu.VMEM((tq, D_MODEL), BF16)],
        compiler_params=_params("arbitrary", "arbitrary"),
        name="attn_prompt",
    )(sinks, q, kv, kv, ga, wo)


def _attn_sample_kernel(sinks_ref, q_ref, kvn_ref, ck_ref, cv_ref, ga_ref, wo_ref,
                        out_ref, ko_ref, vo_ref, attn_scr, *, bb, tpad, dec_seq):
    nk = WINDOW + tpad
    lo3 = lax.broadcasted_iota(jnp.int32, (bb, tpad, LANES), 2) < HEAD_DIM
    row = lax.broadcasted_iota(jnp.int32, (4 * tpad, nk), 0)
    si = lax.broadcasted_iota(jnp.int32, (4 * tpad, nk), 1)
    tq = row % tpad
    valid = (si > tq) & (si <= tq + WINDOW) & (si < WINDOW + dec_seq)
    hrow = lax.broadcasted_iota(jnp.int32, (4 * tpad, 1), 0) // tpad

    for t in range(2):
        sl = slice(t * LANES, (t + 1) * LANES)
        kk = jnp.concatenate([ck_ref[:, :, sl], kvn_ref[:, :, sl]], axis=1)
        vv = jnp.concatenate([cv_ref[:, :, sl],
                              kvn_ref[:, :, D_KV + t * LANES:D_KV + (t + 1) * LANES]], axis=1)
        ko_ref[:, :, sl] = kk[:, dec_seq:dec_seq + WINDOW, :]
        vo_ref[:, :, sl] = vv[:, dec_seq:dec_seq + WINDOW, :]
        kds = _dup_heads(kk)
        vds = _dup_heads(vv)
        for par_kv in range(2):
            j = 2 * t + par_kv
            kd, vd = kds[par_kv], vds[par_kv]
            parts = []
            for r in range(2):
                c0 = j * 2 * LANES + r * LANES
                qp = q_ref[:, :, c0:c0 + LANES]
                parts += [jnp.where(lo3, qp, 0.0), jnp.where(lo3, 0.0, qp)]
            lhs = jnp.concatenate(parts, axis=1).astype(BF16)
            s = jnp.einsum('bqd,bkd->bqk', lhs, kd, preferred_element_type=F32)
            sink = jnp.zeros((4 * tpad, 1), F32)
            for g in range(4):
                sink = jnp.where(hrow == g, sinks_ref[4 * j + g], sink)
            p = _sink_softmax(s, valid[None], sink[None])
            o = jnp.einsum('bqk,bkd->bqd', p, vd, preferred_element_type=F32)
            for r in range(2):
                c0 = j * 2 * LANES + r * LANES
                o_even = o[:, (2 * r) * tpad:(2 * r + 1) * tpad, :]
                o_odd = o[:, (2 * r + 1) * tpad:(2 * r + 2) * tpad, :]
                attn_scr[:, :, c0:c0 + LANES] = jnp.where(lo3, o_even, o_odd)

    attn = attn_scr[...].reshape(bb * tpad, D_MODEL).astype(BF16)
    a_out = jnp.dot(attn, wo_ref[...], preferred_element_type=F32)
    out_ref[...] = ga_ref[...].astype(F32) * a_out


def _attn_sample(sinks, q3, kvn3, ck, cv, ga2, wo, bb, dec_seq):
    db, tpad, _ = q3.shape
    blk3 = lambda r, w: pl.BlockSpec((bb, r, w), lambda i: (i, 0, 0))
    row = pl.BlockSpec((bb * tpad, D_MODEL), lambda i: (i, 0))
    return pl.pallas_call(
        functools.partial(_attn_sample_kernel, bb=bb, tpad=tpad, dec_seq=dec_seq),
        grid=(db // bb,),
        in_specs=[pl.BlockSpec(memory_space=pltpu.SMEM), blk3(tpad, D_MODEL), blk3(tpad, 2 * D_KV),
                  blk3(WINDOW, D_KV), blk3(WINDOW, D_KV), row, _const_spec(wo.shape)],
        out_specs=[row, blk3(WINDOW, D_KV), blk3(WINDOW, D_KV)],
        out_shape=[jax.ShapeDtypeStruct((db * tpad, D_MODEL), F32),
                   jax.ShapeDtypeStruct((db, WINDOW, D_KV), F32),
                   jax.ShapeDtypeStruct((db, WINDOW, D_KV), F32)],
        scratch_shapes=[pltpu.VMEM((bb, tpad, D_MODEL), F32)],
        compiler_params=_params("arbitrary"),
        name="attn_sample",
    )(sinks, q3, kvn3, ck, cv, ga2, wo)


def _s5_discretize(lam_re, lam_im, log_dt, b_re, b_im):
    dt = jnp.exp(log_dt)[:, None]
    decay = jnp.exp(lam_re * dt)
    ab_re = decay * jnp.cos(lam_im * dt)
    ab_im = decay * jnp.sin(lam_im * dt)
    nr, ni = ab_re - 1.0, ab_im
    den = lam_re * lam_re + lam_im * lam_im
    f_re = ((nr * lam_re + ni * lam_im) / den)[..., None]
    f_im = ((ni * lam_re - nr * lam_im) / den)[..., None]
    return ab_re, ab_im, f_re * b_re - f_im * b_im, f_re * b_im + f_im * b_re


def _tile_state(a):
    return a.reshape(a.shape[:-2] + (N_LANE_TILES, STATE_PER_TILE))


def _untile_state(a):
    return a.reshape(a.shape[:-2] + (SSM_GROUPS, SSM_STATE))


def _s5_weights(lam_re, lam_im, log_dt, b_re, b_im, c_re, c_im):
    ab_re, ab_im, bb_re, bb_im = _s5_discretize(lam_re, lam_im, log_dt, b_re, b_im)
    eye = jnp.eye(GROUPS_PER_TILE, dtype=F32)

    def bw(bb):
        x = bb.reshape(N_LANE_TILES, GROUPS_PER_TILE, SSM_STATE, SSM_CH)
        return jnp.einsum('lgpc,gh->lgchp', x, eye).reshape(N_LANE_TILES, LANES, STATE_PER_TILE)

    def cw(c):
        x = c.reshape(N_LANE_TILES, GROUPS_PER_TILE, SSM_CH, SSM_STATE)
        return jnp.einsum('lgcp,gh->lgphc', x, eye).reshape(N_LANE_TILES, STATE_PER_TILE, LANES)

    b_w = jnp.concatenate([bw(bb_re), bw(bb_im)], axis=2).astype(BF16)
    c_w = jnp.concatenate([cw(c_re), -cw(c_im)], axis=1).astype(BF16)
    a = jnp.concatenate([_tile_state(ab_re), _tile_state(ab_im)], axis=1)
    return b_w, c_w, a


def _cpow2(a, n_squarings):
    re, im = a[:, :STATE_PER_TILE], a[:, STATE_PER_TILE:]
    for _ in range(n_squarings):
        re, im = re * re - im * im, 2.0 * re * im
    return jnp.concatenate([re, im], axis=1)


def _cmul_add(ar, ai, hr, hi, xr, xi):
    return ar * hr - ai * hi + xr, ar * hi + ai * hr + xi


def _split(v):
    return v[:, :STATE_PER_TILE], v[:, STATE_PER_TILE:]


def _s5_prompt_kernel(u_ref, perm_ref, permt_ref, bw_ref, cw_ref, a_ref, as_ref, mk_ref, d_ref, h0_ref,
                      g_ref, hT_ref, carry_scr, st_scr, y_scr, *, chunk):
    n = pl.program_id(1)
    seg_len = chunk // SUBLANES

    @pl.when(n == 0)
    def _():
        carry_scr[...] = h0_ref[0]

    u_perm = jnp.dot(perm_ref[...], u_ref[...], preferred_element_type=F32).astype(BF16)
    row8 = lax.broadcasted_iota(jnp.int32, (SUBLANES, STATE_PER_TILE), 0)

    for lt in range(N_LANE_TILES):
        ul = u_perm[:, lt * LANES:(lt + 1) * LANES]
        st_scr[...] = jnp.dot(ul, bw_ref[lt], preferred_element_type=F32)
        ar, ai = _split(a_ref[lt])

        def load(i):
            r0 = pl.multiple_of(i * SUBLANES, SUBLANES)
            return (st_scr[pl.ds(r0, SUBLANES), :STATE_PER_TILE],
                    st_scr[pl.ds(r0, SUBLANES), STATE_PER_TILE:])

        def pass1(i, c):
            return _cmul_add(ar, ai, c[0], c[1], *load(i))

        zero = jnp.zeros((SUBLANES, STATE_PER_TILE), F32)
        lr, li = lax.fori_loop(0, seg_len, pass1, (zero, zero), unroll=4)

        pr, pi = _split(carry_scr[lt])
        vr = jnp.where(row8 == 0, pltpu.roll(pr, 1, 0), pltpu.roll(lr, 1, 0))
        vi = jnp.where(row8 == 0, pltpu.roll(pi, 1, 0), pltpu.roll(li, 1, 0))
        for k, shift in enumerate((1, 2, 4)):
            mr, mi = _split(mk_ref[k, lt])
            vr, vi = _cmul_add(mr, mi, pltpu.roll(vr, shift, 0), pltpu.roll(vi, shift, 0), vr, vi)
        asr, asi = _split(as_ref[lt])
        er, ei = _cmul_add(asr, asi, vr, vi, lr, li)
        carry_scr[lt] = jnp.concatenate([er, ei], axis=1)

        def pass2(i, c):
            hr, hi = _cmul_add(ar, ai, c[0], c[1], *load(i))
            r0 = pl.multiple_of(i * SUBLANES, SUBLANES)
            st_scr[pl.ds(r0, SUBLANES), :STATE_PER_TILE] = hr
            st_scr[pl.ds(r0, SUBLANES), STATE_PER_TILE:] = hi
            return hr, hi

        lax.fori_loop(0, seg_len, pass2, (vr, vi), unroll=4)

        y = jnp.dot(st_scr[...].astype(BF16), cw_ref[lt], preferred_element_type=F32)
        y_scr[:, lt * LANES:(lt + 1) * LANES] = y + d_ref[:, lt * LANES:(lt + 1) * LANES] * ul.astype(F32)

    g = jax.nn.gelu(y_scr[...]).astype(BF16)
    g_ref[...] = jnp.dot(permt_ref[...], g, preferred_element_type=F32).astype(BF16)
    hT_ref[0] = carry_scr[...]


def _s5_prompt(u, b_w, c_w, a, d_skip, h0, batch, seq, chunk):
    seg_len = chunk // SUBLANES
    n_sq = int(math.log2(seg_len))
    assert 2 ** n_sq == seg_len
    a_seg = _cpow2(a, n_sq)
    rows = jnp.arange(SUBLANES)[None, :, None]
    bcast = lambda v: jnp.broadcast_to(v[:, None, :], (N_LANE_TILES, SUBLANES, 2 * STATE_PER_TILE))
    mk = jnp.stack([jnp.where(rows >= 2 ** k, bcast(_cpow2(a_seg, k)), 0.0) for k in range(3)])
    r = jnp.arange(chunk)
    tok = (r % SUBLANES) * seg_len + r // SUBLANES
    perm = (tok[:, None] == jnp.arange(chunk)[None, :]).astype(BF16)
    nc = seq // chunk
    row = pl.BlockSpec((chunk, D_MODEL), lambda b, i: (b * nc + i, 0))
    state = pl.BlockSpec((1, N_LANE_TILES, SUBLANES, 2 * STATE_PER_TILE), lambda b, i: (b, 0, 0, 0))
    return pl.pallas_call(
        functools.partial(_s5_prompt_kernel, chunk=chunk),
        grid=(batch, nc),
        in_specs=[row, _const_spec(perm.shape), _const_spec(perm.shape), _const_spec(b_w.shape),
                  _const_spec(c_w.shape), _const_spec((N_LANE_TILES, SUBLANES, 2 * STATE_PER_TILE)),
                  _const_spec((N_LANE_TILES, SUBLANES, 2 * STATE_PER_TILE)), _const_spec(mk.shape),
                  _const_spec((1, D_MODEL)), state],
        out_specs=[row, state],
        out_shape=[jax.ShapeDtypeStruct((batch * seq, D_MODEL), BF16),
                   jax.ShapeDtypeStruct((batch, N_LANE_TILES, SUBLANES, 2 * STATE_PER_TILE), F32)],
        scratch_shapes=[pltpu.VMEM((N_LANE_TILES, SUBLANES, 2 * STATE_PER_TILE), F32),
                        pltpu.VMEM((chunk, 2 * STATE_PER_TILE), F32),
                        pltpu.VMEM((chunk, D_MODEL), F32)],
        compiler_params=_params("arbitrary", "arbitrary"),
        name="s5_prompt",
    )(u, perm, perm.T, b_w, c_w, bcast(a), bcast(a_seg), mk, d_skip, h0)


def _s5_sample_kernel(u_ref, perm_ref, permt_ref, bw_ref, cw_ref, a_ref, d_ref, h0_ref,
                      g_ref, hT_ref, st_scr, y_scr, *, db, dec_seq):
    u_perm = jnp.dot(perm_ref[...], u_ref[...], preferred_element_type=F32).astype(BF16)
    for lt in range(N_LANE_TILES):
        ul = u_perm[:, lt * LANES:(lt + 1) * LANES]
        bu = jnp.dot(ul, bw_ref[lt], preferred_element_type=F32)
        ar, ai = _split(a_ref[lt])
        hr, hi = _split(h0_ref[lt])
        for t in range(dec_seq):
            xr, xi = _split(bu[t * db:(t + 1) * db])
            hr, hi = _cmul_add(ar, ai, hr, hi, xr, xi)
            st_scr[t * db:(t + 1) * db, :STATE_PER_TILE] = hr
            st_scr[t * db:(t + 1) * db, STATE_PER_TILE:] = hi
        hT_ref[lt] = jnp.concatenate([hr, hi], axis=1)
        y = jnp.dot(st_scr[...].astype(BF16), cw_ref[lt], preferred_element_type=F32)
        y_scr[:, lt * LANES:(lt + 1) * LANES] = y + d_ref[:, lt * LANES:(lt + 1) * LANES] * ul.astype(F32)
    g = jax.nn.gelu(y_scr[...]).astype(BF16)
    g_ref[...] = jnp.dot(permt_ref[...], g, preferred_element_type=F32).astype(BF16)


def _s5_sample(u, b_w, c_w, a, d_skip, h0, db, dec_seq):
    n = db * dec_seq
    r = jnp.arange(n)
    tok = (r % db) * dec_seq + r // db
    perm = (tok[:, None] == jnp.arange(n)[None, :]).astype(BF16)
    a1 = a[:, None, :]
    return pl.pallas_call(
        functools.partial(_s5_sample_kernel, db=db, dec_seq=dec_seq),
        grid=(1,),
        in_specs=[_const_spec(u.shape), _const_spec(perm.shape), _const_spec(perm.shape),
                  _const_spec(b_w.shape), _const_spec(c_w.shape), _const_spec(a1.shape),
                  _const_spec((1, D_MODEL)), _const_spec(h0.shape)],
        out_specs=[_const_spec(u.shape), _const_spec(h0.shape)],
        out_shape=[jax.ShapeDtypeStruct(u.shape, BF16), jax.ShapeDtypeStruct(h0.shape, F32)],
        scratch_shapes=[pltpu.VMEM((n, 2 * STATE_PER_TILE), F32), pltpu.VMEM((n, D_MODEL), F32)],
        compiler_params=_params("arbitrary"),
        name="s5_sample",
    )(u, perm, perm.T, b_w, c_w, a1, d_skip, h0)


def _post_kernel(g_ref, ma_ref, gs_ref, x_ref, wglu_ref, wout_ref, gffn_ref, wup_ref, wdown_ref,
                 gfin_ref, out_ref):
    glu = jnp.dot(g_ref[...], wglu_ref[...], preferred_element_type=F32)
    s_out = glu[:, :D_MODEL] * jax.nn.sigmoid(glu[:, D_MODEL:])
    merged = ma_ref[...] + gs_ref[...].astype(F32) * s_out
    x1 = x_ref[...] + jnp.dot(merged.astype(BF16), wout_ref[...], preferred_element_type=F32)
    h2 = _rmsnorm(x1, gffn_ref[...]).astype(BF16)
    up = jnp.dot(h2, wup_ref[...], preferred_element_type=F32)
    act = jnp.square(jnp.maximum(up, 0.0)).astype(BF16)
    x2 = x1 + jnp.dot(act, wdown_ref[...], preferred_element_type=F32)
    out_ref[...] = _rmsnorm(x2, gfin_ref[...])


def _post(g, ma, gs, x, w_glu, w_out, g_ffn, w_up, w_down, g_final, tm):
    n = x.shape[0]
    row = pl.BlockSpec((tm, D_MODEL), lambda i: (i, 0))
    return pl.pallas_call(
        _post_kernel,
        grid=(n // tm,),
        in_specs=[row, row, row, row, _const_spec(w_glu.shape), _const_spec(w_out.shape),
                  _const_spec((1, D_MODEL)), _const_spec(w_up.shape), _const_spec(w_down.shape),
                  _const_spec((1, D_MODEL))],
        out_specs=row,
        out_shape=jax.ShapeDtypeStruct((n, D_MODEL), F32),
        compiler_params=_params("arbitrary"),
        name="post",
    )(g, ma, gs, x, w_glu, w_out, g_ffn, w_up, w_down, g_final)


def _tile(n, pref):
    t = pref
    while n % t:
        t //= 2
    return t


def kernel(x_prompt, x_sample, cache_k, cache_v, state_ssm_re, state_ssm_im, g_mix, w_in, attn_sinks,
           w_attn_o, ssm_lambda_re, ssm_lambda_im, ssm_log_dt, ssm_b_re, ssm_b_im, ssm_c_re, ssm_c_im,
           ssm_d, w_glu, w_out, g_ffn, w_up, w_down, g_final):
    batch, seq, _ = x_prompt.shape
    db, dec_seq, _ = x_sample.shape
    assert w_in.shape[0] == 1, "one layer"
    assert seq % WINDOW == 0 and dec_seq <= SUBLANES and db % SUBLANES == 0

    vec = lambda v: v.reshape(1, D_MODEL).astype(F32)
    w_in_b, wo_b = w_in[0].astype(BF16), w_attn_o[0].astype(BF16)
    w_glu_b, w_out_b = w_glu[0].astype(BF16), w_out[0].astype(BF16)
    w_up_b, w_down_b = w_up[0].astype(BF16), w_down[0].astype(BF16)
    sinks = attn_sinks[0].astype(F32)
    b_w, c_w, a = _s5_weights(ssm_lambda_re[0], ssm_lambda_im[0], ssm_log_dt[0], ssm_b_re[0], ssm_b_im[0],
                              ssm_c_re[0], ssm_c_im[0])
    d_skip = vec(ssm_d[0])

    def state_in(re, im):
        return jnp.concatenate([_tile_state(re), _tile_state(im)], axis=-1)

    def state_out(h):
        return (_untile_state(h[..., :STATE_PER_TILE])[None], _untile_state(h[..., STATE_PER_TILE:])[None])

    xp = x_prompt.reshape(batch * seq, D_MODEL)
    q, kv, u, ga, gs = _proj(xp, vec(g_mix[0]), w_in_b, _tile(batch * seq, 512))
    ma = _attn_prompt(sinks, q, kv, ga, wo_b, batch, seq, _tile(seq, 512))
    chunk = _tile(seq, 256)
    h0p = jnp.zeros((batch, N_LANE_TILES, SUBLANES, 2 * STATE_PER_TILE), F32)
    g_act, hT = _s5_prompt(u, b_w, c_w, a, d_skip, h0p, batch, seq, chunk)
    y_prompt = _post(g_act, ma, gs, xp, w_glu_b, w_out_b, vec(g_ffn[0]), w_up_b, w_down_b, vec(g_final),
                     _tile(batch * seq, 256)).reshape(batch, seq, D_MODEL)
    kv_last = kv.reshape(batch, seq, 2 * D_KV)[:, seq - WINDOW:]
    k_prompt = kv_last[..., :D_KV].reshape(1, batch, WINDOW, N_KV_HEADS, HEAD_DIM)
    v_prompt = kv_last[..., D_KV:].reshape(1, batch, WINDOW, N_KV_HEADS, HEAD_DIM)
    ssm_re_prompt, ssm_im_prompt = state_out(hT[:, :, SUBLANES - 1, :])

    ns = db * dec_seq
    xs = x_sample.reshape(ns, D_MODEL)
    q, kv, u, ga, gs = _proj(xs, vec(g_mix[0]), w_in_b, _tile(ns, 512))
    tpad = SUBLANES
    pad3 = lambda v: jnp.pad(v.reshape(db, dec_seq, -1).astype(F32), ((0, 0), (0, tpad - dec_seq), (0, 0)))
    ga_pad = pad3(ga).astype(BF16).reshape(db * tpad, D_MODEL)
    ma_pad, k_new, v_new = _attn_sample(
        sinks, pad3(q), pad3(kv), cache_k[0].reshape(db, WINDOW, D_KV), cache_v[0].reshape(db, WINDOW, D_KV),
        ga_pad, wo_b, _tile(db, 16), dec_seq)
    ma = ma_pad.reshape(db, tpad, D_MODEL)[:, :dec_seq].reshape(ns, D_MODEL)
    h0s = jnp.swapaxes(state_in(state_ssm_re[0].astype(F32), state_ssm_im[0].astype(F32)), 0, 1)
    g_act, hT = _s5_sample(u, b_w, c_w, a, d_skip, h0s, db, dec_seq)
    y_sample = _post(g_act, ma, gs, xs, w_glu_b, w_out_b, vec(g_ffn[0]), w_up_b, w_down_b, vec(g_final),
                     _tile(ns, 256)).reshape(db, dec_seq, D_MODEL)
    k_sample = k_new.reshape(1, db, WINDOW, N_KV_HEADS, HEAD_DIM)
    v_sample = v_new.reshape(1, db, WINDOW, N_KV_HEADS, HEAD_DIM)
    ssm_re_sample, ssm_im_sample = state_out(jnp.swapaxes(hT, 0, 1))

    return (y_prompt, y_sample, k_prompt, v_prompt, ssm_re_prompt, ssm_im_prompt,
            k_sample, v_sample, ssm_re_sample, ssm_im_sample)
```

```python
import functools

import jax
import jax.numpy as jnp
from jax import lax
from jax.experimental import pallas as pl
from jax.experimental.pallas import tpu as pltpu

F32 = jnp.float32
BF16 = jnp.bfloat16

D_MODEL = 1024
HEAD_DIM = 64
N_HEADS = 16
N_KV_HEADS = 4
WINDOW = 128
D_KV = N_KV_HEADS * HEAD_DIM
SSM_CH = 16
SSM_GROUPS = 64
SSM_STATE = 64
RMS_EPS = 1e-5

LANES = 128
SUBLANES = 8
BF16_ROWS = 16
N_LANE_TILES = D_MODEL // LANES
GROUPS_PER_TILE = LANES // SSM_CH
CHUNK = 16
CHUNK_LANES = CHUNK * D_MODEL
GROUP_IO = CHUNK * SSM_CH
N_PAIRS = SSM_GROUPS // 2
PAIRS_PER_TILE = GROUPS_PER_TILE // 2
PAIR_COLS = 2 * LANES
TILE_COLS = PAIRS_PER_TILE * PAIR_COLS
STATE_COLS = N_PAIRS * PAIR_COLS
HALF_COLS = STATE_COLS // 2
PERM_ROWS = CHUNK * BF16_ROWS
VMEM_LIMIT = 56 * 1024 * 1024

_Q0, _KV0, _U0, _GA0, _GS0, _END = 0, 1024, 1536, 2560, 3584, 4608


def _rmsnorm(x, g):
    return x * lax.rsqrt(jnp.mean(x * x, axis=-1, keepdims=True) + RMS_EPS) * g


def _params(*sem):
    return pltpu.CompilerParams(dimension_semantics=sem, vmem_limit_bytes=VMEM_LIMIT)


def _const_spec(shape):
    nd = len(shape)
    return pl.BlockSpec(shape, lambda *_: (0,) * nd, pipeline_mode=pl.Buffered(1))


def _chunk_perm():
    r = jnp.arange(PERM_ROWS)
    tok = (r % BF16_ROWS) * CHUNK + r // BF16_ROWS
    return (tok[:, None] == jnp.arange(PERM_ROWS)[None, :]).astype(BF16)


def _piece_transpose(cols, masks):
    for d, msk in zip((4, 2, 1), masks):
        new = list(cols)
        for v in range(GROUPS_PER_TILE):
            if v & d == 0:
                a, b = cols[v], cols[v + d]
                new[v] = jnp.where(msk, pltpu.roll(b, SSM_CH * d, 1), a)
                new[v + d] = jnp.where(msk, b, pltpu.roll(a, LANES - SSM_CH * d, 1))
        cols = new
    return cols


def _piece_masks():
    piece = lax.broadcasted_iota(jnp.int32, (SUBLANES, LANES), 1) // SSM_CH
    return [(piece & d) != 0 for d in (4, 2, 1)]


def _words(x):
    return pltpu.bitcast(x, jnp.int32)


def _store_chunk_rows(x, perm_ref, out_ref):
    masks = _piece_masks()
    for hb in range(x.shape[0] // PERM_ROWS):
        xp = jnp.dot(perm_ref[...], x[hb * PERM_ROWS:(hb + 1) * PERM_ROWS],
                     preferred_element_type=F32).astype(BF16)
        for j in range(N_LANE_TILES):
            for hf in range(2):
                cols = [_words(xp[(8 * hf + k) * BF16_ROWS:(8 * hf + k + 1) * BF16_ROWS, j * LANES:(j + 1) * LANES])
                        for k in range(8)]
                for gl, col in enumerate(_piece_transpose(cols, masks)):
                    c0 = (GROUPS_PER_TILE * j + gl) * GROUP_IO + hf * LANES
                    out_ref[hb * BF16_ROWS:(hb + 1) * BF16_ROWS, c0:c0 + LANES] = pltpu.bitcast(col, BF16)


def _load_chunk_rows(in_ref, permt_ref):
    masks = _piece_masks()
    blocks = []
    for hb in range(in_ref.shape[0] // BF16_ROWS):
        tiles = [[None] * N_LANE_TILES for _ in range(CHUNK)]
        for j in range(N_LANE_TILES):
            for hf in range(2):
                cols = []
                for gl in range(GROUPS_PER_TILE):
                    c0 = (GROUPS_PER_TILE * j + gl) * GROUP_IO + hf * LANES
                    cols.append(_words(in_ref[hb * BF16_ROWS:(hb + 1) * BF16_ROWS, c0:c0 + LANES]))
                for k, col in enumerate(_piece_transpose(cols, masks)):
                    tiles[8 * hf + k][j] = pltpu.bitcast(col, BF16)
        xp = jnp.concatenate([jnp.concatenate(row, axis=1) for row in tiles], axis=0)
        blocks.append(jnp.dot(permt_ref[...], xp, preferred_element_type=F32).astype(BF16))
    return blocks[0] if len(blocks) == 1 else jnp.concatenate(blocks, axis=0)


def _proj_kernel(x_ref, g_ref, w_ref, perm_ref, q_ref, kv_ref, u_ref, ga_ref, gs_ref, *, chunked_u):
    h = _rmsnorm(x_ref[...], g_ref[...]).astype(BF16)

    def seg(lo, hi):
        return jnp.dot(h, w_ref[:, lo:hi], preferred_element_type=F32)

    q_ref[...] = seg(_Q0, _KV0).astype(BF16)
    kv_ref[...] = seg(_KV0, _U0)
    u = seg(_U0, _GA0).astype(BF16)
    if chunked_u:
        _store_chunk_rows(u, perm_ref, u_ref)
    else:
        u_ref[...] = u
    ga_ref[...] = jax.nn.sigmoid(seg(_GA0, _GS0)).astype(BF16)
    gs_ref[...] = jax.nn.sigmoid(seg(_GS0, _END)).astype(BF16)


def _proj(x, g_mix, w_in, perm, tm, chunked_u):
    n = x.shape[0]
    row = lambda w: pl.BlockSpec((tm, w), lambda i: (i, 0))
    u_spec = pl.BlockSpec((tm // CHUNK, CHUNK_LANES), lambda i: (i, 0)) if chunked_u else row(D_MODEL)
    u_shape = (n // CHUNK, CHUNK_LANES) if chunked_u else (n, D_MODEL)
    return pl.pallas_call(
        functools.partial(_proj_kernel, chunked_u=chunked_u),
        grid=(n // tm,),
        in_specs=[row(D_MODEL), _const_spec((1, D_MODEL)), _const_spec(w_in.shape), _const_spec(perm.shape)],
        out_specs=[row(D_MODEL), row(2 * D_KV), u_spec, row(D_MODEL), row(D_MODEL)],
        out_shape=[jax.ShapeDtypeStruct((n, D_MODEL), BF16),
                   jax.ShapeDtypeStruct((n, 2 * D_KV), F32),
                   jax.ShapeDtypeStruct(u_shape, BF16),
                   jax.ShapeDtypeStruct((n, D_MODEL), BF16),
                   jax.ShapeDtypeStruct((n, D_MODEL), BF16)],
        compiler_params=_params("arbitrary"),
        name="proj",
    )(x, g_mix, w_in, perm)


def _dup_heads(tile):
    lo = lax.broadcasted_iota(jnp.int32, tile.shape, tile.ndim - 1) < HEAD_DIM
    rolled = pltpu.roll(tile, HEAD_DIM, tile.ndim - 1)
    return (jnp.where(lo, tile, rolled).astype(BF16), jnp.where(lo, rolled, tile).astype(BF16))


def _sink_softmax(s, valid, sink):
    s = jnp.where(valid, s * (HEAD_DIM ** -0.5), -jnp.inf)
    m = jnp.maximum(jnp.max(s, axis=-1, keepdims=True), sink)
    p = jnp.exp(s - m)
    denom = jnp.sum(p, axis=-1, keepdims=True) + jnp.exp(sink - m)
    return (p / denom).astype(BF16)


def _attn_prompt_kernel(sinks_ref, q_ref, kvc_ref, kvp_ref, ga_ref, wo_ref, out_ref,
                        kd_scr, vd_scr, attn_scr, *, tq):
    i = pl.program_id(1)
    kv_full = jnp.concatenate([kvp_ref[...], kvc_ref[...]], axis=0)
    for t in range(2):
        ke, ko = _dup_heads(kv_full[:, t * LANES:(t + 1) * LANES])
        ve, vo = _dup_heads(kv_full[:, D_KV + t * LANES:D_KV + (t + 1) * LANES])
        kd_scr[2 * t], kd_scr[2 * t + 1] = ke, ko
        vd_scr[2 * t], vd_scr[2 * t + 1] = ve, vo

    qi = lax.broadcasted_iota(jnp.int32, (WINDOW, 2 * WINDOW), 0)
    si = lax.broadcasted_iota(jnp.int32, (WINDOW, 2 * WINDOW), 1)
    band = (si > qi) & (si <= qi + WINDOW)
    in_block = si >= WINDOW
    lo = lax.broadcasted_iota(jnp.int32, (WINDOW, LANES), 1) < HEAD_DIM

    def block(jb, carry):
        q0 = pl.multiple_of(jb * WINDOW, WINDOW)
        has_prev = (i > 0) | (jb > 0)
        valid = band & (in_block | has_prev)
        for j in range(N_KV_HEADS):
            kd = kd_scr[j, pl.ds(q0, 2 * WINDOW), :]
            vd = vd_scr[j, pl.ds(q0, 2 * WINDOW), :]
            for r in range(2):
                c0 = j * 2 * LANES + r * LANES
                qp = q_ref[pl.ds(q0, WINDOW), c0:c0 + LANES]
                outs = []
                for par in range(2):
                    head = 4 * j + 2 * r + par
                    qm = jnp.where(lo if par == 0 else ~lo, qp, jnp.zeros_like(qp))
                    s = lax.dot_general(qm, kd, (((1,), (1,)), ((), ())), preferred_element_type=F32)
                    p = _sink_softmax(s, valid, sinks_ref[head])
                    outs.append(jnp.dot(p, vd, preferred_element_type=F32))
                attn_scr[pl.ds(q0, WINDOW), c0:c0 + LANES] = jnp.where(lo, outs[0], outs[1]).astype(BF16)
        return carry

    lax.fori_loop(0, tq // WINDOW, block, 0)
    a_out = jnp.dot(attn_scr[...], wo_ref[...], preferred_element_type=F32)
    out_ref[...] = ga_ref[...].astype(F32) * a_out


def _attn_prompt(sinks, q, kv, ga, wo, batch, seq, tq):
    nq = seq // tq
    bpt = tq // WINDOW
    row = lambda w: pl.BlockSpec((tq, w), lambda b, i: (b * nq + i, 0))
    prev = pl.BlockSpec((WINDOW, 2 * D_KV),
                        lambda b, i: (jnp.maximum((b * nq + i) * bpt - 1, 0), 0))
    return pl.pallas_call(
        functools.partial(_attn_prompt_kernel, tq=tq),
        grid=(batch, nq),
        in_specs=[pl.BlockSpec(memory_space=pltpu.SMEM), row(D_MODEL), row(2 * D_KV), prev,
                  row(D_MODEL), _const_spec(wo.shape)],
        out_specs=row(D_MODEL),
        out_shape=jax.ShapeDtypeStruct((batch * seq, D_MODEL), F32),
        scratch_shapes=[pltpu.VMEM((N_KV_HEADS, WINDOW + tq, LANES), BF16),
                        pltpu.VMEM((N_KV_HEADS, WINDOW + tq, LANES), BF16),
                        pltpu.VMEM((tq, D_MODEL), BF16)],
        compiler_params=_params("arbitrary", "arbitrary"),
        name="attn_prompt",
    )(sinks, q, kv, kv, ga, wo)


def _attn_sample_kernel(sinks_ref, q_ref, kvn_ref, ck_ref, cv_ref, ga_ref, wo_ref,
                        out_ref, ko_ref, vo_ref, attn_scr, *, bb, tpad, dec_seq):
    nk = WINDOW + tpad
    lo3 = lax.broadcasted_iota(jnp.int32, (bb, tpad, LANES), 2) < HEAD_DIM
    row = lax.broadcasted_iota(jnp.int32, (4 * tpad, nk), 0)
    si = lax.broadcasted_iota(jnp.int32, (4 * tpad, nk), 1)
    tq = row % tpad
    valid = (si > tq) & (si <= tq + WINDOW) & (si < WINDOW + dec_seq)
    hrow = lax.broadcasted_iota(jnp.int32, (4 * tpad, 1), 0) // tpad

    for t in range(2):
        sl = slice(t * LANES, (t + 1) * LANES)
        kk = jnp.concatenate([ck_ref[:, :, sl], kvn_ref[:, :, sl]], axis=1)
        vv = jnp.concatenate([cv_ref[:, :, sl],
                              kvn_ref[:, :, D_KV + t * LANES:D_KV + (t + 1) * LANES]], axis=1)
        ko_ref[:, :, sl] = kk[:, dec_seq:dec_seq + WINDOW, :]
        vo_ref[:, :, sl] = vv[:, dec_seq:dec_seq + WINDOW, :]
        kds = _dup_heads(kk)
        vds = _dup_heads(vv)
        for par_kv in range(2):
            j = 2 * t + par_kv
            kd, vd = kds[par_kv], vds[par_kv]
            parts = []
            for r in range(2):
                c0 = j * 2 * LANES + r * LANES
                qp = q_ref[:, :, c0:c0 + LANES]
                parts += [jnp.where(lo3, qp, 0.0), jnp.where(lo3, 0.0, qp)]
            lhs = jnp.concatenate(parts, axis=1).astype(BF16)
            s = jnp.einsum('bqd,bkd->bqk', lhs, kd, preferred_element_type=F32)
            sink = jnp.zeros((4 * tpad, 1), F32)
            for g in range(4):
                sink = jnp.where(hrow == g, sinks_ref[4 * j + g], sink)
            p = _sink_softmax(s, valid[None], sink[None])
            o = jnp.einsum('bqk,bkd->bqd', p, vd, preferred_element_type=F32)
            for r in range(2):
                c0 = j * 2 * LANES + r * LANES
                o_even = o[:, (2 * r) * tpad:(2 * r + 1) * tpad, :]
                o_odd = o[:, (2 * r + 1) * tpad:(2 * r + 2) * tpad, :]
                attn_scr[:, :, c0:c0 + LANES] = jnp.where(lo3, o_even, o_odd)

    attn = attn_scr[...].reshape(bb * tpad, D_MODEL).astype(BF16)
    a_out = jnp.dot(attn, wo_ref[...], preferred_element_type=F32)
    out_ref[...] = ga_ref[...].astype(F32) * a_out


def _attn_sample(sinks, q3, kvn3, ck, cv, ga2, wo, bb, dec_seq):
    db, tpad, _ = q3.shape
    blk3 = lambda r, w: pl.BlockSpec((bb, r, w), lambda i: (i, 0, 0))
    row = pl.BlockSpec((bb * tpad, D_MODEL), lambda i: (i, 0))
    return pl.pallas_call(
        functools.partial(_attn_sample_kernel, bb=bb, tpad=tpad, dec_seq=dec_seq),
        grid=(db // bb,),
        in_specs=[pl.BlockSpec(memory_space=pltpu.SMEM), blk3(tpad, D_MODEL), blk3(tpad, 2 * D_KV),
                  blk3(WINDOW, D_KV), blk3(WINDOW, D_KV), row, _const_spec(wo.shape)],
        out_specs=[row, blk3(WINDOW, D_KV), blk3(WINDOW, D_KV)],
        out_shape=[jax.ShapeDtypeStruct((db * tpad, D_MODEL), F32),
                   jax.ShapeDtypeStruct((db, WINDOW, D_KV), F32),
                   jax.ShapeDtypeStruct((db, WINDOW, D_KV), F32)],
        scratch_shapes=[pltpu.VMEM((bb, tpad, D_MODEL), F32)],
        compiler_params=_params("arbitrary"),
        name="attn_sample",
    )(sinks, q3, kvn3, ck, cv, ga2, wo)


def _s5_discretize(lam_re, lam_im, log_dt, b_re, b_im):
    dt = jnp.exp(log_dt)[:, None]
    decay = jnp.exp(lam_re * dt)
    ab_re = decay * jnp.cos(lam_im * dt)
    ab_im = decay * jnp.sin(lam_im * dt)
    nr, ni = ab_re - 1.0, ab_im
    den = lam_re * lam_re + lam_im * lam_im
    f_re = ((nr * lam_re + ni * lam_im) / den)[..., None]
    f_im = ((ni * lam_re - nr * lam_im) / den)[..., None]
    return ab_re, ab_im, f_re * b_re - f_im * b_im, f_re * b_im + f_im * b_re


def _pair_cols(a):
    return a.reshape(a.shape[:-2] + (N_PAIRS, LANES))


def _state_cols(re, im):
    return jnp.stack([_pair_cols(re), _pair_cols(im)], axis=-2).reshape(re.shape[:-2] + (STATE_COLS,))


def _state_split(h):
    h = h.reshape(h.shape[:-1] + (N_PAIRS, 2, 2, SSM_STATE))
    unpair = lambda a: a.reshape(a.shape[:-3] + (SSM_GROUPS, SSM_STATE))
    return unpair(h[..., 0, :, :]), unpair(h[..., 1, :, :])


def _s5_weights(lam_re, lam_im, log_dt, b_re, b_im, c_re, c_im, n_tok):
    hi = lax.Precision.HIGHEST
    ab_re, ab_im, bb_re, bb_im = _s5_discretize(lam_re, lam_im, log_dt, b_re, b_im)
    pr, pi = [jnp.ones_like(ab_re)], [jnp.zeros_like(ab_im)]
    for _ in range(CHUNK):
        pr, pi = pr + [pr[-1] * ab_re - pi[-1] * ab_im], pi + [pr[-1] * ab_im + pi[-1] * ab_re]
    pr, pi = jnp.stack(pr), jnp.stack(pi)
    ca_re = c_re[None] * pr[:, :, None, :] - c_im[None] * pi[:, :, None, :]
    ca_im = c_re[None] * pi[:, :, None, :] + c_im[None] * pr[:, :, None, :]
    kern = (jnp.einsum('kgop,gpi->kgoi', ca_re[:CHUNK], bb_re, precision=hi)
            - jnp.einsum('kgop,gpi->kgoi', ca_im[:CHUNK], bb_im, precision=hi))
    s_idx = jnp.arange(CHUNK)[:, None]
    t_idx = jnp.arange(CHUNK)[None, :]
    lag = t_idx - s_idx
    m = jnp.where((lag >= 0)[:, :, None, None, None], kern[jnp.clip(lag, 0, CHUNK - 1)], 0.0)
    m = m.transpose(2, 0, 4, 1, 3).reshape(SSM_GROUPS, GROUP_IO, GROUP_IO)
    k_s = jnp.clip(n_tok - 1 - jnp.arange(CHUNK), 0, CHUNK)
    live = (jnp.arange(CHUNK) < n_tok)[:, None, None, None]
    ab_b_re = jnp.where(live, pr[k_s][..., None] * bb_re[None] - pi[k_s][..., None] * bb_im[None], 0.0)
    ab_b_im = jnp.where(live, pr[k_s][..., None] * bb_im[None] + pi[k_s][..., None] * bb_re[None], 0.0)
    in_pair = jax.nn.one_hot(jnp.arange(SSM_GROUPS) % 2, 2, dtype=F32)
    ws = jnp.stack([ab_b_re, ab_b_im], axis=0).transpose(2, 1, 4, 0, 3)
    ws = ws.reshape(SSM_GROUPS, GROUP_IO, 2, 1, SSM_STATE) * in_pair[:, None, None, :, None]
    ws = ws.reshape(SSM_GROUPS, GROUP_IO, PAIR_COLS)
    why = jnp.stack([ca_re[1:], -ca_im[1:]], axis=0).transpose(2, 0, 4, 1, 3)
    why = why.reshape(SSM_GROUPS, 2, 1, SSM_STATE, GROUP_IO) * in_pair[:, None, :, None, None]
    why = why.reshape(SSM_GROUPS, PAIR_COLS, GROUP_IO)
    return m.astype(BF16), ws.astype(BF16), why.astype(BF16), pr, pi


def _group_io(ref, g):
    return ref.at[:, g * GROUP_IO:(g + 1) * GROUP_IO]


def _s5_emit_tile(u_ref, st_scr, m_ref, why_ref, dt_ref, g_ref, j):
    for gl in range(GROUPS_PER_TILE):
        g = GROUPS_PER_TILE * j + gl
        q0 = (gl // 2) * PAIR_COLS
        u_g = _group_io(u_ref, g)[...]
        y = (jnp.dot(u_g, m_ref[g], preferred_element_type=F32)
             + jnp.dot(st_scr[:, q0:q0 + PAIR_COLS].astype(BF16), why_ref[g], preferred_element_type=F32)
             + dt_ref[g] * u_g.astype(F32))
        _group_io(g_ref, g)[...] = jax.nn.gelu(y).astype(BF16)


def _s5_local_states(u_ref, ws_ref, j, q):
    g = GROUPS_PER_TILE * j + 2 * q
    return (jnp.dot(_group_io(u_ref, g)[...], ws_ref[g], preferred_element_type=F32)
            + jnp.dot(_group_io(u_ref, g + 1)[...], ws_ref[g + 1], preferred_element_type=F32))


def _cmul_add(ar, ai, hr, hi, xr, xi):
    return ar * hr - ai * hi + xr, ar * hi + ai * hr + xi


def _s5_chain_kernel(u_ref, m_ref, ws_ref, why_ref, mk_ref, pw_ref, dt_ref, h0_ref, g_ref, hT_ref,
                     carry_scr, st_scr, *, rows):
    @pl.when(pl.program_id(1) == 0)
    def _():
        carry_scr[...] = jnp.broadcast_to(h0_ref[0], (SUBLANES, STATE_COLS))

    row0 = lax.broadcasted_iota(jnp.int32, (SUBLANES, LANES), 0) == 0
    last = lambda h: jnp.broadcast_to(h[SUBLANES - 1:, :], (SUBLANES, LANES))

    for j in range(N_LANE_TILES):
        for q in range(PAIRS_PER_TILE):
            st_scr[:, q * PAIR_COLS:(q + 1) * PAIR_COLS] = _s5_local_states(u_ref, ws_ref, j, q)

        def step(r, carry):
            r0 = pl.multiple_of(r * SUBLANES, SUBLANES)
            out = []
            for q in range(PAIRS_PER_TILE):
                re_c = slice(q * PAIR_COLS, q * PAIR_COLS + LANES)
                im_c = slice(q * PAIR_COLS + LANES, (q + 1) * PAIR_COLS)
                hc = slice((j * PAIRS_PER_TILE + q) * LANES, (j * PAIRS_PER_TILE + q + 1) * LANES)
                xr = st_scr[pl.ds(r0, SUBLANES), re_c]
                xi = st_scr[pl.ds(r0, SUBLANES), im_c]
                for k, shift in enumerate((1, 2, 4)):
                    xr, xi = _cmul_add(mk_ref[k, 0, :, hc], mk_ref[k, 1, :, hc],
                                       pltpu.roll(xr, shift, 0), pltpu.roll(xi, shift, 0), xr, xi)
                cr, ci = carry[q]
                hr, hi = _cmul_add(pw_ref[0, :, hc], pw_ref[1, :, hc], cr, ci, xr, xi)
                st_scr[pl.ds(r0, SUBLANES), re_c] = jnp.where(row0, cr, pltpu.roll(hr, 1, 0))
                st_scr[pl.ds(r0, SUBLANES), im_c] = jnp.where(row0, ci, pltpu.roll(hi, 1, 0))
                out.append((last(hr), last(hi)))
            return tuple(out)

        c0 = j * TILE_COLS
        init = tuple((carry_scr[:, c0 + q * PAIR_COLS:c0 + q * PAIR_COLS + LANES],
                      carry_scr[:, c0 + q * PAIR_COLS + LANES:c0 + (q + 1) * PAIR_COLS])
                     for q in range(PAIRS_PER_TILE))
        final = lax.fori_loop(0, rows // SUBLANES, step, init)
        for q in range(PAIRS_PER_TILE):
            carry_scr[:, c0 + q * PAIR_COLS:c0 + q * PAIR_COLS + LANES] = final[q][0]
            carry_scr[:, c0 + q * PAIR_COLS + LANES:c0 + (q + 1) * PAIR_COLS] = final[q][1]

        _s5_emit_tile(u_ref, st_scr, m_ref, why_ref, dt_ref, g_ref, j)

    hT_ref[0] = carry_scr[...]


def _s5_chain(u16, m, ws, why, pr, pi, d_tiled, h0, batch, n_rows, rows):
    a_re, a_im = pr[CHUNK].reshape(HALF_COLS), pi[CHUNK].reshape(HALF_COLS)
    pw_r, pw_i, mk = [a_re], [a_im], []
    for _ in range(SUBLANES - 1):
        pw_r, pw_i = pw_r + [pw_r[-1] * a_re - pw_i[-1] * a_im], pw_i + [pw_r[-1] * a_im + pw_i[-1] * a_re]
    sub = jnp.arange(SUBLANES)[:, None]
    for shift in (1, 2, 4):
        cm = jnp.stack([pw_r[shift - 1], pw_i[shift - 1]])[:, None, :]
        mk.append(jnp.where(sub >= shift, jnp.broadcast_to(cm, (2, SUBLANES, HALF_COLS)), 0.0))
    mk = jnp.stack(mk)
    pw = jnp.stack([jnp.stack(pw_r), jnp.stack(pw_i)])
    nblk = n_rows // rows
    row = pl.BlockSpec((rows, CHUNK_LANES), lambda b, i: (b * nblk + i, 0))
    return pl.pallas_call(
        functools.partial(_s5_chain_kernel, rows=rows),
        grid=(batch, nblk),
        in_specs=[row, _const_spec(m.shape), _const_spec(ws.shape), _const_spec(why.shape),
                  _const_spec(mk.shape), _const_spec(pw.shape), _const_spec(d_tiled.shape),
                  pl.BlockSpec((1, 1, STATE_COLS), lambda b, i: (b, 0, 0))],
        out_specs=[row, pl.BlockSpec((1, SUBLANES, STATE_COLS), lambda b, i: (b, 0, 0))],
        out_shape=[jax.ShapeDtypeStruct(u16.shape, BF16),
                   jax.ShapeDtypeStruct((batch, SUBLANES, STATE_COLS), F32)],
        scratch_shapes=[pltpu.VMEM((SUBLANES, STATE_COLS), F32), pltpu.VMEM((rows, TILE_COLS), F32)],
        compiler_params=_params("arbitrary", "arbitrary"),
        name="s5_chain",
    )(u16, m, ws, why, mk, pw, d_tiled, h0)


def _s5_rows_kernel(u_ref, m_ref, ws_ref, why_ref, an_ref, dt_ref, h0_ref, g_ref, hT_ref,
                    st_scr):
    for j in range(N_LANE_TILES):
        for q in range(PAIRS_PER_TILE):
            c0 = j * TILE_COLS + q * PAIR_COLS
            hc = slice((j * PAIRS_PER_TILE + q) * LANES, (j * PAIRS_PER_TILE + q + 1) * LANES)
            h0 = h0_ref[:, c0:c0 + PAIR_COLS]
            local = _s5_local_states(u_ref, ws_ref, j, q)
            hr, hi = _cmul_add(an_ref[0, :, hc], an_ref[1, :, hc], h0[:, :LANES], h0[:, LANES:],
                               local[:, :LANES], local[:, LANES:])
            hT_ref[:, c0:c0 + LANES] = hr
            hT_ref[:, c0 + LANES:c0 + PAIR_COLS] = hi
            st_scr[:, q * PAIR_COLS:(q + 1) * PAIR_COLS] = h0
        _s5_emit_tile(u_ref, st_scr, m_ref, why_ref, dt_ref, g_ref, j)


def _s5_rows(u16, m, ws, why, an, d_tiled, h0, rows):
    n = u16.shape[0]
    row = lambda w: pl.BlockSpec((rows, w), lambda i: (i, 0))
    return pl.pallas_call(
        _s5_rows_kernel,
        grid=(n // rows,),
        in_specs=[row(CHUNK_LANES), _const_spec(m.shape), _const_spec(ws.shape), _const_spec(why.shape),
                  _const_spec(an.shape), _const_spec(d_tiled.shape), row(STATE_COLS)],
        out_specs=[row(CHUNK_LANES), row(STATE_COLS)],
        out_shape=[jax.ShapeDtypeStruct(u16.shape, BF16), jax.ShapeDtypeStruct((n, STATE_COLS), F32)],
        scratch_shapes=[pltpu.VMEM((rows, TILE_COLS), F32)],
        compiler_params=_params("arbitrary"),
        name="s5_rows",
    )(u16, m, ws, why, an, d_tiled, h0)


def _post_kernel(g_ref, ma_ref, gs_ref, x_ref, permt_ref, wglu_ref, wout_ref, gffn_ref, wup_ref, wdown_ref,
                 gfin_ref, out_ref, *, chunked_g):
    g = _load_chunk_rows(g_ref, permt_ref) if chunked_g else g_ref[...]
    glu = jnp.dot(g, wglu_ref[...], preferred_element_type=F32)
    s_out = glu[:, :D_MODEL] * jax.nn.sigmoid(glu[:, D_MODEL:])
    merged = ma_ref[...] + gs_ref[...].astype(F32) * s_out
    x1 = x_ref[...] + jnp.dot(merged.astype(BF16), wout_ref[...], preferred_element_type=F32)
    h2 = _rmsnorm(x1, gffn_ref[...]).astype(BF16)
    up = jnp.dot(h2, wup_ref[...], preferred_element_type=F32)
    act = jnp.square(jnp.maximum(up, 0.0)).astype(BF16)
    x2 = x1 + jnp.dot(act, wdown_ref[...], preferred_element_type=F32)
    out_ref[...] = _rmsnorm(x2, gfin_ref[...])


def _post(g, ma, gs, x, permt, w_glu, w_out, g_ffn, w_up, w_down, g_final, tm, chunked_g):
    n = x.shape[0]
    row = pl.BlockSpec((tm, D_MODEL), lambda i: (i, 0))
    g_spec = pl.BlockSpec((tm // CHUNK, CHUNK_LANES), lambda i: (i, 0)) if chunked_g else row
    return pl.pallas_call(
        functools.partial(_post_kernel, chunked_g=chunked_g),
        grid=(n // tm,),
        in_specs=[g_spec, row, row, row, _const_spec(permt.shape), _const_spec(w_glu.shape),
                  _const_spec(w_out.shape), _const_spec((1, D_MODEL)), _const_spec(w_up.shape),
                  _const_spec(w_down.shape), _const_spec((1, D_MODEL))],
        out_specs=row,
        out_shape=jax.ShapeDtypeStruct((n, D_MODEL), F32),
        compiler_params=_params("arbitrary"),
        name="post",
    )(g, ma, gs, x, permt, w_glu, w_out, g_ffn, w_up, w_down, g_final)


def _tile(n, pref):
    t = pref
    while n % t:
        t //= 2
    return t


def kernel(x_prompt, x_sample, cache_k, cache_v, state_ssm_re, state_ssm_im, g_mix, w_in, attn_sinks,
           w_attn_o, ssm_lambda_re, ssm_lambda_im, ssm_log_dt, ssm_b_re, ssm_b_im, ssm_c_re, ssm_c_im,
           ssm_d, w_glu, w_out, g_ffn, w_up, w_down, g_final):
    batch, seq, _ = x_prompt.shape
    db, dec_seq, _ = x_sample.shape
    assert w_in.shape[0] == 1, "one layer"
    assert seq % PERM_ROWS == 0 and dec_seq <= SUBLANES and db % SUBLANES == 0

    vec = lambda v: v.reshape(1, D_MODEL).astype(F32)
    w_in_b, wo_b = w_in[0].astype(BF16), w_attn_o[0].astype(BF16)
    w_glu_b, w_out_b = w_glu[0].astype(BF16), w_out[0].astype(BF16)
    w_up_b, w_down_b = w_up[0].astype(BF16), w_down[0].astype(BF16)
    sinks = attn_sinks[0].astype(F32)
    s5p = (ssm_lambda_re[0], ssm_lambda_im[0], ssm_log_dt[0], ssm_b_re[0], ssm_b_im[0], ssm_c_re[0], ssm_c_im[0])
    d_skip = jnp.tile(ssm_d[0].astype(F32).reshape(SSM_GROUPS, 1, SSM_CH), (1, 1, CHUNK))
    perm = _chunk_perm()
    post_w = (w_glu_b, w_out_b, vec(g_ffn[0]), w_up_b, w_down_b, vec(g_final))

    xp = x_prompt.reshape(batch * seq, D_MODEL)
    q, kv, u16, ga, gs = _proj(xp, vec(g_mix[0]), w_in_b, perm, _tile(batch * seq, 512), True)
    ma = _attn_prompt(sinks, q, kv, ga, wo_b, batch, seq, _tile(seq, 512))
    m, ws, why, pr, pi = _s5_weights(*s5p, CHUNK)
    n_rows = seq // CHUNK
    g16, hT = _s5_chain(u16, m, ws, why, pr, pi, d_skip, jnp.zeros((batch, 1, STATE_COLS), F32),
                        batch, n_rows, _tile(n_rows, 128))
    y_prompt = _post(g16, ma, gs, xp, perm.T, *post_w, PERM_ROWS, True).reshape(batch, seq, D_MODEL)
    kv_last = kv.reshape(batch, seq, 2 * D_KV)[:, seq - WINDOW:]
    k_prompt = kv_last[..., :D_KV].reshape(1, batch, WINDOW, N_KV_HEADS, HEAD_DIM)
    v_prompt = kv_last[..., D_KV:].reshape(1, batch, WINDOW, N_KV_HEADS, HEAD_DIM)
    hr, hi = _state_split(hT[:, 0])
    ssm_re_prompt, ssm_im_prompt = hr[None], hi[None]

    ns = db * dec_seq
    xs = x_sample.reshape(ns, D_MODEL)
    q, kv, u, ga, gs = _proj(xs, vec(g_mix[0]), w_in_b, perm, _tile(ns, 512), False)
    tpad = SUBLANES
    pad3 = lambda v: jnp.pad(v.reshape(db, dec_seq, -1).astype(F32), ((0, 0), (0, tpad - dec_seq), (0, 0)))
    ga_pad = pad3(ga).astype(BF16).reshape(db * tpad, D_MODEL)
    ma_pad, k_new, v_new = _attn_sample(
        sinks, pad3(q), pad3(kv), cache_k[0].reshape(db, WINDOW, D_KV), cache_v[0].reshape(db, WINDOW, D_KV),
        ga_pad, wo_b, _tile(db, 16), dec_seq)
    ma = ma_pad.reshape(db, tpad, D_MODEL)[:, :dec_seq].reshape(ns, D_MODEL)
    u16 = u.reshape(db, dec_seq, SSM_GROUPS, SSM_CH).transpose(0, 2, 1, 3)
    u16 = jnp.pad(u16, ((0, 0), (0, 0), (0, CHUNK - dec_seq), (0, 0))).reshape(db, CHUNK_LANES)
    m, ws, why, pr, pi = _s5_weights(*s5p, dec_seq)
    an = jnp.stack([pr[dec_seq].reshape(1, HALF_COLS), pi[dec_seq].reshape(1, HALF_COLS)])
    h0 = _state_cols(state_ssm_re[0].astype(F32), state_ssm_im[0].astype(F32))
    g16, hT = _s5_rows(u16, m, ws, why, an, d_skip, h0, _tile(db, 64))
    g_act = g16.reshape(db, SSM_GROUPS, CHUNK, SSM_CH)[:, :, :dec_seq].transpose(0, 2, 1, 3).reshape(ns, D_MODEL)
    y_sample = _post(g_act, ma, gs, xs, perm.T, *post_w, _tile(ns, 256), False).reshape(db, dec_seq, D_MODEL)
    k_sample = k_new.reshape(1, db, WINDOW, N_KV_HEADS, HEAD_DIM)
    v_sample = v_new.reshape(1, db, WINDOW, N_KV_HEADS, HEAD_DIM)
    hr, hi = _state_split(hT)
    ssm_re_sample, ssm_im_sample = hr[None], hi[None]

    return (y_prompt, y_sample, k_prompt, v_prompt, ssm_re_prompt, ssm_im_prompt,
            k_sample, v_sample, ssm_re_sample, ssm_im_sample)
```

```python
import functools

import jax
import jax.numpy as jnp
from jax import lax
from jax.experimental import pallas as pl
from jax.experimental.pallas import tpu as pltpu

F32 = jnp.float32
BF16 = jnp.bfloat16

D_MODEL = 1024
HEAD_DIM = 64
N_HEADS = 16
N_KV_HEADS = 4
WINDOW = 128
D_KV = N_KV_HEADS * HEAD_DIM
SSM_CH = 16
SSM_GROUPS = 64
SSM_STATE = 64
RMS_EPS = 1e-5

LANES = 128
SUBLANES = 8
BF16_ROWS = 16
N_LANE_TILES = D_MODEL // LANES
GROUPS_PER_TILE = LANES // SSM_CH
CHUNK = 16
CHUNK_LANES = CHUNK * D_MODEL
GROUP_IO = CHUNK * SSM_CH
N_PAIRS = SSM_GROUPS // 2
PAIRS_PER_TILE = GROUPS_PER_TILE // 2
PAIR_COLS = 2 * LANES
TILE_COLS = PAIRS_PER_TILE * PAIR_COLS
STATE_COLS = N_PAIRS * PAIR_COLS
HALF_COLS = STATE_COLS // 2
PERM_ROWS = CHUNK * BF16_ROWS
VMEM_LIMIT = 56 * 1024 * 1024

_Q0, _KV0, _U0, _GA0, _GS0, _END = 0, 1024, 1536, 2560, 3584, 4608


def _rmsnorm(x, g):
    return x * lax.rsqrt(jnp.mean(x * x, axis=-1, keepdims=True) + RMS_EPS) * g


def _params(*sem):
    return pltpu.CompilerParams(dimension_semantics=sem, vmem_limit_bytes=VMEM_LIMIT)


def _const_spec(shape):
    nd = len(shape)
    return pl.BlockSpec(shape, lambda *_: (0,) * nd, pipeline_mode=pl.Buffered(1))


def _chunk_perm():
    r = jnp.arange(PERM_ROWS)
    tok = (r % BF16_ROWS) * CHUNK + r // BF16_ROWS
    return (tok[:, None] == jnp.arange(PERM_ROWS)[None, :]).astype(BF16)


def _piece_transpose(cols, masks):
    for d, msk in zip((4, 2, 1), masks):
        new = list(cols)
        for v in range(GROUPS_PER_TILE):
            if v & d == 0:
                a, b = cols[v], cols[v + d]
                new[v] = jnp.where(msk, pltpu.roll(b, SSM_CH * d, 1), a)
                new[v + d] = jnp.where(msk, b, pltpu.roll(a, LANES - SSM_CH * d, 1))
        cols = new
    return cols


def _piece_masks():
    piece = lax.broadcasted_iota(jnp.int32, (BF16_ROWS, LANES), 1) // SSM_CH
    return [(piece & d) != 0 for d in (4, 2, 1)]


def _store_chunk_rows(x, perm_ref, out_ref):
    masks = _piece_masks()
    for hb in range(x.shape[0] // PERM_ROWS):
        xp = jnp.dot(perm_ref[...], x[hb * PERM_ROWS:(hb + 1) * PERM_ROWS],
                     preferred_element_type=F32)
        for j in range(N_LANE_TILES):
            for hf in range(2):
                cols = [xp[(8 * hf + k) * BF16_ROWS:(8 * hf + k + 1) * BF16_ROWS, j * LANES:(j + 1) * LANES]
                        for k in range(8)]
                for gl, col in enumerate(_piece_transpose(cols, masks)):
                    c0 = (GROUPS_PER_TILE * j + gl) * GROUP_IO + hf * LANES
                    out_ref[hb * BF16_ROWS:(hb + 1) * BF16_ROWS, c0:c0 + LANES] = col.astype(BF16)


def _load_chunk_rows(in_ref, permt_ref):
    masks = _piece_masks()
    blocks = []
    for hb in range(in_ref.shape[0] // BF16_ROWS):
        tiles = [[None] * N_LANE_TILES for _ in range(CHUNK)]
        for j in range(N_LANE_TILES):
            for hf in range(2):
                cols = []
                for gl in range(GROUPS_PER_TILE):
                    c0 = (GROUPS_PER_TILE * j + gl) * GROUP_IO + hf * LANES
                    cols.append(in_ref[hb * BF16_ROWS:(hb + 1) * BF16_ROWS, c0:c0 + LANES].astype(F32))
                for k, col in enumerate(_piece_transpose(cols, masks)):
                    tiles[8 * hf + k][j] = col.astype(BF16)
        xp = jnp.concatenate([jnp.concatenate(row, axis=1) for row in tiles], axis=0)
        blocks.append(jnp.dot(permt_ref[...], xp, preferred_element_type=F32).astype(BF16))
    return blocks[0] if len(blocks) == 1 else jnp.concatenate(blocks, axis=0)


def _proj_kernel(x_ref, g_ref, w_ref, perm_ref, q_ref, kv_ref, u_ref, ga_ref, gs_ref, *, chunked_u):
    h = _rmsnorm(x_ref[...], g_ref[...]).astype(BF16)

    def seg(lo, hi):
        return jnp.dot(h, w_ref[:, lo:hi], preferred_element_type=F32)

    q_ref[...] = seg(_Q0, _KV0).astype(BF16)
    kv_ref[...] = seg(_KV0, _U0)
    u = seg(_U0, _GA0).astype(BF16)
    if chunked_u:
        _store_chunk_rows(u, perm_ref, u_ref)
    else:
        u_ref[...] = u
    ga_ref[...] = jax.nn.sigmoid(seg(_GA0, _GS0)).astype(BF16)
    gs_ref[...] = jax.nn.sigmoid(seg(_GS0, _END)).astype(BF16)


def _proj(x, g_mix, w_in, perm, tm, chunked_u):
    n = x.shape[0]
    row = lambda w: pl.BlockSpec((tm, w), lambda i: (i, 0))
    u_spec = pl.BlockSpec((tm // CHUNK, CHUNK_LANES), lambda i: (i, 0)) if chunked_u else row(D_MODEL)
    u_shape = (n // CHUNK, CHUNK_LANES) if chunked_u else (n, D_MODEL)
    return pl.pallas_call(
        functools.partial(_proj_kernel, chunked_u=chunked_u),
        grid=(n // tm,),
        in_specs=[row(D_MODEL), _const_spec((1, D_MODEL)), _const_spec(w_in.shape), _const_spec(perm.shape)],
        out_specs=[row(D_MODEL), row(2 * D_KV), u_spec, row(D_MODEL), row(D_MODEL)],
        out_shape=[jax.ShapeDtypeStruct((n, D_MODEL), BF16),
                   jax.ShapeDtypeStruct((n, 2 * D_KV), F32),
                   jax.ShapeDtypeStruct(u_shape, BF16),
                   jax.ShapeDtypeStruct((n, D_MODEL), BF16),
                   jax.ShapeDtypeStruct((n, D_MODEL), BF16)],
        compiler_params=_params("arbitrary"),
        name="proj",
    )(x, g_mix, w_in, perm)


def _dup_heads(tile):
    lo = lax.broadcasted_iota(jnp.int32, tile.shape, tile.ndim - 1) < HEAD_DIM
    rolled = pltpu.roll(tile, HEAD_DIM, tile.ndim - 1)
    return (jnp.where(lo, tile, rolled).astype(BF16), jnp.where(lo, rolled, tile).astype(BF16))


def _sink_softmax(s, valid, sink):
    s = jnp.where(valid, s * (HEAD_DIM ** -0.5), -jnp.inf)
    m = jnp.maximum(jnp.max(s, axis=-1, keepdims=True), sink)
    p = jnp.exp(s - m)
    denom = jnp.sum(p, axis=-1, keepdims=True) + jnp.exp(sink - m)
    return (p / denom).astype(BF16)


def _attn_prompt_kernel(sinks_ref, q_ref, kvc_ref, kvp_ref, ga_ref, wo_ref, out_ref,
                        kd_scr, vd_scr, attn_scr, *, tq):
    i = pl.program_id(1)
    kv_full = jnp.concatenate([kvp_ref[...], kvc_ref[...]], axis=0)
    for t in range(2):
        ke, ko = _dup_heads(kv_full[:, t * LANES:(t + 1) * LANES])
        ve, vo = _dup_heads(kv_full[:, D_KV + t * LANES:D_KV + (t + 1) * LANES])
        kd_scr[2 * t], kd_scr[2 * t + 1] = ke, ko
        vd_scr[2 * t], vd_scr[2 * t + 1] = ve, vo

    qi = lax.broadcasted_iota(jnp.int32, (WINDOW, 2 * WINDOW), 0)
    si = lax.broadcasted_iota(jnp.int32, (WINDOW, 2 * WINDOW), 1)
    band = (si > qi) & (si <= qi + WINDOW)
    in_block = si >= WINDOW
    lo = lax.broadcasted_iota(jnp.int32, (WINDOW, LANES), 1) < HEAD_DIM

    def block(jb, carry):
        q0 = pl.multiple_of(jb * WINDOW, WINDOW)
        has_prev = (i > 0) | (jb > 0)
        valid = band & (in_block | has_prev)
        for j in range(N_KV_HEADS):
            kd = kd_scr[j, pl.ds(q0, 2 * WINDOW), :]
            vd = vd_scr[j, pl.ds(q0, 2 * WINDOW), :]
            for r in range(2):
                c0 = j * 2 * LANES + r * LANES
                qp = q_ref[pl.ds(q0, WINDOW), c0:c0 + LANES]
                outs = []
                for par in range(2):
                    head = 4 * j + 2 * r + par
                    qm = jnp.where(lo if par == 0 else ~lo, qp, jnp.zeros_like(qp))
                    s = lax.dot_general(qm, kd, (((1,), (1,)), ((), ())), preferred_element_type=F32)
                    p = _sink_softmax(s, valid, sinks_ref[head])
                    outs.append(jnp.dot(p, vd, preferred_element_type=F32))
                attn_scr[pl.ds(q0, WINDOW), c0:c0 + LANES] = jnp.where(lo, outs[0], outs[1]).astype(BF16)
        return carry

    lax.fori_loop(0, tq // WINDOW, block, 0)
    a_out = jnp.dot(attn_scr[...], wo_ref[...], preferred_element_type=F32)
    out_ref[...] = ga_ref[...].astype(F32) * a_out


def _attn_prompt(sinks, q, kv, ga, wo, batch, seq, tq):
    nq = seq // tq
    bpt = tq // WINDOW
    row = lambda w: pl.BlockSpec((tq, w), lambda b, i: (b * nq + i, 0))
    prev = pl.BlockSpec((WINDOW, 2 * D_KV),
                        lambda b, i: (jnp.maximum((b * nq + i) * bpt - 1, 0), 0))
    return pl.pallas_call(
        functools.partial(_attn_prompt_kernel, tq=tq),
        grid=(batch, nq),
        in_specs=[pl.BlockSpec(memory_space=pltpu.SMEM), row(D_MODEL), row(2 * D_KV), prev,
                  row(D_MODEL), _const_spec(wo.shape)],
        out_specs=row(D_MODEL),
        out_shape=jax.ShapeDtypeStruct((batch * seq, D_MODEL), F32),
        scratch_shapes=[pltpu.VMEM((N_KV_HEADS, WINDOW + tq, LANES), BF16),
                        pltpu.VMEM((N_KV_HEADS, WINDOW + tq, LANES), BF16),
                        pltpu.VMEM((tq, D_MODEL), BF16)],
        compiler_params=_params("arbitrary", "arbitrary"),
        name="attn_prompt",
    )(sinks, q, kv, kv, ga, wo)


def _attn_sample_kernel(sinks_ref, q_ref, kvn_ref, ck_ref, cv_ref, ga_ref, wo_ref,
                        out_ref, ko_ref, vo_ref, attn_scr, *, bb, tpad, dec_seq):
    nk = WINDOW + tpad
    lo3 = lax.broadcasted_iota(jnp.int32, (bb, tpad, LANES), 2) < HEAD_DIM
    row = lax.broadcasted_iota(jnp.int32, (4 * tpad, nk), 0)
    si = lax.broadcasted_iota(jnp.int32, (4 * tpad, nk), 1)
    tq = row % tpad
    valid = (si > tq) & (si <= tq + WINDOW) & (si < WINDOW + dec_seq)
    hrow = lax.broadcasted_iota(jnp.int32, (4 * tpad, 1), 0) // tpad

    for t in range(2):
        sl = slice(t * LANES, (t + 1) * LANES)
        kk = jnp.concatenate([ck_ref[:, :, sl], kvn_ref[:, :, sl]], axis=1)
        vv = jnp.concatenate([cv_ref[:, :, sl],
                              kvn_ref[:, :, D_KV + t * LANES:D_KV + (t + 1) * LANES]], axis=1)
        ko_ref[:, :, sl] = kk[:, dec_seq:dec_seq + WINDOW, :]
        vo_ref[:, :, sl] = vv[:, dec_seq:dec_seq + WINDOW, :]
        kds = _dup_heads(kk)
        vds = _dup_heads(vv)
        for par_kv in range(2):
            j = 2 * t + par_kv
            kd, vd = kds[par_kv], vds[par_kv]
            parts = []
            for r in range(2):
                c0 = j * 2 * LANES + r * LANES
                qp = q_ref[:, :, c0:c0 + LANES]
                parts += [jnp.where(lo3, qp, 0.0), jnp.where(lo3, 0.0, qp)]
            lhs = jnp.concatenate(parts, axis=1).astype(BF16)
            s = jnp.einsum('bqd,bkd->bqk', lhs, kd, preferred_element_type=F32)
            sink = jnp.zeros((4 * tpad, 1), F32)
            for g in range(4):
                sink = jnp.where(hrow == g, sinks_ref[4 * j + g], sink)
            p = _sink_softmax(s, valid[None], sink[None])
            o = jnp.einsum('bqk,bkd->bqd', p, vd, preferred_element_type=F32)
            for r in range(2):
                c0 = j * 2 * LANES + r * LANES
                o_even = o[:, (2 * r) * tpad:(2 * r + 1) * tpad, :]
                o_odd = o[:, (2 * r + 1) * tpad:(2 * r + 2) * tpad, :]
                attn_scr[:, :, c0:c0 + LANES] = jnp.where(lo3, o_even, o_odd)

    attn = attn_scr[...].reshape(bb * tpad, D_MODEL).astype(BF16)
    a_out = jnp.dot(attn, wo_ref[...], preferred_element_type=F32)
    out_ref[...] = ga_ref[...].astype(F32) * a_out


def _attn_sample(sinks, q3, kvn3, ck, cv, ga2, wo, bb, dec_seq):
    db, tpad, _ = q3.shape
    blk3 = lambda r, w: pl.BlockSpec((bb, r, w), lambda i: (i, 0, 0))
    row = pl.BlockSpec((bb * tpad, D_MODEL), lambda i: (i, 0))
    return pl.pallas_call(
        functools.partial(_attn_sample_kernel, bb=bb, tpad=tpad, dec_seq=dec_seq),
        grid=(db // bb,),
        in_specs=[pl.BlockSpec(memory_space=pltpu.SMEM), blk3(tpad, D_MODEL), blk3(tpad, 2 * D_KV),
                  blk3(WINDOW, D_KV), blk3(WINDOW, D_KV), row, _const_spec(wo.shape)],
        out_specs=[row, blk3(WINDOW, D_KV), blk3(WINDOW, D_KV)],
        out_shape=[jax.ShapeDtypeStruct((db * tpad, D_MODEL), F32),
                   jax.ShapeDtypeStruct((db, WINDOW, D_KV), F32),
                   jax.ShapeDtypeStruct((db, WINDOW, D_KV), F32)],
        scratch_shapes=[pltpu.VMEM((bb, tpad, D_MODEL), F32)],
        compiler_params=_params("arbitrary"),
        name="attn_sample",
    )(sinks, q3, kvn3, ck, cv, ga2, wo)


def _s5_discretize(lam_re, lam_im, log_dt, b_re, b_im):
    dt = jnp.exp(log_dt)[:, None]
    decay = jnp.exp(lam_re * dt)
    ab_re = decay * jnp.cos(lam_im * dt)
    ab_im = decay * jnp.sin(lam_im * dt)
    nr, ni = ab_re - 1.0, ab_im
    den = lam_re * lam_re + lam_im * lam_im
    f_re = ((nr * lam_re + ni * lam_im) / den)[..., None]
    f_im = ((ni * lam_re - nr * lam_im) / den)[..., None]
    return ab_re, ab_im, f_re * b_re - f_im * b_im, f_re * b_im + f_im * b_re


def _pair_cols(a):
    return a.reshape(a.shape[:-2] + (N_PAIRS, LANES))


def _state_cols(re, im):
    return jnp.stack([_pair_cols(re), _pair_cols(im)], axis=-2).reshape(re.shape[:-2] + (STATE_COLS,))


def _state_split(h):
    h = h.reshape(h.shape[:-1] + (N_PAIRS, 2, 2, SSM_STATE))
    unpair = lambda a: a.reshape(a.shape[:-3] + (SSM_GROUPS, SSM_STATE))
    return unpair(h[..., 0, :, :]), unpair(h[..., 1, :, :])


def _shift_lanes(x, n):
    a, b = x[:, :LANES], x[:, LANES:]
    lane = lax.broadcasted_iota(jnp.int32, a.shape, 1)
    if n == 0:
        return x
    if n >= LANES:
        r = n - LANES
        hi = a if r == 0 else jnp.where(lane < r, 0.0, pltpu.roll(a, r, 1))
        return jnp.concatenate([jnp.zeros_like(a), hi], axis=1)
    ra, rb = pltpu.roll(a, n, 1), pltpu.roll(b, n, 1)
    return jnp.concatenate([jnp.where(lane < n, 0.0, ra), jnp.where(lane < n, ra, rb)], axis=1)


def _dot_nt_split(lhs, rhs):
    nt = lambda a, b: lax.dot_general(a, b, (((1,), (1,)), ((), ())), preferred_element_type=F32)
    l_hi, r_hi = lhs.astype(BF16), rhs.astype(BF16)
    l_lo = (lhs - l_hi.astype(F32)).astype(BF16)
    r_lo = (rhs - r_hi.astype(F32)).astype(BF16)
    return nt(l_hi, r_hi) + nt(l_hi, r_lo) + nt(l_lo, r_hi)


def _s5_prep_kernel(a_ref, c_ref, bt_ref, m_ref, ws_ref, wsn_ref, whyt_ref, mk_ref, pw_ref, an_ref, *, n_tok):
    ar, ai = a_ref[0, 0], a_ref[0, 1]
    cr, ci = c_ref[0, 0], c_ref[0, 1]
    br, bi = bt_ref[0, 0], bt_ref[0, 1]
    pr, pi = [jnp.ones_like(ar)], [jnp.zeros_like(ai)]
    for _ in range(CHUNK):
        pr, pi = pr + [pr[-1] * ar - pi[-1] * ai], pi + [pr[-1] * ai + pi[-1] * ar]
    first = lax.broadcasted_iota(jnp.int32, (SSM_CH, LANES), 1) < SSM_STATE
    ca = [(cr * pr[t] - ci * pi[t], -(cr * pi[t] + ci * pr[t])) for t in range(CHUNK + 1)]
    ca_full = jnp.concatenate([jnp.concatenate(ca[t], axis=1) for t in range(CHUNK)], axis=0)
    zero = jnp.zeros((SSM_CH, LANES), F32)
    for e in range(2):
        own = first if e == 0 else ~first
        pick = lambda v: jnp.where(own, v, zero)
        for t in range(CHUNK):
            rows = slice(t * SSM_CH, (t + 1) * SSM_CH)
            whyt_ref[e, rows, :LANES] = pick(ca[t + 1][0]).astype(BF16)
            whyt_ref[e, rows, LANES:] = pick(ca[t + 1][1]).astype(BF16)
            for ref, k in ((ws_ref, CHUNK - 1 - t), (wsn_ref, n_tok - 1 - t)):
                if k >= 0:
                    ref[e, rows, :LANES] = pick(br * pr[k] - bi * pi[k]).astype(BF16)
                    ref[e, rows, LANES:] = pick(br * pi[k] + bi * pr[k]).astype(BF16)
                else:
                    ref[e, rows, :] = jnp.zeros((SSM_CH, PAIR_COLS), BF16)
        k_row = _dot_nt_split(jnp.concatenate([pick(br), pick(bi)], axis=1), ca_full)
        for s in range(CHUNK):
            m_ref[e, s * SSM_CH:(s + 1) * SSM_CH, :] = _shift_lanes(k_row, s * SSM_CH).astype(BF16)
    sub = lax.broadcasted_iota(jnp.int32, (SUBLANES, LANES), 0)
    qr, qi = [pr[CHUNK]], [pi[CHUNK]]
    for _ in range(SUBLANES - 1):
        qr, qi = qr + [qr[-1] * pr[CHUNK] - qi[-1] * pi[CHUNK]], qi + [qr[-1] * pi[CHUNK] + qi[-1] * pr[CHUNK]]
    for part, q in enumerate((qr, qi)):
        rows = jnp.zeros((SUBLANES, LANES), F32)
        for s in range(SUBLANES):
            rows = jnp.where(sub == s, q[s], rows)
        pw_ref[part] = rows
        for k, shift in enumerate((1, 2, 4)):
            mk_ref[k, part] = jnp.where(sub >= shift, q[shift - 1], 0.0)
    an_ref[0] = jnp.broadcast_to(pr[n_tok], (SUBLANES, LANES))
    an_ref[1] = jnp.broadcast_to(pi[n_tok], (SUBLANES, LANES))


def _s5_prep(lam_re, lam_im, log_dt, b_re, b_im, c_re, c_im, n_tok):
    ab_re, ab_im, bb_re, bb_im = _s5_discretize(lam_re, lam_im, log_dt, b_re, b_im)
    pair = lambda v: v.reshape(N_PAIRS, 2, v.shape[1], SSM_STATE).transpose(0, 2, 1, 3).reshape(
        N_PAIRS, v.shape[1], LANES)
    a = jnp.stack([pair(ab_re[:, None, :]), pair(ab_im[:, None, :])], axis=1)
    c = jnp.stack([pair(c_re), pair(c_im)], axis=1)
    bt = jnp.stack([pair(bb_re.transpose(0, 2, 1)), pair(bb_im.transpose(0, 2, 1))], axis=1)
    blk4 = lambda r: pl.BlockSpec((1, 2, r, LANES), lambda q: (q, 0, 0, 0))
    w_spec = pl.BlockSpec((2, GROUP_IO, PAIR_COLS), lambda q: (q, 0, 0))
    w_shape = jax.ShapeDtypeStruct((SSM_GROUPS, GROUP_IO, PAIR_COLS), BF16)
    return pl.pallas_call(
        functools.partial(_s5_prep_kernel, n_tok=n_tok),
        grid=(N_PAIRS,),
        in_specs=[blk4(1), blk4(SSM_CH), blk4(SSM_CH)],
        out_specs=[w_spec, w_spec, w_spec, w_spec,
                   pl.BlockSpec((3, 2, SUBLANES, LANES), lambda q: (0, 0, 0, q)),
                   pl.BlockSpec((2, SUBLANES, LANES), lambda q: (0, 0, q)),
                   pl.BlockSpec((2, SUBLANES, LANES), lambda q: (0, 0, q))],
        out_shape=[w_shape, w_shape, w_shape, w_shape,
                   jax.ShapeDtypeStruct((3, 2, SUBLANES, HALF_COLS), F32),
                   jax.ShapeDtypeStruct((2, SUBLANES, HALF_COLS), F32),
                   jax.ShapeDtypeStruct((2, SUBLANES, HALF_COLS), F32)],
        compiler_params=_params("arbitrary"),
        name="s5_prep",
    )(a, c, bt)


def _group_io(ref, g):
    return ref.at[:, g * GROUP_IO:(g + 1) * GROUP_IO]


def _s5_emit_tile(u_ref, st_scr, m_ref, whyt_ref, dt_ref, g_ref, j):
    for gl in range(GROUPS_PER_TILE):
        g = GROUPS_PER_TILE * j + gl
        q0 = (gl // 2) * PAIR_COLS
        u_g = _group_io(u_ref, g)[...]
        y = (jnp.dot(u_g, m_ref[g], preferred_element_type=F32)
             + lax.dot_general(st_scr[:, q0:q0 + PAIR_COLS].astype(BF16), whyt_ref[g],
                               (((1,), (1,)), ((), ())), preferred_element_type=F32)
             + dt_ref[g] * u_g.astype(F32))
        _group_io(g_ref, g)[...] = jax.nn.gelu(y).astype(BF16)


def _s5_local_states(u_ref, ws_ref, j, q):
    g = GROUPS_PER_TILE * j + 2 * q
    return (jnp.dot(_group_io(u_ref, g)[...], ws_ref[g], preferred_element_type=F32)
            + jnp.dot(_group_io(u_ref, g + 1)[...], ws_ref[g + 1], preferred_element_type=F32))


def _cmul_add(ar, ai, hr, hi, xr, xi):
    return ar * hr - ai * hi + xr, ar * hi + ai * hr + xi


def _s5_chain_kernel(u_ref, m_ref, ws_ref, whyt_ref, mk_ref, pw_ref, dt_ref, h0_ref, g_ref, hT_ref,
                     carry_scr, st_scr, *, rows):
    @pl.when(pl.program_id(1) == 0)
    def _():
        carry_scr[...] = jnp.broadcast_to(h0_ref[0], (SUBLANES, STATE_COLS))

    row0 = lax.broadcasted_iota(jnp.int32, (SUBLANES, LANES), 0) == 0
    last = lambda h: jnp.broadcast_to(h[SUBLANES - 1:, :], (SUBLANES, LANES))

    for j in range(N_LANE_TILES):
        for q in range(PAIRS_PER_TILE):
            st_scr[:, q * PAIR_COLS:(q + 1) * PAIR_COLS] = _s5_local_states(u_ref, ws_ref, j, q)

        def step(r, carry):
            r0 = pl.multiple_of(r * SUBLANES, SUBLANES)
            out = []
            for q in range(PAIRS_PER_TILE):
                re_c = slice(q * PAIR_COLS, q * PAIR_COLS + LANES)
                im_c = slice(q * PAIR_COLS + LANES, (q + 1) * PAIR_COLS)
                hc = slice((j * PAIRS_PER_TILE + q) * LANES, (j * PAIRS_PER_TILE + q + 1) * LANES)
                xr = st_scr[pl.ds(r0, SUBLANES), re_c]
                xi = st_scr[pl.ds(r0, SUBLANES), im_c]
                for k, shift in enumerate((1, 2, 4)):
                    xr, xi = _cmul_add(mk_ref[k, 0, :, hc], mk_ref[k, 1, :, hc],
                                       pltpu.roll(xr, shift, 0), pltpu.roll(xi, shift, 0), xr, xi)
                cr, ci = carry[q]
                hr, hi = _cmul_add(pw_ref[0, :, hc], pw_ref[1, :, hc], cr, ci, xr, xi)
                st_scr[pl.ds(r0, SUBLANES), re_c] = jnp.where(row0, cr, pltpu.roll(hr, 1, 0))
                st_scr[pl.ds(r0, SUBLANES), im_c] = jnp.where(row0, ci, pltpu.roll(hi, 1, 0))
                out.append((last(hr), last(hi)))
            return tuple(out)

        c0 = j * TILE_COLS
        init = tuple((carry_scr[:, c0 + q * PAIR_COLS:c0 + q * PAIR_COLS + LANES],
                      carry_scr[:, c0 + q * PAIR_COLS + LANES:c0 + (q + 1) * PAIR_COLS])
                     for q in range(PAIRS_PER_TILE))
        final = lax.fori_loop(0, rows // SUBLANES, step, init)
        for q in range(PAIRS_PER_TILE):
            carry_scr[:, c0 + q * PAIR_COLS:c0 + q * PAIR_COLS + LANES] = final[q][0]
            carry_scr[:, c0 + q * PAIR_COLS + LANES:c0 + (q + 1) * PAIR_COLS] = final[q][1]

        _s5_emit_tile(u_ref, st_scr, m_ref, whyt_ref, dt_ref, g_ref, j)

    hT_ref[0] = carry_scr[...]


def _s5_chain(u16, m, ws, why, mk, pw, d_tiled, h0, batch, n_rows, rows):
    nblk = n_rows // rows
    row = pl.BlockSpec((rows, CHUNK_LANES), lambda b, i: (b * nblk + i, 0))
    return pl.pallas_call(
        functools.partial(_s5_chain_kernel, rows=rows),
        grid=(batch, nblk),
        in_specs=[row, _const_spec(m.shape), _const_spec(ws.shape), _const_spec(why.shape),
                  _const_spec(mk.shape), _const_spec(pw.shape), _const_spec(d_tiled.shape),
                  pl.BlockSpec((1, 1, STATE_COLS), lambda b, i: (b, 0, 0))],
        out_specs=[row, pl.BlockSpec((1, SUBLANES, STATE_COLS), lambda b, i: (b, 0, 0))],
        out_shape=[jax.ShapeDtypeStruct(u16.shape, BF16),
                   jax.ShapeDtypeStruct((batch, SUBLANES, STATE_COLS), F32)],
        scratch_shapes=[pltpu.VMEM((SUBLANES, STATE_COLS), F32), pltpu.VMEM((rows, TILE_COLS), F32)],
        compiler_params=_params("arbitrary", "arbitrary"),
        name="s5_chain",
    )(u16, m, ws, why, mk, pw, d_tiled, h0)


def _s5_rows_kernel(u_ref, m_ref, ws_ref, whyt_ref, an_ref, dt_ref, h0_ref, g_ref, hT_ref,
                    st_scr):
    for j in range(N_LANE_TILES):
        for q in range(PAIRS_PER_TILE):
            c0 = j * TILE_COLS + q * PAIR_COLS
            hc = slice((j * PAIRS_PER_TILE + q) * LANES, (j * PAIRS_PER_TILE + q + 1) * LANES)
            h0 = h0_ref[:, c0:c0 + PAIR_COLS]
            local = _s5_local_states(u_ref, ws_ref, j, q)
            hr, hi = _cmul_add(an_ref[0, :1, hc], an_ref[1, :1, hc], h0[:, :LANES], h0[:, LANES:],
                               local[:, :LANES], local[:, LANES:])
            hT_ref[:, c0:c0 + LANES] = hr
            hT_ref[:, c0 + LANES:c0 + PAIR_COLS] = hi
            st_scr[:, q * PAIR_COLS:(q + 1) * PAIR_COLS] = h0
        _s5_emit_tile(u_ref, st_scr, m_ref, whyt_ref, dt_ref, g_ref, j)


def _s5_rows(u16, m, ws, why, an, d_tiled, h0, rows):
    n = u16.shape[0]
    row = lambda w: pl.BlockSpec((rows, w), lambda i: (i, 0))
    return pl.pallas_call(
        _s5_rows_kernel,
        grid=(n // rows,),
        in_specs=[row(CHUNK_LANES), _const_spec(m.shape), _const_spec(ws.shape), _const_spec(why.shape),
                  _const_spec(an.shape), _const_spec(d_tiled.shape), row(STATE_COLS)],
        out_specs=[row(CHUNK_LANES), row(STATE_COLS)],
        out_shape=[jax.ShapeDtypeStruct(u16.shape, BF16), jax.ShapeDtypeStruct((n, STATE_COLS), F32)],
        scratch_shapes=[pltpu.VMEM((rows, TILE_COLS), F32)],
        compiler_params=_params("arbitrary"),
        name="s5_rows",
    )(u16, m, ws, why, an, d_tiled, h0)


def _post_kernel(g_ref, ma_ref, gs_ref, x_ref, permt_ref, wglu_ref, wout_ref, gffn_ref, wup_ref, wdown_ref,
                 gfin_ref, out_ref, *, chunked_g):
    g = _load_chunk_rows(g_ref, permt_ref) if chunked_g else g_ref[...]
    glu = jnp.dot(g, wglu_ref[...], preferred_element_type=F32)
    s_out = glu[:, :D_MODEL] * jax.nn.sigmoid(glu[:, D_MODEL:])
    merged = ma_ref[...] + gs_ref[...].astype(F32) * s_out
    x1 = x_ref[...] + jnp.dot(merged.astype(BF16), wout_ref[...], preferred_element_type=F32)
    h2 = _rmsnorm(x1, gffn_ref[...]).astype(BF16)
    up = jnp.dot(h2, wup_ref[...], preferred_element_type=F32)
    act = jnp.square(jnp.maximum(up, 0.0)).astype(BF16)
    x2 = x1 + jnp.dot(act, wdown_ref[...], preferred_element_type=F32)
    out_ref[...] = _rmsnorm(x2, gfin_ref[...])


def _post(g, ma, gs, x, permt, w_glu, w_out, g_ffn, w_up, w_down, g_final, tm, chunked_g):
    n = x.shape[0]
    row = pl.BlockSpec((tm, D_MODEL), lambda i: (i, 0))
    g_spec = pl.BlockSpec((tm // CHUNK, CHUNK_LANES), lambda i: (i, 0)) if chunked_g else row
    return pl.pallas_call(
        functools.partial(_post_kernel, chunked_g=chunked_g),
        grid=(n // tm,),
        in_specs=[g_spec, row, row, row, _const_spec(permt.shape), _const_spec(w_glu.shape),
                  _const_spec(w_out.shape), _const_spec((1, D_MODEL)), _const_spec(w_up.shape),
                  _const_spec(w_down.shape), _const_spec((1, D_MODEL))],
        out_specs=row,
        out_shape=jax.ShapeDtypeStruct((n, D_MODEL), F32),
        compiler_params=_params("arbitrary"),
        name="post",
    )(g, ma, gs, x, permt, w_glu, w_out, g_ffn, w_up, w_down, g_final)


def _tile(n, pref):
    t = pref
    while n % t:
        t //= 2
    return t


def kernel(x_prompt, x_sample, cache_k, cache_v, state_ssm_re, state_ssm_im, g_mix, w_in, attn_sinks,
           w_attn_o, ssm_lambda_re, ssm_lambda_im, ssm_log_dt, ssm_b_re, ssm_b_im, ssm_c_re, ssm_c_im,
           ssm_d, w_glu, w_out, g_ffn, w_up, w_down, g_final):
    batch, seq, _ = x_prompt.shape
    db, dec_seq, _ = x_sample.shape
    assert w_in.shape[0] == 1, "one layer"
    assert seq % PERM_ROWS == 0 and dec_seq <= SUBLANES and db % SUBLANES == 0

    vec = lambda v: v.reshape(1, D_MODEL).astype(F32)
    w_in_b, wo_b = w_in[0].astype(BF16), w_attn_o[0].astype(BF16)
    w_glu_b, w_out_b = w_glu[0].astype(BF16), w_out[0].astype(BF16)
    w_up_b, w_down_b = w_up[0].astype(BF16), w_down[0].astype(BF16)
    sinks = attn_sinks[0].astype(F32)
    s5p = (ssm_lambda_re[0], ssm_lambda_im[0], ssm_log_dt[0], ssm_b_re[0], ssm_b_im[0], ssm_c_re[0], ssm_c_im[0])
    d_skip = jnp.tile(ssm_d[0].astype(F32).reshape(SSM_GROUPS, 1, SSM_CH), (1, 1, CHUNK))
    perm = _chunk_perm()
    post_w = (w_glu_b, w_out_b, vec(g_ffn[0]), w_up_b, w_down_b, vec(g_final))

    xp = x_prompt.reshape(batch * seq, D_MODEL)
    q, kv, u16, ga, gs = _proj(xp, vec(g_mix[0]), w_in_b, perm, _tile(batch * seq, 512), True)
    ma = _attn_prompt(sinks, q, kv, ga, wo_b, batch, seq, _tile(seq, 512))
    m, ws, ws_n, whyt, mk, pw, an = _s5_prep(*s5p, dec_seq)
    n_rows = seq // CHUNK
    g16, hT = _s5_chain(u16, m, ws, whyt, mk, pw, d_skip, jnp.zeros((batch, 1, STATE_COLS), F32),
                        batch, n_rows, _tile(n_rows, 128))
    y_prompt = _post(g16, ma, gs, xp, perm.T, *post_w, PERM_ROWS, True).reshape(batch, seq, D_MODEL)
    kv_last = kv.reshape(batch, seq, 2 * D_KV)[:, seq - WINDOW:]
    k_prompt = kv_last[..., :D_KV].reshape(1, batch, WINDOW, N_KV_HEADS, HEAD_DIM)
    v_prompt = kv_last[..., D_KV:].reshape(1, batch, WINDOW, N_KV_HEADS, HEAD_DIM)
    hr, hi = _state_split(hT[:, 0])
    ssm_re_prompt, ssm_im_prompt = hr[None], hi[None]

    ns = db * dec_seq
    xs = x_sample.reshape(ns, D_MODEL)
    q, kv, u, ga, gs = _proj(xs, vec(g_mix[0]), w_in_b, perm, _tile(ns, 512), False)
    tpad = SUBLANES
    pad3 = lambda v: jnp.pad(v.reshape(db, dec_seq, -1).astype(F32), ((0, 0), (0, tpad - dec_seq), (0, 0)))
    ga_pad = pad3(ga).astype(BF16).reshape(db * tpad, D_MODEL)
    ma_pad, k_new, v_new = _attn_sample(
        sinks, pad3(q), pad3(kv), cache_k[0].reshape(db, WINDOW, D_KV), cache_v[0].reshape(db, WINDOW, D_KV),
        ga_pad, wo_b, _tile(db, 16), dec_seq)
    ma = ma_pad.reshape(db, tpad, D_MODEL)[:, :dec_seq].reshape(ns, D_MODEL)
    u16 = u.reshape(db, dec_seq, SSM_GROUPS, SSM_CH).transpose(0, 2, 1, 3)
    u16 = jnp.pad(u16, ((0, 0), (0, 0), (0, CHUNK - dec_seq), (0, 0))).reshape(db, CHUNK_LANES)
    h0 = _state_cols(state_ssm_re[0].astype(F32), state_ssm_im[0].astype(F32))
    g16, hT = _s5_rows(u16, m, ws_n, whyt, an, d_skip, h0, _tile(db, 64))
    g_act = g16.reshape(db, SSM_GROUPS, CHUNK, SSM_CH)[:, :, :dec_seq].transpose(0, 2, 1, 3).reshape(ns, D_MODEL)
    y_sample = _post(g_act, ma, gs, xs, perm.T, *post_w, _tile(ns, 256), False).reshape(db, dec_seq, D_MODEL)
    k_sample = k_new.reshape(1, db, WINDOW, N_KV_HEADS, HEAD_DIM)
    v_sample = v_new.reshape(1, db, WINDOW, N_KV_HEADS, HEAD_DIM)
    hr, hi = _state_split(hT)
    ssm_re_sample, ssm_im_sample = hr[None], hi[None]

    return (y_prompt, y_sample, k_prompt, v_prompt, ssm_re_prompt, ssm_im_prompt,
            k_sample, v_sample, ssm_re_sample, ssm_im_sample)
```

```python
import functools

import jax
import jax.numpy as jnp
from jax import lax
from jax.experimental import pallas as pl
from jax.experimental.pallas import tpu as pltpu

F32 = jnp.float32
BF16 = jnp.bfloat16

D_MODEL = 1024
HEAD_DIM = 64
N_HEADS = 16
N_KV_HEADS = 4
WINDOW = 128
D_KV = N_KV_HEADS * HEAD_DIM
SSM_CH = 16
SSM_GROUPS = 64
SSM_STATE = 64
RMS_EPS = 1e-5
LOG2E = 1.4426950408889634
Q_SCALE = HEAD_DIM ** -0.5 * LOG2E

LANES = 128
SUBLANES = 8
BF16_ROWS = 16
N_LANE_TILES = D_MODEL // LANES
GROUPS_PER_TILE = LANES // SSM_CH
CHUNK = 16
CHUNK_LANES = CHUNK * D_MODEL
GROUP_IO = CHUNK * SSM_CH
N_PAIRS = SSM_GROUPS // 2
PAIRS_PER_TILE = GROUPS_PER_TILE // 2
PAIR_COLS = 2 * LANES
TILE_COLS = PAIRS_PER_TILE * PAIR_COLS
STATE_COLS = N_PAIRS * PAIR_COLS
HALF_COLS = STATE_COLS // 2
PERM_ROWS = CHUNK * BF16_ROWS
VMEM_LIMIT = 56 * 1024 * 1024

_Q0, _KV0, _U0, _GA0, _GS0, _END = 0, 1024, 1536, 2560, 3584, 4608


def _rmsnorm(x, g):
    return x * lax.rsqrt(jnp.mean(x * x, axis=-1, keepdims=True) + RMS_EPS) * g


def _params(*sem):
    return pltpu.CompilerParams(dimension_semantics=sem, vmem_limit_bytes=VMEM_LIMIT)


def _const_spec(shape):
    nd = len(shape)
    return pl.BlockSpec(shape, lambda *_: (0,) * nd, pipeline_mode=pl.Buffered(1))


def _chunk_perm():
    r = jnp.arange(PERM_ROWS)
    tok = (r % BF16_ROWS) * CHUNK + r // BF16_ROWS
    return (tok[:, None] == jnp.arange(PERM_ROWS)[None, :]).astype(BF16)


def _piece_transpose(cols, masks):
    for d, msk in zip((4, 2, 1), masks):
        new = list(cols)
        for v in range(GROUPS_PER_TILE):
            if v & d == 0:
                a, b = cols[v], cols[v + d]
                new[v] = jnp.where(msk, pltpu.roll(b, SSM_CH * d, 1), a)
                new[v + d] = jnp.where(msk, b, pltpu.roll(a, LANES - SSM_CH * d, 1))
        cols = new
    return cols


def _piece_masks():
    piece = lax.broadcasted_iota(jnp.int32, (BF16_ROWS, LANES), 1) // SSM_CH
    return [(piece & d) != 0 for d in (4, 2, 1)]


def _store_chunk_rows(x, perm_ref, out_ref):
    masks = _piece_masks()
    for hb in range(x.shape[0] // PERM_ROWS):
        xp = jnp.dot(perm_ref[...], x[hb * PERM_ROWS:(hb + 1) * PERM_ROWS],
                     preferred_element_type=F32)
        for j in range(N_LANE_TILES):
            for hf in range(2):
                cols = [xp[(8 * hf + k) * BF16_ROWS:(8 * hf + k + 1) * BF16_ROWS, j * LANES:(j + 1) * LANES]
                        for k in range(8)]
                for gl, col in enumerate(_piece_transpose(cols, masks)):
                    c0 = (GROUPS_PER_TILE * j + gl) * GROUP_IO + hf * LANES
                    out_ref[hb * BF16_ROWS:(hb + 1) * BF16_ROWS, c0:c0 + LANES] = col.astype(BF16)


def _load_chunk_rows(in_ref, permt_ref):
    masks = _piece_masks()
    blocks = []
    for hb in range(in_ref.shape[0] // BF16_ROWS):
        tiles = [[None] * N_LANE_TILES for _ in range(CHUNK)]
        for j in range(N_LANE_TILES):
            for hf in range(2):
                cols = []
                for gl in range(GROUPS_PER_TILE):
                    c0 = (GROUPS_PER_TILE * j + gl) * GROUP_IO + hf * LANES
                    cols.append(in_ref[hb * BF16_ROWS:(hb + 1) * BF16_ROWS, c0:c0 + LANES].astype(F32))
                for k, col in enumerate(_piece_transpose(cols, masks)):
                    tiles[8 * hf + k][j] = col.astype(BF16)
        xp = jnp.concatenate([jnp.concatenate(row, axis=1) for row in tiles], axis=0)
        blocks.append(jnp.dot(permt_ref[...], xp, preferred_element_type=F32).astype(BF16))
    return blocks[0] if len(blocks) == 1 else jnp.concatenate(blocks, axis=0)


def _proj_kernel(x_ref, g_ref, w_ref, perm_ref, q_ref, kv_ref, u_ref, ga_ref, gs_ref, *, chunked_u):
    h = _rmsnorm(x_ref[...], g_ref[...]).astype(BF16)

    def seg(lo, hi):
        return jnp.dot(h, w_ref[:, lo:hi], preferred_element_type=F32)

    q_ref[...] = (seg(_Q0, _KV0) * Q_SCALE).astype(BF16)
    kv_ref[...] = seg(_KV0, _U0)
    u = seg(_U0, _GA0).astype(BF16)
    if chunked_u:
        _store_chunk_rows(u, perm_ref, u_ref)
    else:
        u_ref[...] = u
    ga_ref[...] = jax.nn.sigmoid(seg(_GA0, _GS0)).astype(BF16)
    gs_ref[...] = jax.nn.sigmoid(seg(_GS0, _END)).astype(BF16)


def _proj(x, g_mix, w_in, perm, tm, chunked_u):
    n = x.shape[0]
    row = lambda w: pl.BlockSpec((tm, w), lambda i: (i, 0))
    u_spec = pl.BlockSpec((tm // CHUNK, CHUNK_LANES), lambda i: (i, 0)) if chunked_u else row(D_MODEL)
    u_shape = (n // CHUNK, CHUNK_LANES) if chunked_u else (n, D_MODEL)
    return pl.pallas_call(
        functools.partial(_proj_kernel, chunked_u=chunked_u),
        grid=(n // tm,),
        in_specs=[row(D_MODEL), _const_spec((1, D_MODEL)), _const_spec(w_in.shape), _const_spec(perm.shape)],
        out_specs=[row(D_MODEL), row(2 * D_KV), u_spec, row(D_MODEL), row(D_MODEL)],
        out_shape=[jax.ShapeDtypeStruct((n, D_MODEL), BF16),
                   jax.ShapeDtypeStruct((n, 2 * D_KV), F32),
                   jax.ShapeDtypeStruct(u_shape, BF16),
                   jax.ShapeDtypeStruct((n, D_MODEL), BF16),
                   jax.ShapeDtypeStruct((n, D_MODEL), BF16)],
        compiler_params=_params("arbitrary"),
        name="proj",
    )(x, g_mix, w_in, perm)


def _dup_heads(tile):
    lo = lax.broadcasted_iota(jnp.int32, tile.shape, tile.ndim - 1) < HEAD_DIM
    rolled = pltpu.roll(tile, HEAD_DIM, tile.ndim - 1)
    return (jnp.where(lo, tile, rolled).astype(BF16), jnp.where(lo, rolled, tile).astype(BF16))


def _sink_softmax(s, sink):
    sink2 = sink * LOG2E
    m = jnp.maximum(jnp.max(s, axis=-1, keepdims=True), sink2)
    p = jnp.exp2(s - m)
    denom = jnp.sum(p, axis=-1, keepdims=True) + jnp.exp2(sink2 - m)
    return p.astype(BF16), 1.0 / denom


def _attn_prompt_kernel(sinks_ref, q_ref, kvc_ref, kvp_ref, ga_ref, wo_ref, out_ref,
                        kd_scr, vd_scr, bias_scr, attn_scr, *, tq):
    i = pl.program_id(1)
    kv_full = jnp.concatenate([kvp_ref[...], kvc_ref[...]], axis=0)
    for t in range(2):
        ke, ko = _dup_heads(kv_full[:, t * LANES:(t + 1) * LANES])
        ve, vo = _dup_heads(kv_full[:, D_KV + t * LANES:D_KV + (t + 1) * LANES])
        kd_scr[2 * t], kd_scr[2 * t + 1] = ke, ko
        vd_scr[2 * t], vd_scr[2 * t + 1] = ve, vo

    qi = lax.broadcasted_iota(jnp.int32, (WINDOW, 2 * WINDOW), 0)
    si = lax.broadcasted_iota(jnp.int32, (WINDOW, 2 * WINDOW), 1)
    band = (si > qi) & (si <= qi + WINDOW)
    in_block = si >= WINDOW
    lo = lax.broadcasted_iota(jnp.int32, (WINDOW, LANES), 1) < HEAD_DIM

    def block(jb, carry):
        q0 = pl.multiple_of(jb * WINDOW, WINDOW)
        has_prev = (i > 0) | (jb > 0)
        bias_scr[...] = jnp.where(band & (in_block | has_prev), 0.0, -jnp.inf)

        def head_pair(hp, c):
            j = hp // 2
            c0 = pl.multiple_of(hp * LANES, LANES)
            kd = kd_scr[j, pl.ds(q0, 2 * WINDOW), :]
            vd = vd_scr[j, pl.ds(q0, 2 * WINDOW), :]
            qp = q_ref[pl.ds(q0, WINDOW), pl.ds(c0, LANES)]
            outs = []
            for par in range(2):
                qm = jnp.where(lo if par == 0 else ~lo, qp, jnp.zeros_like(qp))
                s = lax.dot_general(qm, kd, (((1,), (1,)), ((), ())), preferred_element_type=F32)
                p, inv = _sink_softmax(s + bias_scr[...], sinks_ref[2 * hp + par])
                outs.append(jnp.dot(p, vd, preferred_element_type=F32) * inv)
            attn_scr[pl.ds(q0, WINDOW), pl.ds(c0, LANES)] = jnp.where(lo, outs[0], outs[1]).astype(BF16)
            return c

        lax.fori_loop(0, N_HEADS // 2, head_pair, 0, unroll=8)
        return carry

    lax.fori_loop(0, tq // WINDOW, block, 0)
    a_out = jnp.dot(attn_scr[...], wo_ref[...], preferred_element_type=F32)
    out_ref[...] = ga_ref[...].astype(F32) * a_out


def _attn_prompt(sinks, q, kv, ga, wo, batch, seq, tq):
    nq = seq // tq
    bpt = tq // WINDOW
    row = lambda w: pl.BlockSpec((tq, w), lambda b, i: (b * nq + i, 0))
    prev = pl.BlockSpec((WINDOW, 2 * D_KV),
                        lambda b, i: (jnp.maximum((b * nq + i) * bpt - 1, 0), 0))
    return pl.pallas_call(
        functools.partial(_attn_prompt_kernel, tq=tq),
        grid=(batch, nq),
        in_specs=[pl.BlockSpec(memory_space=pltpu.SMEM), row(D_MODEL), row(2 * D_KV), prev,
                  row(D_MODEL), _const_spec(wo.shape)],
        out_specs=row(D_MODEL),
        out_shape=jax.ShapeDtypeStruct((batch * seq, D_MODEL), F32),
        scratch_shapes=[pltpu.VMEM((N_KV_HEADS, WINDOW + tq, LANES), BF16),
                        pltpu.VMEM((N_KV_HEADS, WINDOW + tq, LANES), BF16),
                        pltpu.VMEM((WINDOW, 2 * WINDOW), F32),
                        pltpu.VMEM((tq, D_MODEL), BF16)],
        compiler_params=_params("arbitrary", "arbitrary"),
        name="attn_prompt",
    )(sinks, q, kv, kv, ga, wo)


def _attn_sample_kernel(sinks_ref, q_ref, kvn_ref, ck_ref, cv_ref, ga_ref, wo_ref,
                        out_ref, ko_ref, vo_ref, attn_scr, *, bb, tpad, dec_seq):
    nk = WINDOW + tpad
    lo3 = lax.broadcasted_iota(jnp.int32, (bb, tpad, LANES), 2) < HEAD_DIM
    row = lax.broadcasted_iota(jnp.int32, (4 * tpad, nk), 0)
    si = lax.broadcasted_iota(jnp.int32, (4 * tpad, nk), 1)
    tq = row % tpad
    valid = (si > tq) & (si <= tq + WINDOW) & (si < WINDOW + dec_seq)
    hrow = lax.broadcasted_iota(jnp.int32, (4 * tpad, 1), 0) // tpad

    for t in range(2):
        sl = slice(t * LANES, (t + 1) * LANES)
        kk = jnp.concatenate([ck_ref[:, :, sl], kvn_ref[:, :, sl]], axis=1)
        vv = jnp.concatenate([cv_ref[:, :, sl],
                              kvn_ref[:, :, D_KV + t * LANES:D_KV + (t + 1) * LANES]], axis=1)
        ko_ref[:, :, sl] = kk[:, dec_seq:dec_seq + WINDOW, :]
        vo_ref[:, :, sl] = vv[:, dec_seq:dec_seq + WINDOW, :]
        kds = _dup_heads(kk)
        vds = _dup_heads(vv)
        for par_kv in range(2):
            j = 2 * t + par_kv
            kd, vd = kds[par_kv], vds[par_kv]
            parts = []
            for r in range(2):
                c0 = j * 2 * LANES + r * LANES
                qp = q_ref[:, :, c0:c0 + LANES]
                parts += [jnp.where(lo3, qp, 0.0), jnp.where(lo3, 0.0, qp)]
            lhs = jnp.concatenate(parts, axis=1).astype(BF16)
            s = jnp.einsum('bqd,bkd->bqk', lhs, kd, preferred_element_type=F32)
            sink = jnp.zeros((4 * tpad, 1), F32)
            for g in range(4):
                sink = jnp.where(hrow == g, sinks_ref[4 * j + g], sink)
            p, inv = _sink_softmax(jnp.where(valid[None], s, -jnp.inf), sink[None])
            o = jnp.einsum('bqk,bkd->bqd', p, vd, preferred_element_type=F32) * inv
            for r in range(2):
                c0 = j * 2 * LANES + r * LANES
                o_even = o[:, (2 * r) * tpad:(2 * r + 1) * tpad, :]
                o_odd = o[:, (2 * r + 1) * tpad:(2 * r + 2) * tpad, :]
                attn_scr[:, :, c0:c0 + LANES] = jnp.where(lo3, o_even, o_odd)

    attn = attn_scr[...].reshape(bb * tpad, D_MODEL).astype(BF16)
    a_out = jnp.dot(attn, wo_ref[...], preferred_element_type=F32)
    out_ref[...] = ga_ref[...].astype(F32) * a_out


def _attn_sample(sinks, q3, kvn3, ck, cv, ga2, wo, bb, dec_seq):
    db, tpad, _ = q3.shape
    blk3 = lambda r, w: pl.BlockSpec((bb, r, w), lambda i: (i, 0, 0))
    row = pl.BlockSpec((bb * tpad, D_MODEL), lambda i: (i, 0))
    return pl.pallas_call(
        functools.partial(_attn_sample_kernel, bb=bb, tpad=tpad, dec_seq=dec_seq),
        grid=(db // bb,),
        in_specs=[pl.BlockSpec(memory_space=pltpu.SMEM), blk3(tpad, D_MODEL), blk3(tpad, 2 * D_KV),
                  blk3(WINDOW, D_KV), blk3(WINDOW, D_KV), row, _const_spec(wo.shape)],
        out_specs=[row, blk3(WINDOW, D_KV), blk3(WINDOW, D_KV)],
        out_shape=[jax.ShapeDtypeStruct((db * tpad, D_MODEL), F32),
                   jax.ShapeDtypeStruct((db, WINDOW, D_KV), F32),
                   jax.ShapeDtypeStruct((db, WINDOW, D_KV), F32)],
        scratch_shapes=[pltpu.VMEM((bb, tpad, D_MODEL), F32)],
        compiler_params=_params("arbitrary"),
        name="attn_sample",
    )(sinks, q3, kvn3, ck, cv, ga2, wo)


def _s5_discretize(lam_re, lam_im, log_dt, b_re, b_im):
    dt = jnp.exp(log_dt)[:, None]
    decay = jnp.exp(lam_re * dt)
    ab_re = decay * jnp.cos(lam_im * dt)
    ab_im = decay * jnp.sin(lam_im * dt)
    nr, ni = ab_re - 1.0, ab_im
    den = lam_re * lam_re + lam_im * lam_im
    f_re = ((nr * lam_re + ni * lam_im) / den)[..., None]
    f_im = ((ni * lam_re - nr * lam_im) / den)[..., None]
    return ab_re, ab_im, f_re * b_re - f_im * b_im, f_re * b_im + f_im * b_re


def _pair_cols(a):
    return a.reshape(a.shape[:-2] + (N_PAIRS, LANES))


def _state_cols(re, im):
    return jnp.stack([_pair_cols(re), _pair_cols(im)], axis=-2).reshape(re.shape[:-2] + (STATE_COLS,))


def _state_split(h):
    h = h.reshape(h.shape[:-1] + (N_PAIRS, 2, 2, SSM_STATE))
    unpair = lambda a: a.reshape(a.shape[:-3] + (SSM_GROUPS, SSM_STATE))
    return unpair(h[..., 0, :, :]), unpair(h[..., 1, :, :])


def _shift_lanes(x, n):
    a, b = x[:, :LANES], x[:, LANES:]
    lane = lax.broadcasted_iota(jnp.int32, a.shape, 1)
    if n == 0:
        return x
    if n >= LANES:
        r = n - LANES
        hi = a if r == 0 else jnp.where(lane < r, 0.0, pltpu.roll(a, r, 1))
        return jnp.concatenate([jnp.zeros_like(a), hi], axis=1)
    ra, rb = pltpu.roll(a, n, 1), pltpu.roll(b, n, 1)
    return jnp.concatenate([jnp.where(lane < n, 0.0, ra), jnp.where(lane < n, ra, rb)], axis=1)


def _dot_nt_split(lhs, rhs):
    nt = lambda a, b: lax.dot_general(a, b, (((1,), (1,)), ((), ())), preferred_element_type=F32)
    l_hi, r_hi = lhs.astype(BF16), rhs.astype(BF16)
    l_lo = (lhs - l_hi.astype(F32)).astype(BF16)
    r_lo = (rhs - r_hi.astype(F32)).astype(BF16)
    return nt(l_hi, r_hi) + nt(l_hi, r_lo) + nt(l_lo, r_hi)


def _s5_prep_kernel(a_ref, c_ref, bt_ref, m_ref, ws_ref, wsn_ref, whyt_ref, mk_ref, pw_ref, an_ref, *, n_tok):
    ar, ai = a_ref[0, 0], a_ref[0, 1]
    cr, ci = c_ref[0, 0], c_ref[0, 1]
    br, bi = bt_ref[0, 0], bt_ref[0, 1]
    pr, pi = [jnp.ones_like(ar)], [jnp.zeros_like(ai)]
    for _ in range(CHUNK):
        pr, pi = pr + [pr[-1] * ar - pi[-1] * ai], pi + [pr[-1] * ai + pi[-1] * ar]
    first = lax.broadcasted_iota(jnp.int32, (SSM_CH, LANES), 1) < SSM_STATE
    ca = [(cr * pr[t] - ci * pi[t], -(cr * pi[t] + ci * pr[t])) for t in range(CHUNK + 1)]
    ca_full = jnp.concatenate([jnp.concatenate(ca[t], axis=1) for t in range(CHUNK)], axis=0)
    zero = jnp.zeros((SSM_CH, LANES), F32)
    for e in range(2):
        own = first if e == 0 else ~first
        pick = lambda v: jnp.where(own, v, zero)
        for t in range(CHUNK):
            rows = slice(t * SSM_CH, (t + 1) * SSM_CH)
            whyt_ref[e, rows, :LANES] = pick(ca[t + 1][0]).astype(BF16)
            whyt_ref[e, rows, LANES:] = pick(ca[t + 1][1]).astype(BF16)
            for ref, k in ((ws_ref, CHUNK - 1 - t), (wsn_ref, n_tok - 1 - t)):
                if k >= 0:
                    ref[e, rows, :LANES] = pick(br * pr[k] - bi * pi[k]).astype(BF16)
                    ref[e, rows, LANES:] = pick(br * pi[k] + bi * pr[k]).astype(BF16)
                else:
                    ref[e, rows, :] = jnp.zeros((SSM_CH, PAIR_COLS), BF16)
        k_row = _dot_nt_split(jnp.concatenate([pick(br), pick(bi)], axis=1), ca_full)
        for s in range(CHUNK):
            m_ref[e, s * SSM_CH:(s + 1) * SSM_CH, :] = _shift_lanes(k_row, s * SSM_CH).astype(BF16)
    sub = lax.broadcasted_iota(jnp.int32, (SUBLANES, LANES), 0)
    qr, qi = [pr[CHUNK]], [pi[CHUNK]]
    for _ in range(SUBLANES - 1):
        qr, qi = qr + [qr[-1] * pr[CHUNK] - qi[-1] * pi[CHUNK]], qi + [qr[-1] * pi[CHUNK] + qi[-1] * pr[CHUNK]]
    for part, q in enumerate((qr, qi)):
        rows = jnp.zeros((SUBLANES, LANES), F32)
        for s in range(SUBLANES):
            rows = jnp.where(sub == s, q[s], rows)
        pw_ref[part] = rows
        for k, shift in enumerate((1, 2, 4)):
            mk_ref[k, part] = jnp.where(sub >= shift, q[shift - 1], 0.0)
    an_ref[0] = jnp.broadcast_to(pr[n_tok], (SUBLANES, LANES))
    an_ref[1] = jnp.broadcast_to(pi[n_tok], (SUBLANES, LANES))


def _s5_prep(lam_re, lam_im, log_dt, b_re, b_im, c_re, c_im, n_tok):
    ab_re, ab_im, bb_re, bb_im = _s5_discretize(lam_re, lam_im, log_dt, b_re, b_im)
    pair = lambda v: v.reshape(N_PAIRS, 2, v.shape[1], SSM_STATE).transpose(0, 2, 1, 3).reshape(
        N_PAIRS, v.shape[1], LANES)
    a = jnp.stack([pair(ab_re[:, None, :]), pair(ab_im[:, None, :])], axis=1)
    c = jnp.stack([pair(c_re), pair(c_im)], axis=1)
    bt = jnp.stack([pair(bb_re.transpose(0, 2, 1)), pair(bb_im.transpose(0, 2, 1))], axis=1)
    blk4 = lambda r: pl.BlockSpec((1, 2, r, LANES), lambda q: (q, 0, 0, 0))
    w_spec = pl.BlockSpec((2, GROUP_IO, PAIR_COLS), lambda q: (q, 0, 0))
    w_shape = jax.ShapeDtypeStruct((SSM_GROUPS, GROUP_IO, PAIR_COLS), BF16)
    return pl.pallas_call(
        functools.partial(_s5_prep_kernel, n_tok=n_tok),
        grid=(N_PAIRS,),
        in_specs=[blk4(1), blk4(SSM_CH), blk4(SSM_CH)],
        out_specs=[w_spec, w_spec, w_spec, w_spec,
                   pl.BlockSpec((3, 2, SUBLANES, LANES), lambda q: (0, 0, 0, q)),
                   pl.BlockSpec((2, SUBLANES, LANES), lambda q: (0, 0, q)),
                   pl.BlockSpec((2, SUBLANES, LANES), lambda q: (0, 0, q))],
        out_shape=[w_shape, w_shape, w_shape, w_shape,
                   jax.ShapeDtypeStruct((3, 2, SUBLANES, HALF_COLS), F32),
                   jax.ShapeDtypeStruct((2, SUBLANES, HALF_COLS), F32),
                   jax.ShapeDtypeStruct((2, SUBLANES, HALF_COLS), F32)],
        compiler_params=_params("arbitrary"),
        name="s5_prep",
    )(a, c, bt)


def _group_io(ref, g):
    return ref.at[:, g * GROUP_IO:(g + 1) * GROUP_IO]


def _s5_emit_tile(u_ref, st_scr, m_ref, whyt_ref, dt_ref, g_ref, j):
    for gl in range(GROUPS_PER_TILE):
        g = GROUPS_PER_TILE * j + gl
        q0 = (gl // 2) * PAIR_COLS
        u_g = _group_io(u_ref, g)[...]
        y = (jnp.dot(u_g, m_ref[g], preferred_element_type=F32)
             + lax.dot_general(st_scr[:, q0:q0 + PAIR_COLS].astype(BF16), whyt_ref[g],
                               (((1,), (1,)), ((), ())), preferred_element_type=F32)
             + dt_ref[g] * u_g.astype(F32))
        _group_io(g_ref, g)[...] = jax.nn.gelu(y).astype(BF16)


def _s5_local_states(u_ref, ws_ref, j, q):
    g = GROUPS_PER_TILE * j + 2 * q
    return (jnp.dot(_group_io(u_ref, g)[...], ws_ref[g], preferred_element_type=F32)
            + jnp.dot(_group_io(u_ref, g + 1)[...], ws_ref[g + 1], preferred_element_type=F32))


def _cmul_add(ar, ai, hr, hi, xr, xi):
    return ar * hr - ai * hi + xr, ar * hi + ai * hr + xi


def _s5_chain_kernel(u_ref, m_ref, ws_ref, whyt_ref, mk_ref, pw_ref, dt_ref, h0_ref, g_ref, hT_ref,
                     carry_scr, st_scr, *, rows):
    @pl.when(pl.program_id(1) == 0)
    def _():
        carry_scr[...] = jnp.broadcast_to(h0_ref[0], (SUBLANES, STATE_COLS))

    row0 = lax.broadcasted_iota(jnp.int32, (SUBLANES, LANES), 0) == 0
    last = lambda h: jnp.broadcast_to(h[SUBLANES - 1:, :], (SUBLANES, LANES))

    for j in range(N_LANE_TILES):
        for q in range(PAIRS_PER_TILE):
            st_scr[:, q * PAIR_COLS:(q + 1) * PAIR_COLS] = _s5_local_states(u_ref, ws_ref, j, q)

        def step(r, carry):
            r0 = pl.multiple_of(r * SUBLANES, SUBLANES)
            out = []
            for q in range(PAIRS_PER_TILE):
                re_c = slice(q * PAIR_COLS, q * PAIR_COLS + LANES)
                im_c = slice(q * PAIR_COLS + LANES, (q + 1) * PAIR_COLS)
                hc = slice((j * PAIRS_PER_TILE + q) * LANES, (j * PAIRS_PER_TILE + q + 1) * LANES)
                xr = st_scr[pl.ds(r0, SUBLANES), re_c]
                xi = st_scr[pl.ds(r0, SUBLANES), im_c]
                for k, shift in enumerate((1, 2, 4)):
                    xr, xi = _cmul_add(mk_ref[k, 0, :, hc], mk_ref[k, 1, :, hc],
                                       pltpu.roll(xr, shift, 0), pltpu.roll(xi, shift, 0), xr, xi)
                cr, ci = carry[q]
                hr, hi = _cmul_add(pw_ref[0, :, hc], pw_ref[1, :, hc], cr, ci, xr, xi)
                st_scr[pl.ds(r0, SUBLANES), re_c] = jnp.where(row0, cr, pltpu.roll(hr, 1, 0))
                st_scr[pl.ds(r0, SUBLANES), im_c] = jnp.where(row0, ci, pltpu.roll(hi, 1, 0))
                out.append((last(hr), last(hi)))
            return tuple(out)

        c0 = j * TILE_COLS
        init = tuple((carry_scr[:, c0 + q * PAIR_COLS:c0 + q * PAIR_COLS + LANES],
                      carry_scr[:, c0 + q * PAIR_COLS + LANES:c0 + (q + 1) * PAIR_COLS])
                     for q in range(PAIRS_PER_TILE))
        final = lax.fori_loop(0, rows // SUBLANES, step, init)
        for q in range(PAIRS_PER_TILE):
            carry_scr[:, c0 + q * PAIR_COLS:c0 + q * PAIR_COLS + LANES] = final[q][0]
            carry_scr[:, c0 + q * PAIR_COLS + LANES:c0 + (q + 1) * PAIR_COLS] = final[q][1]

        _s5_emit_tile(u_ref, st_scr, m_ref, whyt_ref, dt_ref, g_ref, j)

    hT_ref[0] = carry_scr[...]


def _s5_chain(u16, m, ws, why, mk, pw, d_tiled, h0, batch, n_rows, rows):
    nblk = n_rows // rows
    row = pl.BlockSpec((rows, CHUNK_LANES), lambda b, i: (b * nblk + i, 0))
    return pl.pallas_call(
        functools.partial(_s5_chain_kernel, rows=rows),
        grid=(batch, nblk),
        in_specs=[row, _const_spec(m.shape), _const_spec(ws.shape), _const_spec(why.shape),
                  _const_spec(mk.shape), _const_spec(pw.shape), _const_spec(d_tiled.shape),
                  pl.BlockSpec((1, 1, STATE_COLS), lambda b, i: (b, 0, 0))],
        out_specs=[row, pl.BlockSpec((1, SUBLANES, STATE_COLS), lambda b, i: (b, 0, 0))],
        out_shape=[jax.ShapeDtypeStruct(u16.shape, BF16),
                   jax.ShapeDtypeStruct((batch, SUBLANES, STATE_COLS), F32)],
        scratch_shapes=[pltpu.VMEM((SUBLANES, STATE_COLS), F32), pltpu.VMEM((rows, TILE_COLS), F32)],
        compiler_params=_params("arbitrary", "arbitrary"),
        name="s5_chain",
    )(u16, m, ws, why, mk, pw, d_tiled, h0)


def _s5_rows_kernel(u_ref, m_ref, ws_ref, whyt_ref, an_ref, dt_ref, h0_ref, g_ref, hT_ref,
                    st_scr):
    for j in range(N_LANE_TILES):
        for q in range(PAIRS_PER_TILE):
            c0 = j * TILE_COLS + q * PAIR_COLS
            hc = slice((j * PAIRS_PER_TILE + q) * LANES, (j * PAIRS_PER_TILE + q + 1) * LANES)
            h0 = h0_ref[:, c0:c0 + PAIR_COLS]
            local = _s5_local_states(u_ref, ws_ref, j, q)
            hr, hi = _cmul_add(an_ref[0, :1, hc], an_ref[1, :1, hc], h0[:, :LANES], h0[:, LANES:],
                               local[:, :LANES], local[:, LANES:])
            hT_ref[:, c0:c0 + LANES] = hr
            hT_ref[:, c0 + LANES:c0 + PAIR_COLS] = hi
            st_scr[:, q * PAIR_COLS:(q + 1) * PAIR_COLS] = h0
        _s5_emit_tile(u_ref, st_scr, m_ref, whyt_ref, dt_ref, g_ref, j)


def _s5_rows(u16, m, ws, why, an, d_tiled, h0, rows):
    n = u16.shape[0]
    row = lambda w: pl.BlockSpec((rows, w), lambda i: (i, 0))
    return pl.pallas_call(
        _s5_rows_kernel,
        grid=(n // rows,),
        in_specs=[row(CHUNK_LANES), _const_spec(m.shape), _const_spec(ws.shape), _const_spec(why.shape),
                  _const_spec(an.shape), _const_spec(d_tiled.shape), row(STATE_COLS)],
        out_specs=[row(CHUNK_LANES), row(STATE_COLS)],
        out_shape=[jax.ShapeDtypeStruct(u16.shape, BF16), jax.ShapeDtypeStruct((n, STATE_COLS), F32)],
        scratch_shapes=[pltpu.VMEM((rows, TILE_COLS), F32)],
        compiler_params=_params("arbitrary"),
        name="s5_rows",
    )(u16, m, ws, why, an, d_tiled, h0)


def _post_kernel(g_ref, ma_ref, gs_ref, x_ref, permt_ref, wglu_ref, wout_ref, gffn_ref, wup_ref, wdown_ref,
                 gfin_ref, out_ref, *, chunked_g):
    g = _load_chunk_rows(g_ref, permt_ref) if chunked_g else g_ref[...]
    glu = jnp.dot(g, wglu_ref[...], preferred_element_type=F32)
    s_out = glu[:, :D_MODEL] * jax.nn.sigmoid(glu[:, D_MODEL:])
    merged = ma_ref[...] + gs_ref[...].astype(F32) * s_out
    x1 = x_ref[...] + jnp.dot(merged.astype(BF16), wout_ref[...], preferred_element_type=F32)
    h2 = _rmsnorm(x1, gffn_ref[...]).astype(BF16)
    up = jnp.dot(h2, wup_ref[...], preferred_element_type=F32)
    act = jnp.square(jnp.maximum(up, 0.0)).astype(BF16)
    x2 = x1 + jnp.dot(act, wdown_ref[...], preferred_element_type=F32)
    out_ref[...] = _rmsnorm(x2, gfin_ref[...])


def _post(g, ma, gs, x, permt, w_glu, w_out, g_ffn, w_up, w_down, g_final, tm, chunked_g):
    n = x.shape[0]
    row = pl.BlockSpec((tm, D_MODEL), lambda i: (i, 0))
    g_spec = pl.BlockSpec((tm // CHUNK, CHUNK_LANES), lambda i: (i, 0)) if chunked_g else row
    return pl.pallas_call(
        functools.partial(_post_kernel, chunked_g=chunked_g),
        grid=(n // tm,),
        in_specs=[g_spec, row, row, row, _const_spec(permt.shape), _const_spec(w_glu.shape),
                  _const_spec(w_out.shape), _const_spec((1, D_MODEL)), _const_spec(w_up.shape),
                  _const_spec(w_down.shape), _const_spec((1, D_MODEL))],
        out_specs=row,
        out_shape=jax.ShapeDtypeStruct((n, D_MODEL), F32),
        compiler_params=_params("arbitrary"),
        name="post",
    )(g, ma, gs, x, permt, w_glu, w_out, g_ffn, w_up, w_down, g_final)


def _tile(n, pref):
    t = pref
    while n % t:
        t //= 2
    return t


def kernel(x_prompt, x_sample, cache_k, cache_v, state_ssm_re, state_ssm_im, g_mix, w_in, attn_sinks,
           w_attn_o, ssm_lambda_re, ssm_lambda_im, ssm_log_dt, ssm_b_re, ssm_b_im, ssm_c_re, ssm_c_im,
           ssm_d, w_glu, w_out, g_ffn, w_up, w_down, g_final):
    batch, seq, _ = x_prompt.shape
    db, dec_seq, _ = x_sample.shape
    assert w_in.shape[0] == 1, "one layer"
    assert seq % PERM_ROWS == 0 and dec_seq <= SUBLANES and db % SUBLANES == 0

    vec = lambda v: v.reshape(1, D_MODEL).astype(F32)
    w_in_b, wo_b = w_in[0].astype(BF16), w_attn_o[0].astype(BF16)
    w_glu_b, w_out_b = w_glu[0].astype(BF16), w_out[0].astype(BF16)
    w_up_b, w_down_b = w_up[0].astype(BF16), w_down[0].astype(BF16)
    sinks = attn_sinks[0].astype(F32)
    s5p = (ssm_lambda_re[0], ssm_lambda_im[0], ssm_log_dt[0], ssm_b_re[0], ssm_b_im[0], ssm_c_re[0], ssm_c_im[0])
    d_skip = jnp.tile(ssm_d[0].astype(F32).reshape(SSM_GROUPS, 1, SSM_CH), (1, 1, CHUNK))
    perm = _chunk_perm()
    post_w = (w_glu_b, w_out_b, vec(g_ffn[0]), w_up_b, w_down_b, vec(g_final))

    xp = x_prompt.reshape(batch * seq, D_MODEL)
    q, kv, u16, ga, gs = _proj(xp, vec(g_mix[0]), w_in_b, perm, _tile(batch * seq, 512), True)
    ma = _attn_prompt(sinks, q, kv, ga, wo_b, batch, seq, _tile(seq, 512))
    m, ws, ws_n, whyt, mk, pw, an = _s5_prep(*s5p, dec_seq)
    n_rows = seq // CHUNK
    g16, hT = _s5_chain(u16, m, ws, whyt, mk, pw, d_skip, jnp.zeros((batch, 1, STATE_COLS), F32),
                        batch, n_rows, _tile(n_rows, 128))
    y_prompt = _post(g16, ma, gs, xp, perm.T, *post_w, PERM_ROWS, True).reshape(batch, seq, D_MODEL)
    kv_last = kv.reshape(batch, seq, 2 * D_KV)[:, seq - WINDOW:]
    k_prompt = kv_last[..., :D_KV].reshape(1, batch, WINDOW, N_KV_HEADS, HEAD_DIM)
    v_prompt = kv_last[..., D_KV:].reshape(1, batch, WINDOW, N_KV_HEADS, HEAD_DIM)
    hr, hi = _state_split(hT[:, 0])
    ssm_re_prompt, ssm_im_prompt = hr[None], hi[None]

    ns = db * dec_seq
    xs = x_sample.reshape(ns, D_MODEL)
    q, kv, u, ga, gs = _proj(xs, vec(g_mix[0]), w_in_b, perm, _tile(ns, 512), False)
    tpad = SUBLANES
    pad3 = lambda v: jnp.pad(v.reshape(db, dec_seq, -1).astype(F32), ((0, 0), (0, tpad - dec_seq), (0, 0)))
    ga_pad = pad3(ga).astype(BF16).reshape(db * tpad, D_MODEL)
    ma_pad, k_new, v_new = _attn_sample(
        sinks, pad3(q), pad3(kv), cache_k[0].reshape(db, WINDOW, D_KV), cache_v[0].reshape(db, WINDOW, D_KV),
        ga_pad, wo_b, _tile(db, 16), dec_seq)
    ma = ma_pad.reshape(db, tpad, D_MODEL)[:, :dec_seq].reshape(ns, D_MODEL)
    u16 = u.reshape(db, dec_seq, SSM_GROUPS, SSM_CH).transpose(0, 2, 1, 3)
    u16 = jnp.pad(u16, ((0, 0), (0, 0), (0, CHUNK - dec_seq), (0, 0))).reshape(db, CHUNK_LANES)
    h0 = _state_cols(state_ssm_re[0].astype(F32), state_ssm_im[0].astype(F32))
    g16, hT = _s5_rows(u16, m, ws_n, whyt, an, d_skip, h0, _tile(db, 64))
    g_act = g16.reshape(db, SSM_GROUPS, CHUNK, SSM_CH)[:, :, :dec_seq].transpose(0, 2, 1, 3).reshape(ns, D_MODEL)
    y_sample = _post(g_act, ma, gs, xs, perm.T, *post_w, _tile(ns, 256), False).reshape(db, dec_seq, D_MODEL)
    k_sample = k_new.reshape(1, db, WINDOW, N_KV_HEADS, HEAD_DIM)
    v_sample = v_new.reshape(1, db, WINDOW, N_KV_HEADS, HEAD_DIM)
    hr, hi = _state_split(hT)
    ssm_re_sample, ssm_im_sample = hr[None], hi[None]

    return (y_prompt, y_sample, k_prompt, v_prompt, ssm_re_prompt, ssm_im_prompt,
            k_sample, v_sample, ssm_re_sample, ssm_im_sample)
```

```python
import functools

import jax
import jax.numpy as jnp
from jax import lax
from jax.experimental import pallas as pl
from jax.experimental.pallas import tpu as pltpu

F32 = jnp.float32
BF16 = jnp.bfloat16

D_MODEL = 1024
HEAD_DIM = 64
N_HEADS = 16
N_KV_HEADS = 4
WINDOW = 128
D_KV = N_KV_HEADS * HEAD_DIM
SSM_CH = 16
SSM_GROUPS = 64
SSM_STATE = 64
D_FF = 4 * D_MODEL
FF_CHUNK = 1024
RMS_EPS = 1e-5
LOG2E = 1.4426950408889634
Q_SCALE = HEAD_DIM ** -0.5 * LOG2E

LANES = 128
SUBLANES = 8
BF16_ROWS = 16
N_LANE_TILES = D_MODEL // LANES
GROUPS_PER_TILE = LANES // SSM_CH
CHUNK = 16
CHUNK_LANES = CHUNK * D_MODEL
GROUP_IO = CHUNK * SSM_CH
N_PAIRS = SSM_GROUPS // 2
PAIRS_PER_TILE = GROUPS_PER_TILE // 2
PAIR_COLS = 2 * LANES
TILE_COLS = PAIRS_PER_TILE * PAIR_COLS
STATE_COLS = N_PAIRS * PAIR_COLS
HALF_COLS = STATE_COLS // 2
PERM_ROWS = CHUNK * BF16_ROWS
VMEM_LIMIT = 56 * 1024 * 1024

_Q0, _KV0, _U0, _GA0, _GS0, _END = 0, 1024, 1536, 2560, 3584, 4608


def _rmsnorm(x, g):
    return x * lax.rsqrt(jnp.mean(x * x, axis=-1, keepdims=True) + RMS_EPS) * g


def _params(*sem):
    return pltpu.CompilerParams(dimension_semantics=sem, vmem_limit_bytes=VMEM_LIMIT)


def _const_spec(shape):
    nd = len(shape)
    return pl.BlockSpec(shape, lambda *_: (0,) * nd, pipeline_mode=pl.Buffered(1))


def _chunk_perm():
    r = jnp.arange(PERM_ROWS)
    tok = (r % BF16_ROWS) * CHUNK + r // BF16_ROWS
    return (tok[:, None] == jnp.arange(PERM_ROWS)[None, :]).astype(BF16)


def _piece_transpose(cols, masks):
    for d, msk in zip((4, 2, 1), masks):
        new = list(cols)
        for v in range(GROUPS_PER_TILE):
            if v & d == 0:
                a, b = cols[v], cols[v + d]
                new[v] = jnp.where(msk, pltpu.roll(b, SSM_CH * d, 1), a)
                new[v + d] = jnp.where(msk, b, pltpu.roll(a, LANES - SSM_CH * d, 1))
        cols = new
    return cols


def _piece_masks():
    piece = lax.broadcasted_iota(jnp.int32, (BF16_ROWS, LANES), 1) // SSM_CH
    return [(piece & d) != 0 for d in (4, 2, 1)]


def _store_chunk_rows(x, perm_ref, out_ref):
    masks = _piece_masks()
    for hb in range(x.shape[0] // PERM_ROWS):
        xp = jnp.dot(perm_ref[...], x[hb * PERM_ROWS:(hb + 1) * PERM_ROWS],
                     preferred_element_type=F32)
        for j in range(N_LANE_TILES):
            for hf in range(2):
                cols = [xp[(8 * hf + k) * BF16_ROWS:(8 * hf + k + 1) * BF16_ROWS, j * LANES:(j + 1) * LANES]
                        for k in range(8)]
                for gl, col in enumerate(_piece_transpose(cols, masks)):
                    c0 = (GROUPS_PER_TILE * j + gl) * GROUP_IO + hf * LANES
                    out_ref[hb * BF16_ROWS:(hb + 1) * BF16_ROWS, c0:c0 + LANES] = col.astype(BF16)


def _load_chunk_rows(in_ref, permt_ref):
    masks = _piece_masks()
    blocks = []
    for hb in range(in_ref.shape[0] // BF16_ROWS):
        tiles = [[None] * N_LANE_TILES for _ in range(CHUNK)]
        for j in range(N_LANE_TILES):
            for hf in range(2):
                cols = []
                for gl in range(GROUPS_PER_TILE):
                    c0 = (GROUPS_PER_TILE * j + gl) * GROUP_IO + hf * LANES
                    cols.append(in_ref[hb * BF16_ROWS:(hb + 1) * BF16_ROWS, c0:c0 + LANES].astype(F32))
                for k, col in enumerate(_piece_transpose(cols, masks)):
                    tiles[8 * hf + k][j] = col.astype(BF16)
        xp = jnp.concatenate([jnp.concatenate(row, axis=1) for row in tiles], axis=0)
        blocks.append(jnp.dot(permt_ref[...], xp, preferred_element_type=F32).astype(BF16))
    return blocks[0] if len(blocks) == 1 else jnp.concatenate(blocks, axis=0)


def _proj_kernel(x_ref, g_ref, w_ref, perm_ref, q_ref, kv_ref, u_ref, ga_ref, gs_ref, *, chunked_u):
    h = _rmsnorm(x_ref[...], g_ref[...]).astype(BF16)

    def seg(lo, hi):
        return jnp.dot(h, w_ref[:, lo:hi], preferred_element_type=F32)

    q_ref[...] = (seg(_Q0, _KV0) * Q_SCALE).astype(BF16)
    kv_ref[...] = seg(_KV0, _U0)
    u = seg(_U0, _GA0).astype(BF16)
    if chunked_u:
        _store_chunk_rows(u, perm_ref, u_ref)
    else:
        u_ref[...] = u
    ga_ref[...] = jax.nn.sigmoid(seg(_GA0, _GS0)).astype(BF16)
    gs_ref[...] = jax.nn.sigmoid(seg(_GS0, _END)).astype(BF16)


def _proj(x, g_mix, w_in, perm, tm, chunked_u):
    n = x.shape[0]
    row = lambda w: pl.BlockSpec((tm, w), lambda i: (i, 0))
    u_spec = pl.BlockSpec((tm // CHUNK, CHUNK_LANES), lambda i: (i, 0)) if chunked_u else row(D_MODEL)
    u_shape = (n // CHUNK, CHUNK_LANES) if chunked_u else (n, D_MODEL)
    return pl.pallas_call(
        functools.partial(_proj_kernel, chunked_u=chunked_u),
        grid=(n // tm,),
        in_specs=[row(D_MODEL), _const_spec((1, D_MODEL)), _const_spec(w_in.shape), _const_spec(perm.shape)],
        out_specs=[row(D_MODEL), row(2 * D_KV), u_spec, row(D_MODEL), row(D_MODEL)],
        out_shape=[jax.ShapeDtypeStruct((n, D_MODEL), BF16),
                   jax.ShapeDtypeStruct((n, 2 * D_KV), F32),
                   jax.ShapeDtypeStruct(u_shape, BF16),
                   jax.ShapeDtypeStruct((n, D_MODEL), BF16),
                   jax.ShapeDtypeStruct((n, D_MODEL), BF16)],
        compiler_params=_params("arbitrary"),
        name="proj",
    )(x, g_mix, w_in, perm)


def _dup_heads(tile):
    lo = lax.broadcasted_iota(jnp.int32, tile.shape, tile.ndim - 1) < HEAD_DIM
    rolled = pltpu.roll(tile, HEAD_DIM, tile.ndim - 1)
    return (jnp.where(lo, tile, rolled).astype(BF16), jnp.where(lo, rolled, tile).astype(BF16))


def _sink_softmax(s, sink):
    sink2 = sink * LOG2E
    m = jnp.maximum(jnp.max(s, axis=-1, keepdims=True), sink2)
    p = jnp.exp2(s - m)
    denom = jnp.sum(p, axis=-1, keepdims=True) + jnp.exp2(sink2 - m)
    return p.astype(BF16), 1.0 / denom


def _attn_prompt_kernel(sinks_ref, q_ref, kvc_ref, kvp_ref, ga_ref, wo_ref, out_ref,
                        kd_scr, vd_scr, bias_scr, attn_scr, *, tq):
    i = pl.program_id(1)
    kv_full = jnp.concatenate([kvp_ref[...], kvc_ref[...]], axis=0)
    for t in range(2):
        ke, ko = _dup_heads(kv_full[:, t * LANES:(t + 1) * LANES])
        ve, vo = _dup_heads(kv_full[:, D_KV + t * LANES:D_KV + (t + 1) * LANES])
        kd_scr[2 * t], kd_scr[2 * t + 1] = ke, ko
        vd_scr[2 * t], vd_scr[2 * t + 1] = ve, vo

    qi = lax.broadcasted_iota(jnp.int32, (WINDOW, 2 * WINDOW), 0)
    si = lax.broadcasted_iota(jnp.int32, (WINDOW, 2 * WINDOW), 1)
    band = (si > qi) & (si <= qi + WINDOW)
    in_block = si >= WINDOW
    lo = lax.broadcasted_iota(jnp.int32, (WINDOW, LANES), 1) < HEAD_DIM

    def attend(jb):
        q0 = pl.multiple_of(jb * WINDOW, WINDOW)
        has_prev = (i > 0) | (jb > 0)
        bias_scr[...] = jnp.where(band & (in_block | has_prev), 0.0, -jnp.inf)

        def head_pair(hp, c):
            j = hp // 2
            c0 = pl.multiple_of(hp * LANES, LANES)
            kd = kd_scr[j, pl.ds(q0, 2 * WINDOW), :]
            vd = vd_scr[j, pl.ds(q0, 2 * WINDOW), :]
            qp = q_ref[pl.ds(q0, WINDOW), pl.ds(c0, LANES)]
            outs = []
            for par in range(2):
                qm = jnp.where(lo if par == 0 else ~lo, qp, jnp.zeros_like(qp))
                s = lax.dot_general(qm, kd, (((1,), (1,)), ((), ())), preferred_element_type=F32)
                p, inv = _sink_softmax(s + bias_scr[...], sinks_ref[2 * hp + par])
                outs.append(jnp.dot(p, vd, preferred_element_type=F32) * inv)
            attn_scr[pl.ds(q0, WINDOW), pl.ds(c0, LANES)] = jnp.where(lo, outs[0], outs[1]).astype(BF16)
            return c

        lax.fori_loop(0, N_HEADS // 2, head_pair, 0, unroll=8)

    def project(jb):
        rows = pl.ds(pl.multiple_of(jb * WINDOW, WINDOW), WINDOW)
        a_out = jnp.dot(attn_scr[rows, :], wo_ref[...], preferred_element_type=F32)
        out_ref[rows, :] = ga_ref[rows, :].astype(F32) * a_out

    def step(jb, carry):
        project(jb - 1)
        attend(jb)
        return carry

    n_blocks = tq // WINDOW
    attend(jnp.int32(0))
    lax.fori_loop(1, n_blocks, step, 0)
    project(jnp.int32(n_blocks - 1))


def _attn_prompt(sinks, q, kv, ga, wo, batch, seq, tq):
    nq = seq // tq
    bpt = tq // WINDOW
    row = lambda w: pl.BlockSpec((tq, w), lambda b, i: (b * nq + i, 0))
    prev = pl.BlockSpec((WINDOW, 2 * D_KV),
                        lambda b, i: (jnp.maximum((b * nq + i) * bpt - 1, 0), 0))
    return pl.pallas_call(
        functools.partial(_attn_prompt_kernel, tq=tq),
        grid=(batch, nq),
        in_specs=[pl.BlockSpec(memory_space=pltpu.SMEM), row(D_MODEL), row(2 * D_KV), prev,
                  row(D_MODEL), _const_spec(wo.shape)],
        out_specs=row(D_MODEL),
        out_shape=jax.ShapeDtypeStruct((batch * seq, D_MODEL), F32),
        scratch_shapes=[pltpu.VMEM((N_KV_HEADS, WINDOW + tq, LANES), BF16),
                        pltpu.VMEM((N_KV_HEADS, WINDOW + tq, LANES), BF16),
                        pltpu.VMEM((WINDOW, 2 * WINDOW), F32),
                        pltpu.VMEM((tq, D_MODEL), BF16)],
        compiler_params=_params("arbitrary", "arbitrary"),
        name="attn_prompt",
    )(sinks, q, kv, kv, ga, wo)


def _attn_sample_kernel(sinks_ref, q_ref, kvn_ref, ck_ref, cv_ref, ga_ref, wo_ref,
                        out_ref, ko_ref, vo_ref, attn_scr, *, bb, tpad, dec_seq):
    nk = WINDOW + tpad
    lo3 = lax.broadcasted_iota(jnp.int32, (bb, tpad, LANES), 2) < HEAD_DIM
    row = lax.broadcasted_iota(jnp.int32, (4 * tpad, nk), 0)
    si = lax.broadcasted_iota(jnp.int32, (4 * tpad, nk), 1)
    tq = row % tpad
    valid = (si > tq) & (si <= tq + WINDOW) & (si < WINDOW + dec_seq)
    hrow = lax.broadcasted_iota(jnp.int32, (4 * tpad, 1), 0) // tpad

    for t in range(2):
        sl = slice(t * LANES, (t + 1) * LANES)
        kk = jnp.concatenate([ck_ref[:, :, sl], kvn_ref[:, :, sl]], axis=1)
        vv = jnp.concatenate([cv_ref[:, :, sl],
                              kvn_ref[:, :, D_KV + t * LANES:D_KV + (t + 1) * LANES]], axis=1)
        ko_ref[:, :, sl] = kk[:, dec_seq:dec_seq + WINDOW, :]
        vo_ref[:, :, sl] = vv[:, dec_seq:dec_seq + WINDOW, :]
        kds = _dup_heads(kk)
        vds = _dup_heads(vv)
        for par_kv in range(2):
            j = 2 * t + par_kv
            kd, vd = kds[par_kv], vds[par_kv]
            parts = []
            for r in range(2):
                c0 = j * 2 * LANES + r * LANES
                qp = q_ref[:, :, c0:c0 + LANES]
                parts += [jnp.where(lo3, qp, 0.0), jnp.where(lo3, 0.0, qp)]
            lhs = jnp.concatenate(parts, axis=1).astype(BF16)
            s = jnp.einsum('bqd,bkd->bqk', lhs, kd, preferred_element_type=F32)
            sink = jnp.zeros((4 * tpad, 1), F32)
            for g in range(4):
                sink = jnp.where(hrow == g, sinks_ref[4 * j + g], sink)
            p, inv = _sink_softmax(jnp.where(valid[None], s, -jnp.inf), sink[None])
            o = jnp.einsum('bqk,bkd->bqd', p, vd, preferred_element_type=F32) * inv
            for r in range(2):
                c0 = j * 2 * LANES + r * LANES
                o_even = o[:, (2 * r) * tpad:(2 * r + 1) * tpad, :]
                o_odd = o[:, (2 * r + 1) * tpad:(2 * r + 2) * tpad, :]
                attn_scr[:, :, c0:c0 + LANES] = jnp.where(lo3, o_even, o_odd)

    attn = attn_scr[...].reshape(bb * tpad, D_MODEL).astype(BF16)
    a_out = jnp.dot(attn, wo_ref[...], preferred_element_type=F32)
    out_ref[...] = ga_ref[...].astype(F32) * a_out


def _attn_sample(sinks, q3, kvn3, ck, cv, ga2, wo, bb, dec_seq):
    db, tpad, _ = q3.shape
    blk3 = lambda r, w: pl.BlockSpec((bb, r, w), lambda i: (i, 0, 0))
    row = pl.BlockSpec((bb * tpad, D_MODEL), lambda i: (i, 0))
    return pl.pallas_call(
        functools.partial(_attn_sample_kernel, bb=bb, tpad=tpad, dec_seq=dec_seq),
        grid=(db // bb,),
        in_specs=[pl.BlockSpec(memory_space=pltpu.SMEM), blk3(tpad, D_MODEL), blk3(tpad, 2 * D_KV),
                  blk3(WINDOW, D_KV), blk3(WINDOW, D_KV), row, _const_spec(wo.shape)],
        out_specs=[row, blk3(WINDOW, D_KV), blk3(WINDOW, D_KV)],
        out_shape=[jax.ShapeDtypeStruct((db * tpad, D_MODEL), F32),
                   jax.ShapeDtypeStruct((db, WINDOW, D_KV), F32),
                   jax.ShapeDtypeStruct((db, WINDOW, D_KV), F32)],
        scratch_shapes=[pltpu.VMEM((bb, tpad, D_MODEL), F32)],
        compiler_params=_params("arbitrary"),
        name="attn_sample",
    )(sinks, q3, kvn3, ck, cv, ga2, wo)


def _s5_discretize(lam_re, lam_im, log_dt, b_re, b_im):
    dt = jnp.exp(log_dt)[:, None]
    decay = jnp.exp(lam_re * dt)
    ab_re = decay * jnp.cos(lam_im * dt)
    ab_im = decay * jnp.sin(lam_im * dt)
    nr, ni = ab_re - 1.0, ab_im
    den = lam_re * lam_re + lam_im * lam_im
    f_re = ((nr * lam_re + ni * lam_im) / den)[..., None]
    f_im = ((ni * lam_re - nr * lam_im) / den)[..., None]
    return ab_re, ab_im, f_re * b_re - f_im * b_im, f_re * b_im + f_im * b_re


def _pair_cols(a):
    return a.reshape(a.shape[:-2] + (N_PAIRS, LANES))


def _state_cols(re, im):
    return jnp.stack([_pair_cols(re), _pair_cols(im)], axis=-2).reshape(re.shape[:-2] + (STATE_COLS,))


def _state_split(h):
    h = h.reshape(h.shape[:-1] + (N_PAIRS, 2, 2, SSM_STATE))
    unpair = lambda a: a.reshape(a.shape[:-3] + (SSM_GROUPS, SSM_STATE))
    return unpair(h[..., 0, :, :]), unpair(h[..., 1, :, :])


def _shift_lanes(x, n):
    a, b = x[:, :LANES], x[:, LANES:]
    lane = lax.broadcasted_iota(jnp.int32, a.shape, 1)
    if n == 0:
        return x
    if n >= LANES:
        r = n - LANES
        hi = a if r == 0 else jnp.where(lane < r, 0.0, pltpu.roll(a, r, 1))
        return jnp.concatenate([jnp.zeros_like(a), hi], axis=1)
    ra, rb = pltpu.roll(a, n, 1), pltpu.roll(b, n, 1)
    return jnp.concatenate([jnp.where(lane < n, 0.0, ra), jnp.where(lane < n, ra, rb)], axis=1)


def _dot_nt_split(lhs, rhs):
    nt = lambda a, b: lax.dot_general(a, b, (((1,), (1,)), ((), ())), preferred_element_type=F32)
    l_hi, r_hi = lhs.astype(BF16), rhs.astype(BF16)
    l_lo = (lhs - l_hi.astype(F32)).astype(BF16)
    r_lo = (rhs - r_hi.astype(F32)).astype(BF16)
    return nt(l_hi, r_hi) + nt(l_hi, r_lo) + nt(l_lo, r_hi)


def _s5_prep_kernel(a_ref, c_ref, bt_ref, m_ref, ws_ref, wsn_ref, whyt_ref, mk_ref, pw_ref, an_ref, *, n_tok):
    ar, ai = a_ref[0, 0], a_ref[0, 1]
    cr, ci = c_ref[0, 0], c_ref[0, 1]
    br, bi = bt_ref[0, 0], bt_ref[0, 1]
    pr, pi = [jnp.ones_like(ar)], [jnp.zeros_like(ai)]
    for _ in range(CHUNK):
        pr, pi = pr + [pr[-1] * ar - pi[-1] * ai], pi + [pr[-1] * ai + pi[-1] * ar]
    first = lax.broadcasted_iota(jnp.int32, (SSM_CH, LANES), 1) < SSM_STATE
    ca = [(cr * pr[t] - ci * pi[t], -(cr * pi[t] + ci * pr[t])) for t in range(CHUNK + 1)]
    ca_full = jnp.concatenate([jnp.concatenate(ca[t], axis=1) for t in range(CHUNK)], axis=0)
    zero = jnp.zeros((SSM_CH, LANES), F32)
    for e in range(2):
        own = first if e == 0 else ~first
        pick = lambda v: jnp.where(own, v, zero)
        for t in range(CHUNK):
            rows = slice(t * SSM_CH, (t + 1) * SSM_CH)
            whyt_ref[e, rows, :LANES] = pick(ca[t + 1][0]).astype(BF16)
            whyt_ref[e, rows, LANES:] = pick(ca[t + 1][1]).astype(BF16)
            for ref, k in ((ws_ref, CHUNK - 1 - t), (wsn_ref, n_tok - 1 - t)):
                if k >= 0:
                    ref[e, rows, :LANES] = pick(br * pr[k] - bi * pi[k]).astype(BF16)
                    ref[e, rows, LANES:] = pick(br * pi[k] + bi * pr[k]).astype(BF16)
                else:
                    ref[e, rows, :] = jnp.zeros((SSM_CH, PAIR_COLS), BF16)
        k_row = _dot_nt_split(jnp.concatenate([pick(br), pick(bi)], axis=1), ca_full)
        for s in range(CHUNK):
            m_ref[e, s * SSM_CH:(s + 1) * SSM_CH, :] = _shift_lanes(k_row, s * SSM_CH).astype(BF16)
    sub = lax.broadcasted_iota(jnp.int32, (SUBLANES, LANES), 0)
    qr, qi = [pr[CHUNK]], [pi[CHUNK]]
    for _ in range(SUBLANES - 1):
        qr, qi = qr + [qr[-1] * pr[CHUNK] - qi[-1] * pi[CHUNK]], qi + [qr[-1] * pi[CHUNK] + qi[-1] * pr[CHUNK]]
    for part, q in enumerate((qr, qi)):
        rows = jnp.zeros((SUBLANES, LANES), F32)
        for s in range(SUBLANES):
            rows = jnp.where(sub == s, q[s], rows)
        pw_ref[part] = rows
        for k, shift in enumerate((1, 2, 4)):
            mk_ref[k, part] = jnp.where(sub >= shift, q[shift - 1], 0.0)
    an_ref[0] = jnp.broadcast_to(pr[n_tok], (SUBLANES, LANES))
    an_ref[1] = jnp.broadcast_to(pi[n_tok], (SUBLANES, LANES))


def _s5_prep(lam_re, lam_im, log_dt, b_re, b_im, c_re, c_im, n_tok):
    ab_re, ab_im, bb_re, bb_im = _s5_discretize(lam_re, lam_im, log_dt, b_re, b_im)
    pair = lambda v: v.reshape(N_PAIRS, 2, v.shape[1], SSM_STATE).transpose(0, 2, 1, 3).reshape(
        N_PAIRS, v.shape[1], LANES)
    a = jnp.stack([pair(ab_re[:, None, :]), pair(ab_im[:, None, :])], axis=1)
    c = jnp.stack([pair(c_re), pair(c_im)], axis=1)
    bt = jnp.stack([pair(bb_re.transpose(0, 2, 1)), pair(bb_im.transpose(0, 2, 1))], axis=1)
    blk4 = lambda r: pl.BlockSpec((1, 2, r, LANES), lambda q: (q, 0, 0, 0))
    w_spec = pl.BlockSpec((2, GROUP_IO, PAIR_COLS), lambda q: (q, 0, 0))
    w_shape = jax.ShapeDtypeStruct((SSM_GROUPS, GROUP_IO, PAIR_COLS), BF16)
    return pl.pallas_call(
        functools.partial(_s5_prep_kernel, n_tok=n_tok),
        grid=(N_PAIRS,),
        in_specs=[blk4(1), blk4(SSM_CH), blk4(SSM_CH)],
        out_specs=[w_spec, w_spec, w_spec, w_spec,
                   pl.BlockSpec((3, 2, SUBLANES, LANES), lambda q: (0, 0, 0, q)),
                   pl.BlockSpec((2, SUBLANES, LANES), lambda q: (0, 0, q)),
                   pl.BlockSpec((2, SUBLANES, LANES), lambda q: (0, 0, q))],
        out_shape=[w_shape, w_shape, w_shape, w_shape,
                   jax.ShapeDtypeStruct((3, 2, SUBLANES, HALF_COLS), F32),
                   jax.ShapeDtypeStruct((2, SUBLANES, HALF_COLS), F32),
                   jax.ShapeDtypeStruct((2, SUBLANES, HALF_COLS), F32)],
        compiler_params=_params("arbitrary"),
        name="s5_prep",
    )(a, c, bt)


def _group_io(ref, g):
    return ref.at[:, g * GROUP_IO:(g + 1) * GROUP_IO]


def _s5_emit_tile(u_ref, st_scr, m_ref, whyt_ref, dt_ref, g_ref, j):
    for gl in range(GROUPS_PER_TILE):
        g = GROUPS_PER_TILE * j + gl
        q0 = (gl // 2) * PAIR_COLS
        u_g = _group_io(u_ref, g)[...]
        y = (jnp.dot(u_g, m_ref[g], preferred_element_type=F32)
             + lax.dot_general(st_scr[:, q0:q0 + PAIR_COLS].astype(BF16), whyt_ref[g],
                               (((1,), (1,)), ((), ())), preferred_element_type=F32)
             + dt_ref[g] * u_g.astype(F32))
        _group_io(g_ref, g)[...] = jax.nn.gelu(y).astype(BF16)


def _s5_local_states(u_ref, ws_ref, j, q):
    g = GROUPS_PER_TILE * j + 2 * q
    return (jnp.dot(_group_io(u_ref, g)[...], ws_ref[g], preferred_element_type=F32)
            + jnp.dot(_group_io(u_ref, g + 1)[...], ws_ref[g + 1], preferred_element_type=F32))


def _cmul_add(ar, ai, hr, hi, xr, xi):
    return ar * hr - ai * hi + xr, ar * hi + ai * hr + xi


def _s5_chain_kernel(u_ref, m_ref, ws_ref, whyt_ref, mk_ref, pw_ref, dt_ref, h0_ref, g_ref, hT_ref,
                     carry_scr, st_scr, *, rows):
    @pl.when(pl.program_id(1) == 0)
    def _():
        carry_scr[...] = jnp.broadcast_to(h0_ref[0], (SUBLANES, STATE_COLS))

    row0 = lax.broadcasted_iota(jnp.int32, (SUBLANES, LANES), 0) == 0
    last = lambda h: jnp.broadcast_to(h[SUBLANES - 1:, :], (SUBLANES, LANES))

    for j in range(N_LANE_TILES):
        for q in range(PAIRS_PER_TILE):
            st_scr[:, q * PAIR_COLS:(q + 1) * PAIR_COLS] = _s5_local_states(u_ref, ws_ref, j, q)

        def step(r, carry):
            r0 = pl.multiple_of(r * SUBLANES, SUBLANES)
            out = []
            for q in range(PAIRS_PER_TILE):
                re_c = slice(q * PAIR_COLS, q * PAIR_COLS + LANES)
                im_c = slice(q * PAIR_COLS + LANES, (q + 1) * PAIR_COLS)
                hc = slice((j * PAIRS_PER_TILE + q) * LANES, (j * PAIRS_PER_TILE + q + 1) * LANES)
                xr = st_scr[pl.ds(r0, SUBLANES), re_c]
                xi = st_scr[pl.ds(r0, SUBLANES), im_c]
                for k, shift in enumerate((1, 2, 4)):
                    xr, xi = _cmul_add(mk_ref[k, 0, :, hc], mk_ref[k, 1, :, hc],
                                       pltpu.roll(xr, shift, 0), pltpu.roll(xi, shift, 0), xr, xi)
                cr, ci = carry[q]
                hr, hi = _cmul_add(pw_ref[0, :, hc], pw_ref[1, :, hc], cr, ci, xr, xi)
                st_scr[pl.ds(r0, SUBLANES), re_c] = jnp.where(row0, cr, pltpu.roll(hr, 1, 0))
                st_scr[pl.ds(r0, SUBLANES), im_c] = jnp.where(row0, ci, pltpu.roll(hi, 1, 0))
                out.append((last(hr), last(hi)))
            return tuple(out)

        c0 = j * TILE_COLS
        init = tuple((carry_scr[:, c0 + q * PAIR_COLS:c0 + q * PAIR_COLS + LANES],
                      carry_scr[:, c0 + q * PAIR_COLS + LANES:c0 + (q + 1) * PAIR_COLS])
                     for q in range(PAIRS_PER_TILE))
        final = lax.fori_loop(0, rows // SUBLANES, step, init)
        for q in range(PAIRS_PER_TILE):
            carry_scr[:, c0 + q * PAIR_COLS:c0 + q * PAIR_COLS + LANES] = final[q][0]
            carry_scr[:, c0 + q * PAIR_COLS + LANES:c0 + (q + 1) * PAIR_COLS] = final[q][1]

        _s5_emit_tile(u_ref, st_scr, m_ref, whyt_ref, dt_ref, g_ref, j)

    hT_ref[0] = carry_scr[...]


def _s5_chain(u16, m, ws, why, mk, pw, d_tiled, h0, batch, n_rows, rows):
    nblk = n_rows // rows
    row = pl.BlockSpec((rows, CHUNK_LANES), lambda b, i: (b * nblk + i, 0))
    return pl.pallas_call(
        functools.partial(_s5_chain_kernel, rows=rows),
        grid=(batch, nblk),
        in_specs=[row, _const_spec(m.shape), _const_spec(ws.shape), _const_spec(why.shape),
                  _const_spec(mk.shape), _const_spec(pw.shape), _const_spec(d_tiled.shape),
                  pl.BlockSpec((1, 1, STATE_COLS), lambda b, i: (b, 0, 0))],
        out_specs=[row, pl.BlockSpec((1, SUBLANES, STATE_COLS), lambda b, i: (b, 0, 0))],
        out_shape=[jax.ShapeDtypeStruct(u16.shape, BF16),
                   jax.ShapeDtypeStruct((batch, SUBLANES, STATE_COLS), F32)],
        scratch_shapes=[pltpu.VMEM((SUBLANES, STATE_COLS), F32), pltpu.VMEM((rows, TILE_COLS), F32)],
        compiler_params=_params("arbitrary", "arbitrary"),
        name="s5_chain",
    )(u16, m, ws, why, mk, pw, d_tiled, h0)


def _s5_rows_kernel(u_ref, m_ref, ws_ref, whyt_ref, an_ref, dt_ref, h0_ref, g_ref, hT_ref,
                    st_scr):
    for j in range(N_LANE_TILES):
        for q in range(PAIRS_PER_TILE):
            c0 = j * TILE_COLS + q * PAIR_COLS
            hc = slice((j * PAIRS_PER_TILE + q) * LANES, (j * PAIRS_PER_TILE + q + 1) * LANES)
            h0 = h0_ref[:, c0:c0 + PAIR_COLS]
            local = _s5_local_states(u_ref, ws_ref, j, q)
            hr, hi = _cmul_add(an_ref[0, :1, hc], an_ref[1, :1, hc], h0[:, :LANES], h0[:, LANES:],
                               local[:, :LANES], local[:, LANES:])
            hT_ref[:, c0:c0 + LANES] = hr
            hT_ref[:, c0 + LANES:c0 + PAIR_COLS] = hi
            st_scr[:, q * PAIR_COLS:(q + 1) * PAIR_COLS] = h0
        _s5_emit_tile(u_ref, st_scr, m_ref, whyt_ref, dt_ref, g_ref, j)


def _s5_rows(u16, m, ws, why, an, d_tiled, h0, rows):
    n = u16.shape[0]
    row = lambda w: pl.BlockSpec((rows, w), lambda i: (i, 0))
    return pl.pallas_call(
        _s5_rows_kernel,
        grid=(n // rows,),
        in_specs=[row(CHUNK_LANES), _const_spec(m.shape), _const_spec(ws.shape), _const_spec(why.shape),
                  _const_spec(an.shape), _const_spec(d_tiled.shape), row(STATE_COLS)],
        out_specs=[row(CHUNK_LANES), row(STATE_COLS)],
        out_shape=[jax.ShapeDtypeStruct(u16.shape, BF16), jax.ShapeDtypeStruct((n, STATE_COLS), F32)],
        scratch_shapes=[pltpu.VMEM((rows, TILE_COLS), F32)],
        compiler_params=_params("arbitrary"),
        name="s5_rows",
    )(u16, m, ws, why, an, d_tiled, h0)


def _post_kernel(g_ref, ma_ref, gs_ref, x_ref, permt_ref, wglu_ref, wout_ref, gffn_ref, wup_ref, wdown_ref,
                 gfin_ref, out_ref, *, chunked_g):
    g = _load_chunk_rows(g_ref, permt_ref) if chunked_g else g_ref[...]
    glu = jnp.dot(g, wglu_ref[...], preferred_element_type=F32)
    s_out = glu[:, :D_MODEL] * jax.nn.sigmoid(glu[:, D_MODEL:])
    merged = ma_ref[...] + gs_ref[...].astype(F32) * s_out
    x1 = x_ref[...] + jnp.dot(merged.astype(BF16), wout_ref[...], preferred_element_type=F32)
    h2 = _rmsnorm(x1, gffn_ref[...]).astype(BF16)
    x2 = x1
    for c in range(0, D_FF, FF_CHUNK):
        up = jnp.dot(h2, wup_ref[:, c:c + FF_CHUNK], preferred_element_type=F32)
        act = jnp.square(jnp.maximum(up, 0.0)).astype(BF16)
        x2 = x2 + jnp.dot(act, wdown_ref[c:c + FF_CHUNK, :], preferred_element_type=F32)
    out_ref[...] = _rmsnorm(x2, gfin_ref[...])


def _post(g, ma, gs, x, permt, w_glu, w_out, g_ffn, w_up, w_down, g_final, tm, chunked_g):
    n = x.shape[0]
    row = pl.BlockSpec((tm, D_MODEL), lambda i: (i, 0))
    g_spec = pl.BlockSpec((tm // CHUNK, CHUNK_LANES), lambda i: (i, 0)) if chunked_g else row
    return pl.pallas_call(
        functools.partial(_post_kernel, chunked_g=chunked_g),
        grid=(n // tm,),
        in_specs=[g_spec, row, row, row, _const_spec(permt.shape), _const_spec(w_glu.shape),
                  _const_spec(w_out.shape), _const_spec((1, D_MODEL)), _const_spec(w_up.shape),
                  _const_spec(w_down.shape), _const_spec((1, D_MODEL))],
        out_specs=row,
        out_shape=jax.ShapeDtypeStruct((n, D_MODEL), F32),
        compiler_params=_params("arbitrary"),
        name="post",
    )(g, ma, gs, x, permt, w_glu, w_out, g_ffn, w_up, w_down, g_final)


def _tile(n, pref):
    t = pref
    while n % t:
        t //= 2
    return t


def kernel(x_prompt, x_sample, cache_k, cache_v, state_ssm_re, state_ssm_im, g_mix, w_in, attn_sinks,
           w_attn_o, ssm_lambda_re, ssm_lambda_im, ssm_log_dt, ssm_b_re, ssm_b_im, ssm_c_re, ssm_c_im,
           ssm_d, w_glu, w_out, g_ffn, w_up, w_down, g_final):
    batch, seq, _ = x_prompt.shape
    db, dec_seq, _ = x_sample.shape
    assert w_in.shape[0] == 1, "one layer"
    assert seq % PERM_ROWS == 0 and dec_seq <= SUBLANES and db % SUBLANES == 0

    vec = lambda v: v.reshape(1, D_MODEL).astype(F32)
    w_in_b, wo_b = w_in[0].astype(BF16), w_attn_o[0].astype(BF16)
    w_glu_b, w_out_b = w_glu[0].astype(BF16), w_out[0].astype(BF16)
    w_up_b, w_down_b = w_up[0].astype(BF16), w_down[0].astype(BF16)
    sinks = attn_sinks[0].astype(F32)
    s5p = (ssm_lambda_re[0], ssm_lambda_im[0], ssm_log_dt[0], ssm_b_re[0], ssm_b_im[0], ssm_c_re[0], ssm_c_im[0])
    d_skip = jnp.tile(ssm_d[0].astype(F32).reshape(SSM_GROUPS, 1, SSM_CH), (1, 1, CHUNK))
    perm = _chunk_perm()
    post_w = (w_glu_b, w_out_b, vec(g_ffn[0]), w_up_b, w_down_b, vec(g_final))

    xp = x_prompt.reshape(batch * seq, D_MODEL)
    q, kv, u16, ga, gs = _proj(xp, vec(g_mix[0]), w_in_b, perm, _tile(batch * seq, 512), True)
    ma = _attn_prompt(sinks, q, kv, ga, wo_b, batch, seq, _tile(seq, 512))
    m, ws, ws_n, whyt, mk, pw, an = _s5_prep(*s5p, dec_seq)
    n_rows = seq // CHUNK
    g16, hT = _s5_chain(u16, m, ws, whyt, mk, pw, d_skip, jnp.zeros((batch, 1, STATE_COLS), F32),
                        batch, n_rows, _tile(n_rows, 128))
    y_prompt = _post(g16, ma, gs, xp, perm.T, *post_w, _tile(batch * seq, 512), True).reshape(batch, seq, D_MODEL)
    kv_last = kv.reshape(batch, seq, 2 * D_KV)[:, seq - WINDOW:]
    k_prompt = kv_last[..., :D_KV].reshape(1, batch, WINDOW, N_KV_HEADS, HEAD_DIM)
    v_prompt = kv_last[..., D_KV:].reshape(1, batch, WINDOW, N_KV_HEADS, HEAD_DIM)
    hr, hi = _state_split(hT[:, 0])
    ssm_re_prompt, ssm_im_prompt = hr[None], hi[None]

    ns = db * dec_seq
    xs = x_sample.reshape(ns, D_MODEL)
    q, kv, u, ga, gs = _proj(xs, vec(g_mix[0]), w_in_b, perm, _tile(ns, 512), False)
    tpad = SUBLANES
    pad3 = lambda v: jnp.pad(v.reshape(db, dec_seq, -1).astype(F32), ((0, 0), (0, tpad - dec_seq), (0, 0)))
    ga_pad = pad3(ga).astype(BF16).reshape(db * tpad, D_MODEL)
    ma_pad, k_new, v_new = _attn_sample(
        sinks, pad3(q), pad3(kv), cache_k[0].reshape(db, WINDOW, D_KV), cache_v[0].reshape(db, WINDOW, D_KV),
        ga_pad, wo_b, _tile(db, 16), dec_seq)
    ma = ma_pad.reshape(db, tpad, D_MODEL)[:, :dec_seq].reshape(ns, D_MODEL)
    u16 = u.reshape(db, dec_seq, SSM_GROUPS, SSM_CH).transpose(0, 2, 1, 3)
    u16 = jnp.pad(u16, ((0, 0), (0, 0), (0, CHUNK - dec_seq), (0, 0))).reshape(db, CHUNK_LANES)
    h0 = _state_cols(state_ssm_re[0].astype(F32), state_ssm_im[0].astype(F32))
    g16, hT = _s5_rows(u16, m, ws_n, whyt, an, d_skip, h0, _tile(db, 64))
    g_act = g16.reshape(db, SSM_GROUPS, CHUNK, SSM_CH)[:, :, :dec_seq].transpose(0, 2, 1, 3).reshape(ns, D_MODEL)
    y_sample = _post(g_act, ma, gs, xs, perm.T, *post_w, _tile(ns, 256), False).reshape(db, dec_seq, D_MODEL)
    k_sample = k_new.reshape(1, db, WINDOW, N_KV_HEADS, HEAD_DIM)
    v_sample = v_new.reshape(1, db, WINDOW, N_KV_HEADS, HEAD_DIM)
    hr, hi = _state_split(hT)
    ssm_re_sample, ssm_im_sample = hr[None], hi[None]

    return (y_prompt, y_sample, k_prompt, v_prompt, ssm_re_prompt, ssm_im_prompt,
            k_sample, v_sample, ssm_re_sample, ssm_im_sample)
```

```python
import functools

import jax
import jax.numpy as jnp
from jax import lax
from jax.experimental import pallas as pl
from jax.experimental.pallas import tpu as pltpu

F32 = jnp.float32
BF16 = jnp.bfloat16

D_MODEL = 1024
HEAD_DIM = 64
N_HEADS = 16
N_KV_HEADS = 4
WINDOW = 128
D_KV = N_KV_HEADS * HEAD_DIM
SSM_CH = 16
SSM_GROUPS = 64
SSM_STATE = 64
D_FF = 4 * D_MODEL
FF_CHUNK = 1024
RMS_EPS = 1e-5
LOG2E = 1.4426950408889634
Q_SCALE = HEAD_DIM ** -0.5 * LOG2E

LANES = 128
SUBLANES = 8
BF16_ROWS = 16
N_LANE_TILES = D_MODEL // LANES
GROUPS_PER_TILE = LANES // SSM_CH
CHUNK = 16
CHUNK_LANES = CHUNK * D_MODEL
GROUP_IO = CHUNK * SSM_CH
N_PAIRS = SSM_GROUPS // 2
PAIRS_PER_TILE = GROUPS_PER_TILE // 2
PAIR_COLS = 2 * LANES
TILE_COLS = PAIRS_PER_TILE * PAIR_COLS
STATE_COLS = N_PAIRS * PAIR_COLS
HALF_COLS = STATE_COLS // 2
PERM_ROWS = CHUNK * BF16_ROWS
VMEM_LIMIT = 56 * 1024 * 1024

_Q0, _KV0, _U0, _GA0, _GS0, _END = 0, 1024, 1536, 2560, 3584, 4608


def _rmsnorm(x, g):
    return x * lax.rsqrt(jnp.mean(x * x, axis=-1, keepdims=True) + RMS_EPS) * g


def _params(*sem):
    return pltpu.CompilerParams(dimension_semantics=sem, vmem_limit_bytes=VMEM_LIMIT)


def _const_spec(shape):
    nd = len(shape)
    return pl.BlockSpec(shape, lambda *_: (0,) * nd, pipeline_mode=pl.Buffered(1))


def _chunk_perm(n_tok):
    n_chunks = PERM_ROWS // n_tok
    r = jnp.arange(PERM_ROWS)
    tok = (r % n_chunks) * n_tok + r // n_chunks
    return (tok[:, None] == jnp.arange(PERM_ROWS)[None, :]).astype(BF16)


def _piece_transpose(cols, masks):
    for d, msk in zip((4, 2, 1), masks):
        new = list(cols)
        for v in range(GROUPS_PER_TILE):
            if v & d == 0:
                a, b = cols[v], cols[v + d]
                new[v] = jnp.where(msk, pltpu.roll(b, SSM_CH * d, 1), a)
                new[v + d] = jnp.where(msk, b, pltpu.roll(a, LANES - SSM_CH * d, 1))
        cols = new
    return cols


def _piece_masks(rows):
    piece = lax.broadcasted_iota(jnp.int32, (rows, LANES), 1) // SSM_CH
    return [(piece & d) != 0 for d in (4, 2, 1)]


def _store_chunk_rows(x, perm_ref, out_ref, n_tok):
    n_chunks = PERM_ROWS // n_tok
    masks = _piece_masks(n_chunks)
    zero = jnp.zeros((n_chunks, LANES), F32)
    for hb in range(x.shape[0] // PERM_ROWS):
        rows = slice(hb * n_chunks, (hb + 1) * n_chunks)
        xp = jnp.dot(perm_ref[...], x[hb * PERM_ROWS:(hb + 1) * PERM_ROWS],
                     preferred_element_type=F32)
        for j in range(N_LANE_TILES):
            for hf in range(2):
                live = 8 * hf < n_tok
                cols = [xp[t * n_chunks:(t + 1) * n_chunks, j * LANES:(j + 1) * LANES] if t < n_tok else zero
                        for t in range(8 * hf, 8 * hf + 8)]
                for gl, col in enumerate(_piece_transpose(cols, masks) if live else cols):
                    c0 = (GROUPS_PER_TILE * j + gl) * GROUP_IO + hf * LANES
                    out_ref[rows, c0:c0 + LANES] = col.astype(BF16)


def _load_chunk_rows(in_ref, permt_ref, n_tok):
    n_chunks = PERM_ROWS // n_tok
    masks = _piece_masks(n_chunks)
    blocks = []
    for hb in range(in_ref.shape[0] // n_chunks):
        rows = slice(hb * n_chunks, (hb + 1) * n_chunks)
        tiles = [[None] * N_LANE_TILES for _ in range(n_tok)]
        for j in range(N_LANE_TILES):
            for hf in range(-(-n_tok // 8)):
                cols = []
                for gl in range(GROUPS_PER_TILE):
                    c0 = (GROUPS_PER_TILE * j + gl) * GROUP_IO + hf * LANES
                    cols.append(in_ref[rows, c0:c0 + LANES].astype(F32))
                for k, col in enumerate(_piece_transpose(cols, masks)):
                    if 8 * hf + k < n_tok:
                        tiles[8 * hf + k][j] = col.astype(BF16)
        xp = jnp.concatenate([jnp.concatenate(row, axis=1) for row in tiles], axis=0)
        blocks.append(jnp.dot(permt_ref[...], xp, preferred_element_type=F32).astype(BF16))
    return blocks[0] if len(blocks) == 1 else jnp.concatenate(blocks, axis=0)


def _proj_kernel(x_ref, g_ref, w_ref, perm_ref, q_ref, kv_ref, u_ref, ga_ref, gs_ref, *, n_tok):
    h = _rmsnorm(x_ref[...], g_ref[...]).astype(BF16)

    def seg(lo, hi):
        return jnp.dot(h, w_ref[:, lo:hi], preferred_element_type=F32)

    q_ref[...] = (seg(_Q0, _KV0) * Q_SCALE).astype(BF16)
    kv_ref[...] = seg(_KV0, _U0)
    _store_chunk_rows(seg(_U0, _GA0).astype(BF16), perm_ref, u_ref, n_tok)
    ga_ref[...] = jax.nn.sigmoid(seg(_GA0, _GS0)).astype(BF16)
    gs_ref[...] = jax.nn.sigmoid(seg(_GS0, _END)).astype(BF16)


def _proj(x, g_mix, w_in, perm, tm, n_tok):
    n = x.shape[0]
    row = lambda w: pl.BlockSpec((tm, w), lambda i: (i, 0))
    return pl.pallas_call(
        functools.partial(_proj_kernel, n_tok=n_tok),
        grid=(n // tm,),
        in_specs=[row(D_MODEL), _const_spec((1, D_MODEL)), _const_spec(w_in.shape), _const_spec(perm.shape)],
        out_specs=[row(D_MODEL), row(2 * D_KV), pl.BlockSpec((tm // n_tok, CHUNK_LANES), lambda i: (i, 0)),
                   row(D_MODEL), row(D_MODEL)],
        out_shape=[jax.ShapeDtypeStruct((n, D_MODEL), BF16),
                   jax.ShapeDtypeStruct((n, 2 * D_KV), F32),
                   jax.ShapeDtypeStruct((n // n_tok, CHUNK_LANES), BF16),
                   jax.ShapeDtypeStruct((n, D_MODEL), BF16),
                   jax.ShapeDtypeStruct((n, D_MODEL), BF16)],
        compiler_params=_params("arbitrary"),
        name="proj",
    )(x, g_mix, w_in, perm)


def _dup_heads(tile):
    lo = lax.broadcasted_iota(jnp.int32, tile.shape, tile.ndim - 1) < HEAD_DIM
    rolled = pltpu.roll(tile, HEAD_DIM, tile.ndim - 1)
    return (jnp.where(lo, tile, rolled).astype(BF16), jnp.where(lo, rolled, tile).astype(BF16))


def _sink_softmax(s, sink):
    sink2 = sink * LOG2E
    m = jnp.maximum(jnp.max(s, axis=-1, keepdims=True), sink2)
    p = jnp.exp2(s - m)
    denom = jnp.sum(p, axis=-1, keepdims=True) + jnp.exp2(sink2 - m)
    return p.astype(BF16), 1.0 / denom


def _attn_prompt_kernel(sinks_ref, q_ref, kvc_ref, kvp_ref, ga_ref, wo_ref, out_ref,
                        kd_scr, vd_scr, bias_scr, attn_scr, *, tq):
    i = pl.program_id(1)
    kv_full = jnp.concatenate([kvp_ref[...], kvc_ref[...]], axis=0)
    for t in range(2):
        ke, ko = _dup_heads(kv_full[:, t * LANES:(t + 1) * LANES])
        ve, vo = _dup_heads(kv_full[:, D_KV + t * LANES:D_KV + (t + 1) * LANES])
        kd_scr[2 * t], kd_scr[2 * t + 1] = ke, ko
        vd_scr[2 * t], vd_scr[2 * t + 1] = ve, vo

    qi = lax.broadcasted_iota(jnp.int32, (WINDOW, 2 * WINDOW), 0)
    si = lax.broadcasted_iota(jnp.int32, (WINDOW, 2 * WINDOW), 1)
    band = (si > qi) & (si <= qi + WINDOW)
    in_block = si >= WINDOW
    lo = lax.broadcasted_iota(jnp.int32, (WINDOW, LANES), 1) < HEAD_DIM

    def attend(jb):
        q0 = pl.multiple_of(jb * WINDOW, WINDOW)
        has_prev = (i > 0) | (jb > 0)
        bias_scr[...] = jnp.where(band & (in_block | has_prev), 0.0, -jnp.inf)

        def head_pair(hp, c):
            j = hp // 2
            c0 = pl.multiple_of(hp * LANES, LANES)
            kd = kd_scr[j, pl.ds(q0, 2 * WINDOW), :]
            vd = vd_scr[j, pl.ds(q0, 2 * WINDOW), :]
            qp = q_ref[pl.ds(q0, WINDOW), pl.ds(c0, LANES)]
            outs = []
            for par in range(2):
                qm = jnp.where(lo if par == 0 else ~lo, qp, jnp.zeros_like(qp))
                s = lax.dot_general(qm, kd, (((1,), (1,)), ((), ())), preferred_element_type=F32)
                p, inv = _sink_softmax(s + bias_scr[...], sinks_ref[2 * hp + par])
                outs.append(jnp.dot(p, vd, preferred_element_type=F32) * inv)
            attn_scr[pl.ds(q0, WINDOW), pl.ds(c0, LANES)] = jnp.where(lo, outs[0], outs[1]).astype(BF16)
            return c

        lax.fori_loop(0, N_HEADS // 2, head_pair, 0, unroll=8)

    def project(jb):
        rows = pl.ds(pl.multiple_of(jb * WINDOW, WINDOW), WINDOW)
        a_out = jnp.dot(attn_scr[rows, :], wo_ref[...], preferred_element_type=F32)
        out_ref[rows, :] = ga_ref[rows, :].astype(F32) * a_out

    def step(jb, carry):
        project(jb - 1)
        attend(jb)
        return carry

    n_blocks = tq // WINDOW
    attend(jnp.int32(0))
    lax.fori_loop(1, n_blocks, step, 0)
    project(jnp.int32(n_blocks - 1))


def _attn_prompt(sinks, q, kv, ga, wo, batch, seq, tq):
    nq = seq // tq
    bpt = tq // WINDOW
    row = lambda w: pl.BlockSpec((tq, w), lambda b, i: (b * nq + i, 0))
    prev = pl.BlockSpec((WINDOW, 2 * D_KV),
                        lambda b, i: (jnp.maximum((b * nq + i) * bpt - 1, 0), 0))
    return pl.pallas_call(
        functools.partial(_attn_prompt_kernel, tq=tq),
        grid=(batch, nq),
        in_specs=[pl.BlockSpec(memory_space=pltpu.SMEM), row(D_MODEL), row(2 * D_KV), prev,
                  row(D_MODEL), _const_spec(wo.shape)],
        out_specs=row(D_MODEL),
        out_shape=jax.ShapeDtypeStruct((batch * seq, D_MODEL), F32),
        scratch_shapes=[pltpu.VMEM((N_KV_HEADS, WINDOW + tq, LANES), BF16),
                        pltpu.VMEM((N_KV_HEADS, WINDOW + tq, LANES), BF16),
                        pltpu.VMEM((WINDOW, 2 * WINDOW), F32),
                        pltpu.VMEM((tq, D_MODEL), BF16)],
        compiler_params=_params("arbitrary", "arbitrary"),
        name="attn_prompt",
    )(sinks, q, kv, kv, ga, wo)


def _attn_sample_kernel(sinks_ref, q_ref, kvn_ref, ck_ref, cv_ref, ga_ref, wo_ref,
                        out_ref, ko_ref, vo_ref, attn_scr, *, bb, tpad, dec_seq):
    nk = WINDOW + tpad
    lo3 = lax.broadcasted_iota(jnp.int32, (bb, tpad, LANES), 2) < HEAD_DIM
    row = lax.broadcasted_iota(jnp.int32, (4 * tpad, nk), 0)
    si = lax.broadcasted_iota(jnp.int32, (4 * tpad, nk), 1)
    tq = row % tpad
    valid = (si > tq) & (si <= tq + WINDOW) & (si < WINDOW + dec_seq)
    hrow = lax.broadcasted_iota(jnp.int32, (4 * tpad, 1), 0) // tpad

    for t in range(2):
        sl = slice(t * LANES, (t + 1) * LANES)
        kk = jnp.concatenate([ck_ref[:, :, sl], kvn_ref[:, :, sl]], axis=1)
        vv = jnp.concatenate([cv_ref[:, :, sl],
                              kvn_ref[:, :, D_KV + t * LANES:D_KV + (t + 1) * LANES]], axis=1)
        ko_ref[:, :, sl] = kk[:, dec_seq:dec_seq + WINDOW, :]
        vo_ref[:, :, sl] = vv[:, dec_seq:dec_seq + WINDOW, :]
        kds = _dup_heads(kk)
        vds = _dup_heads(vv)
        for par_kv in range(2):
            j = 2 * t + par_kv
            kd, vd = kds[par_kv], vds[par_kv]
            parts = []
            for r in range(2):
                c0 = j * 2 * LANES + r * LANES
                qp = q_ref[:, :, c0:c0 + LANES]
                parts += [jnp.where(lo3, qp, 0.0), jnp.where(lo3, 0.0, qp)]
            lhs = jnp.concatenate(parts, axis=1).astype(BF16)
            s = jnp.einsum('bqd,bkd->bqk', lhs, kd, preferred_element_type=F32)
            sink = jnp.zeros((4 * tpad, 1), F32)
            for g in range(4):
                sink = jnp.where(hrow == g, sinks_ref[4 * j + g], sink)
            p, inv = _sink_softmax(jnp.where(valid[None], s, -jnp.inf), sink[None])
            o = jnp.einsum('bqk,bkd->bqd', p, vd, preferred_element_type=F32) * inv
            for r in range(2):
                c0 = j * 2 * LANES + r * LANES
                o_even = o[:, (2 * r) * tpad:(2 * r + 1) * tpad, :]
                o_odd = o[:, (2 * r + 1) * tpad:(2 * r + 2) * tpad, :]
                attn_scr[:, :, c0:c0 + LANES] = jnp.where(lo3, o_even, o_odd)

    attn = attn_scr[...].reshape(bb * tpad, D_MODEL).astype(BF16)
    a_out = jnp.dot(attn, wo_ref[...], preferred_element_type=F32)
    out_ref[...] = ga_ref[...].astype(F32) * a_out


def _attn_sample(sinks, q3, kvn3, ck, cv, ga2, wo, bb, dec_seq):
    db, tpad, _ = q3.shape
    blk3 = lambda r, w: pl.BlockSpec((bb, r, w), lambda i: (i, 0, 0))
    row = pl.BlockSpec((bb * tpad, D_MODEL), lambda i: (i, 0))
    return pl.pallas_call(
        functools.partial(_attn_sample_kernel, bb=bb, tpad=tpad, dec_seq=dec_seq),
        grid=(db // bb,),
        in_specs=[pl.BlockSpec(memory_space=pltpu.SMEM), blk3(tpad, D_MODEL), blk3(tpad, 2 * D_KV),
                  blk3(WINDOW, D_KV), blk3(WINDOW, D_KV), row, _const_spec(wo.shape)],
        out_specs=[row, blk3(WINDOW, D_KV), blk3(WINDOW, D_KV)],
        out_shape=[jax.ShapeDtypeStruct((db * tpad, D_MODEL), F32),
                   jax.ShapeDtypeStruct((db, WINDOW, D_KV), F32),
                   jax.ShapeDtypeStruct((db, WINDOW, D_KV), F32)],
        scratch_shapes=[pltpu.VMEM((bb, tpad, D_MODEL), F32)],
        compiler_params=_params("arbitrary"),
        name="attn_sample",
    )(sinks, q3, kvn3, ck, cv, ga2, wo)


def _s5_discretize(lam_re, lam_im, log_dt, b_re, b_im):
    dt = jnp.exp(log_dt)[:, None]
    decay = jnp.exp(lam_re * dt)
    ab_re = decay * jnp.cos(lam_im * dt)
    ab_im = decay * jnp.sin(lam_im * dt)
    nr, ni = ab_re - 1.0, ab_im
    den = lam_re * lam_re + lam_im * lam_im
    f_re = ((nr * lam_re + ni * lam_im) / den)[..., None]
    f_im = ((ni * lam_re - nr * lam_im) / den)[..., None]
    return ab_re, ab_im, f_re * b_re - f_im * b_im, f_re * b_im + f_im * b_re


def _pair_cols(a):
    return a.reshape(a.shape[:-2] + (N_PAIRS, LANES))


def _state_cols(re, im):
    return jnp.stack([_pair_cols(re), _pair_cols(im)], axis=-2).reshape(re.shape[:-2] + (STATE_COLS,))


def _state_split(h):
    h = h.reshape(h.shape[:-1] + (N_PAIRS, 2, 2, SSM_STATE))
    unpair = lambda a: a.reshape(a.shape[:-3] + (SSM_GROUPS, SSM_STATE))
    return unpair(h[..., 0, :, :]), unpair(h[..., 1, :, :])


def _shift_lanes(x, n):
    a, b = x[:, :LANES], x[:, LANES:]
    lane = lax.broadcasted_iota(jnp.int32, a.shape, 1)
    if n == 0:
        return x
    if n >= LANES:
        r = n - LANES
        hi = a if r == 0 else jnp.where(lane < r, 0.0, pltpu.roll(a, r, 1))
        return jnp.concatenate([jnp.zeros_like(a), hi], axis=1)
    ra, rb = pltpu.roll(a, n, 1), pltpu.roll(b, n, 1)
    return jnp.concatenate([jnp.where(lane < n, 0.0, ra), jnp.where(lane < n, ra, rb)], axis=1)


def _dot_nt_split(lhs, rhs):
    nt = lambda a, b: lax.dot_general(a, b, (((1,), (1,)), ((), ())), preferred_element_type=F32)
    l_hi, r_hi = lhs.astype(BF16), rhs.astype(BF16)
    l_lo = (lhs - l_hi.astype(F32)).astype(BF16)
    r_lo = (rhs - r_hi.astype(F32)).astype(BF16)
    return nt(l_hi, r_hi) + nt(l_hi, r_lo) + nt(l_lo, r_hi)


def _s5_prep_kernel(a_ref, c_ref, bt_ref, m_ref, ws_ref, wsn_ref, whyt_ref, mk_ref, pw_ref, an_ref, *, n_tok):
    ar, ai = a_ref[0, 0], a_ref[0, 1]
    cr, ci = c_ref[0, 0], c_ref[0, 1]
    br, bi = bt_ref[0, 0], bt_ref[0, 1]
    pr, pi = [jnp.ones_like(ar)], [jnp.zeros_like(ai)]
    for _ in range(CHUNK):
        pr, pi = pr + [pr[-1] * ar - pi[-1] * ai], pi + [pr[-1] * ai + pi[-1] * ar]
    first = lax.broadcasted_iota(jnp.int32, (SSM_CH, LANES), 1) < SSM_STATE
    ca = [(cr * pr[t] - ci * pi[t], -(cr * pi[t] + ci * pr[t])) for t in range(CHUNK + 1)]
    ca_full = jnp.concatenate([jnp.concatenate(ca[t], axis=1) for t in range(CHUNK)], axis=0)
    zero = jnp.zeros((SSM_CH, LANES), F32)
    for e in range(2):
        own = first if e == 0 else ~first
        pick = lambda v: jnp.where(own, v, zero)
        for t in range(CHUNK):
            rows = slice(t * SSM_CH, (t + 1) * SSM_CH)
            whyt_ref[e, rows, :LANES] = pick(ca[t + 1][0]).astype(BF16)
            whyt_ref[e, rows, LANES:] = pick(ca[t + 1][1]).astype(BF16)
            for ref, k in ((ws_ref, CHUNK - 1 - t), (wsn_ref, n_tok - 1 - t)):
                if k >= 0:
                    ref[e, rows, :LANES] = pick(br * pr[k] - bi * pi[k]).astype(BF16)
                    ref[e, rows, LANES:] = pick(br * pi[k] + bi * pr[k]).astype(BF16)
                else:
                    ref[e, rows, :] = jnp.zeros((SSM_CH, PAIR_COLS), BF16)
        k_row = _dot_nt_split(jnp.concatenate([pick(br), pick(bi)], axis=1), ca_full)
        for s in range(CHUNK):
            m_ref[e, s * SSM_CH:(s + 1) * SSM_CH, :] = _shift_lanes(k_row, s * SSM_CH).astype(BF16)
    sub = lax.broadcasted_iota(jnp.int32, (SUBLANES, LANES), 0)
    qr, qi = [pr[CHUNK]], [pi[CHUNK]]
    for _ in range(SUBLANES - 1):
        qr, qi = qr + [qr[-1] * pr[CHUNK] - qi[-1] * pi[CHUNK]], qi + [qr[-1] * pi[CHUNK] + qi[-1] * pr[CHUNK]]
    for part, q in enumerate((qr, qi)):
        rows = jnp.zeros((SUBLANES, LANES), F32)
        for s in range(SUBLANES):
            rows = jnp.where(sub == s, q[s], rows)
        pw_ref[part] = rows
        for k, shift in enumerate((1, 2, 4)):
            mk_ref[k, part] = jnp.where(sub >= shift, q[shift - 1], 0.0)
    an_ref[0] = jnp.broadcast_to(pr[n_tok], (SUBLANES, LANES))
    an_ref[1] = jnp.broadcast_to(pi[n_tok], (SUBLANES, LANES))


def _s5_prep(lam_re, lam_im, log_dt, b_re, b_im, c_re, c_im, n_tok):
    ab_re, ab_im, bb_re, bb_im = _s5_discretize(lam_re, lam_im, log_dt, b_re, b_im)
    pair = lambda v: v.reshape(N_PAIRS, 2, v.shape[1], SSM_STATE).transpose(0, 2, 1, 3).reshape(
        N_PAIRS, v.shape[1], LANES)
    a = jnp.stack([pair(ab_re[:, None, :]), pair(ab_im[:, None, :])], axis=1)
    c = jnp.stack([pair(c_re), pair(c_im)], axis=1)
    bt = jnp.stack([pair(bb_re.transpose(0, 2, 1)), pair(bb_im.transpose(0, 2, 1))], axis=1)
    blk4 = lambda r: pl.BlockSpec((1, 2, r, LANES), lambda q: (q, 0, 0, 0))
    w_spec = pl.BlockSpec((2, GROUP_IO, PAIR_COLS), lambda q: (q, 0, 0))
    w_shape = jax.ShapeDtypeStruct((SSM_GROUPS, GROUP_IO, PAIR_COLS), BF16)
    return pl.pallas_call(
        functools.partial(_s5_prep_kernel, n_tok=n_tok),
        grid=(N_PAIRS,),
        in_specs=[blk4(1), blk4(SSM_CH), blk4(SSM_CH)],
        out_specs=[w_spec, w_spec, w_spec, w_spec,
                   pl.BlockSpec((3, 2, SUBLANES, LANES), lambda q: (0, 0, 0, q)),
                   pl.BlockSpec((2, SUBLANES, LANES), lambda q: (0, 0, q)),
                   pl.BlockSpec((2, SUBLANES, LANES), lambda q: (0, 0, q))],
        out_shape=[w_shape, w_shape, w_shape, w_shape,
                   jax.ShapeDtypeStruct((3, 2, SUBLANES, HALF_COLS), F32),
                   jax.ShapeDtypeStruct((2, SUBLANES, HALF_COLS), F32),
                   jax.ShapeDtypeStruct((2, SUBLANES, HALF_COLS), F32)],
        compiler_params=_params("arbitrary"),
        name="s5_prep",
    )(a, c, bt)


def _group_io(ref, g):
    return ref.at[:, g * GROUP_IO:(g + 1) * GROUP_IO]


def _s5_emit_tile(u_ref, st_scr, m_ref, whyt_ref, dt_ref, g_ref, j):
    for gl in range(GROUPS_PER_TILE):
        g = GROUPS_PER_TILE * j + gl
        q0 = (gl // 2) * PAIR_COLS
        u_g = _group_io(u_ref, g)[...]
        y = (jnp.dot(u_g, m_ref[g], preferred_element_type=F32)
             + lax.dot_general(st_scr[:, q0:q0 + PAIR_COLS].astype(BF16), whyt_ref[g],
                               (((1,), (1,)), ((), ())), preferred_element_type=F32)
             + dt_ref[g] * u_g.astype(F32))
        _group_io(g_ref, g)[...] = jax.nn.gelu(y).astype(BF16)


def _s5_local_states(u_ref, ws_ref, j, q):
    g = GROUPS_PER_TILE * j + 2 * q
    return (jnp.dot(_group_io(u_ref, g)[...], ws_ref[g], preferred_element_type=F32)
            + jnp.dot(_group_io(u_ref, g + 1)[...], ws_ref[g + 1], preferred_element_type=F32))


def _cmul_add(ar, ai, hr, hi, xr, xi):
    return ar * hr - ai * hi + xr, ar * hi + ai * hr + xi


def _s5_chain_kernel(u_ref, m_ref, ws_ref, whyt_ref, mk_ref, pw_ref, dt_ref, h0_ref, g_ref, hT_ref,
                     carry_scr, st_scr, *, rows):
    @pl.when(pl.program_id(1) == 0)
    def _():
        carry_scr[...] = jnp.broadcast_to(h0_ref[0], (SUBLANES, STATE_COLS))

    row0 = lax.broadcasted_iota(jnp.int32, (SUBLANES, LANES), 0) == 0
    last = lambda h: jnp.broadcast_to(h[SUBLANES - 1:, :], (SUBLANES, LANES))

    for j in range(N_LANE_TILES):
        for q in range(PAIRS_PER_TILE):
            st_scr[:, q * PAIR_COLS:(q + 1) * PAIR_COLS] = _s5_local_states(u_ref, ws_ref, j, q)

        def step(r, carry):
            r0 = pl.multiple_of(r * SUBLANES, SUBLANES)
            out = []
            for q in range(PAIRS_PER_TILE):
                re_c = slice(q * PAIR_COLS, q * PAIR_COLS + LANES)
                im_c = slice(q * PAIR_COLS + LANES, (q + 1) * PAIR_COLS)
                hc = slice((j * PAIRS_PER_TILE + q) * LANES, (j * PAIRS_PER_TILE + q + 1) * LANES)
                xr = st_scr[pl.ds(r0, SUBLANES), re_c]
                xi = st_scr[pl.ds(r0, SUBLANES), im_c]
                for k, shift in enumerate((1, 2, 4)):
                    xr, xi = _cmul_add(mk_ref[k, 0, :, hc], mk_ref[k, 1, :, hc],
                                       pltpu.roll(xr, shift, 0), pltpu.roll(xi, shift, 0), xr, xi)
                cr, ci = carry[q]
                hr, hi = _cmul_add(pw_ref[0, :, hc], pw_ref[1, :, hc], cr, ci, xr, xi)
                st_scr[pl.ds(r0, SUBLANES), re_c] = jnp.where(row0, cr, pltpu.roll(hr, 1, 0))
                st_scr[pl.ds(r0, SUBLANES), im_c] = jnp.where(row0, ci, pltpu.roll(hi, 1, 0))
                out.append((last(hr), last(hi)))
            return tuple(out)

        c0 = j * TILE_COLS
        init = tuple((carry_scr[:, c0 + q * PAIR_COLS:c0 + q * PAIR_COLS + LANES],
                      carry_scr[:, c0 + q * PAIR_COLS + LANES:c0 + (q + 1) * PAIR_COLS])
                     for q in range(PAIRS_PER_TILE))
        final = lax.fori_loop(0, rows // SUBLANES, step, init)
        for q in range(PAIRS_PER_TILE):
            carry_scr[:, c0 + q * PAIR_COLS:c0 + q * PAIR_COLS + LANES] = final[q][0]
            carry_scr[:, c0 + q * PAIR_COLS + LANES:c0 + (q + 1) * PAIR_COLS] = final[q][1]

        _s5_emit_tile(u_ref, st_scr, m_ref, whyt_ref, dt_ref, g_ref, j)

    hT_ref[0] = carry_scr[...]


def _s5_chain(u16, m, ws, why, mk, pw, d_tiled, h0, batch, n_rows, rows):
    nblk = n_rows // rows
    row = pl.BlockSpec((rows, CHUNK_LANES), lambda b, i: (b * nblk + i, 0))
    return pl.pallas_call(
        functools.partial(_s5_chain_kernel, rows=rows),
        grid=(batch, nblk),
        in_specs=[row, _const_spec(m.shape), _const_spec(ws.shape), _const_spec(why.shape),
                  _const_spec(mk.shape), _const_spec(pw.shape), _const_spec(d_tiled.shape),
                  pl.BlockSpec((1, 1, STATE_COLS), lambda b, i: (b, 0, 0))],
        out_specs=[row, pl.BlockSpec((1, SUBLANES, STATE_COLS), lambda b, i: (b, 0, 0))],
        out_shape=[jax.ShapeDtypeStruct(u16.shape, BF16),
                   jax.ShapeDtypeStruct((batch, SUBLANES, STATE_COLS), F32)],
        scratch_shapes=[pltpu.VMEM((SUBLANES, STATE_COLS), F32), pltpu.VMEM((rows, TILE_COLS), F32)],
        compiler_params=_params("arbitrary", "arbitrary"),
        name="s5_chain",
    )(u16, m, ws, why, mk, pw, d_tiled, h0)


def _s5_rows_kernel(u_ref, m_ref, ws_ref, whyt_ref, an_ref, dt_ref, h0_ref, g_ref, hT_ref,
                    st_scr):
    for j in range(N_LANE_TILES):
        for q in range(PAIRS_PER_TILE):
            c0 = j * TILE_COLS + q * PAIR_COLS
            hc = slice((j * PAIRS_PER_TILE + q) * LANES, (j * PAIRS_PER_TILE + q + 1) * LANES)
            h0 = h0_ref[:, c0:c0 + PAIR_COLS]
            local = _s5_local_states(u_ref, ws_ref, j, q)
            hr, hi = _cmul_add(an_ref[0, :1, hc], an_ref[1, :1, hc], h0[:, :LANES], h0[:, LANES:],
                               local[:, :LANES], local[:, LANES:])
            hT_ref[:, c0:c0 + LANES] = hr
            hT_ref[:, c0 + LANES:c0 + PAIR_COLS] = hi
            st_scr[:, q * PAIR_COLS:(q + 1) * PAIR_COLS] = h0
        _s5_emit_tile(u_ref, st_scr, m_ref, whyt_ref, dt_ref, g_ref, j)


def _s5_rows(u16, m, ws, why, an, d_tiled, h0, rows):
    n = u16.shape[0]
    row = lambda w: pl.BlockSpec((rows, w), lambda i: (i, 0))
    return pl.pallas_call(
        _s5_rows_kernel,
        grid=(n // rows,),
        in_specs=[row(CHUNK_LANES), _const_spec(m.shape), _const_spec(ws.shape), _const_spec(why.shape),
                  _const_spec(an.shape), _const_spec(d_tiled.shape), row(STATE_COLS)],
        out_specs=[row(CHUNK_LANES), row(STATE_COLS)],
        out_shape=[jax.ShapeDtypeStruct(u16.shape, BF16), jax.ShapeDtypeStruct((n, STATE_COLS), F32)],
        scratch_shapes=[pltpu.VMEM((rows, TILE_COLS), F32)],
        compiler_params=_params("arbitrary"),
        name="s5_rows",
    )(u16, m, ws, why, an, d_tiled, h0)


def _post_kernel(g_ref, ma_ref, gs_ref, x_ref, permt_ref, wglu_ref, wout_ref, gffn_ref, wup_ref, wdown_ref,
                 gfin_ref, out_ref, *, n_tok):
    g = _load_chunk_rows(g_ref, permt_ref, n_tok)
    glu = jnp.dot(g, wglu_ref[...], preferred_element_type=F32)
    s_out = glu[:, :D_MODEL] * jax.nn.sigmoid(glu[:, D_MODEL:])
    merged = ma_ref[...] + gs_ref[...].astype(F32) * s_out
    x1 = x_ref[...] + jnp.dot(merged.astype(BF16), wout_ref[...], preferred_element_type=F32)
    h2 = _rmsnorm(x1, gffn_ref[...]).astype(BF16)
    x2 = x1
    for c in range(0, D_FF, FF_CHUNK):
        up = jnp.dot(h2, wup_ref[:, c:c + FF_CHUNK], preferred_element_type=F32)
        act = jnp.square(jnp.maximum(up, 0.0)).astype(BF16)
        x2 = x2 + jnp.dot(act, wdown_ref[c:c + FF_CHUNK, :], preferred_element_type=F32)
    out_ref[...] = _rmsnorm(x2, gfin_ref[...])


def _post(g, ma, gs, x, permt, w_glu, w_out, g_ffn, w_up, w_down, g_final, tm, n_tok):
    n = x.shape[0]
    row = pl.BlockSpec((tm, D_MODEL), lambda i: (i, 0))
    g_spec = pl.BlockSpec((tm // n_tok, CHUNK_LANES), lambda i: (i, 0))
    return pl.pallas_call(
        functools.partial(_post_kernel, n_tok=n_tok),
        grid=(n // tm,),
        in_specs=[g_spec, row, row, row, _const_spec(permt.shape), _const_spec(w_glu.shape),
                  _const_spec(w_out.shape), _const_spec((1, D_MODEL)), _const_spec(w_up.shape),
                  _const_spec(w_down.shape), _const_spec((1, D_MODEL))],
        out_specs=row,
        out_shape=jax.ShapeDtypeStruct((n, D_MODEL), F32),
        compiler_params=_params("arbitrary"),
        name="post",
    )(g, ma, gs, x, permt, w_glu, w_out, g_ffn, w_up, w_down, g_final)


def _tile(n, pref):
    t = pref
    while n % t:
        t //= 2
    return t


def kernel(x_prompt, x_sample, cache_k, cache_v, state_ssm_re, state_ssm_im, g_mix, w_in, attn_sinks,
           w_attn_o, ssm_lambda_re, ssm_lambda_im, ssm_log_dt, ssm_b_re, ssm_b_im, ssm_c_re, ssm_c_im,
           ssm_d, w_glu, w_out, g_ffn, w_up, w_down, g_final):
    batch, seq, _ = x_prompt.shape
    db, dec_seq, _ = x_sample.shape
    assert w_in.shape[0] == 1, "one layer"
    assert seq % PERM_ROWS == 0 and dec_seq in (1, 2, 4, 8) and (db * dec_seq) % PERM_ROWS == 0

    vec = lambda v: v.reshape(1, D_MODEL).astype(F32)
    w_in_b, wo_b = w_in[0].astype(BF16), w_attn_o[0].astype(BF16)
    w_glu_b, w_out_b = w_glu[0].astype(BF16), w_out[0].astype(BF16)
    w_up_b, w_down_b = w_up[0].astype(BF16), w_down[0].astype(BF16)
    sinks = attn_sinks[0].astype(F32)
    s5p = (ssm_lambda_re[0], ssm_lambda_im[0], ssm_log_dt[0], ssm_b_re[0], ssm_b_im[0], ssm_c_re[0], ssm_c_im[0])
    d_skip = jnp.tile(ssm_d[0].astype(F32).reshape(SSM_GROUPS, 1, SSM_CH), (1, 1, CHUNK))
    perm = _chunk_perm(CHUNK)
    post_w = (w_glu_b, w_out_b, vec(g_ffn[0]), w_up_b, w_down_b, vec(g_final))

    xp = x_prompt.reshape(batch * seq, D_MODEL)
    q, kv, u16, ga, gs = _proj(xp, vec(g_mix[0]), w_in_b, perm, _tile(batch * seq, 512), CHUNK)
    ma = _attn_prompt(sinks, q, kv, ga, wo_b, batch, seq, _tile(seq, 512))
    m, ws, ws_n, whyt, mk, pw, an = _s5_prep(*s5p, dec_seq)
    n_rows = seq // CHUNK
    g16, hT = _s5_chain(u16, m, ws, whyt, mk, pw, d_skip, jnp.zeros((batch, 1, STATE_COLS), F32),
                        batch, n_rows, _tile(n_rows, 128))
    y_prompt = _post(g16, ma, gs, xp, perm.T, *post_w, _tile(batch * seq, 512), CHUNK).reshape(batch, seq, D_MODEL)
    kv_last = kv.reshape(batch, seq, 2 * D_KV)[:, seq - WINDOW:]
    k_prompt = kv_last[..., :D_KV].reshape(1, batch, WINDOW, N_KV_HEADS, HEAD_DIM)
    v_prompt = kv_last[..., D_KV:].reshape(1, batch, WINDOW, N_KV_HEADS, HEAD_DIM)
    hr, hi = _state_split(hT[:, 0])
    ssm_re_prompt, ssm_im_prompt = hr[None], hi[None]

    ns = db * dec_seq
    xs = x_sample.reshape(ns, D_MODEL)
    perm_s = _chunk_perm(dec_seq)
    q, kv, u16, ga, gs = _proj(xs, vec(g_mix[0]), w_in_b, perm_s, _tile(ns, 512), dec_seq)
    tpad = SUBLANES
    pad3 = lambda v: jnp.pad(v.reshape(db, dec_seq, -1).astype(F32), ((0, 0), (0, tpad - dec_seq), (0, 0)))
    ga_pad = pad3(ga).astype(BF16).reshape(db * tpad, D_MODEL)
    ma_pad, k_new, v_new = _attn_sample(
        sinks, pad3(q), pad3(kv), cache_k[0].reshape(db, WINDOW, D_KV), cache_v[0].reshape(db, WINDOW, D_KV),
        ga_pad, wo_b, _tile(db, 16), dec_seq)
    ma = ma_pad.reshape(db, tpad, D_MODEL)[:, :dec_seq].reshape(ns, D_MODEL)
    h0 = _state_cols(state_ssm_re[0].astype(F32), state_ssm_im[0].astype(F32))
    g16, hT = _s5_rows(u16, m, ws_n, whyt, an, d_skip, h0, _tile(db, 64))
    y_sample = _post(g16, ma, gs, xs, perm_s.T, *post_w, PERM_ROWS, dec_seq).reshape(db, dec_seq, D_MODEL)
    k_sample = k_new.reshape(1, db, WINDOW, N_KV_HEADS, HEAD_DIM)
    v_sample = v_new.reshape(1, db, WINDOW, N_KV_HEADS, HEAD_DIM)
    hr, hi = _state_split(hT)
    ssm_re_sample, ssm_im_sample = hr[None], hi[None]

    return (y_prompt, y_sample, k_prompt, v_prompt, ssm_re_prompt, ssm_im_prompt,
            k_sample, v_sample, ssm_re_sample, ssm_im_sample)
```

```python
import functools
import math

import jax
import jax.numpy as jnp
from jax import lax
from jax.experimental import pallas as pl
from jax.experimental.pallas import tpu as pltpu

F32 = jnp.float32
BF16 = jnp.bfloat16

D_MODEL = 1024
HEAD_DIM = 64
N_HEADS = 16
N_KV_HEADS = 4
WINDOW = 128
D_KV = N_KV_HEADS * HEAD_DIM
SSM_CH = 16
SSM_GROUPS = 64
SSM_STATE = 64
D_FF = 4 * D_MODEL
FF_CHUNK = 1024
RMS_EPS = 1e-5
LOG2E = 1.4426950408889634
Q_SCALE = HEAD_DIM ** -0.5 * LOG2E

LANES = 128
SUBLANES = 8
BF16_ROWS = 16
N_LANE_TILES = D_MODEL // LANES
GROUPS_PER_TILE = LANES // SSM_CH
CHUNK = 16
CHUNK_LANES = CHUNK * D_MODEL
GROUP_IO = CHUNK * SSM_CH
N_PAIRS = SSM_GROUPS // 2
PAIRS_PER_TILE = GROUPS_PER_TILE // 2
PAIR_COLS = 2 * LANES
TILE_COLS = PAIRS_PER_TILE * PAIR_COLS
STATE_COLS = N_PAIRS * PAIR_COLS
HALF_COLS = STATE_COLS // 2
PERM_ROWS = CHUNK * BF16_ROWS
VMEM_LIMIT = 56 * 1024 * 1024

_Q0, _KV0, _U0, _GA0, _GS0, _END = 0, 1024, 1536, 2560, 3584, 4608


def _rmsnorm(x, g):
    return x * lax.rsqrt(jnp.mean(x * x, axis=-1, keepdims=True) + RMS_EPS) * g


def _params(*sem):
    return pltpu.CompilerParams(dimension_semantics=sem, vmem_limit_bytes=VMEM_LIMIT)


def _const_spec(shape):
    nd = len(shape)
    return pl.BlockSpec(shape, lambda *_: (0,) * nd, pipeline_mode=pl.Buffered(1))


def _chunk_perm(n_tok):
    n_chunks = PERM_ROWS // n_tok
    r = jnp.arange(PERM_ROWS)
    tok = (r % n_chunks) * n_tok + r // n_chunks
    return (tok[:, None] == jnp.arange(PERM_ROWS)[None, :]).astype(BF16)


def _piece_transpose(cols, masks):
    for d, msk in zip((4, 2, 1), masks):
        new = list(cols)
        for v in range(GROUPS_PER_TILE):
            if v & d == 0:
                a, b = cols[v], cols[v + d]
                new[v] = jnp.where(msk, pltpu.roll(b, SSM_CH * d, 1), a)
                new[v + d] = jnp.where(msk, b, pltpu.roll(a, LANES - SSM_CH * d, 1))
        cols = new
    return cols


def _piece_masks(rows):
    piece = lax.broadcasted_iota(jnp.int32, (rows, LANES), 1) // SSM_CH
    return [(piece & d) != 0 for d in (4, 2, 1)]


def _store_chunk_rows(x, perm_ref, out_ref, n_tok):
    n_chunks = PERM_ROWS // n_tok
    masks = _piece_masks(n_chunks)
    zero = jnp.zeros((n_chunks, LANES), F32)
    for hb in range(x.shape[0] // PERM_ROWS):
        rows = slice(hb * n_chunks, (hb + 1) * n_chunks)
        xp = jnp.dot(perm_ref[...], x[hb * PERM_ROWS:(hb + 1) * PERM_ROWS],
                     preferred_element_type=F32)
        for j in range(N_LANE_TILES):
            for hf in range(2):
                live = 8 * hf < n_tok
                cols = [xp[t * n_chunks:(t + 1) * n_chunks, j * LANES:(j + 1) * LANES] if t < n_tok else zero
                        for t in range(8 * hf, 8 * hf + 8)]
                for gl, col in enumerate(_piece_transpose(cols, masks) if live else cols):
                    c0 = (GROUPS_PER_TILE * j + gl) * GROUP_IO + hf * LANES
                    out_ref[rows, c0:c0 + LANES] = col.astype(BF16)


def _load_chunk_rows(in_ref, permt_ref, n_tok):
    n_chunks = PERM_ROWS // n_tok
    masks = _piece_masks(n_chunks)
    blocks = []
    for hb in range(in_ref.shape[0] // n_chunks):
        rows = slice(hb * n_chunks, (hb + 1) * n_chunks)
        tiles = [[None] * N_LANE_TILES for _ in range(n_tok)]
        for j in range(N_LANE_TILES):
            for hf in range(-(-n_tok // 8)):
                cols = []
                for gl in range(GROUPS_PER_TILE):
                    c0 = (GROUPS_PER_TILE * j + gl) * GROUP_IO + hf * LANES
                    cols.append(in_ref[rows, c0:c0 + LANES].astype(F32))
                for k, col in enumerate(_piece_transpose(cols, masks)):
                    if 8 * hf + k < n_tok:
                        tiles[8 * hf + k][j] = col.astype(BF16)
        xp = jnp.concatenate([jnp.concatenate(row, axis=1) for row in tiles], axis=0)
        blocks.append(jnp.dot(permt_ref[...], xp, preferred_element_type=F32).astype(BF16))
    return blocks[0] if len(blocks) == 1 else jnp.concatenate(blocks, axis=0)


def _two_part_specs(tm, n_first):
    first = lambda rows, w: pl.BlockSpec((rows, w), lambda i: (jnp.minimum(i, n_first - 1), 0))
    second = lambda rows, w: pl.BlockSpec((rows, w), lambda i: (jnp.maximum(i - n_first, 0), 0),
                                          pipeline_mode=pl.Buffered(1))
    return first, second


def _proj_kernel(xp_ref, xs_ref, g_ref, w_ref, permp_ref, perms_ref, q_ref, kv_ref, up_ref, us_ref, ga_ref,
                 gs_ref, *, n_prompt_tiles, dec_seq):
    in_prompt = pl.program_id(0) < n_prompt_tiles
    x = jnp.where(in_prompt, xp_ref[...], xs_ref[...])
    h = _rmsnorm(x, g_ref[...]).astype(BF16)

    def seg(lo, hi):
        return jnp.dot(h, w_ref[:, lo:hi], preferred_element_type=F32)

    q_ref[...] = (seg(_Q0, _KV0) * Q_SCALE).astype(BF16)
    kv_ref[...] = seg(_KV0, _U0)
    u = seg(_U0, _GA0).astype(BF16)
    ga_ref[...] = jax.nn.sigmoid(seg(_GA0, _GS0)).astype(BF16)
    gs_ref[...] = jax.nn.sigmoid(seg(_GS0, _END)).astype(BF16)

    @pl.when(in_prompt)
    def _():
        _store_chunk_rows(u, permp_ref, up_ref, CHUNK)

    @pl.when(jnp.logical_not(in_prompt))
    def _():
        _store_chunk_rows(u, perms_ref, us_ref, dec_seq)


def _proj(xp, xs, g_mix, w_in, perm_p, perm_s, tm, dec_seq):
    n_p, n_s = xp.shape[0], xs.shape[0]
    n = n_p + n_s
    first, second = _two_part_specs(tm, n_p // tm)
    row = lambda w: pl.BlockSpec((tm, w), lambda i: (i, 0))
    return pl.pallas_call(
        functools.partial(_proj_kernel, n_prompt_tiles=n_p // tm, dec_seq=dec_seq),
        grid=(n // tm,),
        in_specs=[first(tm, D_MODEL), second(tm, D_MODEL), _const_spec((1, D_MODEL)), _const_spec(w_in.shape),
                  _const_spec(perm_p.shape), _const_spec(perm_s.shape)],
        out_specs=[row(D_MODEL), row(2 * D_KV), first(tm // CHUNK, CHUNK_LANES),
                   second(tm // dec_seq, CHUNK_LANES), row(D_MODEL), row(D_MODEL)],
        out_shape=[jax.ShapeDtypeStruct((n, D_MODEL), BF16),
                   jax.ShapeDtypeStruct((n, 2 * D_KV), F32),
                   jax.ShapeDtypeStruct((n_p // CHUNK, CHUNK_LANES), BF16),
                   jax.ShapeDtypeStruct((n_s // dec_seq, CHUNK_LANES), BF16),
                   jax.ShapeDtypeStruct((n, D_MODEL), BF16),
                   jax.ShapeDtypeStruct((n, D_MODEL), BF16)],
        compiler_params=_params("arbitrary"),
        name="proj",
    )(xp, xs, g_mix, w_in, perm_p, perm_s)


def _dup_heads(tile):
    lo = lax.broadcasted_iota(jnp.int32, tile.shape, tile.ndim - 1) < HEAD_DIM
    rolled = pltpu.roll(tile, HEAD_DIM, tile.ndim - 1)
    return (jnp.where(lo, tile, rolled).astype(BF16), jnp.where(lo, rolled, tile).astype(BF16))


def _sink_softmax(s, sink, axis=-1):
    sink2 = sink * LOG2E
    m = jnp.maximum(jnp.max(s, axis=axis, keepdims=True), sink2)
    p = jnp.exp2(s - m)
    denom = jnp.sum(p, axis=axis, keepdims=True) + jnp.exp2(sink2 - m)
    return p.astype(BF16), 1.0 / denom


def _attn_prompt_kernel(sinks_ref, q_ref, kvc_ref, kvp_ref, ga_ref, wo_ref, out_ref,
                        kd_scr, vd_scr, bias_scr, attn_scr, *, tq):
    i = pl.program_id(1)
    kv_full = jnp.concatenate([kvp_ref[...], kvc_ref[...]], axis=0)
    for t in range(2):
        kd_scr[2 * t], kd_scr[2 * t + 1] = _dup_heads(kv_full[:, t * LANES:(t + 1) * LANES])
        vd_scr[2 * t], vd_scr[2 * t + 1] = _dup_heads(kv_full[:, D_KV + t * LANES:D_KV + (t + 1) * LANES])

    qi = lax.broadcasted_iota(jnp.int32, (WINDOW, 2 * WINDOW), 0)
    si = lax.broadcasted_iota(jnp.int32, (WINDOW, 2 * WINDOW), 1)
    band = (si > qi) & (si <= qi + WINDOW)
    in_block = si >= WINDOW
    lo = lax.broadcasted_iota(jnp.int32, (WINDOW, LANES), 1) < HEAD_DIM

    def attend(jb):
        q0 = pl.multiple_of(jb * WINDOW, WINDOW)
        has_prev = (i > 0) | (jb > 0)
        bias_scr[...] = jnp.where(band & (in_block | has_prev), 0.0, -jnp.inf)

        def head_pair(hp, c):
            j = hp // 2
            c0 = pl.multiple_of(hp * LANES, LANES)
            kd = kd_scr[j, pl.ds(q0, 2 * WINDOW), :]
            vd = vd_scr[j, pl.ds(q0, 2 * WINDOW), :]
            qp = q_ref[pl.ds(q0, WINDOW), pl.ds(c0, LANES)]
            outs = []
            for par in range(2):
                qm = jnp.where(lo if par == 0 else ~lo, qp, jnp.zeros_like(qp))
                s = lax.dot_general(qm, kd, (((1,), (1,)), ((), ())), preferred_element_type=F32)
                p, inv = _sink_softmax(s + bias_scr[...], sinks_ref[2 * hp + par])
                outs.append(jnp.dot(p, vd, preferred_element_type=F32) * inv)
            attn_scr[pl.ds(q0, WINDOW), pl.ds(c0, LANES)] = jnp.where(lo, outs[0], outs[1]).astype(BF16)
            return c

        lax.fori_loop(0, N_HEADS // 2, head_pair, 0, unroll=8)

    def project(jb):
        rows = pl.ds(pl.multiple_of(jb * WINDOW, WINDOW), WINDOW)
        a_out = jnp.dot(attn_scr[rows, :], wo_ref[...], preferred_element_type=F32)
        out_ref[rows, :] = ga_ref[rows, :].astype(F32) * a_out

    def step(jb, carry):
        project(jb - 1)
        attend(jb)
        return carry

    n_blocks = tq // WINDOW
    attend(jnp.int32(0))
    lax.fori_loop(1, n_blocks, step, 0)
    project(jnp.int32(n_blocks - 1))


def _attn_prompt(sinks, q, kv, ga, wo, batch, seq, tq):
    nq = seq // tq
    bpt = tq // WINDOW
    row = lambda w: pl.BlockSpec((tq, w), lambda b, i: (b * nq + i, 0))
    prev = pl.BlockSpec((WINDOW, 2 * D_KV),
                        lambda b, i: (jnp.maximum((b * nq + i) * bpt - 1, 0), 0))
    return pl.pallas_call(
        functools.partial(_attn_prompt_kernel, tq=tq),
        grid=(batch, nq),
        in_specs=[pl.BlockSpec(memory_space=pltpu.SMEM), row(D_MODEL), row(2 * D_KV), prev,
                  row(D_MODEL), _const_spec(wo.shape)],
        out_specs=row(D_MODEL),
        out_shape=jax.ShapeDtypeStruct((batch * seq, D_MODEL), F32),
        scratch_shapes=[pltpu.VMEM((N_KV_HEADS, WINDOW + tq, LANES), BF16),
                        pltpu.VMEM((N_KV_HEADS, WINDOW + tq, LANES), BF16),
                        pltpu.VMEM((WINDOW, 2 * WINDOW), F32),
                        pltpu.VMEM((tq, D_MODEL), BF16)],
        compiler_params=_params("arbitrary", "arbitrary"),
        name="attn_prompt",
    )(sinks, q, kv, kv, ga, wo)


def _attn_sample_kernel(sinks_ref, q_ref, kvn_ref, ck_ref, cv_ref, ga_ref, wo_ref,
                        out_ref, ko_ref, vo_ref, attn_scr, *, bb, tpad, dec_seq):
    nk = WINDOW + tpad
    lo3 = lax.broadcasted_iota(jnp.int32, (bb, tpad, LANES), 2) < HEAD_DIM
    row = lax.broadcasted_iota(jnp.int32, (4 * tpad, nk), 0)
    si = lax.broadcasted_iota(jnp.int32, (4 * tpad, nk), 1)
    tq = row % tpad
    valid = (si > tq) & (si <= tq + WINDOW) & (si < WINDOW + dec_seq)
    hrow = lax.broadcasted_iota(jnp.int32, (4 * tpad, 1), 0) // tpad

    for t in range(2):
        sl = slice(t * LANES, (t + 1) * LANES)
        kk = jnp.concatenate([ck_ref[:, :, sl], kvn_ref[:, :, sl]], axis=1)
        vv = jnp.concatenate([cv_ref[:, :, sl],
                              kvn_ref[:, :, D_KV + t * LANES:D_KV + (t + 1) * LANES]], axis=1)
        ko_ref[:, :, sl] = kk[:, dec_seq:dec_seq + WINDOW, :]
        vo_ref[:, :, sl] = vv[:, dec_seq:dec_seq + WINDOW, :]
        kds = _dup_heads(kk)
        vds = _dup_heads(vv)
        for par_kv in range(2):
            j = 2 * t + par_kv
            kd, vd = kds[par_kv], vds[par_kv]
            parts = []
            for r in range(2):
                c0 = j * 2 * LANES + r * LANES
                qp = q_ref[:, :, c0:c0 + LANES]
                parts += [jnp.where(lo3, qp, 0.0), jnp.where(lo3, 0.0, qp)]
            lhs = jnp.concatenate(parts, axis=1).astype(BF16)
            s = jnp.einsum('bqd,bkd->bqk', lhs, kd, preferred_element_type=F32)
            sink = jnp.zeros((4 * tpad, 1), F32)
            for g in range(4):
                sink = jnp.where(hrow == g, sinks_ref[4 * j + g], sink)
            p, inv = _sink_softmax(jnp.where(valid[None], s, -jnp.inf), sink[None])
            o = jnp.einsum('bqk,bkd->bqd', p, vd, preferred_element_type=F32) * inv
            for r in range(2):
                c0 = j * 2 * LANES + r * LANES
                o_even = o[:, (2 * r) * tpad:(2 * r + 1) * tpad, :]
                o_odd = o[:, (2 * r + 1) * tpad:(2 * r + 2) * tpad, :]
                attn_scr[:, :, c0:c0 + LANES] = jnp.where(lo3, o_even, o_odd)

    attn = attn_scr[...].reshape(bb * tpad, D_MODEL).astype(BF16)
    a_out = jnp.dot(attn, wo_ref[...], preferred_element_type=F32)
    out_ref[...] = ga_ref[...].astype(F32) * a_out


def _attn_sample(sinks, q3, kvn3, ck, cv, ga2, wo, bb, dec_seq):
    db, tpad, _ = q3.shape
    blk3 = lambda r, w: pl.BlockSpec((bb, r, w), lambda i: (i, 0, 0))
    row = pl.BlockSpec((bb * tpad, D_MODEL), lambda i: (i, 0))
    return pl.pallas_call(
        functools.partial(_attn_sample_kernel, bb=bb, tpad=tpad, dec_seq=dec_seq),
        grid=(db // bb,),
        in_specs=[pl.BlockSpec(memory_space=pltpu.SMEM), blk3(tpad, D_MODEL), blk3(tpad, 2 * D_KV),
                  blk3(WINDOW, D_KV), blk3(WINDOW, D_KV), row, _const_spec(wo.shape)],
        out_specs=[row, blk3(WINDOW, D_KV), blk3(WINDOW, D_KV)],
        out_shape=[jax.ShapeDtypeStruct((db * tpad, D_MODEL), F32),
                   jax.ShapeDtypeStruct((db, WINDOW, D_KV), F32),
                   jax.ShapeDtypeStruct((db, WINDOW, D_KV), F32)],
        scratch_shapes=[pltpu.VMEM((bb, tpad, D_MODEL), F32)],
        compiler_params=_params("arbitrary"),
        name="attn_sample",
    )(sinks, q3, kvn3, ck, cv, ga2, wo)


def _s5_discretize(lam_re, lam_im, log_dt, b_re, b_im):
    dt = jnp.exp(log_dt)[:, None]
    decay = jnp.exp(lam_re * dt)
    ab_re = decay * jnp.cos(lam_im * dt)
    ab_im = decay * jnp.sin(lam_im * dt)
    nr, ni = ab_re - 1.0, ab_im
    den = lam_re * lam_re + lam_im * lam_im
    f_re = ((nr * lam_re + ni * lam_im) / den)[..., None]
    f_im = ((ni * lam_re - nr * lam_im) / den)[..., None]
    return ab_re, ab_im, f_re * b_re - f_im * b_im, f_re * b_im + f_im * b_re


def _pair_cols(a):
    return a.reshape(a.shape[:-2] + (N_PAIRS, LANES))


def _state_cols(re, im):
    return jnp.stack([_pair_cols(re), _pair_cols(im)], axis=-2).reshape(re.shape[:-2] + (STATE_COLS,))


def _state_split(h):
    h = h.reshape(h.shape[:-1] + (N_PAIRS, 2, 2, SSM_STATE))
    unpair = lambda a: a.reshape(a.shape[:-3] + (SSM_GROUPS, SSM_STATE))
    return unpair(h[..., 0, :, :]), unpair(h[..., 1, :, :])


def _shift_lanes(x, n):
    a, b = x[:, :LANES], x[:, LANES:]
    lane = lax.broadcasted_iota(jnp.int32, a.shape, 1)
    if n == 0:
        return x
    if n >= LANES:
        r = n - LANES
        hi = a if r == 0 else jnp.where(lane < r, 0.0, pltpu.roll(a, r, 1))
        return jnp.concatenate([jnp.zeros_like(a), hi], axis=1)
    ra, rb = pltpu.roll(a, n, 1), pltpu.roll(b, n, 1)
    return jnp.concatenate([jnp.where(lane < n, 0.0, ra), jnp.where(lane < n, ra, rb)], axis=1)


def _dot_nt_split(lhs, rhs):
    nt = lambda a, b: lax.dot_general(a, b, (((1,), (1,)), ((), ())), preferred_element_type=F32)
    l_hi, r_hi = lhs.astype(BF16), rhs.astype(BF16)
    l_lo = (lhs - l_hi.astype(F32)).astype(BF16)
    r_lo = (rhs - r_hi.astype(F32)).astype(BF16)
    return nt(l_hi, r_hi) + nt(l_hi, r_lo) + nt(l_lo, r_hi)


def _s5_prep_kernel(a_ref, c_ref, bt_ref, m_ref, ws_ref, wsn_ref, whyt_ref, mk_ref, pw_ref, an_ref, *, n_tok):
    ar, ai = a_ref[0, 0], a_ref[0, 1]
    cr, ci = c_ref[0, 0], c_ref[0, 1]
    br, bi = bt_ref[0, 0], bt_ref[0, 1]
    pr, pi = [jnp.ones_like(ar)], [jnp.zeros_like(ai)]
    for _ in range(CHUNK):
        pr, pi = pr + [pr[-1] * ar - pi[-1] * ai], pi + [pr[-1] * ai + pi[-1] * ar]
    first = lax.broadcasted_iota(jnp.int32, (SSM_CH, LANES), 1) < SSM_STATE
    ca = [(cr * pr[t] - ci * pi[t], -(cr * pi[t] + ci * pr[t])) for t in range(CHUNK + 1)]
    ca_full = jnp.concatenate([jnp.concatenate(ca[t], axis=1) for t in range(CHUNK)], axis=0)
    zero = jnp.zeros((SSM_CH, LANES), F32)
    for e in range(2):
        own = first if e == 0 else ~first
        pick = lambda v: jnp.where(own, v, zero)
        for t in range(CHUNK):
            rows = slice(t * SSM_CH, (t + 1) * SSM_CH)
            whyt_ref[e, rows, :LANES] = pick(ca[t + 1][0]).astype(BF16)
            whyt_ref[e, rows, LANES:] = pick(ca[t + 1][1]).astype(BF16)
            for ref, k in ((ws_ref, CHUNK - 1 - t), (wsn_ref, n_tok - 1 - t)):
                if k >= 0:
                    ref[e, rows, :LANES] = pick(br * pr[k] - bi * pi[k]).astype(BF16)
                    ref[e, rows, LANES:] = pick(br * pi[k] + bi * pr[k]).astype(BF16)
                else:
                    ref[e, rows, :] = jnp.zeros((SSM_CH, PAIR_COLS), BF16)
        k_row = _dot_nt_split(jnp.concatenate([pick(br), pick(bi)], axis=1), ca_full)
        for s in range(CHUNK):
            m_ref[e, s * SSM_CH:(s + 1) * SSM_CH, :] = _shift_lanes(k_row, s * SSM_CH).astype(BF16)
    sub = lax.broadcasted_iota(jnp.int32, (SUBLANES, LANES), 0)
    qr, qi = [pr[CHUNK]], [pi[CHUNK]]
    for _ in range(SUBLANES - 1):
        qr, qi = qr + [qr[-1] * pr[CHUNK] - qi[-1] * pi[CHUNK]], qi + [qr[-1] * pi[CHUNK] + qi[-1] * pr[CHUNK]]
    for part, q in enumerate((qr, qi)):
        rows = jnp.zeros((SUBLANES, LANES), F32)
        for s in range(SUBLANES):
            rows = jnp.where(sub == s, q[s], rows)
        pw_ref[part] = rows
        for k, shift in enumerate((1, 2, 4)):
            mk_ref[k, part] = jnp.where(sub >= shift, q[shift - 1], 0.0)
    an_ref[0] = jnp.broadcast_to(pr[n_tok], (SUBLANES, LANES))
    an_ref[1] = jnp.broadcast_to(pi[n_tok], (SUBLANES, LANES))


def _s5_prep(lam_re, lam_im, log_dt, b_re, b_im, c_re, c_im, n_tok):
    ab_re, ab_im, bb_re, bb_im = _s5_discretize(lam_re, lam_im, log_dt, b_re, b_im)
    pair = lambda v: v.reshape(N_PAIRS, 2, v.shape[1], SSM_STATE).transpose(0, 2, 1, 3).reshape(
        N_PAIRS, v.shape[1], LANES)
    a = jnp.stack([pair(ab_re[:, None, :]), pair(ab_im[:, None, :])], axis=1)
    c = jnp.stack([pair(c_re), pair(c_im)], axis=1)
    bt = jnp.stack([pair(bb_re.transpose(0, 2, 1)), pair(bb_im.transpose(0, 2, 1))], axis=1)
    blk4 = lambda r: pl.BlockSpec((1, 2, r, LANES), lambda q: (q, 0, 0, 0))
    w_spec = pl.BlockSpec((2, GROUP_IO, PAIR_COLS), lambda q: (q, 0, 0))
    w_shape = jax.ShapeDtypeStruct((SSM_GROUPS, GROUP_IO, PAIR_COLS), BF16)
    return pl.pallas_call(
        functools.partial(_s5_prep_kernel, n_tok=n_tok),
        grid=(N_PAIRS,),
        in_specs=[blk4(1), blk4(SSM_CH), blk4(SSM_CH)],
        out_specs=[w_spec, w_spec, w_spec, w_spec,
                   pl.BlockSpec((3, 2, SUBLANES, LANES), lambda q: (0, 0, 0, q)),
                   pl.BlockSpec((2, SUBLANES, LANES), lambda q: (0, 0, q)),
                   pl.BlockSpec((2, SUBLANES, LANES), lambda q: (0, 0, q))],
        out_shape=[w_shape, w_shape, w_shape, w_shape,
                   jax.ShapeDtypeStruct((3, 2, SUBLANES, HALF_COLS), F32),
                   jax.ShapeDtypeStruct((2, SUBLANES, HALF_COLS), F32),
                   jax.ShapeDtypeStruct((2, SUBLANES, HALF_COLS), F32)],
        compiler_params=_params("arbitrary"),
        name="s5_prep",
    )(a, c, bt)


def _group_io(ref, g):
    return ref.at[:, g * GROUP_IO:(g + 1) * GROUP_IO]


def _s5_emit_tile(u_ref, st_scr, m_ref, whyt_ref, dt_ref, g_ref, j):
    for gl in range(GROUPS_PER_TILE):
        g = GROUPS_PER_TILE * j + gl
        q0 = (gl // 2) * PAIR_COLS
        u_g = _group_io(u_ref, g)[...]
        y = (jnp.dot(u_g, m_ref[g], preferred_element_type=F32)
             + lax.dot_general(st_scr[:, q0:q0 + PAIR_COLS].astype(BF16), whyt_ref[g],
                               (((1,), (1,)), ((), ())), preferred_element_type=F32)
             + dt_ref[g] * u_g.astype(F32))
        _group_io(g_ref, g)[...] = jax.nn.gelu(y).astype(BF16)


def _s5_local_states(u_ref, ws_ref, j, q):
    g = GROUPS_PER_TILE * j + 2 * q
    return (jnp.dot(_group_io(u_ref, g)[...], ws_ref[g], preferred_element_type=F32)
            + jnp.dot(_group_io(u_ref, g + 1)[...], ws_ref[g + 1], preferred_element_type=F32))


def _cmul_add(ar, ai, hr, hi, xr, xi):
    return ar * hr - ai * hi + xr, ar * hi + ai * hr + xi


def _s5_chain_kernel(u_ref, m_ref, ws_ref, whyt_ref, mk_ref, pw_ref, dt_ref, h0_ref, g_ref, hT_ref,
                     carry_scr, st_scr, *, rows):
    @pl.when(pl.program_id(1) == 0)
    def _():
        carry_scr[...] = jnp.broadcast_to(h0_ref[0], (SUBLANES, STATE_COLS))

    row0 = lax.broadcasted_iota(jnp.int32, (SUBLANES, LANES), 0) == 0
    last = lambda h: jnp.broadcast_to(h[SUBLANES - 1:, :], (SUBLANES, LANES))

    for j in range(N_LANE_TILES):
        for q in range(PAIRS_PER_TILE):
            st_scr[:, q * PAIR_COLS:(q + 1) * PAIR_COLS] = _s5_local_states(u_ref, ws_ref, j, q)

        def step(r, carry):
            r0 = pl.multiple_of(r * SUBLANES, SUBLANES)
            out = []
            for q in range(PAIRS_PER_TILE):
                re_c = slice(q * PAIR_COLS, q * PAIR_COLS + LANES)
                im_c = slice(q * PAIR_COLS + LANES, (q + 1) * PAIR_COLS)
                hc = slice((j * PAIRS_PER_TILE + q) * LANES, (j * PAIRS_PER_TILE + q + 1) * LANES)
                xr = st_scr[pl.ds(r0, SUBLANES), re_c]
                xi = st_scr[pl.ds(r0, SUBLANES), im_c]
                for k, shift in enumerate((1, 2, 4)):
                    xr, xi = _cmul_add(mk_ref[k, 0, :, hc], mk_ref[k, 1, :, hc],
                                       pltpu.roll(xr, shift, 0), pltpu.roll(xi, shift, 0), xr, xi)
                cr, ci = carry[q]
                hr, hi = _cmul_add(pw_ref[0, :, hc], pw_ref[1, :, hc], cr, ci, xr, xi)
                st_scr[pl.ds(r0, SUBLANES), re_c] = jnp.where(row0, cr, pltpu.roll(hr, 1, 0))
                st_scr[pl.ds(r0, SUBLANES), im_c] = jnp.where(row0, ci, pltpu.roll(hi, 1, 0))
                out.append((last(hr), last(hi)))
            return tuple(out)

        c0 = j * TILE_COLS
        init = tuple((carry_scr[:, c0 + q * PAIR_COLS:c0 + q * PAIR_COLS + LANES],
                      carry_scr[:, c0 + q * PAIR_COLS + LANES:c0 + (q + 1) * PAIR_COLS])
                     for q in range(PAIRS_PER_TILE))
        final = lax.fori_loop(0, rows // SUBLANES, step, init)
        for q in range(PAIRS_PER_TILE):
            carry_scr[:, c0 + q * PAIR_COLS:c0 + q * PAIR_COLS + LANES] = final[q][0]
            carry_scr[:, c0 + q * PAIR_COLS + LANES:c0 + (q + 1) * PAIR_COLS] = final[q][1]

        _s5_emit_tile(u_ref, st_scr, m_ref, whyt_ref, dt_ref, g_ref, j)

    hT_ref[0] = carry_scr[...]


def _s5_chain(u16, m, ws, why, mk, pw, d_tiled, h0, batch, n_rows, rows):
    nblk = n_rows // rows
    row = pl.BlockSpec((rows, CHUNK_LANES), lambda b, i: (b * nblk + i, 0))
    return pl.pallas_call(
        functools.partial(_s5_chain_kernel, rows=rows),
        grid=(batch, nblk),
        in_specs=[row, _const_spec(m.shape), _const_spec(ws.shape), _const_spec(why.shape),
                  _const_spec(mk.shape), _const_spec(pw.shape), _const_spec(d_tiled.shape),
                  pl.BlockSpec((1, 1, STATE_COLS), lambda b, i: (b, 0, 0))],
        out_specs=[row, pl.BlockSpec((1, SUBLANES, STATE_COLS), lambda b, i: (b, 0, 0))],
        out_shape=[jax.ShapeDtypeStruct(u16.shape, BF16),
                   jax.ShapeDtypeStruct((batch, SUBLANES, STATE_COLS), F32)],
        scratch_shapes=[pltpu.VMEM((SUBLANES, STATE_COLS), F32), pltpu.VMEM((rows, TILE_COLS), F32)],
        compiler_params=_params("arbitrary", "arbitrary"),
        name="s5_chain",
    )(u16, m, ws, why, mk, pw, d_tiled, h0)


def _s5_rows_kernel(u_ref, m_ref, ws_ref, whyt_ref, an_ref, dt_ref, h0_ref, g_ref, hT_ref,
                    st_scr):
    for j in range(N_LANE_TILES):
        for q in range(PAIRS_PER_TILE):
            c0 = j * TILE_COLS + q * PAIR_COLS
            hc = slice((j * PAIRS_PER_TILE + q) * LANES, (j * PAIRS_PER_TILE + q + 1) * LANES)
            h0 = h0_ref[:, c0:c0 + PAIR_COLS]
            local = _s5_local_states(u_ref, ws_ref, j, q)
            hr, hi = _cmul_add(an_ref[0, :1, hc], an_ref[1, :1, hc], h0[:, :LANES], h0[:, LANES:],
                               local[:, :LANES], local[:, LANES:])
            hT_ref[:, c0:c0 + LANES] = hr
            hT_ref[:, c0 + LANES:c0 + PAIR_COLS] = hi
            st_scr[:, q * PAIR_COLS:(q + 1) * PAIR_COLS] = h0
        _s5_emit_tile(u_ref, st_scr, m_ref, whyt_ref, dt_ref, g_ref, j)


def _s5_rows(u16, m, ws, why, an, d_tiled, h0, rows):
    n = u16.shape[0]
    row = lambda w: pl.BlockSpec((rows, w), lambda i: (i, 0))
    return pl.pallas_call(
        _s5_rows_kernel,
        grid=(n // rows,),
        in_specs=[row(CHUNK_LANES), _const_spec(m.shape), _const_spec(ws.shape), _const_spec(why.shape),
                  _const_spec(an.shape), _const_spec(d_tiled.shape), row(STATE_COLS)],
        out_specs=[row(CHUNK_LANES), row(STATE_COLS)],
        out_shape=[jax.ShapeDtypeStruct(u16.shape, BF16), jax.ShapeDtypeStruct((n, STATE_COLS), F32)],
        scratch_shapes=[pltpu.VMEM((rows, TILE_COLS), F32)],
        compiler_params=_params("arbitrary"),
        name="s5_rows",
    )(u16, m, ws, why, an, d_tiled, h0)


def _post_kernel(gp_ref, gsm_ref, map_ref, mas_ref, gate_ref, xp_ref, xs_ref, permtp_ref, permts_ref,
                 wglu_ref, wout_ref, gffn_ref, wup_ref, wdown_ref, gfin_ref, yp_ref, ys_ref, g_scr,
                 *, n_prompt_tiles, dec_seq):
    in_prompt = pl.program_id(0) < n_prompt_tiles

    @pl.when(in_prompt)
    def _():
        g_scr[...] = _load_chunk_rows(gp_ref, permtp_ref, CHUNK)

    @pl.when(jnp.logical_not(in_prompt))
    def _():
        g_scr[...] = _load_chunk_rows(gsm_ref, permts_ref, dec_seq)

    glu = jnp.dot(g_scr[...], wglu_ref[...], preferred_element_type=F32)
    s_out = glu[:, :D_MODEL] * jax.nn.sigmoid(glu[:, D_MODEL:])
    merged = jnp.where(in_prompt, map_ref[...], mas_ref[...]) + gate_ref[...].astype(F32) * s_out
    x = jnp.where(in_prompt, xp_ref[...], xs_ref[...])
    x1 = x + jnp.dot(merged.astype(BF16), wout_ref[...], preferred_element_type=F32)
    h2 = _rmsnorm(x1, gffn_ref[...]).astype(BF16)
    x2 = x1
    for c in range(0, D_FF, FF_CHUNK):
        up = jnp.dot(h2, wup_ref[:, c:c + FF_CHUNK], preferred_element_type=F32)
        act = jnp.square(jnp.maximum(up, 0.0)).astype(BF16)
        x2 = x2 + jnp.dot(act, wdown_ref[c:c + FF_CHUNK, :], preferred_element_type=F32)
    y = _rmsnorm(x2, gfin_ref[...])

    @pl.when(in_prompt)
    def _():
        yp_ref[...] = y

    @pl.when(jnp.logical_not(in_prompt))
    def _():
        ys_ref[...] = y


def _post(g_p, g_s, ma_p, ma_s, gate, xp, xs, permt_p, permt_s, w_glu, w_out, g_ffn, w_up, w_down, g_final,
          tm, dec_seq):
    n_p, n_s = xp.shape[0], xs.shape[0]
    first, second = _two_part_specs(tm, n_p // tm)
    row = pl.BlockSpec((tm, D_MODEL), lambda i: (i, 0))
    return pl.pallas_call(
        functools.partial(_post_kernel, n_prompt_tiles=n_p // tm, dec_seq=dec_seq),
        grid=((n_p + n_s) // tm,),
        in_specs=[first(tm // CHUNK, CHUNK_LANES), second(tm // dec_seq, CHUNK_LANES),
                  first(tm, D_MODEL), second(tm, D_MODEL), row, first(tm, D_MODEL), second(tm, D_MODEL),
                  _const_spec(permt_p.shape), _const_spec(permt_s.shape), _const_spec(w_glu.shape),
                  _const_spec(w_out.shape), _const_spec((1, D_MODEL)), _const_spec(w_up.shape),
                  _const_spec(w_down.shape), _const_spec((1, D_MODEL))],
        out_specs=[first(tm, D_MODEL), second(tm, D_MODEL)],
        out_shape=[jax.ShapeDtypeStruct((n_p, D_MODEL), F32), jax.ShapeDtypeStruct((n_s, D_MODEL), F32)],
        scratch_shapes=[pltpu.VMEM((tm, D_MODEL), BF16)],
        compiler_params=_params("arbitrary"),
        name="post",
    )(g_p, g_s, ma_p, ma_s, gate, xp, xs, permt_p, permt_s, w_glu, w_out, g_ffn, w_up, w_down, g_final)


def _tile(n, pref):
    t = pref
    while n % t:
        t //= 2
    return t


def kernel(x_prompt, x_sample, cache_k, cache_v, state_ssm_re, state_ssm_im, g_mix, w_in, attn_sinks,
           w_attn_o, ssm_lambda_re, ssm_lambda_im, ssm_log_dt, ssm_b_re, ssm_b_im, ssm_c_re, ssm_c_im,
           ssm_d, w_glu, w_out, g_ffn, w_up, w_down, g_final):
    batch, seq, _ = x_prompt.shape
    db, dec_seq, _ = x_sample.shape
    assert w_in.shape[0] == 1, "one layer"
    assert seq % PERM_ROWS == 0 and dec_seq in (1, 2, 4, 8) and (db * dec_seq) % PERM_ROWS == 0

    vec = lambda v: v.reshape(1, D_MODEL).astype(F32)
    w_in_b, wo_b = w_in[0].astype(BF16), w_attn_o[0].astype(BF16)
    w_glu_b, w_out_b = w_glu[0].astype(BF16), w_out[0].astype(BF16)
    w_up_b, w_down_b = w_up[0].astype(BF16), w_down[0].astype(BF16)
    sinks = attn_sinks[0].astype(F32)
    s5p = (ssm_lambda_re[0], ssm_lambda_im[0], ssm_log_dt[0], ssm_b_re[0], ssm_b_im[0], ssm_c_re[0], ssm_c_im[0])
    d_skip = jnp.tile(ssm_d[0].astype(F32).reshape(SSM_GROUPS, 1, SSM_CH), (1, 1, CHUNK))
    perm_p, perm_s = _chunk_perm(CHUNK), _chunk_perm(dec_seq)
    n_p, n_s = batch * seq, db * dec_seq
    xp, xs = x_prompt.reshape(n_p, D_MODEL), x_sample.reshape(n_s, D_MODEL)
    tm = _tile(math.gcd(n_p, n_s), 512)

    q, kv, u16_p, u16_s, ga, gs = _proj(xp, xs, vec(g_mix[0]), w_in_b, perm_p, perm_s, tm, dec_seq)
    m, ws, ws_n, whyt, mk, pw, an = _s5_prep(*s5p, dec_seq)

    ma_p = _attn_prompt(sinks, q, kv, ga, wo_b, batch, seq, _tile(seq, 512))
    n_rows = seq // CHUNK
    g16_p, hT = _s5_chain(u16_p, m, ws, whyt, mk, pw, d_skip, jnp.zeros((batch, 1, STATE_COLS), F32),
                          batch, n_rows, _tile(n_rows, 128))
    kv_last = kv[:n_p].reshape(batch, seq, 2 * D_KV)[:, seq - WINDOW:]
    k_prompt = kv_last[..., :D_KV].reshape(1, batch, WINDOW, N_KV_HEADS, HEAD_DIM)
    v_prompt = kv_last[..., D_KV:].reshape(1, batch, WINDOW, N_KV_HEADS, HEAD_DIM)
    hr, hi = _state_split(hT[:, 0])
    ssm_re_prompt, ssm_im_prompt = hr[None], hi[None]

    tpad = SUBLANES
    pad3 = lambda v: jnp.pad(v[n_p:].reshape(db, dec_seq, -1).astype(F32), ((0, 0), (0, tpad - dec_seq), (0, 0)))
    ga_pad = pad3(ga).astype(BF16).reshape(db * tpad, D_MODEL)
    ma_pad, k_new, v_new = _attn_sample(
        sinks, pad3(q), pad3(kv), cache_k[0].reshape(db, WINDOW, D_KV), cache_v[0].reshape(db, WINDOW, D_KV),
        ga_pad, wo_b, _tile(db, 16), dec_seq)
    ma_s = ma_pad.reshape(db, tpad, D_MODEL)[:, :dec_seq].reshape(n_s, D_MODEL)
    h0 = _state_cols(state_ssm_re[0].astype(F32), state_ssm_im[0].astype(F32))
    g16_s, hT = _s5_rows(u16_s, m, ws_n, whyt, an, d_skip, h0, _tile(db, 64))

    y_p, y_s = _post(g16_p, g16_s, ma_p, ma_s, gs, xp, xs, perm_p.T, perm_s.T, w_glu_b, w_out_b, vec(g_ffn[0]),
                     w_up_b, w_down_b, vec(g_final), tm, dec_seq)
    y_prompt, y_sample = y_p.reshape(batch, seq, D_MODEL), y_s.reshape(db, dec_seq, D_MODEL)
    k_sample = k_new.reshape(1, db, WINDOW, N_KV_HEADS, HEAD_DIM)
    v_sample = v_new.reshape(1, db, WINDOW, N_KV_HEADS, HEAD_DIM)
    hr, hi = _state_split(hT)
    ssm_re_sample, ssm_im_sample = hr[None], hi[None]

    return (y_prompt, y_sample, k_prompt, v_prompt, ssm_re_prompt, ssm_im_prompt,
            k_sample, v_sample, ssm_re_sample, ssm_im_sample)
```

```python
import functools

import jax
import jax.numpy as jnp
from jax import lax
from jax.experimental import pallas as pl
from jax.experimental.pallas import tpu as pltpu

F32 = jnp.float32
BF16 = jnp.bfloat16

D_MODEL = 1024
HEAD_DIM = 64
N_HEADS = 16
N_KV_HEADS = 4
WINDOW = 128
D_KV = N_KV_HEADS * HEAD_DIM
SSM_CH = 16
SSM_GROUPS = 64
SSM_STATE = 64
D_FF = 4 * D_MODEL
FF_CHUNK = 1024
RMS_EPS = 1e-5
LOG2E = 1.4426950408889634
Q_SCALE = HEAD_DIM ** -0.5 * LOG2E

LANES = 128
SUBLANES = 8
BF16_ROWS = 16
N_LANE_TILES = D_MODEL // LANES
GROUPS_PER_TILE = LANES // SSM_CH
CHUNK = 16
CHUNK_LANES = CHUNK * D_MODEL
GROUP_IO = CHUNK * SSM_CH
N_PAIRS = SSM_GROUPS // 2
PAIRS_PER_TILE = GROUPS_PER_TILE // 2
PAIR_COLS = 2 * LANES
TILE_COLS = PAIRS_PER_TILE * PAIR_COLS
STATE_COLS = N_PAIRS * PAIR_COLS
HALF_COLS = STATE_COLS // 2
PERM_ROWS = CHUNK * BF16_ROWS
VMEM_LIMIT = 56 * 1024 * 1024

_Q0, _KV0, _U0, _GA0, _GS0, _END = 0, 1024, 1536, 2560, 3584, 4608


def _rmsnorm(x, g):
    return x * lax.rsqrt(jnp.mean(x * x, axis=-1, keepdims=True) + RMS_EPS) * g


def _params(*sem):
    return pltpu.CompilerParams(dimension_semantics=sem, vmem_limit_bytes=VMEM_LIMIT)


def _const_spec(shape):
    nd = len(shape)
    return pl.BlockSpec(shape, lambda *_: (0,) * nd, pipeline_mode=pl.Buffered(1))


def _chunk_perm(n_tok):
    n_chunks = PERM_ROWS // n_tok
    r = jnp.arange(PERM_ROWS)
    tok = (r % n_chunks) * n_tok + r // n_chunks
    return (tok[:, None] == jnp.arange(PERM_ROWS)[None, :]).astype(BF16)


def _piece_transpose(cols, masks):
    for d, msk in zip((4, 2, 1), masks):
        new = list(cols)
        for v in range(GROUPS_PER_TILE):
            if v & d == 0:
                a, b = cols[v], cols[v + d]
                new[v] = jnp.where(msk, pltpu.roll(b, SSM_CH * d, 1), a)
                new[v + d] = jnp.where(msk, b, pltpu.roll(a, LANES - SSM_CH * d, 1))
        cols = new
    return cols


def _piece_masks(rows):
    piece = lax.broadcasted_iota(jnp.int32, (rows, LANES), 1) // SSM_CH
    return [(piece & d) != 0 for d in (4, 2, 1)]


def _store_chunk_rows(x, perm_ref, out_ref, n_tok):
    n_chunks = PERM_ROWS // n_tok
    masks = _piece_masks(n_chunks)
    zero = jnp.zeros((n_chunks, LANES), F32)
    for hb in range(x.shape[0] // PERM_ROWS):
        rows = slice(hb * n_chunks, (hb + 1) * n_chunks)
        xp = jnp.dot(perm_ref[...], x[hb * PERM_ROWS:(hb + 1) * PERM_ROWS],
                     preferred_element_type=F32)
        for j in range(N_LANE_TILES):
            for hf in range(2):
                live = 8 * hf < n_tok
                cols = [xp[t * n_chunks:(t + 1) * n_chunks, j * LANES:(j + 1) * LANES] if t < n_tok else zero
                        for t in range(8 * hf, 8 * hf + 8)]
                for gl, col in enumerate(_piece_transpose(cols, masks) if live else cols):
                    c0 = (GROUPS_PER_TILE * j + gl) * GROUP_IO + hf * LANES
                    out_ref[rows, c0:c0 + LANES] = col.astype(BF16)


def _load_chunk_rows(in_ref, permt_ref, n_tok):
    n_chunks = PERM_ROWS // n_tok
    masks = _piece_masks(n_chunks)
    blocks = []
    for hb in range(in_ref.shape[0] // n_chunks):
        rows = slice(hb * n_chunks, (hb + 1) * n_chunks)
        tiles = [[None] * N_LANE_TILES for _ in range(n_tok)]
        for j in range(N_LANE_TILES):
            for hf in range(-(-n_tok // 8)):
                cols = []
                for gl in range(GROUPS_PER_TILE):
                    c0 = (GROUPS_PER_TILE * j + gl) * GROUP_IO + hf * LANES
                    cols.append(in_ref[rows, c0:c0 + LANES].astype(F32))
                for k, col in enumerate(_piece_transpose(cols, masks)):
                    if 8 * hf + k < n_tok:
                        tiles[8 * hf + k][j] = col.astype(BF16)
        xp = jnp.concatenate([jnp.concatenate(row, axis=1) for row in tiles], axis=0)
        blocks.append(jnp.dot(permt_ref[...], xp, preferred_element_type=F32).astype(BF16))
    return blocks[0] if len(blocks) == 1 else jnp.concatenate(blocks, axis=0)


def _proj_kernel(x_ref, g_ref, w_ref, perm_ref, q_ref, kv_ref, u_ref, ga_ref, gs_ref, *, n_tok):
    h = _rmsnorm(x_ref[...], g_ref[...]).astype(BF16)

    def seg(lo, hi):
        return jnp.dot(h, w_ref[:, lo:hi], preferred_element_type=F32)

    q_ref[...] = (seg(_Q0, _KV0) * Q_SCALE).astype(BF16)
    kv_ref[...] = seg(_KV0, _U0)
    _store_chunk_rows(seg(_U0, _GA0).astype(BF16), perm_ref, u_ref, n_tok)
    ga_ref[...] = jax.nn.sigmoid(seg(_GA0, _GS0)).astype(BF16)
    gs_ref[...] = jax.nn.sigmoid(seg(_GS0, _END)).astype(BF16)


def _proj(x, g_mix, w_in, perm, tm, n_tok):
    n = x.shape[0]
    row = lambda w: pl.BlockSpec((tm, w), lambda i: (i, 0))
    return pl.pallas_call(
        functools.partial(_proj_kernel, n_tok=n_tok),
        grid=(n // tm,),
        in_specs=[row(D_MODEL), _const_spec((1, D_MODEL)), _const_spec(w_in.shape), _const_spec(perm.shape)],
        out_specs=[row(D_MODEL), row(2 * D_KV), pl.BlockSpec((tm // n_tok, CHUNK_LANES), lambda i: (i, 0)),
                   row(D_MODEL), row(D_MODEL)],
        out_shape=[jax.ShapeDtypeStruct((n, D_MODEL), BF16),
                   jax.ShapeDtypeStruct((n, 2 * D_KV), F32),
                   jax.ShapeDtypeStruct((n // n_tok, CHUNK_LANES), BF16),
                   jax.ShapeDtypeStruct((n, D_MODEL), BF16),
                   jax.ShapeDtypeStruct((n, D_MODEL), BF16)],
        compiler_params=_params("arbitrary"),
        name="proj",
    )(x, g_mix, w_in, perm)


def _dup_heads(tile):
    lo = lax.broadcasted_iota(jnp.int32, tile.shape, tile.ndim - 1) < HEAD_DIM
    rolled = pltpu.roll(tile, HEAD_DIM, tile.ndim - 1)
    return (jnp.where(lo, tile, rolled).astype(BF16), jnp.where(lo, rolled, tile).astype(BF16))


def _sink_softmax(s, sink):
    sink2 = sink * LOG2E
    m = jnp.maximum(jnp.max(s, axis=-1, keepdims=True), sink2)
    p = jnp.exp2(s - m)
    denom = jnp.sum(p, axis=-1, keepdims=True) + jnp.exp2(sink2 - m)
    return p.astype(BF16), 1.0 / denom


def _attn_prompt_kernel(sinks_ref, q_ref, kvc_ref, kvp_ref, ga_ref, wo_ref, out_ref,
                        kd_scr, vd_scr, bias_scr, attn_scr, *, tq):
    i = pl.program_id(1)
    kv_full = jnp.concatenate([kvp_ref[...], kvc_ref[...]], axis=0)
    for t in range(2):
        ke, ko = _dup_heads(kv_full[:, t * LANES:(t + 1) * LANES])
        ve, vo = _dup_heads(kv_full[:, D_KV + t * LANES:D_KV + (t + 1) * LANES])
        kd_scr[2 * t], kd_scr[2 * t + 1] = ke, ko
        vd_scr[2 * t], vd_scr[2 * t + 1] = ve, vo

    qi = lax.broadcasted_iota(jnp.int32, (WINDOW, 2 * WINDOW), 0)
    si = lax.broadcasted_iota(jnp.int32, (WINDOW, 2 * WINDOW), 1)
    band = (si > qi) & (si <= qi + WINDOW)
    in_block = si >= WINDOW
    lo = lax.broadcasted_iota(jnp.int32, (WINDOW, LANES), 1) < HEAD_DIM

    def attend(jb):
        q0 = jb * WINDOW
        has_prev = (i > 0) | (jb > 0)
        bias_scr[...] = jnp.where(band & (in_block | has_prev), 0.0, -jnp.inf)

        def head_pair(hp, c):
            j = hp // 2
            c0 = pl.multiple_of(hp * LANES, LANES)
            kd = kd_scr[j, pl.ds(q0, 2 * WINDOW), :]
            vd = vd_scr[j, pl.ds(q0, 2 * WINDOW), :]
            qp = q_ref[pl.ds(q0, WINDOW), pl.ds(c0, LANES)]
            outs = []
            for par in range(2):
                qm = jnp.where(lo if par == 0 else ~lo, qp, jnp.zeros_like(qp))
                s = lax.dot_general(qm, kd, (((1,), (1,)), ((), ())), preferred_element_type=F32)
                p, inv = _sink_softmax(s + bias_scr[...], sinks_ref[2 * hp + par])
                outs.append(jnp.dot(p, vd, preferred_element_type=F32) * inv)
            attn_scr[pl.ds(q0, WINDOW), pl.ds(c0, LANES)] = jnp.where(lo, outs[0], outs[1]).astype(BF16)
            return c

        lax.fori_loop(0, N_HEADS // 2, head_pair, 0, unroll=8)

    def project(jb):
        rows = pl.ds(jb * WINDOW, WINDOW)
        a_out = jnp.dot(attn_scr[rows, :], wo_ref[...], preferred_element_type=F32)
        out_ref[rows, :] = ga_ref[rows, :].astype(F32) * a_out

    for jb in range(tq // WINDOW):
        attend(jb)
        project(jb)


def _attn_prompt(sinks, q, kv, ga, wo, batch, seq, tq):
    nq = seq // tq
    bpt = tq // WINDOW
    row = lambda w: pl.BlockSpec((tq, w), lambda b, i: (b * nq + i, 0))
    prev = pl.BlockSpec((WINDOW, 2 * D_KV),
                        lambda b, i: (jnp.maximum((b * nq + i) * bpt - 1, 0), 0))
    return pl.pallas_call(
        functools.partial(_attn_prompt_kernel, tq=tq),
        grid=(batch, nq),
        in_specs=[pl.BlockSpec(memory_space=pltpu.SMEM), row(D_MODEL), row(2 * D_KV), prev,
                  row(D_MODEL), _const_spec(wo.shape)],
        out_specs=row(D_MODEL),
        out_shape=jax.ShapeDtypeStruct((batch * seq, D_MODEL), F32),
        scratch_shapes=[pltpu.VMEM((N_KV_HEADS, WINDOW + tq, LANES), BF16),
                        pltpu.VMEM((N_KV_HEADS, WINDOW + tq, LANES), BF16),
                        pltpu.VMEM((WINDOW, 2 * WINDOW), F32),
                        pltpu.VMEM((tq, D_MODEL), BF16)],
        compiler_params=_params("arbitrary", "arbitrary"),
        name="attn_prompt",
    )(sinks, q, kv, kv, ga, wo)


def _attn_sample_kernel(sinks_ref, q_ref, kvn_ref, ck_ref, cv_ref, ga_ref, wo_ref,
                        out_ref, ko_ref, vo_ref, attn_scr, *, bb, tpad, dec_seq):
    nk = WINDOW + tpad
    lo3 = lax.broadcasted_iota(jnp.int32, (bb, tpad, LANES), 2) < HEAD_DIM
    row = lax.broadcasted_iota(jnp.int32, (4 * tpad, nk), 0)
    si = lax.broadcasted_iota(jnp.int32, (4 * tpad, nk), 1)
    tq = row % tpad
    valid = (si > tq) & (si <= tq + WINDOW) & (si < WINDOW + dec_seq)
    hrow = lax.broadcasted_iota(jnp.int32, (4 * tpad, 1), 0) // tpad

    for t in range(2):
        sl = slice(t * LANES, (t + 1) * LANES)
        kk = jnp.concatenate([ck_ref[:, :, sl], kvn_ref[:, :, sl]], axis=1)
        vv = jnp.concatenate([cv_ref[:, :, sl],
                              kvn_ref[:, :, D_KV + t * LANES:D_KV + (t + 1) * LANES]], axis=1)
        ko_ref[:, :, sl] = kk[:, dec_seq:dec_seq + WINDOW, :]
        vo_ref[:, :, sl] = vv[:, dec_seq:dec_seq + WINDOW, :]
        kds = _dup_heads(kk)
        vds = _dup_heads(vv)
        for par_kv in range(2):
            j = 2 * t + par_kv
            kd, vd = kds[par_kv], vds[par_kv]
            parts = []
            for r in range(2):
                c0 = j * 2 * LANES + r * LANES
                qp = q_ref[:, :, c0:c0 + LANES]
                parts += [jnp.where(lo3, qp, 0.0), jnp.where(lo3, 0.0, qp)]
            lhs = jnp.concatenate(parts, axis=1).astype(BF16)
            s = jnp.einsum('bqd,bkd->bqk', lhs, kd, preferred_element_type=F32)
            sink = jnp.zeros((4 * tpad, 1), F32)
            for g in range(4):
                sink = jnp.where(hrow == g, sinks_ref[4 * j + g], sink)
            p, inv = _sink_softmax(jnp.where(valid[None], s, -jnp.inf), sink[None])
            o = jnp.einsum('bqk,bkd->bqd', p, vd, preferred_element_type=F32) * inv
            for r in range(2):
                c0 = j * 2 * LANES + r * LANES
                o_even = o[:, (2 * r) * tpad:(2 * r + 1) * tpad, :]
                o_odd = o[:, (2 * r + 1) * tpad:(2 * r + 2) * tpad, :]
                attn_scr[:, :, c0:c0 + LANES] = jnp.where(lo3, o_even, o_odd)

    attn = attn_scr[...].reshape(bb * tpad, D_MODEL).astype(BF16)
    a_out = jnp.dot(attn, wo_ref[...], preferred_element_type=F32)
    out_ref[...] = ga_ref[...].astype(F32) * a_out


def _attn_sample(sinks, q3, kvn3, ck, cv, ga2, wo, bb, dec_seq):
    db, tpad, _ = q3.shape
    blk3 = lambda r, w: pl.BlockSpec((bb, r, w), lambda i: (i, 0, 0))
    row = pl.BlockSpec((bb * tpad, D_MODEL), lambda i: (i, 0))
    return pl.pallas_call(
        functools.partial(_attn_sample_kernel, bb=bb, tpad=tpad, dec_seq=dec_seq),
        grid=(db // bb,),
        in_specs=[pl.BlockSpec(memory_space=pltpu.SMEM), blk3(tpad, D_MODEL), blk3(tpad, 2 * D_KV),
                  blk3(WINDOW, D_KV), blk3(WINDOW, D_KV), row, _const_spec(wo.shape)],
        out_specs=[row, blk3(WINDOW, D_KV), blk3(WINDOW, D_KV)],
        out_shape=[jax.ShapeDtypeStruct((db * tpad, D_MODEL), F32),
                   jax.ShapeDtypeStruct((db, WINDOW, D_KV), F32),
                   jax.ShapeDtypeStruct((db, WINDOW, D_KV), F32)],
        scratch_shapes=[pltpu.VMEM((bb, tpad, D_MODEL), F32)],
        compiler_params=_params("arbitrary"),
        name="attn_sample",
    )(sinks, q3, kvn3, ck, cv, ga2, wo)


def _s5_discretize(lam_re, lam_im, log_dt, b_re, b_im):
    dt = jnp.exp(log_dt)[:, None]
    decay = jnp.exp(lam_re * dt)
    ab_re = decay * jnp.cos(lam_im * dt)
    ab_im = decay * jnp.sin(lam_im * dt)
    nr, ni = ab_re - 1.0, ab_im
    den = lam_re * lam_re + lam_im * lam_im
    f_re = ((nr * lam_re + ni * lam_im) / den)[..., None]
    f_im = ((ni * lam_re - nr * lam_im) / den)[..., None]
    return ab_re, ab_im, f_re * b_re - f_im * b_im, f_re * b_im + f_im * b_re


def _pair_cols(a):
    return a.reshape(a.shape[:-2] + (N_PAIRS, LANES))


def _state_cols(re, im):
    return jnp.stack([_pair_cols(re), _pair_cols(im)], axis=-2).reshape(re.shape[:-2] + (STATE_COLS,))


def _state_split(h):
    h = h.reshape(h.shape[:-1] + (N_PAIRS, 2, 2, SSM_STATE))
    unpair = lambda a: a.reshape(a.shape[:-3] + (SSM_GROUPS, SSM_STATE))
    return unpair(h[..., 0, :, :]), unpair(h[..., 1, :, :])


def _shift_lanes(x, n):
    a, b = x[:, :LANES], x[:, LANES:]
    lane = lax.broadcasted_iota(jnp.int32, a.shape, 1)
    if n == 0:
        return x
    if n >= LANES:
        r = n - LANES
        hi = a if r == 0 else jnp.where(lane < r, 0.0, pltpu.roll(a, r, 1))
        return jnp.concatenate([jnp.zeros_like(a), hi], axis=1)
    ra, rb = pltpu.roll(a, n, 1), pltpu.roll(b, n, 1)
    return jnp.concatenate([jnp.where(lane < n, 0.0, ra), jnp.where(lane < n, ra, rb)], axis=1)


def _dot_nt_split(lhs, rhs):
    nt = lambda a, b: lax.dot_general(a, b, (((1,), (1,)), ((), ())), preferred_element_type=F32)
    l_hi, r_hi = lhs.astype(BF16), rhs.astype(BF16)
    l_lo = (lhs - l_hi.astype(F32)).astype(BF16)
    r_lo = (rhs - r_hi.astype(F32)).astype(BF16)
    return nt(l_hi, r_hi) + nt(l_hi, r_lo) + nt(l_lo, r_hi)


def _s5_prep_kernel(a_ref, c_ref, bt_ref, m_ref, ws_ref, wsn_ref, whyt_ref, mk_ref, pw_ref, an_ref, *, n_tok):
    ar, ai = a_ref[0, 0], a_ref[0, 1]
    cr, ci = c_ref[0, 0], c_ref[0, 1]
    br, bi = bt_ref[0, 0], bt_ref[0, 1]
    pr, pi = [jnp.ones_like(ar)], [jnp.zeros_like(ai)]
    for _ in range(CHUNK):
        pr, pi = pr + [pr[-1] * ar - pi[-1] * ai], pi + [pr[-1] * ai + pi[-1] * ar]
    first = lax.broadcasted_iota(jnp.int32, (SSM_CH, LANES), 1) < SSM_STATE
    ca = [(cr * pr[t] - ci * pi[t], -(cr * pi[t] + ci * pr[t])) for t in range(CHUNK + 1)]
    ca_full = jnp.concatenate([jnp.concatenate(ca[t], axis=1) for t in range(CHUNK)], axis=0)
    zero = jnp.zeros((SSM_CH, LANES), F32)
    for e in range(2):
        own = first if e == 0 else ~first
        pick = lambda v: jnp.where(own, v, zero)
        for t in range(CHUNK):
            rows = slice(t * SSM_CH, (t + 1) * SSM_CH)
            whyt_ref[e, rows, :LANES] = pick(ca[t + 1][0]).astype(BF16)
            whyt_ref[e, rows, LANES:] = pick(ca[t + 1][1]).astype(BF16)
            for ref, k in ((ws_ref, CHUNK - 1 - t), (wsn_ref, n_tok - 1 - t)):
                if k >= 0:
                    ref[e, rows, :LANES] = pick(br * pr[k] - bi * pi[k]).astype(BF16)
                    ref[e, rows, LANES:] = pick(br * pi[k] + bi * pr[k]).astype(BF16)
                else:
                    ref[e, rows, :] = jnp.zeros((SSM_CH, PAIR_COLS), BF16)
        k_row = _dot_nt_split(jnp.concatenate([pick(br), pick(bi)], axis=1), ca_full)
        for s in range(CHUNK):
            m_ref[e, s * SSM_CH:(s + 1) * SSM_CH, :] = _shift_lanes(k_row, s * SSM_CH).astype(BF16)
    sub = lax.broadcasted_iota(jnp.int32, (SUBLANES, LANES), 0)
    qr, qi = [pr[CHUNK]], [pi[CHUNK]]
    for _ in range(SUBLANES - 1):
        qr, qi = qr + [qr[-1] * pr[CHUNK] - qi[-1] * pi[CHUNK]], qi + [qr[-1] * pi[CHUNK] + qi[-1] * pr[CHUNK]]
    for part, q in enumerate((qr, qi)):
        rows = jnp.zeros((SUBLANES, LANES), F32)
        for s in range(SUBLANES):
            rows = jnp.where(sub == s, q[s], rows)
        pw_ref[part] = rows
        for k, shift in enumerate((1, 2, 4)):
            mk_ref[k, part] = jnp.where(sub >= shift, q[shift - 1], 0.0)
    an_ref[0] = jnp.broadcast_to(pr[n_tok], (SUBLANES, LANES))
    an_ref[1] = jnp.broadcast_to(pi[n_tok], (SUBLANES, LANES))


def _s5_prep(lam_re, lam_im, log_dt, b_re, b_im, c_re, c_im, n_tok):
    ab_re, ab_im, bb_re, bb_im = _s5_discretize(lam_re, lam_im, log_dt, b_re, b_im)
    pair = lambda v: v.reshape(N_PAIRS, 2, v.shape[1], SSM_STATE).transpose(0, 2, 1, 3).reshape(
        N_PAIRS, v.shape[1], LANES)
    a = jnp.stack([pair(ab_re[:, None, :]), pair(ab_im[:, None, :])], axis=1)
    c = jnp.stack([pair(c_re), pair(c_im)], axis=1)
    bt = jnp.stack([pair(bb_re.transpose(0, 2, 1)), pair(bb_im.transpose(0, 2, 1))], axis=1)
    blk4 = lambda r: pl.BlockSpec((1, 2, r, LANES), lambda q: (q, 0, 0, 0))
    w_spec = pl.BlockSpec((2, GROUP_IO, PAIR_COLS), lambda q: (q, 0, 0))
    w_shape = jax.ShapeDtypeStruct((SSM_GROUPS, GROUP_IO, PAIR_COLS), BF16)
    return pl.pallas_call(
        functools.partial(_s5_prep_kernel, n_tok=n_tok),
        grid=(N_PAIRS,),
        in_specs=[blk4(1), blk4(SSM_CH), blk4(SSM_CH)],
        out_specs=[w_spec, w_spec, w_spec, w_spec,
                   pl.BlockSpec((3, 2, SUBLANES, LANES), lambda q: (0, 0, 0, q)),
                   pl.BlockSpec((2, SUBLANES, LANES), lambda q: (0, 0, q)),
                   pl.BlockSpec((2, SUBLANES, LANES), lambda q: (0, 0, q))],
        out_shape=[w_shape, w_shape, w_shape, w_shape,
                   jax.ShapeDtypeStruct((3, 2, SUBLANES, HALF_COLS), F32),
                   jax.ShapeDtypeStruct((2, SUBLANES, HALF_COLS), F32),
                   jax.ShapeDtypeStruct((2, SUBLANES, HALF_COLS), F32)],
        compiler_params=_params("arbitrary"),
        name="s5_prep",
    )(a, c, bt)


def _group_io(ref, g):
    return ref.at[:, g * GROUP_IO:(g + 1) * GROUP_IO]


def _s5_emit_tile(u_ref, st_scr, m_ref, whyt_ref, dt_ref, g_ref, j):
    for gl in range(GROUPS_PER_TILE):
        g = GROUPS_PER_TILE * j + gl
        q0 = (gl // 2) * PAIR_COLS
        u_g = _group_io(u_ref, g)[...]
        y = (jnp.dot(u_g, m_ref[g], preferred_element_type=F32)
             + lax.dot_general(st_scr[:, q0:q0 + PAIR_COLS].astype(BF16), whyt_ref[g],
                               (((1,), (1,)), ((), ())), preferred_element_type=F32)
             + dt_ref[g] * u_g.astype(F32))
        _group_io(g_ref, g)[...] = jax.nn.gelu(y).astype(BF16)


def _s5_local_states(u_ref, ws_ref, j, q):
    g = GROUPS_PER_TILE * j + 2 * q
    return (jnp.dot(_group_io(u_ref, g)[...], ws_ref[g], preferred_element_type=F32)
            + jnp.dot(_group_io(u_ref, g + 1)[...], ws_ref[g + 1], preferred_element_type=F32))


def _cmul_add(ar, ai, hr, hi, xr, xi):
    return ar * hr - ai * hi + xr, ar * hi + ai * hr + xi


def _s5_chain_kernel(u_ref, m_ref, ws_ref, whyt_ref, mk_ref, pw_ref, dt_ref, h0_ref, g_ref, hT_ref,
                     carry_scr, st_scr, *, rows):
    @pl.when(pl.program_id(1) == 0)
    def _():
        carry_scr[...] = jnp.broadcast_to(h0_ref[0], (SUBLANES, STATE_COLS))

    row0 = lax.broadcasted_iota(jnp.int32, (SUBLANES, LANES), 0) == 0
    last = lambda h: jnp.broadcast_to(h[SUBLANES - 1:, :], (SUBLANES, LANES))

    for j in range(N_LANE_TILES):
        for q in range(PAIRS_PER_TILE):
            st_scr[:, q * PAIR_COLS:(q + 1) * PAIR_COLS] = _s5_local_states(u_ref, ws_ref, j, q)

        def step(r, carry):
            r0 = pl.multiple_of(r * SUBLANES, SUBLANES)
            out = []
            for q in range(PAIRS_PER_TILE):
                re_c = slice(q * PAIR_COLS, q * PAIR_COLS + LANES)
                im_c = slice(q * PAIR_COLS + LANES, (q + 1) * PAIR_COLS)
                hc = slice((j * PAIRS_PER_TILE + q) * LANES, (j * PAIRS_PER_TILE + q + 1) * LANES)
                xr = st_scr[pl.ds(r0, SUBLANES), re_c]
                xi = st_scr[pl.ds(r0, SUBLANES), im_c]
                for k, shift in enumerate((1, 2, 4)):
                    xr, xi = _cmul_add(mk_ref[k, 0, :, hc], mk_ref[k, 1, :, hc],
                                       pltpu.roll(xr, shift, 0), pltpu.roll(xi, shift, 0), xr, xi)
                cr, ci = carry[q]
                hr, hi = _cmul_add(pw_ref[0, :, hc], pw_ref[1, :, hc], cr, ci, xr, xi)
                st_scr[pl.ds(r0, SUBLANES), re_c] = jnp.where(row0, cr, pltpu.roll(hr, 1, 0))
                st_scr[pl.ds(r0, SUBLANES), im_c] = jnp.where(row0, ci, pltpu.roll(hi, 1, 0))
                out.append((last(hr), last(hi)))
            return tuple(out)

        c0 = j * TILE_COLS
        init = tuple((carry_scr[:, c0 + q * PAIR_COLS:c0 + q * PAIR_COLS + LANES],
                      carry_scr[:, c0 + q * PAIR_COLS + LANES:c0 + (q + 1) * PAIR_COLS])
                     for q in range(PAIRS_PER_TILE))
        final = lax.fori_loop(0, rows // SUBLANES, step, init, unroll=True)
        for q in range(PAIRS_PER_TILE):
            carry_scr[:, c0 + q * PAIR_COLS:c0 + q * PAIR_COLS + LANES] = final[q][0]
            carry_scr[:, c0 + q * PAIR_COLS + LANES:c0 + (q + 1) * PAIR_COLS] = final[q][1]

        _s5_emit_tile(u_ref, st_scr, m_ref, whyt_ref, dt_ref, g_ref, j)

    hT_ref[0] = carry_scr[...]


def _s5_chain(u16, m, ws, why, mk, pw, d_tiled, h0, batch, n_rows, rows):
    nblk = n_rows // rows
    row = pl.BlockSpec((rows, CHUNK_LANES), lambda b, i: (b * nblk + i, 0))
    return pl.pallas_call(
        functools.partial(_s5_chain_kernel, rows=rows),
        grid=(batch, nblk),
        in_specs=[row, _const_spec(m.shape), _const_spec(ws.shape), _const_spec(why.shape),
                  _const_spec(mk.shape), _const_spec(pw.shape), _const_spec(d_tiled.shape),
                  pl.BlockSpec((1, 1, STATE_COLS), lambda b, i: (b, 0, 0))],
        out_specs=[row, pl.BlockSpec((1, SUBLANES, STATE_COLS), lambda b, i: (b, 0, 0))],
        out_shape=[jax.ShapeDtypeStruct(u16.shape, BF16),
                   jax.ShapeDtypeStruct((batch, SUBLANES, STATE_COLS), F32)],
        scratch_shapes=[pltpu.VMEM((SUBLANES, STATE_COLS), F32), pltpu.VMEM((rows, TILE_COLS), F32)],
        compiler_params=_params("arbitrary", "arbitrary"),
        name="s5_chain",
    )(u16, m, ws, why, mk, pw, d_tiled, h0)


def _s5_rows_kernel(u_ref, m_ref, ws_ref, whyt_ref, an_ref, dt_ref, h0_ref, g_ref, hT_ref,
                    st_scr):
    for j in range(N_LANE_TILES):
        for q in range(PAIRS_PER_TILE):
            c0 = j * TILE_COLS + q * PAIR_COLS
            hc = slice((j * PAIRS_PER_TILE + q) * LANES, (j * PAIRS_PER_TILE + q + 1) * LANES)
            h0 = h0_ref[:, c0:c0 + PAIR_COLS]
            local = _s5_local_states(u_ref, ws_ref, j, q)
            hr, hi = _cmul_add(an_ref[0, :1, hc], an_ref[1, :1, hc], h0[:, :LANES], h0[:, LANES:],
                               local[:, :LANES], local[:, LANES:])
            hT_ref[:, c0:c0 + LANES] = hr
            hT_ref[:, c0 + LANES:c0 + PAIR_COLS] = hi
            st_scr[:, q * PAIR_COLS:(q + 1) * PAIR_COLS] = h0
        _s5_emit_tile(u_ref, st_scr, m_ref, whyt_ref, dt_ref, g_ref, j)


def _s5_rows(u16, m, ws, why, an, d_tiled, h0, rows):
    n = u16.shape[0]
    row = lambda w: pl.BlockSpec((rows, w), lambda i: (i, 0))
    return pl.pallas_call(
        _s5_rows_kernel,
        grid=(n // rows,),
        in_specs=[row(CHUNK_LANES), _const_spec(m.shape), _const_spec(ws.shape), _const_spec(why.shape),
                  _const_spec(an.shape), _const_spec(d_tiled.shape), row(STATE_COLS)],
        out_specs=[row(CHUNK_LANES), row(STATE_COLS)],
        out_shape=[jax.ShapeDtypeStruct(u16.shape, BF16), jax.ShapeDtypeStruct((n, STATE_COLS), F32)],
        scratch_shapes=[pltpu.VMEM((rows, TILE_COLS), F32)],
        compiler_params=_params("arbitrary"),
        name="s5_rows",
    )(u16, m, ws, why, an, d_tiled, h0)


def _post_kernel(g_ref, ma_ref, gs_ref, x_ref, permt_ref, wglu_ref, wout_ref, gffn_ref, wup_ref, wdown_ref,
                 gfin_ref, out_ref, *, n_tok):
    g = _load_chunk_rows(g_ref, permt_ref, n_tok)
    glu = jnp.dot(g, wglu_ref[...], preferred_element_type=F32)
    s_out = glu[:, :D_MODEL] * jax.nn.sigmoid(glu[:, D_MODEL:])
    merged = ma_ref[...] + gs_ref[...].astype(F32) * s_out
    x1 = x_ref[...] + jnp.dot(merged.astype(BF16), wout_ref[...], preferred_element_type=F32)
    h2 = _rmsnorm(x1, gffn_ref[...]).astype(BF16)
    x2 = x1
    for c in range(0, D_FF, FF_CHUNK):
        up = jnp.dot(h2, wup_ref[:, c:c + FF_CHUNK], preferred_element_type=F32)
        act = jnp.square(jnp.maximum(up, 0.0)).astype(BF16)
        x2 = x2 + jnp.dot(act, wdown_ref[c:c + FF_CHUNK, :], preferred_element_type=F32)
    out_ref[...] = _rmsnorm(x2, gfin_ref[...])


def _post(g, ma, gs, x, permt, w_glu, w_out, g_ffn, w_up, w_down, g_final, tm, n_tok):
    n = x.shape[0]
    row = pl.BlockSpec((tm, D_MODEL), lambda i: (i, 0))
    g_spec = pl.BlockSpec((tm // n_tok, CHUNK_LANES), lambda i: (i, 0))
    return pl.pallas_call(
        functools.partial(_post_kernel, n_tok=n_tok),
        grid=(n // tm,),
        in_specs=[g_spec, row, row, row, _const_spec(permt.shape), _const_spec(w_glu.shape),
                  _const_spec(w_out.shape), _const_spec((1, D_MODEL)), _const_spec(w_up.shape),
                  _const_spec(w_down.shape), _const_spec((1, D_MODEL))],
        out_specs=row,
        out_shape=jax.ShapeDtypeStruct((n, D_MODEL), F32),
        compiler_params=_params("arbitrary"),
        name="post",
    )(g, ma, gs, x, permt, w_glu, w_out, g_ffn, w_up, w_down, g_final)


def _tile(n, pref):
    t = pref
    while n % t:
        t //= 2
    return t


def kernel(x_prompt, x_sample, cache_k, cache_v, state_ssm_re, state_ssm_im, g_mix, w_in, attn_sinks,
           w_attn_o, ssm_lambda_re, ssm_lambda_im, ssm_log_dt, ssm_b_re, ssm_b_im, ssm_c_re, ssm_c_im,
           ssm_d, w_glu, w_out, g_ffn, w_up, w_down, g_final):
    batch, seq, _ = x_prompt.shape
    db, dec_seq, _ = x_sample.shape
    assert w_in.shape[0] == 1, "one layer"
    assert seq % PERM_ROWS == 0 and dec_seq in (1, 2, 4, 8) and (db * dec_seq) % PERM_ROWS == 0

    vec = lambda v: v.reshape(1, D_MODEL).astype(F32)
    w_in_b, wo_b = w_in[0].astype(BF16), w_attn_o[0].astype(BF16)
    w_glu_b, w_out_b = w_glu[0].astype(BF16), w_out[0].astype(BF16)
    w_up_b, w_down_b = w_up[0].astype(BF16), w_down[0].astype(BF16)
    sinks = attn_sinks[0].astype(F32)
    s5p = (ssm_lambda_re[0], ssm_lambda_im[0], ssm_log_dt[0], ssm_b_re[0], ssm_b_im[0], ssm_c_re[0], ssm_c_im[0])
    d_skip = jnp.tile(ssm_d[0].astype(F32).reshape(SSM_GROUPS, 1, SSM_CH), (1, 1, CHUNK))
    perm = _chunk_perm(CHUNK)
    post_w = (w_glu_b, w_out_b, vec(g_ffn[0]), w_up_b, w_down_b, vec(g_final))

    xp = x_prompt.reshape(batch * seq, D_MODEL)
    q, kv, u16, ga, gs = _proj(xp, vec(g_mix[0]), w_in_b, perm, _tile(batch * seq, 512), CHUNK)
    ma = _attn_prompt(sinks, q, kv, ga, wo_b, batch, seq, _tile(seq, 512))
    m, ws, ws_n, whyt, mk, pw, an = _s5_prep(*s5p, dec_seq)
    n_rows = seq // CHUNK
    g16, hT = _s5_chain(u16, m, ws, whyt, mk, pw, d_skip, jnp.zeros((batch, 1, STATE_COLS), F32),
                        batch, n_rows, _tile(n_rows, 128))
    y_prompt = _post(g16, ma, gs, xp, perm.T, *post_w, _tile(batch * seq, 512), CHUNK).reshape(batch, seq, D_MODEL)
    kv_last = kv.reshape(batch, seq, 2 * D_KV)[:, seq - WINDOW:]
    k_prompt = kv_last[..., :D_KV].reshape(1, batch, WINDOW, N_KV_HEADS, HEAD_DIM)
    v_prompt = kv_last[..., D_KV:].reshape(1, batch, WINDOW, N_KV_HEADS, HEAD_DIM)
    hr, hi = _state_split(hT[:, 0])
    ssm_re_prompt, ssm_im_prompt = hr[None], hi[None]

    ns = db * dec_seq
    xs = x_sample.reshape(ns, D_MODEL)
    perm_s = _chunk_perm(dec_seq)
    q, kv, u16, ga, gs = _proj(xs, vec(g_mix[0]), w_in_b, perm_s, _tile(ns, 512), dec_seq)
    tpad = SUBLANES
    pad3 = lambda v: jnp.pad(v.reshape(db, dec_seq, -1).astype(F32), ((0, 0), (0, tpad - dec_seq), (0, 0)))
    ga_pad = pad3(ga).astype(BF16).reshape(db * tpad, D_MODEL)
    ma_pad, k_new, v_new = _attn_sample(
        sinks, pad3(q), pad3(kv), cache_k[0].reshape(db, WINDOW, D_KV), cache_v[0].reshape(db, WINDOW, D_KV),
        ga_pad, wo_b, _tile(db, 16), dec_seq)
    ma = ma_pad.reshape(db, tpad, D_MODEL)[:, :dec_seq].reshape(ns, D_MODEL)
    h0 = _state_cols(state_ssm_re[0].astype(F32), state_ssm_im[0].astype(F32))
    g16, hT = _s5_rows(u16, m, ws_n, whyt, an, d_skip, h0, _tile(db, 64))
    y_sample = _post(g16, ma, gs, xs, perm_s.T, *post_w, PERM_ROWS, dec_seq).reshape(db, dec_seq, D_MODEL)
    k_sample = k_new.reshape(1, db, WINDOW, N_KV_HEADS, HEAD_DIM)
    v_sample = v_new.reshape(1, db, WINDOW, N_KV_HEADS, HEAD_DIM)
    hr, hi = _state_split(hT)
    ssm_re_sample, ssm_im_sample = hr[None], hi[None]

    return (y_prompt, y_sample, k_prompt, v_prompt, ssm_re_prompt, ssm_im_prompt,
            k_sample, v_sample, ssm_re_sample, ssm_im_sample)
```

```python
import functools

import jax
import jax.numpy as jnp
from jax import lax
from jax.experimental import pallas as pl
from jax.experimental.pallas import tpu as pltpu

F32 = jnp.float32
BF16 = jnp.bfloat16

D_MODEL = 1024
HEAD_DIM = 64
N_HEADS = 16
N_KV_HEADS = 4
WINDOW = 128
D_KV = N_KV_HEADS * HEAD_DIM
SSM_CH = 16
SSM_GROUPS = 64
SSM_STATE = 64
D_FF = 4 * D_MODEL
FF_CHUNK = 1024
RMS_EPS = 1e-5
LOG2E = 1.4426950408889634
Q_SCALE = HEAD_DIM ** -0.5 * LOG2E

LANES = 128
SUBLANES = 8
BF16_ROWS = 16
N_LANE_TILES = D_MODEL // LANES
GROUPS_PER_TILE = LANES // SSM_CH
CHUNK = 16
CHUNK_LANES = CHUNK * D_MODEL
GROUP_IO = CHUNK * SSM_CH
N_PAIRS = SSM_GROUPS // 2
PAIRS_PER_TILE = GROUPS_PER_TILE // 2
PAIR_COLS = 2 * LANES
TILE_COLS = PAIRS_PER_TILE * PAIR_COLS
STATE_COLS = N_PAIRS * PAIR_COLS
HALF_COLS = STATE_COLS // 2
PERM_ROWS = CHUNK * BF16_ROWS
VMEM_LIMIT = 56 * 1024 * 1024

_Q0, _KV0, _U0, _GA0, _GS0, _END = 0, 1024, 1536, 2560, 3584, 4608


def _rmsnorm(x, g):
    return x * lax.rsqrt(jnp.mean(x * x, axis=-1, keepdims=True) + RMS_EPS) * g


def _params(*sem):
    return pltpu.CompilerParams(dimension_semantics=sem, vmem_limit_bytes=VMEM_LIMIT)


def _const_spec(shape):
    nd = len(shape)
    return pl.BlockSpec(shape, lambda *_: (0,) * nd, pipeline_mode=pl.Buffered(1))


def _chunk_perm(n_tok):
    n_chunks = PERM_ROWS // n_tok
    r = jnp.arange(PERM_ROWS)
    tok = (r % n_chunks) * n_tok + r // n_chunks
    return (tok[:, None] == jnp.arange(PERM_ROWS)[None, :]).astype(BF16)


def _piece_transpose(cols, masks):
    for d, msk in zip((4, 2, 1), masks):
        new = list(cols)
        for v in range(GROUPS_PER_TILE):
            if v & d == 0:
                a, b = cols[v], cols[v + d]
                new[v] = jnp.where(msk, pltpu.roll(b, SSM_CH * d, 1), a)
                new[v + d] = jnp.where(msk, b, pltpu.roll(a, LANES - SSM_CH * d, 1))
        cols = new
    return cols


def _piece_masks(rows):
    piece = lax.broadcasted_iota(jnp.int32, (rows, LANES), 1) // SSM_CH
    return [(piece & d) != 0 for d in (4, 2, 1)]


def _store_chunk_rows(x, perm_ref, out_ref, n_tok):
    n_chunks = PERM_ROWS // n_tok
    masks = _piece_masks(n_chunks)
    zero = jnp.zeros((n_chunks, LANES), F32)
    for hb in range(x.shape[0] // PERM_ROWS):
        rows = slice(hb * n_chunks, (hb + 1) * n_chunks)
        xp = jnp.dot(perm_ref[...], x[hb * PERM_ROWS:(hb + 1) * PERM_ROWS],
                     preferred_element_type=F32)
        for j in range(N_LANE_TILES):
            for hf in range(2):
                live = 8 * hf < n_tok
                cols = [xp[t * n_chunks:(t + 1) * n_chunks, j * LANES:(j + 1) * LANES] if t < n_tok else zero
                        for t in range(8 * hf, 8 * hf + 8)]
                for gl, col in enumerate(_piece_transpose(cols, masks) if live else cols):
                    c0 = (GROUPS_PER_TILE * j + gl) * GROUP_IO + hf * LANES
                    out_ref[rows, c0:c0 + LANES] = col.astype(BF16)


def _load_chunk_rows(in_ref, permt_ref, n_tok):
    n_chunks = PERM_ROWS // n_tok
    masks = _piece_masks(n_chunks)
    blocks = []
    for hb in range(in_ref.shape[0] // n_chunks):
        rows = slice(hb * n_chunks, (hb + 1) * n_chunks)
        tiles = [[None] * N_LANE_TILES for _ in range(n_tok)]
        for j in range(N_LANE_TILES):
            for hf in range(-(-n_tok // 8)):
                cols = []
                for gl in range(GROUPS_PER_TILE):
                    c0 = (GROUPS_PER_TILE * j + gl) * GROUP_IO + hf * LANES
                    cols.append(in_ref[rows, c0:c0 + LANES].astype(F32))
                for k, col in enumerate(_piece_transpose(cols, masks)):
                    if 8 * hf + k < n_tok:
                        tiles[8 * hf + k][j] = col.astype(BF16)
        xp = jnp.concatenate([jnp.concatenate(row, axis=1) for row in tiles], axis=0)
        blocks.append(jnp.dot(permt_ref[...], xp, preferred_element_type=F32).astype(BF16))
    return blocks[0] if len(blocks) == 1 else jnp.concatenate(blocks, axis=0)


def _proj_kernel(x_ref, g_ref, w_ref, perm_ref, *refs, n_tok, n_cast):
    q_ref, kv_ref, u_ref, ga_ref, gs_ref = refs[n_cast:n_cast + 5]
    h = _rmsnorm(x_ref[...], g_ref[...]).astype(BF16)

    def seg(lo, hi):
        return jnp.dot(h, w_ref[:, lo:hi], preferred_element_type=F32)

    q_ref[...] = (seg(_Q0, _KV0) * Q_SCALE).astype(BF16)
    kv_ref[...] = seg(_KV0, _U0)
    _store_chunk_rows(seg(_U0, _GA0).astype(BF16), perm_ref, u_ref, n_tok)
    ga_ref[...] = jax.nn.sigmoid(seg(_GA0, _GS0)).astype(BF16)
    gs_ref[...] = jax.nn.sigmoid(seg(_GS0, _END)).astype(BF16)
    for src, dst in zip(refs[:n_cast], refs[n_cast + 5:]):
        dst[...] = src[...].astype(BF16)


def _proj(x, g_mix, w_in, perm, tm, n_tok, cast=()):
    n = x.shape[0]
    steps = n // tm
    row = lambda w: pl.BlockSpec((tm, w), lambda i: (i, 0))
    slab = lambda w: pl.BlockSpec((w.shape[0] // steps, w.shape[1]), lambda i: (i, 0))
    assert all(w.shape[0] % (steps * BF16_ROWS) == 0 for w in cast)
    return pl.pallas_call(
        functools.partial(_proj_kernel, n_tok=n_tok, n_cast=len(cast)),
        grid=(steps,),
        in_specs=[row(D_MODEL), _const_spec((1, D_MODEL)), _const_spec(w_in.shape), _const_spec(perm.shape)]
        + [slab(w) for w in cast],
        out_specs=[row(D_MODEL), row(2 * D_KV), pl.BlockSpec((tm // n_tok, CHUNK_LANES), lambda i: (i, 0)),
                   row(D_MODEL), row(D_MODEL)] + [slab(w) for w in cast],
        out_shape=[jax.ShapeDtypeStruct((n, D_MODEL), BF16),
                   jax.ShapeDtypeStruct((n, 2 * D_KV), F32),
                   jax.ShapeDtypeStruct((n // n_tok, CHUNK_LANES), BF16),
                   jax.ShapeDtypeStruct((n, D_MODEL), BF16),
                   jax.ShapeDtypeStruct((n, D_MODEL), BF16)]
        + [jax.ShapeDtypeStruct(w.shape, BF16) for w in cast],
        compiler_params=_params("arbitrary"),
        name="proj",
    )(x, g_mix, w_in, perm, *cast)


def _dup_heads(tile):
    lo = lax.broadcasted_iota(jnp.int32, tile.shape, tile.ndim - 1) < HEAD_DIM
    rolled = pltpu.roll(tile, HEAD_DIM, tile.ndim - 1)
    return (jnp.where(lo, tile, rolled).astype(BF16), jnp.where(lo, rolled, tile).astype(BF16))


def _sink_softmax(s, sink):
    sink2 = sink * LOG2E
    m = jnp.maximum(jnp.max(s, axis=-1, keepdims=True), sink2)
    p = jnp.exp2(s - m)
    denom = jnp.sum(p, axis=-1, keepdims=True) + jnp.exp2(sink2 - m)
    return p.astype(BF16), 1.0 / denom


def _attn_prompt_kernel(sinks_ref, q_ref, kvc_ref, kvp_ref, ga_ref, wo_ref, out_ref,
                        kd_scr, vd_scr, bias_scr, attn_scr, *, tq):
    i = pl.program_id(1)
    kv_full = jnp.concatenate([kvp_ref[...], kvc_ref[...]], axis=0)
    for t in range(2):
        ke, ko = _dup_heads(kv_full[:, t * LANES:(t + 1) * LANES])
        ve, vo = _dup_heads(kv_full[:, D_KV + t * LANES:D_KV + (t + 1) * LANES])
        kd_scr[2 * t], kd_scr[2 * t + 1] = ke, ko
        vd_scr[2 * t], vd_scr[2 * t + 1] = ve, vo

    qi = lax.broadcasted_iota(jnp.int32, (WINDOW, 2 * WINDOW), 0)
    si = lax.broadcasted_iota(jnp.int32, (WINDOW, 2 * WINDOW), 1)
    band = (si > qi) & (si <= qi + WINDOW)
    in_block = si >= WINDOW
    lo = lax.broadcasted_iota(jnp.int32, (WINDOW, LANES), 1) < HEAD_DIM

    def attend(jb):
        q0 = jb * WINDOW
        has_prev = (i > 0) | (jb > 0)
        bias_scr[...] = jnp.where(band & (in_block | has_prev), 0.0, -jnp.inf)

        def head_pair(hp, c):
            j = hp // 2
            c0 = pl.multiple_of(hp * LANES, LANES)
            kd = kd_scr[j, pl.ds(q0, 2 * WINDOW), :]
            vd = vd_scr[j, pl.ds(q0, 2 * WINDOW), :]
            qp = q_ref[pl.ds(q0, WINDOW), pl.ds(c0, LANES)]
            outs = []
            for par in range(2):
                qm = jnp.where(lo if par == 0 else ~lo, qp, jnp.zeros_like(qp))
                s = lax.dot_general(qm, kd, (((1,), (1,)), ((), ())), preferred_element_type=F32)
                p, inv = _sink_softmax(s + bias_scr[...], sinks_ref[2 * hp + par])
                outs.append(jnp.dot(p, vd, preferred_element_type=F32) * inv)
            attn_scr[pl.ds(q0, WINDOW), pl.ds(c0, LANES)] = jnp.where(lo, outs[0], outs[1]).astype(BF16)
            return c

        lax.fori_loop(0, N_HEADS // 2, head_pair, 0, unroll=8)

    def project(jb):
        rows = pl.ds(jb * WINDOW, WINDOW)
        a_out = jnp.dot(attn_scr[rows, :], wo_ref[...], preferred_element_type=F32)
        out_ref[rows, :] = ga_ref[rows, :].astype(F32) * a_out

    for jb in range(tq // WINDOW):
        attend(jb)
        project(jb)


def _attn_prompt(sinks, q, kv, ga, wo, batch, seq, tq):
    nq = seq // tq
    bpt = tq // WINDOW
    row = lambda w: pl.BlockSpec((tq, w), lambda b, i: (b * nq + i, 0))
    prev = pl.BlockSpec((WINDOW, 2 * D_KV),
                        lambda b, i: (jnp.maximum((b * nq + i) * bpt - 1, 0), 0))
    return pl.pallas_call(
        functools.partial(_attn_prompt_kernel, tq=tq),
        grid=(batch, nq),
        in_specs=[pl.BlockSpec(memory_space=pltpu.SMEM), row(D_MODEL), row(2 * D_KV), prev,
                  row(D_MODEL), _const_spec(wo.shape)],
        out_specs=row(D_MODEL),
        out_shape=jax.ShapeDtypeStruct((batch * seq, D_MODEL), F32),
        scratch_shapes=[pltpu.VMEM((N_KV_HEADS, WINDOW + tq, LANES), BF16),
                        pltpu.VMEM((N_KV_HEADS, WINDOW + tq, LANES), BF16),
                        pltpu.VMEM((WINDOW, 2 * WINDOW), F32),
                        pltpu.VMEM((tq, D_MODEL), BF16)],
        compiler_params=_params("arbitrary", "arbitrary"),
        name="attn_prompt",
    )(sinks, q, kv, kv, ga, wo)


def _attn_sample_kernel(sinks_ref, q_ref, kvn_ref, ck_ref, cv_ref, ga_ref, wo_ref,
                        out_ref, ko_ref, vo_ref, attn_scr, *, bb, tpad, dec_seq):
    nk = WINDOW + tpad
    lo3 = lax.broadcasted_iota(jnp.int32, (bb, tpad, LANES), 2) < HEAD_DIM
    row = lax.broadcasted_iota(jnp.int32, (4 * tpad, nk), 0)
    si = lax.broadcasted_iota(jnp.int32, (4 * tpad, nk), 1)
    tq = row % tpad
    valid = (si > tq) & (si <= tq + WINDOW) & (si < WINDOW + dec_seq)
    hrow = lax.broadcasted_iota(jnp.int32, (4 * tpad, 1), 0) // tpad

    for t in range(2):
        sl = slice(t * LANES, (t + 1) * LANES)
        kk = jnp.concatenate([ck_ref[:, :, sl], kvn_ref[:, :, sl]], axis=1)
        vv = jnp.concatenate([cv_ref[:, :, sl],
                              kvn_ref[:, :, D_KV + t * LANES:D_KV + (t + 1) * LANES]], axis=1)
        ko_ref[:, :, sl] = kk[:, dec_seq:dec_seq + WINDOW, :]
        vo_ref[:, :, sl] = vv[:, dec_seq:dec_seq + WINDOW, :]
        kds = _dup_heads(kk)
        vds = _dup_heads(vv)
        for par_kv in range(2):
            j = 2 * t + par_kv
            kd, vd = kds[par_kv], vds[par_kv]
            parts = []
            for r in range(2):
                c0 = j * 2 * LANES + r * LANES
                qp = q_ref[:, :, c0:c0 + LANES]
                parts += [jnp.where(lo3, qp, 0.0), jnp.where(lo3, 0.0, qp)]
            lhs = jnp.concatenate(parts, axis=1).astype(BF16)
            s = jnp.einsum('bqd,bkd->bqk', lhs, kd, preferred_element_type=F32)
            sink = jnp.zeros((4 * tpad, 1), F32)
            for g in range(4):
                sink = jnp.where(hrow == g, sinks_ref[4 * j + g], sink)
            p, inv = _sink_softmax(jnp.where(valid[None], s, -jnp.inf), sink[None])
            o = jnp.einsum('bqk,bkd->bqd', p, vd, preferred_element_type=F32) * inv
            for r in range(2):
                c0 = j * 2 * LANES + r * LANES
                o_even = o[:, (2 * r) * tpad:(2 * r + 1) * tpad, :]
                o_odd = o[:, (2 * r + 1) * tpad:(2 * r + 2) * tpad, :]
                attn_scr[:, :, c0:c0 + LANES] = jnp.where(lo3, o_even, o_odd)

    attn = attn_scr[...].reshape(bb * tpad, D_MODEL).astype(BF16)
    a_out = jnp.dot(attn, wo_ref[...], preferred_element_type=F32)
    out_ref[...] = ga_ref[...].astype(F32) * a_out


def _attn_sample(sinks, q3, kvn3, ck, cv, ga2, wo, bb, dec_seq):
    db, tpad, _ = q3.shape
    blk3 = lambda r, w: pl.BlockSpec((bb, r, w), lambda i: (i, 0, 0))
    row = pl.BlockSpec((bb * tpad, D_MODEL), lambda i: (i, 0))
    return pl.pallas_call(
        functools.partial(_attn_sample_kernel, bb=bb, tpad=tpad, dec_seq=dec_seq),
        grid=(db // bb,),
        in_specs=[pl.BlockSpec(memory_space=pltpu.SMEM), blk3(tpad, D_MODEL), blk3(tpad, 2 * D_KV),
                  blk3(WINDOW, D_KV), blk3(WINDOW, D_KV), row, _const_spec(wo.shape)],
        out_specs=[row, blk3(WINDOW, D_KV), blk3(WINDOW, D_KV)],
        out_shape=[jax.ShapeDtypeStruct((db * tpad, D_MODEL), F32),
                   jax.ShapeDtypeStruct((db, WINDOW, D_KV), F32),
                   jax.ShapeDtypeStruct((db, WINDOW, D_KV), F32)],
        scratch_shapes=[pltpu.VMEM((bb, tpad, D_MODEL), F32)],
        compiler_params=_params("arbitrary"),
        name="attn_sample",
    )(sinks, q3, kvn3, ck, cv, ga2, wo)


def _s5_discretize(lam_re, lam_im, log_dt, b_re, b_im):
    dt = jnp.exp(log_dt)[:, None]
    decay = jnp.exp(lam_re * dt)
    ab_re = decay * jnp.cos(lam_im * dt)
    ab_im = decay * jnp.sin(lam_im * dt)
    nr, ni = ab_re - 1.0, ab_im
    den = lam_re * lam_re + lam_im * lam_im
    f_re = ((nr * lam_re + ni * lam_im) / den)[..., None]
    f_im = ((ni * lam_re - nr * lam_im) / den)[..., None]
    return ab_re, ab_im, f_re * b_re - f_im * b_im, f_re * b_im + f_im * b_re


def _state_split(h):
    h = h.reshape(h.shape[:-1] + (N_PAIRS, 2, 2, SSM_STATE))
    unpair = lambda a: a.reshape(a.shape[:-3] + (SSM_GROUPS, SSM_STATE))
    return unpair(h[..., 0, :, :]), unpair(h[..., 1, :, :])


def _shift_lanes(x, n):
    a, b = x[:, :LANES], x[:, LANES:]
    lane = lax.broadcasted_iota(jnp.int32, a.shape, 1)
    if n == 0:
        return x
    if n >= LANES:
        r = n - LANES
        hi = a if r == 0 else jnp.where(lane < r, 0.0, pltpu.roll(a, r, 1))
        return jnp.concatenate([jnp.zeros_like(a), hi], axis=1)
    ra, rb = pltpu.roll(a, n, 1), pltpu.roll(b, n, 1)
    return jnp.concatenate([jnp.where(lane < n, 0.0, ra), jnp.where(lane < n, ra, rb)], axis=1)


def _dot_nt_split(lhs, rhs):
    nt = lambda a, b: lax.dot_general(a, b, (((1,), (1,)), ((), ())), preferred_element_type=F32)
    l_hi, r_hi = lhs.astype(BF16), rhs.astype(BF16)
    l_lo = (lhs - l_hi.astype(F32)).astype(BF16)
    r_lo = (rhs - r_hi.astype(F32)).astype(BF16)
    return nt(l_hi, r_hi) + nt(l_hi, r_lo) + nt(l_lo, r_hi)


def _s5_prep_kernel(a_ref, c_ref, bt_ref, m_ref, ws_ref, wsn_ref, whyt_ref, mk_ref, pw_ref, an_ref, *, n_tok):
    ar, ai = a_ref[0, 0], a_ref[0, 1]
    cr, ci = c_ref[0, 0], c_ref[0, 1]
    br, bi = bt_ref[0, 0], bt_ref[0, 1]
    pr, pi = [jnp.ones_like(ar)], [jnp.zeros_like(ai)]
    for _ in range(CHUNK):
        pr, pi = pr + [pr[-1] * ar - pi[-1] * ai], pi + [pr[-1] * ai + pi[-1] * ar]
    first = lax.broadcasted_iota(jnp.int32, (SSM_CH, LANES), 1) < SSM_STATE
    ca = [(cr * pr[t] - ci * pi[t], -(cr * pi[t] + ci * pr[t])) for t in range(CHUNK + 1)]
    ca_full = jnp.concatenate([jnp.concatenate(ca[t], axis=1) for t in range(CHUNK)], axis=0)
    zero = jnp.zeros((SSM_CH, LANES), F32)
    for e in range(2):
        own = first if e == 0 else ~first
        pick = lambda v: jnp.where(own, v, zero)
        for t in range(CHUNK):
            rows = slice(t * SSM_CH, (t + 1) * SSM_CH)
            whyt_ref[e, rows, :LANES] = pick(ca[t + 1][0]).astype(BF16)
            whyt_ref[e, rows, LANES:] = pick(ca[t + 1][1]).astype(BF16)
            for ref, k in ((ws_ref, CHUNK - 1 - t), (wsn_ref, n_tok - 1 - t)):
                if k >= 0:
                    ref[e, rows, :LANES] = pick(br * pr[k] - bi * pi[k]).astype(BF16)
                    ref[e, rows, LANES:] = pick(br * pi[k] + bi * pr[k]).astype(BF16)
                else:
                    ref[e, rows, :] = jnp.zeros((SSM_CH, PAIR_COLS), BF16)
        k_row = _dot_nt_split(jnp.concatenate([pick(br), pick(bi)], axis=1), ca_full)
        for s in range(CHUNK):
            m_ref[e, s * SSM_CH:(s + 1) * SSM_CH, :] = _shift_lanes(k_row, s * SSM_CH).astype(BF16)
    sub = lax.broadcasted_iota(jnp.int32, (SUBLANES, LANES), 0)
    qr, qi = [pr[CHUNK]], [pi[CHUNK]]
    for _ in range(SUBLANES - 1):
        qr, qi = qr + [qr[-1] * pr[CHUNK] - qi[-1] * pi[CHUNK]], qi + [qr[-1] * pi[CHUNK] + qi[-1] * pr[CHUNK]]
    for part, q in enumerate((qr, qi)):
        rows = jnp.zeros((SUBLANES, LANES), F32)
        for s in range(SUBLANES):
            rows = jnp.where(sub == s, q[s], rows)
        pw_ref[part] = rows
        for k, shift in enumerate((1, 2, 4)):
            mk_ref[k, part] = jnp.where(sub >= shift, q[shift - 1], 0.0)
    an_ref[0] = jnp.broadcast_to(pr[n_tok], (SUBLANES, LANES))
    an_ref[1] = jnp.broadcast_to(pi[n_tok], (SUBLANES, LANES))


def _s5_prep(lam_re, lam_im, log_dt, b_re, b_im, c_re, c_im, n_tok):
    ab_re, ab_im, bb_re, bb_im = _s5_discretize(lam_re, lam_im, log_dt, b_re, b_im)
    pair = lambda v: v.reshape(N_PAIRS, 2, v.shape[1], SSM_STATE).transpose(0, 2, 1, 3).reshape(
        N_PAIRS, v.shape[1], LANES)
    a = jnp.stack([pair(ab_re[:, None, :]), pair(ab_im[:, None, :])], axis=1)
    c = jnp.stack([pair(c_re), pair(c_im)], axis=1)
    bt = jnp.stack([pair(bb_re.transpose(0, 2, 1)), pair(bb_im.transpose(0, 2, 1))], axis=1)
    blk4 = lambda r: pl.BlockSpec((1, 2, r, LANES), lambda q: (q, 0, 0, 0))
    w_spec = pl.BlockSpec((2, GROUP_IO, PAIR_COLS), lambda q: (q, 0, 0))
    w_shape = jax.ShapeDtypeStruct((SSM_GROUPS, GROUP_IO, PAIR_COLS), BF16)
    return pl.pallas_call(
        functools.partial(_s5_prep_kernel, n_tok=n_tok),
        grid=(N_PAIRS,),
        in_specs=[blk4(1), blk4(SSM_CH), blk4(SSM_CH)],
        out_specs=[w_spec, w_spec, w_spec, w_spec,
                   pl.BlockSpec((3, 2, SUBLANES, LANES), lambda q: (0, 0, 0, q)),
                   pl.BlockSpec((2, SUBLANES, LANES), lambda q: (0, 0, q)),
                   pl.BlockSpec((2, SUBLANES, LANES), lambda q: (0, 0, q))],
        out_shape=[w_shape, w_shape, w_shape, w_shape,
                   jax.ShapeDtypeStruct((3, 2, SUBLANES, HALF_COLS), F32),
                   jax.ShapeDtypeStruct((2, SUBLANES, HALF_COLS), F32),
                   jax.ShapeDtypeStruct((2, SUBLANES, HALF_COLS), F32)],
        compiler_params=_params("arbitrary"),
        name="s5_prep",
    )(a, c, bt)


def _group_io(ref, g):
    return ref.at[:, g * GROUP_IO:(g + 1) * GROUP_IO]


def _s5_emit_tile(u_ref, st_scr, m_ref, whyt_ref, dt_ref, g_ref, j):
    for gl in range(GROUPS_PER_TILE):
        g = GROUPS_PER_TILE * j + gl
        q0 = (gl // 2) * PAIR_COLS
        u_g = _group_io(u_ref, g)[...]
        y = (jnp.dot(u_g, m_ref[g], preferred_element_type=F32)
             + lax.dot_general(st_scr[:, q0:q0 + PAIR_COLS].astype(BF16), whyt_ref[g],
                               (((1,), (1,)), ((), ())), preferred_element_type=F32)
             + dt_ref[g] * u_g.astype(F32))
        _group_io(g_ref, g)[...] = jax.nn.gelu(y).astype(BF16)


def _s5_local_states(u_ref, ws_ref, j, q):
    g = GROUPS_PER_TILE * j + 2 * q
    return (jnp.dot(_group_io(u_ref, g)[...], ws_ref[g], preferred_element_type=F32)
            + jnp.dot(_group_io(u_ref, g + 1)[...], ws_ref[g + 1], preferred_element_type=F32))


def _cmul_add(ar, ai, hr, hi, xr, xi):
    return ar * hr - ai * hi + xr, ar * hi + ai * hr + xi


def _s5_chain_kernel(u_ref, m_ref, ws_ref, whyt_ref, mk_ref, pw_ref, dt_ref, h0_ref, g_ref, hT_ref,
                     carry_scr, st_scr, *, rows):
    @pl.when(pl.program_id(1) == 0)
    def _():
        carry_scr[...] = jnp.broadcast_to(h0_ref[0], (SUBLANES, STATE_COLS))

    row0 = lax.broadcasted_iota(jnp.int32, (SUBLANES, LANES), 0) == 0
    last = lambda h: jnp.broadcast_to(h[SUBLANES - 1:, :], (SUBLANES, LANES))

    for j in range(N_LANE_TILES):
        for q in range(PAIRS_PER_TILE):
            st_scr[:, q * PAIR_COLS:(q + 1) * PAIR_COLS] = _s5_local_states(u_ref, ws_ref, j, q)

        def step(r, carry):
            r0 = pl.multiple_of(r * SUBLANES, SUBLANES)
            out = []
            for q in range(PAIRS_PER_TILE):
                re_c = slice(q * PAIR_COLS, q * PAIR_COLS + LANES)
                im_c = slice(q * PAIR_COLS + LANES, (q + 1) * PAIR_COLS)
                hc = slice((j * PAIRS_PER_TILE + q) * LANES, (j * PAIRS_PER_TILE + q + 1) * LANES)
                xr = st_scr[pl.ds(r0, SUBLANES), re_c]
                xi = st_scr[pl.ds(r0, SUBLANES), im_c]
                for k, shift in enumerate((1, 2, 4)):
                    xr, xi = _cmul_add(mk_ref[k, 0, :, hc], mk_ref[k, 1, :, hc],
                                       pltpu.roll(xr, shift, 0), pltpu.roll(xi, shift, 0), xr, xi)
                cr, ci = carry[q]
                hr, hi = _cmul_add(pw_ref[0, :, hc], pw_ref[1, :, hc], cr, ci, xr, xi)
                st_scr[pl.ds(r0, SUBLANES), re_c] = jnp.where(row0, cr, pltpu.roll(hr, 1, 0))
                st_scr[pl.ds(r0, SUBLANES), im_c] = jnp.where(row0, ci, pltpu.roll(hi, 1, 0))
                out.append((last(hr), last(hi)))
            return tuple(out)

        c0 = j * TILE_COLS
        init = tuple((carry_scr[:, c0 + q * PAIR_COLS:c0 + q * PAIR_COLS + LANES],
                      carry_scr[:, c0 + q * PAIR_COLS + LANES:c0 + (q + 1) * PAIR_COLS])
                     for q in range(PAIRS_PER_TILE))
        final = lax.fori_loop(0, rows // SUBLANES, step, init, unroll=True)
        for q in range(PAIRS_PER_TILE):
            carry_scr[:, c0 + q * PAIR_COLS:c0 + q * PAIR_COLS + LANES] = final[q][0]
            carry_scr[:, c0 + q * PAIR_COLS + LANES:c0 + (q + 1) * PAIR_COLS] = final[q][1]

        _s5_emit_tile(u_ref, st_scr, m_ref, whyt_ref, dt_ref, g_ref, j)

    hT_ref[0] = carry_scr[...]


def _s5_chain(u16, m, ws, why, mk, pw, d_tiled, h0, batch, n_rows, rows):
    nblk = n_rows // rows
    row = pl.BlockSpec((rows, CHUNK_LANES), lambda b, i: (b * nblk + i, 0))
    return pl.pallas_call(
        functools.partial(_s5_chain_kernel, rows=rows),
        grid=(batch, nblk),
        in_specs=[row, _const_spec(m.shape), _const_spec(ws.shape), _const_spec(why.shape),
                  _const_spec(mk.shape), _const_spec(pw.shape), _const_spec(d_tiled.shape),
                  pl.BlockSpec((1, 1, STATE_COLS), lambda b, i: (b, 0, 0))],
        out_specs=[row, pl.BlockSpec((1, SUBLANES, STATE_COLS), lambda b, i: (b, 0, 0))],
        out_shape=[jax.ShapeDtypeStruct(u16.shape, BF16),
                   jax.ShapeDtypeStruct((batch, SUBLANES, STATE_COLS), F32)],
        scratch_shapes=[pltpu.VMEM((SUBLANES, STATE_COLS), F32), pltpu.VMEM((rows, TILE_COLS), F32)],
        compiler_params=_params("arbitrary", "arbitrary"),
        name="s5_chain",
    )(u16, m, ws, why, mk, pw, d_tiled, h0)


def _s5_rows_kernel(u_ref, m_ref, ws_ref, whyt_ref, an_ref, dt_ref, h0r_ref, h0i_ref, g_ref, hTr_ref, hTi_ref,
                    st_scr):
    for j in range(N_LANE_TILES):
        for q in range(PAIRS_PER_TILE):
            hc = slice((j * PAIRS_PER_TILE + q) * LANES, (j * PAIRS_PER_TILE + q + 1) * LANES)
            h0r, h0i = h0r_ref[hc, :].T, h0i_ref[hc, :].T
            local = _s5_local_states(u_ref, ws_ref, j, q)
            hr, hi = _cmul_add(an_ref[0, :1, hc], an_ref[1, :1, hc], h0r, h0i, local[:, :LANES], local[:, LANES:])
            hTr_ref[hc, :] = hr.T
            hTi_ref[hc, :] = hi.T
            st_scr[:, q * PAIR_COLS:q * PAIR_COLS + LANES] = h0r
            st_scr[:, q * PAIR_COLS + LANES:(q + 1) * PAIR_COLS] = h0i
        _s5_emit_tile(u_ref, st_scr, m_ref, whyt_ref, dt_ref, g_ref, j)


def _s5_rows(u16, m, ws, why, an, d_tiled, h0_re_t, h0_im_t, rows):
    n = u16.shape[0]
    once = pl.Buffered(1)
    row = pl.BlockSpec((rows, CHUNK_LANES), lambda i: (i, 0), pipeline_mode=once)
    col = pl.BlockSpec((HALF_COLS, rows), lambda i: (0, i), pipeline_mode=once)
    state = jax.ShapeDtypeStruct((HALF_COLS, n), F32)
    return pl.pallas_call(
        _s5_rows_kernel,
        grid=(n // rows,),
        in_specs=[row, _const_spec(m.shape), _const_spec(ws.shape), _const_spec(why.shape),
                  _const_spec(an.shape), _const_spec(d_tiled.shape), col, col],
        out_specs=[row, col, col],
        out_shape=[jax.ShapeDtypeStruct(u16.shape, BF16), state, state],
        scratch_shapes=[pltpu.VMEM((rows, TILE_COLS), F32)],
        compiler_params=_params("arbitrary"),
        name="s5_rows",
    )(u16, m, ws, why, an, d_tiled, h0_re_t, h0_im_t)


def _post_kernel(g_ref, ma_ref, gs_ref, x_ref, permt_ref, wglu_ref, wout_ref, gffn_ref, wup_ref, wdown_ref,
                 gfin_ref, out_ref, *, n_tok):
    g = _load_chunk_rows(g_ref, permt_ref, n_tok)
    glu = jnp.dot(g, wglu_ref[...], preferred_element_type=F32)
    s_out = glu[:, :D_MODEL] * jax.nn.sigmoid(glu[:, D_MODEL:])
    merged = ma_ref[...] + gs_ref[...].astype(F32) * s_out
    x1 = x_ref[...] + jnp.dot(merged.astype(BF16), wout_ref[...], preferred_element_type=F32)
    h2 = _rmsnorm(x1, gffn_ref[...]).astype(BF16)
    x2 = x1
    for c in range(0, D_FF, FF_CHUNK):
        up = jnp.dot(h2, wup_ref[:, c:c + FF_CHUNK], preferred_element_type=F32)
        act = jnp.square(jnp.maximum(up, 0.0)).astype(BF16)
        x2 = x2 + jnp.dot(act, wdown_ref[c:c + FF_CHUNK, :], preferred_element_type=F32)
    out_ref[...] = _rmsnorm(x2, gfin_ref[...])


def _post(g, ma, gs, x, permt, w_glu, w_out, g_ffn, w_up, w_down, g_final, tm, n_tok):
    n = x.shape[0]
    row = pl.BlockSpec((tm, D_MODEL), lambda i: (i, 0))
    g_spec = pl.BlockSpec((tm // n_tok, CHUNK_LANES), lambda i: (i, 0))
    return pl.pallas_call(
        functools.partial(_post_kernel, n_tok=n_tok),
        grid=(n // tm,),
        in_specs=[g_spec, row, row, row, _const_spec(permt.shape), _const_spec(w_glu.shape),
                  _const_spec(w_out.shape), _const_spec((1, D_MODEL)), _const_spec(w_up.shape),
                  _const_spec(w_down.shape), _const_spec((1, D_MODEL))],
        out_specs=row,
        out_shape=jax.ShapeDtypeStruct((n, D_MODEL), F32),
        compiler_params=_params("arbitrary"),
        name="post",
    )(g, ma, gs, x, permt, w_glu, w_out, g_ffn, w_up, w_down, g_final)


def _tile(n, pref):
    t = pref
    while n % t:
        t //= 2
    return t


def kernel(x_prompt, x_sample, cache_k, cache_v, state_ssm_re, state_ssm_im, g_mix, w_in, attn_sinks,
           w_attn_o, ssm_lambda_re, ssm_lambda_im, ssm_log_dt, ssm_b_re, ssm_b_im, ssm_c_re, ssm_c_im,
           ssm_d, w_glu, w_out, g_ffn, w_up, w_down, g_final):
    batch, seq, _ = x_prompt.shape
    db, dec_seq, _ = x_sample.shape
    assert w_in.shape[0] == 1, "one layer"
    assert seq % PERM_ROWS == 0 and dec_seq in (1, 2, 4, 8) and (db * dec_seq) % PERM_ROWS == 0

    vec = lambda v: v.reshape(1, D_MODEL).astype(F32)
    w_in_b = w_in[0].astype(BF16)
    sinks = attn_sinks[0].astype(F32)
    s5p = (ssm_lambda_re[0], ssm_lambda_im[0], ssm_log_dt[0], ssm_b_re[0], ssm_b_im[0], ssm_c_re[0], ssm_c_im[0])
    d_skip = jnp.tile(ssm_d[0].astype(F32).reshape(SSM_GROUPS, 1, SSM_CH), (1, 1, CHUNK))
    perm = _chunk_perm(CHUNK)

    xp = x_prompt.reshape(batch * seq, D_MODEL)
    q, kv, u16, ga, gs, wo_b, w_glu_b, w_out_b, w_up_b, w_down_b = _proj(
        xp, vec(g_mix[0]), w_in_b, perm, _tile(batch * seq, 512), CHUNK,
        cast=(w_attn_o[0], w_glu[0], w_out[0], w_up[0], w_down[0]))
    post_w = (w_glu_b, w_out_b, vec(g_ffn[0]), w_up_b, w_down_b, vec(g_final))
    ma = _attn_prompt(sinks, q, kv, ga, wo_b, batch, seq, _tile(seq, 512))
    m, ws, ws_n, whyt, mk, pw, an = _s5_prep(*s5p, dec_seq)
    n_rows = seq // CHUNK
    g16, hT = _s5_chain(u16, m, ws, whyt, mk, pw, d_skip, jnp.zeros((batch, 1, STATE_COLS), F32),
                        batch, n_rows, _tile(n_rows, 128))
    y_prompt = _post(g16, ma, gs, xp, perm.T, *post_w, _tile(batch * seq, 512), CHUNK).reshape(batch, seq, D_MODEL)
    kv_last = kv.reshape(batch, seq, 2 * D_KV)[:, seq - WINDOW:]
    k_prompt = kv_last[..., :D_KV].reshape(1, batch, WINDOW, N_KV_HEADS, HEAD_DIM)
    v_prompt = kv_last[..., D_KV:].reshape(1, batch, WINDOW, N_KV_HEADS, HEAD_DIM)
    hr, hi = _state_split(hT[:, 0])
    ssm_re_prompt, ssm_im_prompt = hr[None], hi[None]

    ns = db * dec_seq
    xs = x_sample.reshape(ns, D_MODEL)
    perm_s = _chunk_perm(dec_seq)
    q, kv, u16, ga, gs = _proj(xs, vec(g_mix[0]), w_in_b, perm_s, _tile(ns, 512), dec_seq)
    tpad = SUBLANES
    pad3 = lambda v: jnp.pad(v.reshape(db, dec_seq, -1).astype(F32), ((0, 0), (0, tpad - dec_seq), (0, 0)))
    ga_pad = pad3(ga).astype(BF16).reshape(db * tpad, D_MODEL)
    ma_pad, k_new, v_new = _attn_sample(
        sinks, pad3(q), pad3(kv), cache_k[0].reshape(db, WINDOW, D_KV), cache_v[0].reshape(db, WINDOW, D_KV),
        ga_pad, wo_b, _tile(db, 16), dec_seq)
    ma = ma_pad.reshape(db, tpad, D_MODEL)[:, :dec_seq].reshape(ns, D_MODEL)
    to_cols = lambda s: s[0].astype(F32).transpose(1, 2, 0).reshape(HALF_COLS, db)
    from_cols = lambda h: h.reshape(SSM_GROUPS, SSM_STATE, db).transpose(2, 0, 1)[None]
    g16, hT_re, hT_im = _s5_rows(u16, m, ws_n, whyt, an, d_skip, to_cols(state_ssm_re), to_cols(state_ssm_im),
                                 _tile(db, LANES))
    y_sample = _post(g16, ma, gs, xs, perm_s.T, *post_w, PERM_ROWS, dec_seq).reshape(db, dec_seq, D_MODEL)
    k_sample = k_new.reshape(1, db, WINDOW, N_KV_HEADS, HEAD_DIM)
    v_sample = v_new.reshape(1, db, WINDOW, N_KV_HEADS, HEAD_DIM)
    ssm_re_sample, ssm_im_sample = from_cols(hT_re), from_cols(hT_im)

    return (y_prompt, y_sample, k_prompt, v_prompt, ssm_re_prompt, ssm_im_prompt,
            k_sample, v_sample, ssm_re_sample, ssm_im_sample)
```

```python
import functools

import jax
import jax.numpy as jnp
from jax import lax
from jax.experimental import pallas as pl
from jax.experimental.pallas import tpu as pltpu

F32 = jnp.float32
BF16 = jnp.bfloat16

D_MODEL = 1024
HEAD_DIM = 64
N_HEADS = 16
N_KV_HEADS = 4
WINDOW = 128
D_KV = N_KV_HEADS * HEAD_DIM
SSM_CH = 16
SSM_GROUPS = 64
SSM_STATE = 64
D_FF = 4 * D_MODEL
FF_CHUNK = 1024
RMS_EPS = 1e-5
LOG2E = 1.4426950408889634
Q_SCALE = HEAD_DIM ** -0.5 * LOG2E

LANES = 128
SUBLANES = 8
BF16_ROWS = 16
N_LANE_TILES = D_MODEL // LANES
GROUPS_PER_TILE = LANES // SSM_CH
CHUNK = 16
CHUNK_LANES = CHUNK * D_MODEL
GROUP_IO = CHUNK * SSM_CH
N_PAIRS = SSM_GROUPS // 2
PAIRS_PER_TILE = GROUPS_PER_TILE // 2
PAIR_COLS = 2 * LANES
TILE_COLS = PAIRS_PER_TILE * PAIR_COLS
STATE_COLS = N_PAIRS * PAIR_COLS
HALF_COLS = STATE_COLS // 2
PERM_ROWS = CHUNK * BF16_ROWS
PREP_PAIRS = 4
VMEM_LIMIT = 56 * 1024 * 1024

_Q0, _KV0, _U0, _GA0, _GS0, _END = 0, 1024, 1536, 2560, 3584, 4608


def _rmsnorm(x, g):
    return x * lax.rsqrt(jnp.mean(x * x, axis=-1, keepdims=True) + RMS_EPS) * g


def _params(*sem):
    return pltpu.CompilerParams(dimension_semantics=sem, vmem_limit_bytes=VMEM_LIMIT)


def _const_spec(shape):
    nd = len(shape)
    return pl.BlockSpec(shape, lambda *_: (0,) * nd, pipeline_mode=pl.Buffered(1))


def _chunk_perm(n_tok):
    n_chunks = PERM_ROWS // n_tok
    r = jnp.arange(PERM_ROWS)
    tok = (r % n_chunks) * n_tok + r // n_chunks
    return (tok[:, None] == jnp.arange(PERM_ROWS)[None, :]).astype(BF16)


def _piece_transpose(cols, masks):
    for d, msk in zip((4, 2, 1), masks):
        new = list(cols)
        for v in range(GROUPS_PER_TILE):
            if v & d == 0:
                a, b = cols[v], cols[v + d]
                new[v] = jnp.where(msk, pltpu.roll(b, SSM_CH * d, 1), a)
                new[v + d] = jnp.where(msk, b, pltpu.roll(a, LANES - SSM_CH * d, 1))
        cols = new
    return cols


def _piece_masks(rows):
    piece = lax.broadcasted_iota(jnp.int32, (rows, LANES), 1) // SSM_CH
    return [(piece & d) != 0 for d in (4, 2, 1)]


def _store_chunk_rows(x, perm_ref, out_ref, n_tok):
    n_chunks = PERM_ROWS // n_tok
    masks = _piece_masks(n_chunks)
    zero = jnp.zeros((n_chunks, LANES), F32)
    for hb in range(x.shape[0] // PERM_ROWS):
        rows = slice(hb * n_chunks, (hb + 1) * n_chunks)
        xp = jnp.dot(perm_ref[...], x[hb * PERM_ROWS:(hb + 1) * PERM_ROWS],
                     preferred_element_type=F32)
        for j in range(N_LANE_TILES):
            for hf in range(2):
                live = 8 * hf < n_tok
                cols = [xp[t * n_chunks:(t + 1) * n_chunks, j * LANES:(j + 1) * LANES] if t < n_tok else zero
                        for t in range(8 * hf, 8 * hf + 8)]
                for gl, col in enumerate(_piece_transpose(cols, masks) if live else cols):
                    c0 = (GROUPS_PER_TILE * j + gl) * GROUP_IO + hf * LANES
                    out_ref[rows, c0:c0 + LANES] = col.astype(BF16)


def _load_chunk_rows(in_ref, permt_ref, n_tok):
    n_chunks = PERM_ROWS // n_tok
    masks = _piece_masks(n_chunks)
    blocks = []
    for hb in range(in_ref.shape[0] // n_chunks):
        rows = slice(hb * n_chunks, (hb + 1) * n_chunks)
        tiles = [[None] * N_LANE_TILES for _ in range(n_tok)]
        for j in range(N_LANE_TILES):
            for hf in range(-(-n_tok // 8)):
                cols = []
                for gl in range(GROUPS_PER_TILE):
                    c0 = (GROUPS_PER_TILE * j + gl) * GROUP_IO + hf * LANES
                    cols.append(in_ref[rows, c0:c0 + LANES].astype(F32))
                for k, col in enumerate(_piece_transpose(cols, masks)):
                    if 8 * hf + k < n_tok:
                        tiles[8 * hf + k][j] = col.astype(BF16)
        xp = jnp.concatenate([jnp.concatenate(row, axis=1) for row in tiles], axis=0)
        blocks.append(jnp.dot(permt_ref[...], xp, preferred_element_type=F32).astype(BF16))
    return blocks[0] if len(blocks) == 1 else jnp.concatenate(blocks, axis=0)


def _proj_kernel(x_ref, g_ref, w_ref, perm_ref, *refs, n_tok, n_cast):
    q_ref, kv_ref, u_ref, ga_ref, gs_ref = refs[n_cast:n_cast + 5]
    h = _rmsnorm(x_ref[...], g_ref[...]).astype(BF16)

    def seg(lo, hi):
        return jnp.dot(h, w_ref[:, lo:hi], preferred_element_type=F32)

    q_ref[...] = (seg(_Q0, _KV0) * Q_SCALE).astype(BF16)
    kv_ref[...] = seg(_KV0, _U0)
    _store_chunk_rows(seg(_U0, _GA0).astype(BF16), perm_ref, u_ref, n_tok)
    ga_ref[...] = jax.nn.sigmoid(seg(_GA0, _GS0)).astype(BF16)
    gs_ref[...] = jax.nn.sigmoid(seg(_GS0, _END)).astype(BF16)
    for src, dst in zip(refs[:n_cast], refs[n_cast + 5:]):
        dst[...] = src[...].astype(BF16)


def _proj(x, g_mix, w_in, perm, tm, n_tok, cast=()):
    n = x.shape[0]
    steps = n // tm
    row = lambda w: pl.BlockSpec((tm, w), lambda i: (i, 0))
    slab = lambda w: pl.BlockSpec((w.shape[0] // steps, w.shape[1]), lambda i: (i, 0))
    assert all(w.shape[0] % (steps * BF16_ROWS) == 0 for w in cast)
    return pl.pallas_call(
        functools.partial(_proj_kernel, n_tok=n_tok, n_cast=len(cast)),
        grid=(steps,),
        in_specs=[row(D_MODEL), _const_spec((1, D_MODEL)), _const_spec(w_in.shape), _const_spec(perm.shape)]
        + [slab(w) for w in cast],
        out_specs=[row(D_MODEL), row(2 * D_KV), pl.BlockSpec((tm // n_tok, CHUNK_LANES), lambda i: (i, 0)),
                   row(D_MODEL), row(D_MODEL)] + [slab(w) for w in cast],
        out_shape=[jax.ShapeDtypeStruct((n, D_MODEL), BF16),
                   jax.ShapeDtypeStruct((n, 2 * D_KV), F32),
                   jax.ShapeDtypeStruct((n // n_tok, CHUNK_LANES), BF16),
                   jax.ShapeDtypeStruct((n, D_MODEL), BF16),
                   jax.ShapeDtypeStruct((n, D_MODEL), BF16)]
        + [jax.ShapeDtypeStruct(w.shape, BF16) for w in cast],
        compiler_params=_params("arbitrary"),
        name="proj",
    )(x, g_mix, w_in, perm, *cast)


def _dup_heads(tile):
    lo = lax.broadcasted_iota(jnp.int32, tile.shape, tile.ndim - 1) < HEAD_DIM
    rolled = pltpu.roll(tile, HEAD_DIM, tile.ndim - 1)
    return (jnp.where(lo, tile, rolled).astype(BF16), jnp.where(lo, rolled, tile).astype(BF16))


def _sink_softmax(s, sink):
    sink2 = sink * LOG2E
    m = jnp.maximum(jnp.max(s, axis=-1, keepdims=True), sink2)
    p = jnp.exp2(s - m)
    denom = jnp.sum(p, axis=-1, keepdims=True) + jnp.exp2(sink2 - m)
    return p.astype(BF16), 1.0 / denom


def _attn_prompt_kernel(sinks_ref, q_ref, kvc_ref, kvp_ref, ga_ref, wo_ref, out_ref,
                        kd_scr, vd_scr, bias_scr, attn_scr, *, tq):
    i = pl.program_id(1)
    kv_full = jnp.concatenate([kvp_ref[...], kvc_ref[...]], axis=0)
    for t in range(2):
        ke, ko = _dup_heads(kv_full[:, t * LANES:(t + 1) * LANES])
        ve, vo = _dup_heads(kv_full[:, D_KV + t * LANES:D_KV + (t + 1) * LANES])
        kd_scr[2 * t], kd_scr[2 * t + 1] = ke, ko
        vd_scr[2 * t], vd_scr[2 * t + 1] = ve, vo

    qi = lax.broadcasted_iota(jnp.int32, (WINDOW, 2 * WINDOW), 0)
    si = lax.broadcasted_iota(jnp.int32, (WINDOW, 2 * WINDOW), 1)
    band = (si > qi) & (si <= qi + WINDOW)
    in_block = si >= WINDOW
    lo = lax.broadcasted_iota(jnp.int32, (WINDOW, LANES), 1) < HEAD_DIM

    def attend(jb):
        q0 = jb * WINDOW
        has_prev = (i > 0) | (jb > 0)
        bias_scr[...] = jnp.where(band & (in_block | has_prev), 0.0, -jnp.inf)

        def head_pair(hp, c):
            j = hp // 2
            c0 = pl.multiple_of(hp * LANES, LANES)
            kd = kd_scr[j, pl.ds(q0, 2 * WINDOW), :]
            vd = vd_scr[j, pl.ds(q0, 2 * WINDOW), :]
            qp = q_ref[pl.ds(q0, WINDOW), pl.ds(c0, LANES)]
            outs = []
            for par in range(2):
                qm = jnp.where(lo if par == 0 else ~lo, qp, jnp.zeros_like(qp))
                s = lax.dot_general(qm, kd, (((1,), (1,)), ((), ())), preferred_element_type=F32)
                p, inv = _sink_softmax(s + bias_scr[...], sinks_ref[2 * hp + par])
                outs.append(jnp.dot(p, vd, preferred_element_type=F32) * inv)
            attn_scr[pl.ds(q0, WINDOW), pl.ds(c0, LANES)] = jnp.where(lo, outs[0], outs[1]).astype(BF16)
            return c

        lax.fori_loop(0, N_HEADS // 2, head_pair, 0, unroll=8)

    def project(jb):
        rows = pl.ds(jb * WINDOW, WINDOW)
        a_out = jnp.dot(attn_scr[rows, :], wo_ref[...], preferred_element_type=F32)
        out_ref[rows, :] = ga_ref[rows, :].astype(F32) * a_out

    for jb in range(tq // WINDOW):
        attend(jb)
        project(jb)


def _attn_prompt(sinks, q, kv, ga, wo, batch, seq, tq):
    nq = seq // tq
    bpt = tq // WINDOW
    row = lambda w: pl.BlockSpec((tq, w), lambda b, i: (b * nq + i, 0))
    prev = pl.BlockSpec((WINDOW, 2 * D_KV),
                        lambda b, i: (jnp.maximum((b * nq + i) * bpt - 1, 0), 0))
    return pl.pallas_call(
        functools.partial(_attn_prompt_kernel, tq=tq),
        grid=(batch, nq),
        in_specs=[pl.BlockSpec(memory_space=pltpu.SMEM), row(D_MODEL), row(2 * D_KV), prev,
                  row(D_MODEL), _const_spec(wo.shape)],
        out_specs=row(D_MODEL),
        out_shape=jax.ShapeDtypeStruct((batch * seq, D_MODEL), F32),
        scratch_shapes=[pltpu.VMEM((N_KV_HEADS, WINDOW + tq, LANES), BF16),
                        pltpu.VMEM((N_KV_HEADS, WINDOW + tq, LANES), BF16),
                        pltpu.VMEM((WINDOW, 2 * WINDOW), F32),
                        pltpu.VMEM((tq, D_MODEL), BF16)],
        compiler_params=_params("arbitrary", "arbitrary"),
        name="attn_prompt",
    )(sinks, q, kv, kv, ga, wo)


def _attn_sample_kernel(sinks_ref, q_ref, kvn_ref, ck_ref, cv_ref, ga_ref, wo_ref,
                        out_ref, ko_ref, vo_ref, attn_scr, *, bb, tpad, dec_seq):
    nk = WINDOW + tpad
    lo3 = lax.broadcasted_iota(jnp.int32, (bb, tpad, LANES), 2) < HEAD_DIM
    row = lax.broadcasted_iota(jnp.int32, (4 * tpad, nk), 0)
    si = lax.broadcasted_iota(jnp.int32, (4 * tpad, nk), 1)
    tq = row % tpad
    valid = (si > tq) & (si <= tq + WINDOW) & (si < WINDOW + dec_seq)
    hrow = lax.broadcasted_iota(jnp.int32, (4 * tpad, 1), 0) // tpad

    for t in range(2):
        sl = slice(t * LANES, (t + 1) * LANES)
        kk = jnp.concatenate([ck_ref[:, :, sl], kvn_ref[:, :, sl]], axis=1)
        vv = jnp.concatenate([cv_ref[:, :, sl],
                              kvn_ref[:, :, D_KV + t * LANES:D_KV + (t + 1) * LANES]], axis=1)
        ko_ref[:, :, sl] = kk[:, dec_seq:dec_seq + WINDOW, :]
        vo_ref[:, :, sl] = vv[:, dec_seq:dec_seq + WINDOW, :]
        kds = _dup_heads(kk)
        vds = _dup_heads(vv)
        for par_kv in range(2):
            j = 2 * t + par_kv
            kd, vd = kds[par_kv], vds[par_kv]
            parts = []
            for r in range(2):
                c0 = j * 2 * LANES + r * LANES
                qp = q_ref[:, :, c0:c0 + LANES]
                parts += [jnp.where(lo3, qp, 0.0), jnp.where(lo3, 0.0, qp)]
            lhs = jnp.concatenate(parts, axis=1).astype(BF16)
            s = jnp.einsum('bqd,bkd->bqk', lhs, kd, preferred_element_type=F32)
            sink = jnp.zeros((4 * tpad, 1), F32)
            for g in range(4):
                sink = jnp.where(hrow == g, sinks_ref[4 * j + g], sink)
            p, inv = _sink_softmax(jnp.where(valid[None], s, -jnp.inf), sink[None])
            o = jnp.einsum('bqk,bkd->bqd', p, vd, preferred_element_type=F32) * inv
            for r in range(2):
                c0 = j * 2 * LANES + r * LANES
                o_even = o[:, (2 * r) * tpad:(2 * r + 1) * tpad, :]
                o_odd = o[:, (2 * r + 1) * tpad:(2 * r + 2) * tpad, :]
                attn_scr[:, :, c0:c0 + LANES] = jnp.where(lo3, o_even, o_odd)

    attn = attn_scr[...].reshape(bb * tpad, D_MODEL).astype(BF16)
    a_out = jnp.dot(attn, wo_ref[...], preferred_element_type=F32)
    out_ref[...] = ga_ref[...].astype(F32) * a_out


def _attn_sample(sinks, q3, kvn3, ck, cv, ga2, wo, bb, dec_seq):
    db, tpad, _ = q3.shape
    blk3 = lambda r, w: pl.BlockSpec((bb, r, w), lambda i: (i, 0, 0))
    row = pl.BlockSpec((bb * tpad, D_MODEL), lambda i: (i, 0))
    return pl.pallas_call(
        functools.partial(_attn_sample_kernel, bb=bb, tpad=tpad, dec_seq=dec_seq),
        grid=(db // bb,),
        in_specs=[pl.BlockSpec(memory_space=pltpu.SMEM), blk3(tpad, D_MODEL), blk3(tpad, 2 * D_KV),
                  blk3(WINDOW, D_KV), blk3(WINDOW, D_KV), row, _const_spec(wo.shape)],
        out_specs=[row, blk3(WINDOW, D_KV), blk3(WINDOW, D_KV)],
        out_shape=[jax.ShapeDtypeStruct((db * tpad, D_MODEL), F32),
                   jax.ShapeDtypeStruct((db, WINDOW, D_KV), F32),
                   jax.ShapeDtypeStruct((db, WINDOW, D_KV), F32)],
        scratch_shapes=[pltpu.VMEM((bb, tpad, D_MODEL), F32)],
        compiler_params=_params("arbitrary"),
        name="attn_sample",
    )(sinks, q3, kvn3, ck, cv, ga2, wo)


def _s5_discretize(lam_re, lam_im, log_dt, b_re, b_im):
    dt = jnp.exp(log_dt)[:, None]
    decay = jnp.exp(lam_re * dt)
    ab_re = decay * jnp.cos(lam_im * dt)
    ab_im = decay * jnp.sin(lam_im * dt)
    nr, ni = ab_re - 1.0, ab_im
    den = lam_re * lam_re + lam_im * lam_im
    f_re = ((nr * lam_re + ni * lam_im) / den)[..., None]
    f_im = ((ni * lam_re - nr * lam_im) / den)[..., None]
    return ab_re, ab_im, f_re * b_re - f_im * b_im, f_re * b_im + f_im * b_re


def _state_split(h):
    h = h.reshape(h.shape[:-1] + (N_PAIRS, 2, 2, SSM_STATE))
    unpair = lambda a: a.reshape(a.shape[:-3] + (SSM_GROUPS, SSM_STATE))
    return unpair(h[..., 0, :, :]), unpair(h[..., 1, :, :])


def _shift_lanes(x, n):
    a, b = x[:, :LANES], x[:, LANES:]
    lane = lax.broadcasted_iota(jnp.int32, a.shape, 1)
    if n == 0:
        return x
    if n >= LANES:
        r = n - LANES
        hi = a if r == 0 else jnp.where(lane < r, 0.0, pltpu.roll(a, r, 1))
        return jnp.concatenate([jnp.zeros_like(a), hi], axis=1)
    ra, rb = pltpu.roll(a, n, 1), pltpu.roll(b, n, 1)
    return jnp.concatenate([jnp.where(lane < n, 0.0, ra), jnp.where(lane < n, ra, rb)], axis=1)


def _dot_nt_split(lhs, rhs):
    nt = lambda a, b: lax.dot_general(a, b, (((1,), (1,)), ((), ())), preferred_element_type=F32)
    l_hi, r_hi = lhs.astype(BF16), rhs.astype(BF16)
    l_lo = (lhs - l_hi.astype(F32)).astype(BF16)
    r_lo = (rhs - r_hi.astype(F32)).astype(BF16)
    return nt(l_hi, r_hi) + nt(l_hi, r_lo) + nt(l_lo, r_hi)


def _s5_prep_kernel(a_ref, c_ref, bt_ref, m_ref, ws_ref, wsn_ref, whyt_ref, mk_ref, pw_ref, an_ref, *, n_tok):
    for p in range(PREP_PAIRS):
        one, two, lanes = pl.ds(p, 1), pl.ds(2 * p, 2), pl.ds(p * LANES, LANES)
        _s5_prep_pair(a_ref.at[one], c_ref.at[one], bt_ref.at[one], m_ref.at[two], ws_ref.at[two],
                      wsn_ref.at[two], whyt_ref.at[two], mk_ref.at[:, :, :, lanes], pw_ref.at[:, :, lanes],
                      an_ref.at[:, :, lanes], n_tok)


def _s5_prep_pair(a_ref, c_ref, bt_ref, m_ref, ws_ref, wsn_ref, whyt_ref, mk_ref, pw_ref, an_ref, n_tok):
    ar, ai = a_ref[0, 0], a_ref[0, 1]
    cr, ci = c_ref[0, 0], c_ref[0, 1]
    br, bi = bt_ref[0, 0], bt_ref[0, 1]
    pr, pi = [jnp.ones_like(ar)], [jnp.zeros_like(ai)]
    for _ in range(CHUNK):
        pr, pi = pr + [pr[-1] * ar - pi[-1] * ai], pi + [pr[-1] * ai + pi[-1] * ar]
    first = lax.broadcasted_iota(jnp.int32, (SSM_CH, LANES), 1) < SSM_STATE
    ca = [(cr * pr[t] - ci * pi[t], -(cr * pi[t] + ci * pr[t])) for t in range(CHUNK + 1)]
    ca_full = jnp.concatenate([jnp.concatenate(ca[t], axis=1) for t in range(CHUNK)], axis=0)
    zero = jnp.zeros((SSM_CH, LANES), F32)
    for e in range(2):
        own = first if e == 0 else ~first
        pick = lambda v: jnp.where(own, v, zero)
        for t in range(CHUNK):
            rows = slice(t * SSM_CH, (t + 1) * SSM_CH)
            whyt_ref[e, rows, :LANES] = pick(ca[t + 1][0]).astype(BF16)
            whyt_ref[e, rows, LANES:] = pick(ca[t + 1][1]).astype(BF16)
            for ref, k in ((ws_ref, CHUNK - 1 - t), (wsn_ref, n_tok - 1 - t)):
                if k >= 0:
                    ref[e, rows, :LANES] = pick(br * pr[k] - bi * pi[k]).astype(BF16)
                    ref[e, rows, LANES:] = pick(br * pi[k] + bi * pr[k]).astype(BF16)
                else:
                    ref[e, rows, :] = jnp.zeros((SSM_CH, PAIR_COLS), BF16)
        k_row = _dot_nt_split(jnp.concatenate([pick(br), pick(bi)], axis=1), ca_full)
        for s in range(CHUNK):
            m_ref[e, s * SSM_CH:(s + 1) * SSM_CH, :] = _shift_lanes(k_row, s * SSM_CH).astype(BF16)
    sub = lax.broadcasted_iota(jnp.int32, (SUBLANES, LANES), 0)
    qr, qi = [pr[CHUNK]], [pi[CHUNK]]
    for _ in range(SUBLANES - 1):
        qr, qi = qr + [qr[-1] * pr[CHUNK] - qi[-1] * pi[CHUNK]], qi + [qr[-1] * pi[CHUNK] + qi[-1] * pr[CHUNK]]
    for part, q in enumerate((qr, qi)):
        rows = jnp.zeros((SUBLANES, LANES), F32)
        for s in range(SUBLANES):
            rows = jnp.where(sub == s, q[s], rows)
        pw_ref[part] = rows
        for k, shift in enumerate((1, 2, 4)):
            mk_ref[k, part] = jnp.where(sub >= shift, q[shift - 1], 0.0)
    an_ref[0] = jnp.broadcast_to(pr[n_tok], (SUBLANES, LANES))
    an_ref[1] = jnp.broadcast_to(pi[n_tok], (SUBLANES, LANES))


def _s5_prep(lam_re, lam_im, log_dt, b_re, b_im, c_re, c_im, n_tok):
    ab_re, ab_im, bb_re, bb_im = _s5_discretize(lam_re, lam_im, log_dt, b_re, b_im)
    pair = lambda v: v.reshape(N_PAIRS, 2, v.shape[1], SSM_STATE).transpose(0, 2, 1, 3).reshape(
        N_PAIRS, v.shape[1], LANES)
    a = jnp.stack([pair(ab_re[:, None, :]), pair(ab_im[:, None, :])], axis=1)
    c = jnp.stack([pair(c_re), pair(c_im)], axis=1)
    bt = jnp.stack([pair(bb_re.transpose(0, 2, 1)), pair(bb_im.transpose(0, 2, 1))], axis=1)
    blk4 = lambda r: pl.BlockSpec((PREP_PAIRS, 2, r, LANES), lambda q: (q, 0, 0, 0))
    w_spec = pl.BlockSpec((2 * PREP_PAIRS, GROUP_IO, PAIR_COLS), lambda q: (q, 0, 0))
    w_shape = jax.ShapeDtypeStruct((SSM_GROUPS, GROUP_IO, PAIR_COLS), BF16)
    return pl.pallas_call(
        functools.partial(_s5_prep_kernel, n_tok=n_tok),
        grid=(N_PAIRS // PREP_PAIRS,),
        in_specs=[blk4(1), blk4(SSM_CH), blk4(SSM_CH)],
        out_specs=[w_spec, w_spec, w_spec, w_spec,
                   pl.BlockSpec((3, 2, SUBLANES, PREP_PAIRS * LANES), lambda q: (0, 0, 0, q)),
                   pl.BlockSpec((2, SUBLANES, PREP_PAIRS * LANES), lambda q: (0, 0, q)),
                   pl.BlockSpec((2, SUBLANES, PREP_PAIRS * LANES), lambda q: (0, 0, q))],
        out_shape=[w_shape, w_shape, w_shape, w_shape,
                   jax.ShapeDtypeStruct((3, 2, SUBLANES, HALF_COLS), F32),
                   jax.ShapeDtypeStruct((2, SUBLANES, HALF_COLS), F32),
                   jax.ShapeDtypeStruct((2, SUBLANES, HALF_COLS), F32)],
        compiler_params=_params("arbitrary"),
        name="s5_prep",
    )(a, c, bt)


def _group_io(ref, g):
    return ref.at[:, g * GROUP_IO:(g + 1) * GROUP_IO]


def _s5_emit_tile(u_ref, st_scr, m_ref, whyt_ref, dt_ref, g_ref, j):
    for gl in range(GROUPS_PER_TILE):
        g = GROUPS_PER_TILE * j + gl
        q0 = (gl // 2) * PAIR_COLS
        u_g = _group_io(u_ref, g)[...]
        y = (jnp.dot(u_g, m_ref[g], preferred_element_type=F32)
             + lax.dot_general(st_scr[:, q0:q0 + PAIR_COLS].astype(BF16), whyt_ref[g],
                               (((1,), (1,)), ((), ())), preferred_element_type=F32)
             + dt_ref[g] * u_g.astype(F32))
        _group_io(g_ref, g)[...] = jax.nn.gelu(y).astype(BF16)


def _s5_local_states(u_ref, ws_ref, j, q):
    g = GROUPS_PER_TILE * j + 2 * q
    return (jnp.dot(_group_io(u_ref, g)[...], ws_ref[g], preferred_element_type=F32)
            + jnp.dot(_group_io(u_ref, g + 1)[...], ws_ref[g + 1], preferred_element_type=F32))


def _cmul_add(ar, ai, hr, hi, xr, xi):
    return ar * hr - ai * hi + xr, ar * hi + ai * hr + xi


def _s5_chain_kernel(u_ref, m_ref, ws_ref, whyt_ref, mk_ref, pw_ref, dt_ref, h0_ref, g_ref, hT_ref,
                     carry_scr, st_scr, *, rows):
    @pl.when(pl.program_id(1) == 0)
    def _():
        carry_scr[...] = jnp.broadcast_to(h0_ref[0], (SUBLANES, STATE_COLS))

    row0 = lax.broadcasted_iota(jnp.int32, (SUBLANES, LANES), 0) == 0
    last = lambda h: jnp.broadcast_to(h[SUBLANES - 1:, :], (SUBLANES, LANES))

    for j in range(N_LANE_TILES):
        for q in range(PAIRS_PER_TILE):
            st_scr[:, q * PAIR_COLS:(q + 1) * PAIR_COLS] = _s5_local_states(u_ref, ws_ref, j, q)

        def step(r, carry):
            r0 = pl.multiple_of(r * SUBLANES, SUBLANES)
            out = []
            for q in range(PAIRS_PER_TILE):
                re_c = slice(q * PAIR_COLS, q * PAIR_COLS + LANES)
                im_c = slice(q * PAIR_COLS + LANES, (q + 1) * PAIR_COLS)
                hc = slice((j * PAIRS_PER_TILE + q) * LANES, (j * PAIRS_PER_TILE + q + 1) * LANES)
                xr = st_scr[pl.ds(r0, SUBLANES), re_c]
                xi = st_scr[pl.ds(r0, SUBLANES), im_c]
                for k, shift in enumerate((1, 2, 4)):
                    xr, xi = _cmul_add(mk_ref[k, 0, :, hc], mk_ref[k, 1, :, hc],
                                       pltpu.roll(xr, shift, 0), pltpu.roll(xi, shift, 0), xr, xi)
                cr, ci = carry[q]
                hr, hi = _cmul_add(pw_ref[0, :, hc], pw_ref[1, :, hc], cr, ci, xr, xi)
                st_scr[pl.ds(r0, SUBLANES), re_c] = jnp.where(row0, cr, pltpu.roll(hr, 1, 0))
                st_scr[pl.ds(r0, SUBLANES), im_c] = jnp.where(row0, ci, pltpu.roll(hi, 1, 0))
                out.append((last(hr), last(hi)))
            return tuple(out)

        c0 = j * TILE_COLS
        init = tuple((carry_scr[:, c0 + q * PAIR_COLS:c0 + q * PAIR_COLS + LANES],
                      carry_scr[:, c0 + q * PAIR_COLS + LANES:c0 + (q + 1) * PAIR_COLS])
                     for q in range(PAIRS_PER_TILE))
        final = lax.fori_loop(0, rows // SUBLANES, step, init, unroll=True)
        for q in range(PAIRS_PER_TILE):
            carry_scr[:, c0 + q * PAIR_COLS:c0 + q * PAIR_COLS + LANES] = final[q][0]
            carry_scr[:, c0 + q * PAIR_COLS + LANES:c0 + (q + 1) * PAIR_COLS] = final[q][1]

        _s5_emit_tile(u_ref, st_scr, m_ref, whyt_ref, dt_ref, g_ref, j)

    hT_ref[0] = carry_scr[...]


def _s5_chain(u16, m, ws, why, mk, pw, d_tiled, h0, batch, n_rows, rows):
    nblk = n_rows // rows
    row = pl.BlockSpec((rows, CHUNK_LANES), lambda b, i: (b * nblk + i, 0))
    return pl.pallas_call(
        functools.partial(_s5_chain_kernel, rows=rows),
        grid=(batch, nblk),
        in_specs=[row, _const_spec(m.shape), _const_spec(ws.shape), _const_spec(why.shape),
                  _const_spec(mk.shape), _const_spec(pw.shape), _const_spec(d_tiled.shape),
                  pl.BlockSpec((1, 1, STATE_COLS), lambda b, i: (b, 0, 0))],
        out_specs=[row, pl.BlockSpec((1, SUBLANES, STATE_COLS), lambda b, i: (b, 0, 0))],
        out_shape=[jax.ShapeDtypeStruct(u16.shape, BF16),
                   jax.ShapeDtypeStruct((batch, SUBLANES, STATE_COLS), F32)],
        scratch_shapes=[pltpu.VMEM((SUBLANES, STATE_COLS), F32), pltpu.VMEM((rows, TILE_COLS), F32)],
        compiler_params=_params("arbitrary", "arbitrary"),
        name="s5_chain",
    )(u16, m, ws, why, mk, pw, d_tiled, h0)


def _s5_rows_kernel(u_ref, m_ref, ws_ref, whyt_ref, an_ref, dt_ref, h0r_ref, h0i_ref, g_ref, hTr_ref, hTi_ref,
                    st_scr):
    for j in range(N_LANE_TILES):
        for q in range(PAIRS_PER_TILE):
            hc = slice((j * PAIRS_PER_TILE + q) * LANES, (j * PAIRS_PER_TILE + q + 1) * LANES)
            h0r, h0i = h0r_ref[hc, :].T, h0i_ref[hc, :].T
            local = _s5_local_states(u_ref, ws_ref, j, q)
            hr, hi = _cmul_add(an_ref[0, :1, hc], an_ref[1, :1, hc], h0r, h0i, local[:, :LANES], local[:, LANES:])
            hTr_ref[hc, :] = hr.T
            hTi_ref[hc, :] = hi.T
            st_scr[:, q * PAIR_COLS:q * PAIR_COLS + LANES] = h0r
            st_scr[:, q * PAIR_COLS + LANES:(q + 1) * PAIR_COLS] = h0i
        _s5_emit_tile(u_ref, st_scr, m_ref, whyt_ref, dt_ref, g_ref, j)


def _s5_rows(u16, m, ws, why, an, d_tiled, h0_re_t, h0_im_t, rows):
    n = u16.shape[0]
    once = pl.Buffered(1)
    row = pl.BlockSpec((rows, CHUNK_LANES), lambda i: (i, 0), pipeline_mode=once)
    col = pl.BlockSpec((HALF_COLS, rows), lambda i: (0, i), pipeline_mode=once)
    state = jax.ShapeDtypeStruct((HALF_COLS, n), F32)
    return pl.pallas_call(
        _s5_rows_kernel,
        grid=(n // rows,),
        in_specs=[row, _const_spec(m.shape), _const_spec(ws.shape), _const_spec(why.shape),
                  _const_spec(an.shape), _const_spec(d_tiled.shape), col, col],
        out_specs=[row, col, col],
        out_shape=[jax.ShapeDtypeStruct(u16.shape, BF16), state, state],
        scratch_shapes=[pltpu.VMEM((rows, TILE_COLS), F32)],
        compiler_params=_params("arbitrary"),
        name="s5_rows",
    )(u16, m, ws, why, an, d_tiled, h0_re_t, h0_im_t)


def _post_kernel(g_ref, ma_ref, gs_ref, x_ref, permt_ref, wglu_ref, wout_ref, gffn_ref, wup_ref, wdown_ref,
                 gfin_ref, out_ref, *, n_tok):
    g = _load_chunk_rows(g_ref, permt_ref, n_tok)
    glu = jnp.dot(g, wglu_ref[...], preferred_element_type=F32)
    s_out = glu[:, :D_MODEL] * jax.nn.sigmoid(glu[:, D_MODEL:])
    merged = ma_ref[...] + gs_ref[...].astype(F32) * s_out
    x1 = x_ref[...] + jnp.dot(merged.astype(BF16), wout_ref[...], preferred_element_type=F32)
    h2 = _rmsnorm(x1, gffn_ref[...]).astype(BF16)
    x2 = x1
    for c in range(0, D_FF, FF_CHUNK):
        up = jnp.dot(h2, wup_ref[:, c:c + FF_CHUNK], preferred_element_type=F32)
        act = jnp.square(jnp.maximum(up, 0.0)).astype(BF16)
        x2 = x2 + jnp.dot(act, wdown_ref[c:c + FF_CHUNK, :], preferred_element_type=F32)
    out_ref[...] = _rmsnorm(x2, gfin_ref[...])


def _post(g, ma, gs, x, permt, w_glu, w_out, g_ffn, w_up, w_down, g_final, tm, n_tok):
    n = x.shape[0]
    row = pl.BlockSpec((tm, D_MODEL), lambda i: (i, 0))
    g_spec = pl.BlockSpec((tm // n_tok, CHUNK_LANES), lambda i: (i, 0))
    return pl.pallas_call(
        functools.partial(_post_kernel, n_tok=n_tok),
        grid=(n // tm,),
        in_specs=[g_spec, row, row, row, _const_spec(permt.shape), _const_spec(w_glu.shape),
                  _const_spec(w_out.shape), _const_spec((1, D_MODEL)), _const_spec(w_up.shape),
                  _const_spec(w_down.shape), _const_spec((1, D_MODEL))],
        out_specs=row,
        out_shape=jax.ShapeDtypeStruct((n, D_MODEL), F32),
        compiler_params=_params("arbitrary"),
        name="post",
    )(g, ma, gs, x, permt, w_glu, w_out, g_ffn, w_up, w_down, g_final)


def _tile(n, pref):
    t = pref
    while n % t:
        t //= 2
    return t


def kernel(x_prompt, x_sample, cache_k, cache_v, state_ssm_re, state_ssm_im, g_mix, w_in, attn_sinks,
           w_attn_o, ssm_lambda_re, ssm_lambda_im, ssm_log_dt, ssm_b_re, ssm_b_im, ssm_c_re, ssm_c_im,
           ssm_d, w_glu, w_out, g_ffn, w_up, w_down, g_final):
    batch, seq, _ = x_prompt.shape
    db, dec_seq, _ = x_sample.shape
    assert w_in.shape[0] == 1, "one layer"
    assert seq % PERM_ROWS == 0 and dec_seq in (1, 2, 4, 8) and (db * dec_seq) % PERM_ROWS == 0

    vec = lambda v: v.reshape(1, D_MODEL).astype(F32)
    w_in_b = w_in[0].astype(BF16)
    sinks = attn_sinks[0].astype(F32)
    s5p = (ssm_lambda_re[0], ssm_lambda_im[0], ssm_log_dt[0], ssm_b_re[0], ssm_b_im[0], ssm_c_re[0], ssm_c_im[0])
    d_skip = jnp.tile(ssm_d[0].astype(F32).reshape(SSM_GROUPS, 1, SSM_CH), (1, 1, CHUNK))
    perm = _chunk_perm(CHUNK)

    xp = x_prompt.reshape(batch * seq, D_MODEL)
    q, kv, u16, ga, gs, wo_b, w_glu_b, w_out_b, w_up_b, w_down_b = _proj(
        xp, vec(g_mix[0]), w_in_b, perm, _tile(batch * seq, 1024), CHUNK,
        cast=(w_attn_o[0], w_glu[0], w_out[0], w_up[0], w_down[0]))
    post_w = (w_glu_b, w_out_b, vec(g_ffn[0]), w_up_b, w_down_b, vec(g_final))
    ma = _attn_prompt(sinks, q, kv, ga, wo_b, batch, seq, _tile(seq, 512))
    m, ws, ws_n, whyt, mk, pw, an = _s5_prep(*s5p, dec_seq)
    n_rows = seq // CHUNK
    g16, hT = _s5_chain(u16, m, ws, whyt, mk, pw, d_skip, jnp.zeros((batch, 1, STATE_COLS), F32),
                        batch, n_rows, _tile(n_rows, 128))
    y_prompt = _post(g16, ma, gs, xp, perm.T, *post_w, _tile(batch * seq, 512), CHUNK).reshape(batch, seq, D_MODEL)
    kv_last = kv.reshape(batch, seq, 2 * D_KV)[:, seq - WINDOW:]
    k_prompt = kv_last[..., :D_KV].reshape(1, batch, WINDOW, N_KV_HEADS, HEAD_DIM)
    v_prompt = kv_last[..., D_KV:].reshape(1, batch, WINDOW, N_KV_HEADS, HEAD_DIM)
    hr, hi = _state_split(hT[:, 0])
    ssm_re_prompt, ssm_im_prompt = hr[None], hi[None]

    ns = db * dec_seq
    xs = x_sample.reshape(ns, D_MODEL)
    perm_s = _chunk_perm(dec_seq)
    q, kv, u16, ga, gs = _proj(xs, vec(g_mix[0]), w_in_b, perm_s, _tile(ns, 512), dec_seq)
    tpad = SUBLANES
    pad3 = lambda v: jnp.pad(v.reshape(db, dec_seq, -1).astype(F32), ((0, 0), (0, tpad - dec_seq), (0, 0)))
    ga_pad = pad3(ga).astype(BF16).reshape(db * tpad, D_MODEL)
    ma_pad, k_new, v_new = _attn_sample(
        sinks, pad3(q), pad3(kv), cache_k[0].reshape(db, WINDOW, D_KV), cache_v[0].reshape(db, WINDOW, D_KV),
        ga_pad, wo_b, _tile(db, 16), dec_seq)
    ma = ma_pad.reshape(db, tpad, D_MODEL)[:, :dec_seq].reshape(ns, D_MODEL)
    to_cols = lambda s: s[0].astype(F32).transpose(1, 2, 0).reshape(HALF_COLS, db)
    from_cols = lambda h: h.reshape(SSM_GROUPS, SSM_STATE, db).transpose(2, 0, 1)[None]
    g16, hT_re, hT_im = _s5_rows(u16, m, ws_n, whyt, an, d_skip, to_cols(state_ssm_re), to_cols(state_ssm_im),
                                 _tile(db, LANES))
    y_sample = _post(g16, ma, gs, xs, perm_s.T, *post_w, PERM_ROWS, dec_seq).reshape(db, dec_seq, D_MODEL)
    k_sample = k_new.reshape(1, db, WINDOW, N_KV_HEADS, HEAD_DIM)
    v_sample = v_new.reshape(1, db, WINDOW, N_KV_HEADS, HEAD_DIM)
    ssm_re_sample, ssm_im_sample = from_cols(hT_re), from_cols(hT_im)

    return (y_prompt, y_sample, k_prompt, v_prompt, ssm_re_prompt, ssm_im_prompt,
            k_sample, v_sample, ssm_re_sample, ssm_im_sample)
```

```python
import functools

import jax
import jax.numpy as jnp
from jax import lax
from jax.experimental import pallas as pl
from jax.experimental.pallas import tpu as pltpu

F32 = jnp.float32
BF16 = jnp.bfloat16

D_MODEL = 1024
HEAD_DIM = 64
N_HEADS = 16
N_KV_HEADS = 4
WINDOW = 128
D_KV = N_KV_HEADS * HEAD_DIM
SSM_CH = 16
SSM_GROUPS = 64
SSM_STATE = 64
D_FF = 4 * D_MODEL
FF_CHUNK = 1024
RMS_EPS = 1e-5
LOG2E = 1.4426950408889634
Q_SCALE = HEAD_DIM ** -0.5 * LOG2E

LANES = 128
SUBLANES = 8
BF16_ROWS = 16
N_LANE_TILES = D_MODEL // LANES
GROUPS_PER_TILE = LANES // SSM_CH
CHUNK = 16
CHUNK_LANES = CHUNK * D_MODEL
GROUP_IO = CHUNK * SSM_CH
N_PAIRS = SSM_GROUPS // 2
PAIRS_PER_TILE = GROUPS_PER_TILE // 2
PAIR_COLS = 2 * LANES
TILE_COLS = PAIRS_PER_TILE * PAIR_COLS
STATE_COLS = N_PAIRS * PAIR_COLS
HALF_COLS = STATE_COLS // 2
PERM_ROWS = CHUNK * BF16_ROWS
PREP_PAIRS = 4
VMEM_LIMIT = 56 * 1024 * 1024

_Q0, _KV0, _U0, _GA0, _GS0, _END = 0, 1024, 1536, 2560, 3584, 4608


def _rmsnorm(x, g):
    return x * lax.rsqrt(jnp.mean(x * x, axis=-1, keepdims=True) + RMS_EPS) * g


def _params(*sem):
    return pltpu.CompilerParams(dimension_semantics=sem, vmem_limit_bytes=VMEM_LIMIT)


def _const_spec(shape):
    nd = len(shape)
    return pl.BlockSpec(shape, lambda *_: (0,) * nd, pipeline_mode=pl.Buffered(1))


def _chunk_perm(n_tok):
    n_chunks = PERM_ROWS // n_tok
    r = jnp.arange(PERM_ROWS)
    tok = (r % n_chunks) * n_tok + r // n_chunks
    return (tok[:, None] == jnp.arange(PERM_ROWS)[None, :]).astype(BF16)


def _piece_transpose(cols, masks):
    for d, msk in zip((4, 2, 1), masks):
        new = list(cols)
        for v in range(GROUPS_PER_TILE):
            if v & d == 0:
                a, b = cols[v], cols[v + d]
                new[v] = jnp.where(msk, pltpu.roll(b, SSM_CH * d, 1), a)
                new[v + d] = jnp.where(msk, b, pltpu.roll(a, LANES - SSM_CH * d, 1))
        cols = new
    return cols


def _piece_masks(rows):
    piece = lax.broadcasted_iota(jnp.int32, (rows, LANES), 1) // SSM_CH
    return [(piece & d) != 0 for d in (4, 2, 1)]


def _store_chunk_rows(x, perm_ref, out_ref, n_tok):
    n_chunks = PERM_ROWS // n_tok
    masks = _piece_masks(n_chunks)
    zero = jnp.zeros((n_chunks, LANES), F32)
    for hb in range(x.shape[0] // PERM_ROWS):
        rows = slice(hb * n_chunks, (hb + 1) * n_chunks)
        xp = jnp.dot(perm_ref[...], x[hb * PERM_ROWS:(hb + 1) * PERM_ROWS],
                     preferred_element_type=F32)
        for j in range(N_LANE_TILES):
            for hf in range(2):
                live = 8 * hf < n_tok
                cols = [xp[t * n_chunks:(t + 1) * n_chunks, j * LANES:(j + 1) * LANES] if t < n_tok else zero
                        for t in range(8 * hf, 8 * hf + 8)]
                for gl, col in enumerate(_piece_transpose(cols, masks) if live else cols):
                    c0 = (GROUPS_PER_TILE * j + gl) * GROUP_IO + hf * LANES
                    out_ref[rows, c0:c0 + LANES] = col.astype(BF16)


def _load_chunk_rows(in_ref, permt_ref, n_tok):
    n_chunks = PERM_ROWS // n_tok
    masks = _piece_masks(n_chunks)
    blocks = []
    for hb in range(in_ref.shape[0] // n_chunks):
        rows = slice(hb * n_chunks, (hb + 1) * n_chunks)
        tiles = [[None] * N_LANE_TILES for _ in range(n_tok)]
        for j in range(N_LANE_TILES):
            for hf in range(-(-n_tok // 8)):
                cols = []
                for gl in range(GROUPS_PER_TILE):
                    c0 = (GROUPS_PER_TILE * j + gl) * GROUP_IO + hf * LANES
                    cols.append(in_ref[rows, c0:c0 + LANES].astype(F32))
                for k, col in enumerate(_piece_transpose(cols, masks)):
                    if 8 * hf + k < n_tok:
                        tiles[8 * hf + k][j] = col.astype(BF16)
        xp = jnp.concatenate([jnp.concatenate(row, axis=1) for row in tiles], axis=0)
        blocks.append(jnp.dot(permt_ref[...], xp, preferred_element_type=F32).astype(BF16))
    return blocks[0] if len(blocks) == 1 else jnp.concatenate(blocks, axis=0)


def _proj_kernel(x_ref, g_ref, w_ref, perm_ref, *refs, n_tok, n_cast):
    q_ref, kv_ref, u_ref, ga_ref, gs_ref = refs[n_cast:n_cast + 5]
    h = _rmsnorm(x_ref[...], g_ref[...]).astype(BF16)

    def seg(lo, hi):
        return jnp.dot(h, w_ref[:, lo:hi], preferred_element_type=F32)

    q_ref[...] = (seg(_Q0, _KV0) * Q_SCALE).astype(BF16)
    kv_ref[...] = seg(_KV0, _U0)
    _store_chunk_rows(seg(_U0, _GA0).astype(BF16), perm_ref, u_ref, n_tok)
    ga_ref[...] = jax.nn.sigmoid(seg(_GA0, _GS0)).astype(BF16)
    gs_ref[...] = jax.nn.sigmoid(seg(_GS0, _END)).astype(BF16)
    for src, dst in zip(refs[:n_cast], refs[n_cast + 5:]):
        dst[...] = src[...].astype(BF16)


def _proj(x, g_mix, w_in, perm, tm, n_tok, cast=()):
    n = x.shape[0]
    steps = n // tm
    row = lambda w: pl.BlockSpec((tm, w), lambda i: (i, 0))
    slab = lambda w: pl.BlockSpec((w.shape[0] // steps, w.shape[1]), lambda i: (i, 0))
    assert all(w.shape[0] % (steps * BF16_ROWS) == 0 for w in cast)
    return pl.pallas_call(
        functools.partial(_proj_kernel, n_tok=n_tok, n_cast=len(cast)),
        grid=(steps,),
        in_specs=[row(D_MODEL), _const_spec((1, D_MODEL)), _const_spec(w_in.shape), _const_spec(perm.shape)]
        + [slab(w) for w in cast],
        out_specs=[row(D_MODEL), row(2 * D_KV), pl.BlockSpec((tm // n_tok, CHUNK_LANES), lambda i: (i, 0)),
                   row(D_MODEL), row(D_MODEL)] + [slab(w) for w in cast],
        out_shape=[jax.ShapeDtypeStruct((n, D_MODEL), BF16),
                   jax.ShapeDtypeStruct((n, 2 * D_KV), F32),
                   jax.ShapeDtypeStruct((n // n_tok, CHUNK_LANES), BF16),
                   jax.ShapeDtypeStruct((n, D_MODEL), BF16),
                   jax.ShapeDtypeStruct((n, D_MODEL), BF16)]
        + [jax.ShapeDtypeStruct(w.shape, BF16) for w in cast],
        compiler_params=_params("arbitrary"),
        name="proj",
    )(x, g_mix, w_in, perm, *cast)


def _dup_heads(tile):
    lo = lax.broadcasted_iota(jnp.int32, tile.shape, tile.ndim - 1) < HEAD_DIM
    rolled = pltpu.roll(tile, HEAD_DIM, tile.ndim - 1)
    return (jnp.where(lo, tile, rolled).astype(BF16), jnp.where(lo, rolled, tile).astype(BF16))


def _sink_softmax(s, sink):
    sink2 = sink * LOG2E
    m = jnp.maximum(jnp.max(s, axis=-1, keepdims=True), sink2)
    p = jnp.exp2(s - m)
    denom = jnp.sum(p, axis=-1, keepdims=True) + jnp.exp2(sink2 - m)
    return p.astype(BF16), 1.0 / denom


def _attn_prompt_kernel(sinks_ref, q_ref, kvc_ref, kvp_ref, ga_ref, wo_ref, out_ref,
                        kd_scr, vd_scr, bias_scr, attn_scr, *, tq):
    i = pl.program_id(1)
    kv_full = jnp.concatenate([kvp_ref[...], kvc_ref[...]], axis=0)
    lo_kv = lax.broadcasted_iota(jnp.int32, (WINDOW + tq, LANES), 1) < HEAD_DIM
    one = jnp.ones((), BF16)
    for t in range(2):
        kd_scr[2 * t], kd_scr[2 * t + 1] = _dup_heads(kv_full[:, t * LANES:(t + 1) * LANES])
        for par_kv, v2 in enumerate(_dup_heads(kv_full[:, D_KV + t * LANES:D_KV + (t + 1) * LANES])):
            j = 2 * t + par_kv
            vd_scr[2 * j] = jnp.where(lo_kv, v2, one)
            vd_scr[2 * j + 1] = jnp.where(lo_kv, one, v2)

    qi = lax.broadcasted_iota(jnp.int32, (WINDOW, 2 * WINDOW), 0)
    si = lax.broadcasted_iota(jnp.int32, (WINDOW, 2 * WINDOW), 1)
    band = (si > qi) & (si <= qi + WINDOW)
    in_block = si >= WINDOW
    lo = lax.broadcasted_iota(jnp.int32, (WINDOW, LANES), 1) < HEAD_DIM

    def attend(jb):
        q0 = jb * WINDOW
        has_prev = (i > 0) | (jb > 0)
        bias_scr[...] = jnp.where(band & (in_block | has_prev), 0.0, -jnp.inf)

        def head_pair(hp, c):
            j = hp // 2
            c0 = pl.multiple_of(hp * LANES, LANES)
            kd = kd_scr[j, pl.ds(q0, 2 * WINDOW), :]
            qp = q_ref[pl.ds(q0, WINDOW), pl.ds(c0, LANES)]
            res, sink_gap = [], []
            for par in range(2):
                qm = jnp.where(lo if par == 0 else ~lo, qp, jnp.zeros_like(qp))
                s = lax.dot_general(qm, kd, (((1,), (1,)), ((), ())), preferred_element_type=F32) + bias_scr[...]
                sink2 = sinks_ref[2 * hp + par] * LOG2E
                m = jnp.maximum(jnp.max(s, axis=-1, keepdims=True), sink2)
                p = jnp.exp2(s - m).astype(BF16)
                res.append(jnp.dot(p, vd_scr[2 * j + par, pl.ds(q0, 2 * WINDOW), :], preferred_element_type=F32))
                sink_gap.append(sink2 - m)
            o = jnp.where(lo, res[0], res[1])
            row_sum = pltpu.roll(jnp.where(lo, res[1], res[0]), HEAD_DIM, 1)
            denom = row_sum + jnp.exp2(jnp.where(lo, sink_gap[0], sink_gap[1]))
            attn_scr[pl.ds(q0, WINDOW), pl.ds(c0, LANES)] = (o / denom).astype(BF16)
            return c

        lax.fori_loop(0, N_HEADS // 2, head_pair, 0, unroll=8)

    def project(jb):
        rows = pl.ds(jb * WINDOW, WINDOW)
        a_out = jnp.dot(attn_scr[rows, :], wo_ref[...], preferred_element_type=F32)
        out_ref[rows, :] = ga_ref[rows, :].astype(F32) * a_out

    for jb in range(tq // WINDOW):
        attend(jb)
        project(jb)


def _attn_prompt(sinks, q, kv, ga, wo, batch, seq, tq):
    nq = seq // tq
    bpt = tq // WINDOW
    row = lambda w: pl.BlockSpec((tq, w), lambda b, i: (b * nq + i, 0))
    prev = pl.BlockSpec((WINDOW, 2 * D_KV),
                        lambda b, i: (jnp.maximum((b * nq + i) * bpt - 1, 0), 0))
    return pl.pallas_call(
        functools.partial(_attn_prompt_kernel, tq=tq),
        grid=(batch, nq),
        in_specs=[pl.BlockSpec(memory_space=pltpu.SMEM), row(D_MODEL), row(2 * D_KV), prev,
                  row(D_MODEL), _const_spec(wo.shape)],
        out_specs=row(D_MODEL),
        out_shape=jax.ShapeDtypeStruct((batch * seq, D_MODEL), F32),
        scratch_shapes=[pltpu.VMEM((N_KV_HEADS, WINDOW + tq, LANES), BF16),
                        pltpu.VMEM((2 * N_KV_HEADS, WINDOW + tq, LANES), BF16),
                        pltpu.VMEM((WINDOW, 2 * WINDOW), F32),
                        pltpu.VMEM((tq, D_MODEL), BF16)],
        compiler_params=_params("arbitrary", "arbitrary"),
        name="attn_prompt",
    )(sinks, q, kv, kv, ga, wo)


def _attn_sample_kernel(sinks_ref, q_ref, kvn_ref, ck_ref, cv_ref, ga_ref, wo_ref,
                        out_ref, ko_ref, vo_ref, attn_scr, *, bb, tpad, dec_seq):
    nk = WINDOW + tpad
    lo3 = lax.broadcasted_iota(jnp.int32, (bb, tpad, LANES), 2) < HEAD_DIM
    row = lax.broadcasted_iota(jnp.int32, (4 * tpad, nk), 0)
    si = lax.broadcasted_iota(jnp.int32, (4 * tpad, nk), 1)
    tq = row % tpad
    valid = (si > tq) & (si <= tq + WINDOW) & (si < WINDOW + dec_seq)
    hrow = lax.broadcasted_iota(jnp.int32, (4 * tpad, 1), 0) // tpad

    for t in range(2):
        sl = slice(t * LANES, (t + 1) * LANES)
        kk = jnp.concatenate([ck_ref[:, :, sl], kvn_ref[:, :, sl]], axis=1)
        vv = jnp.concatenate([cv_ref[:, :, sl],
                              kvn_ref[:, :, D_KV + t * LANES:D_KV + (t + 1) * LANES]], axis=1)
        ko_ref[:, :, sl] = kk[:, dec_seq:dec_seq + WINDOW, :]
        vo_ref[:, :, sl] = vv[:, dec_seq:dec_seq + WINDOW, :]
        kds = _dup_heads(kk)
        vds = _dup_heads(vv)
        for par_kv in range(2):
            j = 2 * t + par_kv
            kd, vd = kds[par_kv], vds[par_kv]
            parts = []
            for r in range(2):
                c0 = j * 2 * LANES + r * LANES
                qp = q_ref[:, :, c0:c0 + LANES]
                parts += [jnp.where(lo3, qp, 0.0), jnp.where(lo3, 0.0, qp)]
            lhs = jnp.concatenate(parts, axis=1).astype(BF16)
            s = jnp.einsum('bqd,bkd->bqk', lhs, kd, preferred_element_type=F32)
            sink = jnp.zeros((4 * tpad, 1), F32)
            for g in range(4):
                sink = jnp.where(hrow == g, sinks_ref[4 * j + g], sink)
            p, inv = _sink_softmax(jnp.where(valid[None], s, -jnp.inf), sink[None])
            o = jnp.einsum('bqk,bkd->bqd', p, vd, preferred_element_type=F32) * inv
            for r in range(2):
                c0 = j * 2 * LANES + r * LANES
                o_even = o[:, (2 * r) * tpad:(2 * r + 1) * tpad, :]
                o_odd = o[:, (2 * r + 1) * tpad:(2 * r + 2) * tpad, :]
                attn_scr[:, :, c0:c0 + LANES] = jnp.where(lo3, o_even, o_odd)

    attn = attn_scr[...].reshape(bb * tpad, D_MODEL).astype(BF16)
    a_out = jnp.dot(attn, wo_ref[...], preferred_element_type=F32)
    out_ref[...] = ga_ref[...].astype(F32) * a_out


def _attn_sample(sinks, q3, kvn3, ck, cv, ga2, wo, bb, dec_seq):
    db, tpad, _ = q3.shape
    blk3 = lambda r, w: pl.BlockSpec((bb, r, w), lambda i: (i, 0, 0))
    row = pl.BlockSpec((bb * tpad, D_MODEL), lambda i: (i, 0))
    return pl.pallas_call(
        functools.partial(_attn_sample_kernel, bb=bb, tpad=tpad, dec_seq=dec_seq),
        grid=(db // bb,),
        in_specs=[pl.BlockSpec(memory_space=pltpu.SMEM), blk3(tpad, D_MODEL), blk3(tpad, 2 * D_KV),
                  blk3(WINDOW, D_KV), blk3(WINDOW, D_KV), row, _const_spec(wo.shape)],
        out_specs=[row, blk3(WINDOW, D_KV), blk3(WINDOW, D_KV)],
        out_shape=[jax.ShapeDtypeStruct((db * tpad, D_MODEL), F32),
                   jax.ShapeDtypeStruct((db, WINDOW, D_KV), F32),
                   jax.ShapeDtypeStruct((db, WINDOW, D_KV), F32)],
        scratch_shapes=[pltpu.VMEM((bb, tpad, D_MODEL), F32)],
        compiler_params=_params("arbitrary"),
        name="attn_sample",
    )(sinks, q3, kvn3, ck, cv, ga2, wo)


def _s5_discretize(lam_re, lam_im, log_dt, b_re, b_im):
    dt = jnp.exp(log_dt)[:, None]
    decay = jnp.exp(lam_re * dt)
    ab_re = decay * jnp.cos(lam_im * dt)
    ab_im = decay * jnp.sin(lam_im * dt)
    nr, ni = ab_re - 1.0, ab_im
    den = lam_re * lam_re + lam_im * lam_im
    f_re = ((nr * lam_re + ni * lam_im) / den)[..., None]
    f_im = ((ni * lam_re - nr * lam_im) / den)[..., None]
    return ab_re, ab_im, f_re * b_re - f_im * b_im, f_re * b_im + f_im * b_re


def _state_split(h):
    h = h.reshape(h.shape[:-1] + (N_PAIRS, 2, 2, SSM_STATE))
    unpair = lambda a: a.reshape(a.shape[:-3] + (SSM_GROUPS, SSM_STATE))
    return unpair(h[..., 0, :, :]), unpair(h[..., 1, :, :])


def _shift_lanes(x, n):
    a, b = x[:, :LANES], x[:, LANES:]
    lane = lax.broadcasted_iota(jnp.int32, a.shape, 1)
    if n == 0:
        return x
    if n >= LANES:
        r = n - LANES
        hi = a if r == 0 else jnp.where(lane < r, 0.0, pltpu.roll(a, r, 1))
        return jnp.concatenate([jnp.zeros_like(a), hi], axis=1)
    ra, rb = pltpu.roll(a, n, 1), pltpu.roll(b, n, 1)
    return jnp.concatenate([jnp.where(lane < n, 0.0, ra), jnp.where(lane < n, ra, rb)], axis=1)


def _dot_nt_split(lhs, rhs):
    nt = lambda a, b: lax.dot_general(a, b, (((1,), (1,)), ((), ())), preferred_element_type=F32)
    l_hi, r_hi = lhs.astype(BF16), rhs.astype(BF16)
    l_lo = (lhs - l_hi.astype(F32)).astype(BF16)
    r_lo = (rhs - r_hi.astype(F32)).astype(BF16)
    return nt(l_hi, r_hi) + nt(l_hi, r_lo) + nt(l_lo, r_hi)


def _s5_prep_kernel(a_ref, c_ref, bt_ref, m_ref, ws_ref, wsn_ref, whyt_ref, mk_ref, pw_ref, an_ref, *, n_tok):
    for p in range(PREP_PAIRS):
        one, two, lanes = pl.ds(p, 1), pl.ds(2 * p, 2), pl.ds(p * LANES, LANES)
        _s5_prep_pair(a_ref.at[one], c_ref.at[one], bt_ref.at[one], m_ref.at[two], ws_ref.at[two],
                      wsn_ref.at[two], whyt_ref.at[two], mk_ref.at[:, :, :, lanes], pw_ref.at[:, :, lanes],
                      an_ref.at[:, :, lanes], n_tok)


def _s5_prep_pair(a_ref, c_ref, bt_ref, m_ref, ws_ref, wsn_ref, whyt_ref, mk_ref, pw_ref, an_ref, n_tok):
    ar, ai = a_ref[0, 0], a_ref[0, 1]
    cr, ci = c_ref[0, 0], c_ref[0, 1]
    br, bi = bt_ref[0, 0], bt_ref[0, 1]
    pr, pi = [jnp.ones_like(ar)], [jnp.zeros_like(ai)]
    for _ in range(CHUNK):
        pr, pi = pr + [pr[-1] * ar - pi[-1] * ai], pi + [pr[-1] * ai + pi[-1] * ar]
    first = lax.broadcasted_iota(jnp.int32, (SSM_CH, LANES), 1) < SSM_STATE
    ca = [(cr * pr[t] - ci * pi[t], -(cr * pi[t] + ci * pr[t])) for t in range(CHUNK + 1)]
    ca_full = jnp.concatenate([jnp.concatenate(ca[t], axis=1) for t in range(CHUNK)], axis=0)
    zero = jnp.zeros((SSM_CH, LANES), F32)
    for e in range(2):
        own = first if e == 0 else ~first
        pick = lambda v: jnp.where(own, v, zero)
        for t in range(CHUNK):
            rows = slice(t * SSM_CH, (t + 1) * SSM_CH)
            whyt_ref[e, rows, :LANES] = pick(ca[t + 1][0]).astype(BF16)
            whyt_ref[e, rows, LANES:] = pick(ca[t + 1][1]).astype(BF16)
            for ref, k in ((ws_ref, CHUNK - 1 - t), (wsn_ref, n_tok - 1 - t)):
                if k >= 0:
                    ref[e, rows, :LANES] = pick(br * pr[k] - bi * pi[k]).astype(BF16)
                    ref[e, rows, LANES:] = pick(br * pi[k] + bi * pr[k]).astype(BF16)
                else:
                    ref[e, rows, :] = jnp.zeros((SSM_CH, PAIR_COLS), BF16)
        k_row = _dot_nt_split(jnp.concatenate([pick(br), pick(bi)], axis=1), ca_full)
        for s in range(CHUNK):
            m_ref[e, s * SSM_CH:(s + 1) * SSM_CH, :] = _shift_lanes(k_row, s * SSM_CH).astype(BF16)
    sub = lax.broadcasted_iota(jnp.int32, (SUBLANES, LANES), 0)
    qr, qi = [pr[CHUNK]], [pi[CHUNK]]
    for _ in range(SUBLANES - 1):
        qr, qi = qr + [qr[-1] * pr[CHUNK] - qi[-1] * pi[CHUNK]], qi + [qr[-1] * pi[CHUNK] + qi[-1] * pr[CHUNK]]
    for part, q in enumerate((qr, qi)):
        rows = jnp.zeros((SUBLANES, LANES), F32)
        for s in range(SUBLANES):
            rows = jnp.where(sub == s, q[s], rows)
        pw_ref[part] = rows
        for k, shift in enumerate((1, 2, 4)):
            mk_ref[k, part] = jnp.where(sub >= shift, q[shift - 1], 0.0)
    an_ref[0] = jnp.broadcast_to(pr[n_tok], (SUBLANES, LANES))
    an_ref[1] = jnp.broadcast_to(pi[n_tok], (SUBLANES, LANES))


def _s5_prep(lam_re, lam_im, log_dt, b_re, b_im, c_re, c_im, n_tok):
    ab_re, ab_im, bb_re, bb_im = _s5_discretize(lam_re, lam_im, log_dt, b_re, b_im)
    pair = lambda v: v.reshape(N_PAIRS, 2, v.shape[1], SSM_STATE).transpose(0, 2, 1, 3).reshape(
        N_PAIRS, v.shape[1], LANES)
    a = jnp.stack([pair(ab_re[:, None, :]), pair(ab_im[:, None, :])], axis=1)
    c = jnp.stack([pair(c_re), pair(c_im)], axis=1)
    bt = jnp.stack([pair(bb_re.transpose(0, 2, 1)), pair(bb_im.transpose(0, 2, 1))], axis=1)
    blk4 = lambda r: pl.BlockSpec((PREP_PAIRS, 2, r, LANES), lambda q: (q, 0, 0, 0))
    w_spec = pl.BlockSpec((2 * PREP_PAIRS, GROUP_IO, PAIR_COLS), lambda q: (q, 0, 0))
    w_shape = jax.ShapeDtypeStruct((SSM_GROUPS, GROUP_IO, PAIR_COLS), BF16)
    return pl.pallas_call(
        functools.partial(_s5_prep_kernel, n_tok=n_tok),
        grid=(N_PAIRS // PREP_PAIRS,),
        in_specs=[blk4(1), blk4(SSM_CH), blk4(SSM_CH)],
        out_specs=[w_spec, w_spec, w_spec, w_spec,
                   pl.BlockSpec((3, 2, SUBLANES, PREP_PAIRS * LANES), lambda q: (0, 0, 0, q)),
                   pl.BlockSpec((2, SUBLANES, PREP_PAIRS * LANES), lambda q: (0, 0, q)),
                   pl.BlockSpec((2, SUBLANES, PREP_PAIRS * LANES), lambda q: (0, 0, q))],
        out_shape=[w_shape, w_shape, w_shape, w_shape,
                   jax.ShapeDtypeStruct((3, 2, SUBLANES, HALF_COLS), F32),
                   jax.ShapeDtypeStruct((2, SUBLANES, HALF_COLS), F32),
                   jax.ShapeDtypeStruct((2, SUBLANES, HALF_COLS), F32)],
        compiler_params=_params("arbitrary"),
        name="s5_prep",
    )(a, c, bt)


def _group_io(ref, g):
    return ref.at[:, g * GROUP_IO:(g + 1) * GROUP_IO]


def _s5_emit_tile(u_ref, st_scr, m_ref, whyt_ref, dt_ref, g_ref, j):
    for gl in range(GROUPS_PER_TILE):
        g = GROUPS_PER_TILE * j + gl
        q0 = (gl // 2) * PAIR_COLS
        u_g = _group_io(u_ref, g)[...]
        y = (jnp.dot(u_g, m_ref[g], preferred_element_type=F32)
             + lax.dot_general(st_scr[:, q0:q0 + PAIR_COLS].astype(BF16), whyt_ref[g],
                               (((1,), (1,)), ((), ())), preferred_element_type=F32)
             + dt_ref[g] * u_g.astype(F32))
        _group_io(g_ref, g)[...] = jax.nn.gelu(y).astype(BF16)


def _s5_local_states(u_ref, ws_ref, j, q):
    g = GROUPS_PER_TILE * j + 2 * q
    return (jnp.dot(_group_io(u_ref, g)[...], ws_ref[g], preferred_element_type=F32)
            + jnp.dot(_group_io(u_ref, g + 1)[...], ws_ref[g + 1], preferred_element_type=F32))


def _cmul_add(ar, ai, hr, hi, xr, xi):
    return ar * hr - ai * hi + xr, ar * hi + ai * hr + xi


def _s5_chain_kernel(u_ref, m_ref, ws_ref, whyt_ref, mk_ref, pw_ref, dt_ref, h0_ref, g_ref, hT_ref,
                     carry_scr, st_scr, *, rows):
    @pl.when(pl.program_id(1) == 0)
    def _():
        carry_scr[...] = jnp.broadcast_to(h0_ref[0], (SUBLANES, STATE_COLS))

    row0 = lax.broadcasted_iota(jnp.int32, (SUBLANES, LANES), 0) == 0
    last = lambda h: jnp.broadcast_to(h[SUBLANES - 1:, :], (SUBLANES, LANES))

    for j in range(N_LANE_TILES):
        for q in range(PAIRS_PER_TILE):
            st_scr[:, q * PAIR_COLS:(q + 1) * PAIR_COLS] = _s5_local_states(u_ref, ws_ref, j, q)

        def step(r, carry):
            r0 = pl.multiple_of(r * SUBLANES, SUBLANES)
            out = []
            for q in range(PAIRS_PER_TILE):
                re_c = slice(q * PAIR_COLS, q * PAIR_COLS + LANES)
                im_c = slice(q * PAIR_COLS + LANES, (q + 1) * PAIR_COLS)
                hc = slice((j * PAIRS_PER_TILE + q) * LANES, (j * PAIRS_PER_TILE + q + 1) * LANES)
                xr = st_scr[pl.ds(r0, SUBLANES), re_c]
                xi = st_scr[pl.ds(r0, SUBLANES), im_c]
                for k, shift in enumerate((1, 2, 4)):
                    xr, xi = _cmul_add(mk_ref[k, 0, :, hc], mk_ref[k, 1, :, hc],
                                       pltpu.roll(xr, shift, 0), pltpu.roll(xi, shift, 0), xr, xi)
                cr, ci = carry[q]
                hr, hi = _cmul_add(pw_ref[0, :, hc], pw_ref[1, :, hc], cr, ci, xr, xi)
                st_scr[pl.ds(r0, SUBLANES), re_c] = jnp.where(row0, cr, pltpu.roll(hr, 1, 0))
                st_scr[pl.ds(r0, SUBLANES), im_c] = jnp.where(row0, ci, pltpu.roll(hi, 1, 0))
                out.append((last(hr), last(hi)))
            return tuple(out)

        c0 = j * TILE_COLS
        init = tuple((carry_scr[:, c0 + q * PAIR_COLS:c0 + q * PAIR_COLS + LANES],
                      carry_scr[:, c0 + q * PAIR_COLS + LANES:c0 + (q + 1) * PAIR_COLS])
                     for q in range(PAIRS_PER_TILE))
        final = lax.fori_loop(0, rows // SUBLANES, step, init, unroll=True)
        for q in range(PAIRS_PER_TILE):
            carry_scr[:, c0 + q * PAIR_COLS:c0 + q * PAIR_COLS + LANES] = final[q][0]
            carry_scr[:, c0 + q * PAIR_COLS + LANES:c0 + (q + 1) * PAIR_COLS] = final[q][1]

        _s5_emit_tile(u_ref, st_scr, m_ref, whyt_ref, dt_ref, g_ref, j)

    hT_ref[0] = carry_scr[...]


def _s5_chain(u16, m, ws, why, mk, pw, d_tiled, h0, batch, n_rows, rows):
    nblk = n_rows // rows
    row = pl.BlockSpec((rows, CHUNK_LANES), lambda b, i: (b * nblk + i, 0))
    return pl.pallas_call(
        functools.partial(_s5_chain_kernel, rows=rows),
        grid=(batch, nblk),
        in_specs=[row, _const_spec(m.shape), _const_spec(ws.shape), _const_spec(why.shape),
                  _const_spec(mk.shape), _const_spec(pw.shape), _const_spec(d_tiled.shape),
                  pl.BlockSpec((1, 1, STATE_COLS), lambda b, i: (b, 0, 0))],
        out_specs=[row, pl.BlockSpec((1, SUBLANES, STATE_COLS), lambda b, i: (b, 0, 0))],
        out_shape=[jax.ShapeDtypeStruct(u16.shape, BF16),
                   jax.ShapeDtypeStruct((batch, SUBLANES, STATE_COLS), F32)],
        scratch_shapes=[pltpu.VMEM((SUBLANES, STATE_COLS), F32), pltpu.VMEM((rows, TILE_COLS), F32)],
        compiler_params=_params("arbitrary", "arbitrary"),
        name="s5_chain",
    )(u16, m, ws, why, mk, pw, d_tiled, h0)


def _s5_rows_kernel(u_ref, m_ref, ws_ref, whyt_ref, an_ref, dt_ref, h0r_ref, h0i_ref, g_ref, hTr_ref, hTi_ref,
                    st_scr):
    for j in range(N_LANE_TILES):
        for q in range(PAIRS_PER_TILE):
            hc = slice((j * PAIRS_PER_TILE + q) * LANES, (j * PAIRS_PER_TILE + q + 1) * LANES)
            h0r, h0i = h0r_ref[hc, :].T, h0i_ref[hc, :].T
            local = _s5_local_states(u_ref, ws_ref, j, q)
            hr, hi = _cmul_add(an_ref[0, :1, hc], an_ref[1, :1, hc], h0r, h0i, local[:, :LANES], local[:, LANES:])
            hTr_ref[hc, :] = hr.T
            hTi_ref[hc, :] = hi.T
            st_scr[:, q * PAIR_COLS:q * PAIR_COLS + LANES] = h0r
            st_scr[:, q * PAIR_COLS + LANES:(q + 1) * PAIR_COLS] = h0i
        _s5_emit_tile(u_ref, st_scr, m_ref, whyt_ref, dt_ref, g_ref, j)


def _s5_rows(u16, m, ws, why, an, d_tiled, h0_re_t, h0_im_t, rows):
    n = u16.shape[0]
    once = pl.Buffered(1)
    row = pl.BlockSpec((rows, CHUNK_LANES), lambda i: (i, 0), pipeline_mode=once)
    col = pl.BlockSpec((HALF_COLS, rows), lambda i: (0, i), pipeline_mode=once)
    state = jax.ShapeDtypeStruct((HALF_COLS, n), F32)
    return pl.pallas_call(
        _s5_rows_kernel,
        grid=(n // rows,),
        in_specs=[row, _const_spec(m.shape), _const_spec(ws.shape), _const_spec(why.shape),
                  _const_spec(an.shape), _const_spec(d_tiled.shape), col, col],
        out_specs=[row, col, col],
        out_shape=[jax.ShapeDtypeStruct(u16.shape, BF16), state, state],
        scratch_shapes=[pltpu.VMEM((rows, TILE_COLS), F32)],
        compiler_params=_params("arbitrary"),
        name="s5_rows",
    )(u16, m, ws, why, an, d_tiled, h0_re_t, h0_im_t)


def _post_kernel(g_ref, ma_ref, gs_ref, x_ref, permt_ref, wglu_ref, wout_ref, gffn_ref, wup_ref, wdown_ref,
                 gfin_ref, out_ref, *, n_tok):
    g = _load_chunk_rows(g_ref, permt_ref, n_tok)
    glu = jnp.dot(g, wglu_ref[...], preferred_element_type=F32)
    s_out = glu[:, :D_MODEL] * jax.nn.sigmoid(glu[:, D_MODEL:])
    merged = ma_ref[...] + gs_ref[...].astype(F32) * s_out
    x1 = x_ref[...] + jnp.dot(merged.astype(BF16), wout_ref[...], preferred_element_type=F32)
    h2 = _rmsnorm(x1, gffn_ref[...]).astype(BF16)
    x2 = x1
    for c in range(0, D_FF, FF_CHUNK):
        up = jnp.dot(h2, wup_ref[:, c:c + FF_CHUNK], preferred_element_type=F32)
        act = jnp.square(jnp.maximum(up, 0.0)).astype(BF16)
        x2 = x2 + jnp.dot(act, wdown_ref[c:c + FF_CHUNK, :], preferred_element_type=F32)
    out_ref[...] = _rmsnorm(x2, gfin_ref[...])


def _post(g, ma, gs, x, permt, w_glu, w_out, g_ffn, w_up, w_down, g_final, tm, n_tok):
    n = x.shape[0]
    row = pl.BlockSpec((tm, D_MODEL), lambda i: (i, 0))
    g_spec = pl.BlockSpec((tm // n_tok, CHUNK_LANES), lambda i: (i, 0))
    return pl.pallas_call(
        functools.partial(_post_kernel, n_tok=n_tok),
        grid=(n // tm,),
        in_specs=[g_spec, row, row, row, _const_spec(permt.shape), _const_spec(w_glu.shape),
                  _const_spec(w_out.shape), _const_spec((1, D_MODEL)), _const_spec(w_up.shape),
                  _const_spec(w_down.shape), _const_spec((1, D_MODEL))],
        out_specs=row,
        out_shape=jax.ShapeDtypeStruct((n, D_MODEL), F32),
        compiler_params=_params("arbitrary"),
        name="post",
    )(g, ma, gs, x, permt, w_glu, w_out, g_ffn, w_up, w_down, g_final)


def _tile(n, pref):
    t = pref
    while n % t:
        t //= 2
    return t


def kernel(x_prompt, x_sample, cache_k, cache_v, state_ssm_re, state_ssm_im, g_mix, w_in, attn_sinks,
           w_attn_o, ssm_lambda_re, ssm_lambda_im, ssm_log_dt, ssm_b_re, ssm_b_im, ssm_c_re, ssm_c_im,
           ssm_d, w_glu, w_out, g_ffn, w_up, w_down, g_final):
    batch, seq, _ = x_prompt.shape
    db, dec_seq, _ = x_sample.shape
    assert w_in.shape[0] == 1, "one layer"
    assert seq % PERM_ROWS == 0 and dec_seq in (1, 2, 4, 8) and (db * dec_seq) % PERM_ROWS == 0

    vec = lambda v: v.reshape(1, D_MODEL).astype(F32)
    w_in_b = w_in[0].astype(BF16)
    sinks = attn_sinks[0].astype(F32)
    s5p = (ssm_lambda_re[0], ssm_lambda_im[0], ssm_log_dt[0], ssm_b_re[0], ssm_b_im[0], ssm_c_re[0], ssm_c_im[0])
    d_skip = jnp.tile(ssm_d[0].astype(F32).reshape(SSM_GROUPS, 1, SSM_CH), (1, 1, CHUNK))
    perm = _chunk_perm(CHUNK)

    xp = x_prompt.reshape(batch * seq, D_MODEL)
    q, kv, u16, ga, gs, wo_b, w_glu_b, w_out_b, w_up_b, w_down_b = _proj(
        xp, vec(g_mix[0]), w_in_b, perm, _tile(batch * seq, 1024), CHUNK,
        cast=(w_attn_o[0], w_glu[0], w_out[0], w_up[0], w_down[0]))
    post_w = (w_glu_b, w_out_b, vec(g_ffn[0]), w_up_b, w_down_b, vec(g_final))
    ma = _attn_prompt(sinks, q, kv, ga, wo_b, batch, seq, _tile(seq, 512))
    m, ws, ws_n, whyt, mk, pw, an = _s5_prep(*s5p, dec_seq)
    n_rows = seq // CHUNK
    g16, hT = _s5_chain(u16, m, ws, whyt, mk, pw, d_skip, jnp.zeros((batch, 1, STATE_COLS), F32),
                        batch, n_rows, _tile(n_rows, 128))
    y_prompt = _post(g16, ma, gs, xp, perm.T, *post_w, _tile(batch * seq, 512), CHUNK).reshape(batch, seq, D_MODEL)
    kv_last = kv.reshape(batch, seq, 2 * D_KV)[:, seq - WINDOW:]
    k_prompt = kv_last[..., :D_KV].reshape(1, batch, WINDOW, N_KV_HEADS, HEAD_DIM)
    v_prompt = kv_last[..., D_KV:].reshape(1, batch, WINDOW, N_KV_HEADS, HEAD_DIM)
    hr, hi = _state_split(hT[:, 0])
    ssm_re_prompt, ssm_im_prompt = hr[None], hi[None]

    ns = db * dec_seq
    xs = x_sample.reshape(ns, D_MODEL)
    perm_s = _chunk_perm(dec_seq)
    q, kv, u16, ga, gs = _proj(xs, vec(g_mix[0]), w_in_b, perm_s, _tile(ns, 512), dec_seq)
    tpad = SUBLANES
    pad3 = lambda v: jnp.pad(v.reshape(db, dec_seq, -1).astype(F32), ((0, 0), (0, tpad - dec_seq), (0, 0)))
    ga_pad = pad3(ga).astype(BF16).reshape(db * tpad, D_MODEL)
    ma_pad, k_new, v_new = _attn_sample(
        sinks, pad3(q), pad3(kv), cache_k[0].reshape(db, WINDOW, D_KV), cache_v[0].reshape(db, WINDOW, D_KV),
        ga_pad, wo_b, _tile(db, 16), dec_seq)
    ma = ma_pad.reshape(db, tpad, D_MODEL)[:, :dec_seq].reshape(ns, D_MODEL)
    to_cols = lambda s: s[0].astype(F32).transpose(1, 2, 0).reshape(HALF_COLS, db)
    from_cols = lambda h: h.reshape(SSM_GROUPS, SSM_STATE, db).transpose(2, 0, 1)[None]
    g16, hT_re, hT_im = _s5_rows(u16, m, ws_n, whyt, an, d_skip, to_cols(state_ssm_re), to_cols(state_ssm_im),
                                 _tile(db, LANES))
    y_sample = _post(g16, ma, gs, xs, perm_s.T, *post_w, PERM_ROWS, dec_seq).reshape(db, dec_seq, D_MODEL)
    k_sample = k_new.reshape(1, db, WINDOW, N_KV_HEADS, HEAD_DIM)
    v_sample = v_new.reshape(1, db, WINDOW, N_KV_HEADS, HEAD_DIM)
    ssm_re_sample, ssm_im_sample = from_cols(hT_re), from_cols(hT_im)

    return (y_prompt, y_sample, k_prompt, v_prompt, ssm_re_prompt, ssm_im_prompt,
            k_sample, v_sample, ssm_re_sample, ssm_im_sample)
```

```python
import functools

import jax
import jax.numpy as jnp
from jax import lax
from jax.experimental import pallas as pl
from jax.experimental.pallas import tpu as pltpu

F32 = jnp.float32
BF16 = jnp.bfloat16

D_MODEL = 1024
HEAD_DIM = 64
N_HEADS = 16
N_KV_HEADS = 4
WINDOW = 128
D_KV = N_KV_HEADS * HEAD_DIM
SSM_CH = 16
SSM_GROUPS = 64
SSM_STATE = 64
D_FF = 4 * D_MODEL
FF_CHUNK = 1024
RMS_EPS = 1e-5
LOG2E = 1.4426950408889634
Q_SCALE = HEAD_DIM ** -0.5 * LOG2E

LANES = 128
SUBLANES = 8
BF16_ROWS = 16
N_LANE_TILES = D_MODEL // LANES
GROUPS_PER_TILE = LANES // SSM_CH
CHUNK = 16
CHUNK_LANES = CHUNK * D_MODEL
GROUP_IO = CHUNK * SSM_CH
N_PAIRS = SSM_GROUPS // 2
PAIRS_PER_TILE = GROUPS_PER_TILE // 2
PAIR_COLS = 2 * LANES
TILE_COLS = PAIRS_PER_TILE * PAIR_COLS
STATE_COLS = N_PAIRS * PAIR_COLS
HALF_COLS = STATE_COLS // 2
PERM_ROWS = CHUNK * BF16_ROWS
PREP_PAIRS = 4
VMEM_LIMIT = 56 * 1024 * 1024

_Q0, _KV0, _U0, _GA0, _GS0, _END = 0, 1024, 1536, 2560, 3584, 4608


def _rmsnorm(x, g):
    return x * lax.rsqrt(jnp.mean(x * x, axis=-1, keepdims=True) + RMS_EPS) * g


def _params(*sem):
    return pltpu.CompilerParams(dimension_semantics=sem, vmem_limit_bytes=VMEM_LIMIT)


def _const_spec(shape):
    nd = len(shape)
    return pl.BlockSpec(shape, lambda *_: (0,) * nd, pipeline_mode=pl.Buffered(1))


def _chunk_perm(n_tok):
    n_chunks = PERM_ROWS // n_tok
    r = jnp.arange(PERM_ROWS)
    tok = (r % n_chunks) * n_tok + r // n_chunks
    return (tok[:, None] == jnp.arange(PERM_ROWS)[None, :]).astype(BF16)


def _piece_transpose(cols, masks):
    for d, msk in zip((4, 2, 1), masks):
        new = list(cols)
        for v in range(GROUPS_PER_TILE):
            if v & d == 0:
                a, b = cols[v], cols[v + d]
                new[v] = jnp.where(msk, pltpu.roll(b, SSM_CH * d, 1), a)
                new[v + d] = jnp.where(msk, b, pltpu.roll(a, LANES - SSM_CH * d, 1))
        cols = new
    return cols


def _piece_masks(rows):
    piece = lax.broadcasted_iota(jnp.int32, (rows, LANES), 1) // SSM_CH
    return [(piece & d) != 0 for d in (4, 2, 1)]


def _store_chunk_rows(x, perm_ref, out_ref, n_tok):
    n_chunks = PERM_ROWS // n_tok
    masks = _piece_masks(n_chunks)
    zero = jnp.zeros((n_chunks, LANES), F32)
    for hb in range(x.shape[0] // PERM_ROWS):
        rows = slice(hb * n_chunks, (hb + 1) * n_chunks)
        xp = jnp.dot(perm_ref[...], x[hb * PERM_ROWS:(hb + 1) * PERM_ROWS],
                     preferred_element_type=F32)
        for j in range(N_LANE_TILES):
            for hf in range(2):
                live = 8 * hf < n_tok
                cols = [xp[t * n_chunks:(t + 1) * n_chunks, j * LANES:(j + 1) * LANES] if t < n_tok else zero
                        for t in range(8 * hf, 8 * hf + 8)]
                for gl, col in enumerate(_piece_transpose(cols, masks) if live else cols):
                    c0 = (GROUPS_PER_TILE * j + gl) * GROUP_IO + hf * LANES
                    out_ref[rows, c0:c0 + LANES] = col.astype(BF16)


def _load_chunk_rows(in_ref, permt_ref, n_tok):
    n_chunks = PERM_ROWS // n_tok
    masks = _piece_masks(n_chunks)
    blocks = []
    for hb in range(in_ref.shape[0] // n_chunks):
        rows = slice(hb * n_chunks, (hb + 1) * n_chunks)
        tiles = [[None] * N_LANE_TILES for _ in range(n_tok)]
        for j in range(N_LANE_TILES):
            for hf in range(-(-n_tok // 8)):
                cols = []
                for gl in range(GROUPS_PER_TILE):
                    c0 = (GROUPS_PER_TILE * j + gl) * GROUP_IO + hf * LANES
                    cols.append(in_ref[rows, c0:c0 + LANES].astype(F32))
                for k, col in enumerate(_piece_transpose(cols, masks)):
                    if 8 * hf + k < n_tok:
                        tiles[8 * hf + k][j] = col.astype(BF16)
        xp = jnp.concatenate([jnp.concatenate(row, axis=1) for row in tiles], axis=0)
        blocks.append(jnp.dot(permt_ref[...], xp, preferred_element_type=F32).astype(BF16))
    return blocks[0] if len(blocks) == 1 else jnp.concatenate(blocks, axis=0)


def _proj_kernel(x_ref, g_ref, w_ref, perm_ref, *refs, n_tok, n_cast):
    q_ref, kv_ref, u_ref, ga_ref, gs_ref = refs[n_cast:n_cast + 5]
    h = _rmsnorm(x_ref[...], g_ref[...]).astype(BF16)

    def seg(lo, hi):
        return jnp.dot(h, w_ref[:, lo:hi], preferred_element_type=F32)

    q_ref[...] = (seg(_Q0, _KV0) * Q_SCALE).astype(BF16)
    kv_ref[...] = seg(_KV0, _U0)
    _store_chunk_rows(seg(_U0, _GA0).astype(BF16), perm_ref, u_ref, n_tok)
    ga_ref[...] = jax.nn.sigmoid(seg(_GA0, _GS0)).astype(BF16)
    gs_ref[...] = jax.nn.sigmoid(seg(_GS0, _END)).astype(BF16)
    for src, dst in zip(refs[:n_cast], refs[n_cast + 5:]):
        dst[...] = src[...].astype(BF16)


def _proj(x, g_mix, w_in, perm, tm, n_tok, cast=()):
    n = x.shape[0]
    steps = n // tm
    row = lambda w: pl.BlockSpec((tm, w), lambda i: (i, 0))
    slab = lambda w: pl.BlockSpec((w.shape[0] // steps, w.shape[1]), lambda i: (i, 0))
    assert all(w.shape[0] % (steps * BF16_ROWS) == 0 for w in cast)
    return pl.pallas_call(
        functools.partial(_proj_kernel, n_tok=n_tok, n_cast=len(cast)),
        grid=(steps,),
        in_specs=[row(D_MODEL), _const_spec((1, D_MODEL)), _const_spec(w_in.shape), _const_spec(perm.shape)]
        + [slab(w) for w in cast],
        out_specs=[row(D_MODEL), row(2 * D_KV), pl.BlockSpec((tm // n_tok, CHUNK_LANES), lambda i: (i, 0)),
                   row(D_MODEL), row(D_MODEL)] + [slab(w) for w in cast],
        out_shape=[jax.ShapeDtypeStruct((n, D_MODEL), BF16),
                   jax.ShapeDtypeStruct((n, 2 * D_KV), F32),
                   jax.ShapeDtypeStruct((n // n_tok, CHUNK_LANES), BF16),
                   jax.ShapeDtypeStruct((n, D_MODEL), BF16),
                   jax.ShapeDtypeStruct((n, D_MODEL), BF16)]
        + [jax.ShapeDtypeStruct(w.shape, BF16) for w in cast],
        compiler_params=_params("arbitrary"),
        name="proj",
    )(x, g_mix, w_in, perm, *cast)


def _dup_heads(tile):
    lo = lax.broadcasted_iota(jnp.int32, tile.shape, tile.ndim - 1) < HEAD_DIM
    rolled = pltpu.roll(tile, HEAD_DIM, tile.ndim - 1)
    return (jnp.where(lo, tile, rolled).astype(BF16), jnp.where(lo, rolled, tile).astype(BF16))


def _sink_softmax(s, sink):
    sink2 = sink * LOG2E
    m = jnp.maximum(jnp.max(s, axis=-1, keepdims=True), sink2)
    p = jnp.exp2(s - m)
    denom = jnp.sum(p, axis=-1, keepdims=True) + jnp.exp2(sink2 - m)
    return p.astype(BF16), 1.0 / denom


def _attn_prompt_kernel(sinks_ref, q_ref, kvc_ref, kvp_ref, ga_ref, wo_ref, out_ref,
                        kd_scr, vd_scr, bias_scr, attn_scr, *, tq):
    i = pl.program_id(1)
    kv_full = jnp.concatenate([kvp_ref[...], kvc_ref[...]], axis=0)
    for t in range(2):
        ke, ko = _dup_heads(kv_full[:, t * LANES:(t + 1) * LANES])
        ve, vo = _dup_heads(kv_full[:, D_KV + t * LANES:D_KV + (t + 1) * LANES])
        kd_scr[2 * t], kd_scr[2 * t + 1] = ke, ko
        vd_scr[2 * t], vd_scr[2 * t + 1] = ve, vo

    qi = lax.broadcasted_iota(jnp.int32, (WINDOW, 2 * WINDOW), 0)
    si = lax.broadcasted_iota(jnp.int32, (WINDOW, 2 * WINDOW), 1)
    band = (si > qi) & (si <= qi + WINDOW)
    in_block = si >= WINDOW
    lo = lax.broadcasted_iota(jnp.int32, (WINDOW, LANES), 1) < HEAD_DIM

    def attend(jb):
        q0 = jb * WINDOW
        has_prev = (i > 0) | (jb > 0)
        bias_scr[...] = jnp.where(band & (in_block | has_prev), 0.0, -jnp.inf)

        def head_pair(hp, c):
            j = hp // 2
            c0 = pl.multiple_of(hp * LANES, LANES)
            kd = kd_scr[j, pl.ds(q0, 2 * WINDOW), :]
            vd = vd_scr[j, pl.ds(q0, 2 * WINDOW), :]
            qp = q_ref[pl.ds(q0, WINDOW), pl.ds(c0, LANES)]
            outs = []
            for par in range(2):
                qm = jnp.where(lo if par == 0 else ~lo, qp, jnp.zeros_like(qp))
                s = lax.dot_general(qm, kd, (((1,), (1,)), ((), ())), preferred_element_type=F32)
                p, inv = _sink_softmax(s + bias_scr[...], sinks_ref[2 * hp + par])
                outs.append(jnp.dot(p, vd, preferred_element_type=F32) * inv)
            attn_scr[pl.ds(q0, WINDOW), pl.ds(c0, LANES)] = jnp.where(lo, outs[0], outs[1]).astype(BF16)
            return c

        lax.fori_loop(0, N_HEADS // 2, head_pair, 0, unroll=8)

    def project(jb):
        rows = pl.ds(jb * WINDOW, WINDOW)
        a_out = jnp.dot(attn_scr[rows, :], wo_ref[...], preferred_element_type=F32)
        out_ref[rows, :] = ga_ref[rows, :].astype(F32) * a_out

    for jb in range(tq // WINDOW):
        attend(jb)
        project(jb)


def _attn_prompt(sinks, q, kv, ga, wo, batch, seq, tq):
    nq = seq // tq
    bpt = tq // WINDOW
    row = lambda w: pl.BlockSpec((tq, w), lambda b, i: (b * nq + i, 0))
    prev = pl.BlockSpec((WINDOW, 2 * D_KV),
                        lambda b, i: (jnp.maximum((b * nq + i) * bpt - 1, 0), 0))
    return pl.pallas_call(
        functools.partial(_attn_prompt_kernel, tq=tq),
        grid=(batch, nq),
        in_specs=[pl.BlockSpec(memory_space=pltpu.SMEM), row(D_MODEL), row(2 * D_KV), prev,
                  row(D_MODEL), _const_spec(wo.shape)],
        out_specs=row(D_MODEL),
        out_shape=jax.ShapeDtypeStruct((batch * seq, D_MODEL), F32),
        scratch_shapes=[pltpu.VMEM((N_KV_HEADS, WINDOW + tq, LANES), BF16),
                        pltpu.VMEM((N_KV_HEADS, WINDOW + tq, LANES), BF16),
                        pltpu.VMEM((WINDOW, 2 * WINDOW), F32),
                        pltpu.VMEM((tq, D_MODEL), BF16)],
        compiler_params=_params("arbitrary", "arbitrary"),
        name="attn_prompt",
    )(sinks, q, kv, kv, ga, wo)


def _attn_sample_kernel(sinks_ref, q_ref, kvn_ref, ck_ref, cv_ref, ga_ref, wo_ref,
                        out_ref, ko_ref, vo_ref, attn_scr, *, bb, tpad, dec_seq):
    nk = WINDOW + tpad
    lo3 = lax.broadcasted_iota(jnp.int32, (bb, tpad, LANES), 2) < HEAD_DIM
    row = lax.broadcasted_iota(jnp.int32, (4 * tpad, nk), 0)
    si = lax.broadcasted_iota(jnp.int32, (4 * tpad, nk), 1)
    tq = row % tpad
    valid = (si > tq) & (si <= tq + WINDOW) & (si < WINDOW + dec_seq)
    hrow = lax.broadcasted_iota(jnp.int32, (4 * tpad, 1), 0) // tpad

    for t in range(2):
        sl = slice(t * LANES, (t + 1) * LANES)
        kk = jnp.concatenate([ck_ref[:, :, sl], kvn_ref[:, :, sl]], axis=1)
        vv = jnp.concatenate([cv_ref[:, :, sl],
                              kvn_ref[:, :, D_KV + t * LANES:D_KV + (t + 1) * LANES]], axis=1)
        ko_ref[:, :, sl] = kk[:, dec_seq:dec_seq + WINDOW, :]
        vo_ref[:, :, sl] = vv[:, dec_seq:dec_seq + WINDOW, :]
        kds = _dup_heads(kk)
        vds = _dup_heads(vv)
        for par_kv in range(2):
            j = 2 * t + par_kv
            kd, vd = kds[par_kv], vds[par_kv]
            parts = []
            for r in range(2):
                c0 = j * 2 * LANES + r * LANES
                qp = q_ref[:, :, c0:c0 + LANES]
                parts += [jnp.where(lo3, qp, 0.0), jnp.where(lo3, 0.0, qp)]
            lhs = jnp.concatenate(parts, axis=1).astype(BF16)
            s = jnp.einsum('bqd,bkd->bqk', lhs, kd, preferred_element_type=F32)
            sink = jnp.zeros((4 * tpad, 1), F32)
            for g in range(4):
                sink = jnp.where(hrow == g, sinks_ref[4 * j + g], sink)
            p, inv = _sink_softmax(jnp.where(valid[None], s, -jnp.inf), sink[None])
            o = jnp.einsum('bqk,bkd->bqd', p, vd, preferred_element_type=F32) * inv
            for r in range(2):
                c0 = j * 2 * LANES + r * LANES
                o_even = o[:, (2 * r) * tpad:(2 * r + 1) * tpad, :]
                o_odd = o[:, (2 * r + 1) * tpad:(2 * r + 2) * tpad, :]
                attn_scr[:, :, c0:c0 + LANES] = jnp.where(lo3, o_even, o_odd)

    attn = attn_scr[...].reshape(bb * tpad, D_MODEL).astype(BF16)
    a_out = jnp.dot(attn, wo_ref[...], preferred_element_type=F32)
    out_ref[...] = ga_ref[...].astype(F32) * a_out


def _attn_sample(sinks, q3, kvn3, ck, cv, ga2, wo, bb, dec_seq):
    db, tpad, _ = q3.shape
    blk3 = lambda r, w: pl.BlockSpec((bb, r, w), lambda i: (i, 0, 0))
    row = pl.BlockSpec((bb * tpad, D_MODEL), lambda i: (i, 0))
    return pl.pallas_call(
        functools.partial(_attn_sample_kernel, bb=bb, tpad=tpad, dec_seq=dec_seq),
        grid=(db // bb,),
        in_specs=[pl.BlockSpec(memory_space=pltpu.SMEM), blk3(tpad, D_MODEL), blk3(tpad, 2 * D_KV),
                  blk3(WINDOW, D_KV), blk3(WINDOW, D_KV), row, _const_spec(wo.shape)],
        out_specs=[row, blk3(WINDOW, D_KV), blk3(WINDOW, D_KV)],
        out_shape=[jax.ShapeDtypeStruct((db * tpad, D_MODEL), F32),
                   jax.ShapeDtypeStruct((db, WINDOW, D_KV), F32),
                   jax.ShapeDtypeStruct((db, WINDOW, D_KV), F32)],
        scratch_shapes=[pltpu.VMEM((bb, tpad, D_MODEL), F32)],
        compiler_params=_params("arbitrary"),
        name="attn_sample",
    )(sinks, q3, kvn3, ck, cv, ga2, wo)


def _s5_discretize(lam_re, lam_im, log_dt, b_re, b_im):
    dt = jnp.exp(log_dt)[:, None]
    decay = jnp.exp(lam_re * dt)
    ab_re = decay * jnp.cos(lam_im * dt)
    ab_im = decay * jnp.sin(lam_im * dt)
    nr, ni = ab_re - 1.0, ab_im
    den = lam_re * lam_re + lam_im * lam_im
    f_re = ((nr * lam_re + ni * lam_im) / den)[..., None]
    f_im = ((ni * lam_re - nr * lam_im) / den)[..., None]
    return ab_re, ab_im, f_re * b_re - f_im * b_im, f_re * b_im + f_im * b_re


def _state_split(h):
    h = h.reshape(h.shape[:-1] + (N_PAIRS, 2, 2, SSM_STATE))
    unpair = lambda a: a.reshape(a.shape[:-3] + (SSM_GROUPS, SSM_STATE))
    return unpair(h[..., 0, :, :]), unpair(h[..., 1, :, :])


def _shift_lanes(x, n):
    a, b = x[:, :LANES], x[:, LANES:]
    lane = lax.broadcasted_iota(jnp.int32, a.shape, 1)
    if n == 0:
        return x
    if n >= LANES:
        r = n - LANES
        hi = a if r == 0 else jnp.where(lane < r, 0.0, pltpu.roll(a, r, 1))
        return jnp.concatenate([jnp.zeros_like(a), hi], axis=1)
    ra, rb = pltpu.roll(a, n, 1), pltpu.roll(b, n, 1)
    return jnp.concatenate([jnp.where(lane < n, 0.0, ra), jnp.where(lane < n, ra, rb)], axis=1)


def _dot_nt_split(lhs, rhs):
    nt = lambda a, b: lax.dot_general(a, b, (((1,), (1,)), ((), ())), preferred_element_type=F32)
    l_hi, r_hi = lhs.astype(BF16), rhs.astype(BF16)
    l_lo = (lhs - l_hi.astype(F32)).astype(BF16)
    r_lo = (rhs - r_hi.astype(F32)).astype(BF16)
    return nt(l_hi, r_hi) + nt(l_hi, r_lo) + nt(l_lo, r_hi)


def _s5_prep_kernel(a_ref, c_ref, bt_ref, w_ref, m_ref, ws_ref, wsn_ref, whyt_ref, mk_ref, pw_ref, an_ref,
                    wb_ref, *, n_tok):
    wb_ref[...] = w_ref[...].astype(BF16)
    for p in range(PREP_PAIRS):
        one, two, lanes = pl.ds(p, 1), pl.ds(2 * p, 2), pl.ds(p * LANES, LANES)
        _s5_prep_pair(a_ref.at[one], c_ref.at[one], bt_ref.at[one], m_ref.at[two], ws_ref.at[two],
                      wsn_ref.at[two], whyt_ref.at[two], mk_ref.at[:, :, :, lanes], pw_ref.at[:, :, lanes],
                      an_ref.at[:, :, lanes], n_tok)


def _s5_prep_pair(a_ref, c_ref, bt_ref, m_ref, ws_ref, wsn_ref, whyt_ref, mk_ref, pw_ref, an_ref, n_tok):
    ar, ai = a_ref[0, 0], a_ref[0, 1]
    cr, ci = c_ref[0, 0], c_ref[0, 1]
    br, bi = bt_ref[0, 0], bt_ref[0, 1]
    pr, pi = [jnp.ones_like(ar)], [jnp.zeros_like(ai)]
    for _ in range(CHUNK):
        pr, pi = pr + [pr[-1] * ar - pi[-1] * ai], pi + [pr[-1] * ai + pi[-1] * ar]
    first = lax.broadcasted_iota(jnp.int32, (SSM_CH, LANES), 1) < SSM_STATE
    ca = [(cr * pr[t] - ci * pi[t], -(cr * pi[t] + ci * pr[t])) for t in range(CHUNK + 1)]
    ca_full = jnp.concatenate([jnp.concatenate(ca[t], axis=1) for t in range(CHUNK)], axis=0)
    zero = jnp.zeros((SSM_CH, LANES), F32)
    for e in range(2):
        own = first if e == 0 else ~first
        pick = lambda v: jnp.where(own, v, zero)
        for t in range(CHUNK):
            rows = slice(t * SSM_CH, (t + 1) * SSM_CH)
            whyt_ref[e, rows, :LANES] = pick(ca[t + 1][0]).astype(BF16)
            whyt_ref[e, rows, LANES:] = pick(ca[t + 1][1]).astype(BF16)
            for ref, k in ((ws_ref, CHUNK - 1 - t), (wsn_ref, n_tok - 1 - t)):
                if k >= 0:
                    ref[e, rows, :LANES] = pick(br * pr[k] - bi * pi[k]).astype(BF16)
                    ref[e, rows, LANES:] = pick(br * pi[k] + bi * pr[k]).astype(BF16)
                else:
                    ref[e, rows, :] = jnp.zeros((SSM_CH, PAIR_COLS), BF16)
        k_row = _dot_nt_split(jnp.concatenate([pick(br), pick(bi)], axis=1), ca_full)
        for s in range(CHUNK):
            m_ref[e, s * SSM_CH:(s + 1) * SSM_CH, :] = _shift_lanes(k_row, s * SSM_CH).astype(BF16)
    sub = lax.broadcasted_iota(jnp.int32, (SUBLANES, LANES), 0)
    qr, qi = [pr[CHUNK]], [pi[CHUNK]]
    for _ in range(SUBLANES - 1):
        qr, qi = qr + [qr[-1] * pr[CHUNK] - qi[-1] * pi[CHUNK]], qi + [qr[-1] * pi[CHUNK] + qi[-1] * pr[CHUNK]]
    for part, q in enumerate((qr, qi)):
        rows = jnp.zeros((SUBLANES, LANES), F32)
        for s in range(SUBLANES):
            rows = jnp.where(sub == s, q[s], rows)
        pw_ref[part] = rows
        for k, shift in enumerate((1, 2, 4)):
            mk_ref[k, part] = jnp.where(sub >= shift, q[shift - 1], 0.0)
    an_ref[0] = jnp.broadcast_to(pr[n_tok], (SUBLANES, LANES))
    an_ref[1] = jnp.broadcast_to(pi[n_tok], (SUBLANES, LANES))


def _s5_prep(lam_re, lam_im, log_dt, b_re, b_im, c_re, c_im, n_tok, w_in):
    ab_re, ab_im, bb_re, bb_im = _s5_discretize(lam_re, lam_im, log_dt, b_re, b_im)
    pair = lambda v: v.reshape(N_PAIRS, 2, v.shape[1], SSM_STATE).transpose(0, 2, 1, 3).reshape(
        N_PAIRS, v.shape[1], LANES)
    a = jnp.stack([pair(ab_re[:, None, :]), pair(ab_im[:, None, :])], axis=1)
    c = jnp.stack([pair(c_re), pair(c_im)], axis=1)
    bt = jnp.stack([pair(bb_re.transpose(0, 2, 1)), pair(bb_im.transpose(0, 2, 1))], axis=1)
    blk4 = lambda r: pl.BlockSpec((PREP_PAIRS, 2, r, LANES), lambda q: (q, 0, 0, 0))
    w_spec = pl.BlockSpec((2 * PREP_PAIRS, GROUP_IO, PAIR_COLS), lambda q: (q, 0, 0))
    w_shape = jax.ShapeDtypeStruct((SSM_GROUPS, GROUP_IO, PAIR_COLS), BF16)
    steps = N_PAIRS // PREP_PAIRS
    assert w_in.shape[0] % (steps * BF16_ROWS) == 0
    slab = pl.BlockSpec((w_in.shape[0] // steps, w_in.shape[1]), lambda q: (q, 0))
    return pl.pallas_call(
        functools.partial(_s5_prep_kernel, n_tok=n_tok),
        grid=(steps,),
        in_specs=[blk4(1), blk4(SSM_CH), blk4(SSM_CH), slab],
        out_specs=[w_spec, w_spec, w_spec, w_spec,
                   pl.BlockSpec((3, 2, SUBLANES, PREP_PAIRS * LANES), lambda q: (0, 0, 0, q)),
                   pl.BlockSpec((2, SUBLANES, PREP_PAIRS * LANES), lambda q: (0, 0, q)),
                   pl.BlockSpec((2, SUBLANES, PREP_PAIRS * LANES), lambda q: (0, 0, q)), slab],
        out_shape=[w_shape, w_shape, w_shape, w_shape,
                   jax.ShapeDtypeStruct((3, 2, SUBLANES, HALF_COLS), F32),
                   jax.ShapeDtypeStruct((2, SUBLANES, HALF_COLS), F32),
                   jax.ShapeDtypeStruct((2, SUBLANES, HALF_COLS), F32),
                   jax.ShapeDtypeStruct(w_in.shape, BF16)],
        compiler_params=_params("arbitrary"),
        name="s5_prep",
    )(a, c, bt, w_in)


def _group_io(ref, g):
    return ref.at[:, g * GROUP_IO:(g + 1) * GROUP_IO]


def _s5_emit_tile(u_ref, st_scr, m_ref, whyt_ref, dt_ref, g_ref, j):
    for gl in range(GROUPS_PER_TILE):
        g = GROUPS_PER_TILE * j + gl
        q0 = (gl // 2) * PAIR_COLS
        u_g = _group_io(u_ref, g)[...]
        y = (jnp.dot(u_g, m_ref[g], preferred_element_type=F32)
             + lax.dot_general(st_scr[:, q0:q0 + PAIR_COLS].astype(BF16), whyt_ref[g],
                               (((1,), (1,)), ((), ())), preferred_element_type=F32)
             + dt_ref[g] * u_g.astype(F32))
        _group_io(g_ref, g)[...] = jax.nn.gelu(y).astype(BF16)


def _s5_local_states(u_ref, ws_ref, j, q):
    g = GROUPS_PER_TILE * j + 2 * q
    return (jnp.dot(_group_io(u_ref, g)[...], ws_ref[g], preferred_element_type=F32)
            + jnp.dot(_group_io(u_ref, g + 1)[...], ws_ref[g + 1], preferred_element_type=F32))


def _cmul_add(ar, ai, hr, hi, xr, xi):
    return ar * hr - ai * hi + xr, ar * hi + ai * hr + xi


def _s5_chain_kernel(u_ref, m_ref, ws_ref, whyt_ref, mk_ref, pw_ref, dt_ref, h0_ref, g_ref, hT_ref,
                     carry_scr, st_scr, *, rows):
    @pl.when(pl.program_id(1) == 0)
    def _():
        carry_scr[...] = jnp.broadcast_to(h0_ref[0], (SUBLANES, STATE_COLS))

    row0 = lax.broadcasted_iota(jnp.int32, (SUBLANES, LANES), 0) == 0
    last = lambda h: jnp.broadcast_to(h[SUBLANES - 1:, :], (SUBLANES, LANES))

    for j in range(N_LANE_TILES):
        for q in range(PAIRS_PER_TILE):
            st_scr[:, q * PAIR_COLS:(q + 1) * PAIR_COLS] = _s5_local_states(u_ref, ws_ref, j, q)

        def step(r, carry):
            r0 = pl.multiple_of(r * SUBLANES, SUBLANES)
            out = []
            for q in range(PAIRS_PER_TILE):
                re_c = slice(q * PAIR_COLS, q * PAIR_COLS + LANES)
                im_c = slice(q * PAIR_COLS + LANES, (q + 1) * PAIR_COLS)
                hc = slice((j * PAIRS_PER_TILE + q) * LANES, (j * PAIRS_PER_TILE + q + 1) * LANES)
                xr = st_scr[pl.ds(r0, SUBLANES), re_c]
                xi = st_scr[pl.ds(r0, SUBLANES), im_c]
                for k, shift in enumerate((1, 2, 4)):
                    xr, xi = _cmul_add(mk_ref[k, 0, :, hc], mk_ref[k, 1, :, hc],
                                       pltpu.roll(xr, shift, 0), pltpu.roll(xi, shift, 0), xr, xi)
                cr, ci = carry[q]
                hr, hi = _cmul_add(pw_ref[0, :, hc], pw_ref[1, :, hc], cr, ci, xr, xi)
                st_scr[pl.ds(r0, SUBLANES), re_c] = jnp.where(row0, cr, pltpu.roll(hr, 1, 0))
                st_scr[pl.ds(r0, SUBLANES), im_c] = jnp.where(row0, ci, pltpu.roll(hi, 1, 0))
                out.append((last(hr), last(hi)))
            return tuple(out)

        c0 = j * TILE_COLS
        init = tuple((carry_scr[:, c0 + q * PAIR_COLS:c0 + q * PAIR_COLS + LANES],
                      carry_scr[:, c0 + q * PAIR_COLS + LANES:c0 + (q + 1) * PAIR_COLS])
                     for q in range(PAIRS_PER_TILE))
        final = lax.fori_loop(0, rows // SUBLANES, step, init, unroll=True)
        for q in range(PAIRS_PER_TILE):
            carry_scr[:, c0 + q * PAIR_COLS:c0 + q * PAIR_COLS + LANES] = final[q][0]
            carry_scr[:, c0 + q * PAIR_COLS + LANES:c0 + (q + 1) * PAIR_COLS] = final[q][1]

        _s5_emit_tile(u_ref, st_scr, m_ref, whyt_ref, dt_ref, g_ref, j)

    hT_ref[0] = carry_scr[...]


def _s5_chain(u16, m, ws, why, mk, pw, d_tiled, h0, batch, n_rows, rows):
    nblk = n_rows // rows
    row = pl.BlockSpec((rows, CHUNK_LANES), lambda b, i: (b * nblk + i, 0))
    return pl.pallas_call(
        functools.partial(_s5_chain_kernel, rows=rows),
        grid=(batch, nblk),
        in_specs=[row, _const_spec(m.shape), _const_spec(ws.shape), _const_spec(why.shape),
                  _const_spec(mk.shape), _const_spec(pw.shape), _const_spec(d_tiled.shape),
                  pl.BlockSpec((1, 1, STATE_COLS), lambda b, i: (b, 0, 0))],
        out_specs=[row, pl.BlockSpec((1, SUBLANES, STATE_COLS), lambda b, i: (b, 0, 0))],
        out_shape=[jax.ShapeDtypeStruct(u16.shape, BF16),
                   jax.ShapeDtypeStruct((batch, SUBLANES, STATE_COLS), F32)],
        scratch_shapes=[pltpu.VMEM((SUBLANES, STATE_COLS), F32), pltpu.VMEM((rows, TILE_COLS), F32)],
        compiler_params=_params("arbitrary", "arbitrary"),
        name="s5_chain",
    )(u16, m, ws, why, mk, pw, d_tiled, h0)


def _s5_rows_kernel(u_ref, m_ref, ws_ref, whyt_ref, an_ref, dt_ref, h0r_ref, h0i_ref, g_ref, hTr_ref, hTi_ref,
                    st_scr):
    for j in range(N_LANE_TILES):
        for q in range(PAIRS_PER_TILE):
            hc = slice((j * PAIRS_PER_TILE + q) * LANES, (j * PAIRS_PER_TILE + q + 1) * LANES)
            h0r, h0i = h0r_ref[hc, :].T, h0i_ref[hc, :].T
            local = _s5_local_states(u_ref, ws_ref, j, q)
            hr, hi = _cmul_add(an_ref[0, :1, hc], an_ref[1, :1, hc], h0r, h0i, local[:, :LANES], local[:, LANES:])
            hTr_ref[hc, :] = hr.T
            hTi_ref[hc, :] = hi.T
            st_scr[:, q * PAIR_COLS:q * PAIR_COLS + LANES] = h0r
            st_scr[:, q * PAIR_COLS + LANES:(q + 1) * PAIR_COLS] = h0i
        _s5_emit_tile(u_ref, st_scr, m_ref, whyt_ref, dt_ref, g_ref, j)


def _s5_rows(u16, m, ws, why, an, d_tiled, h0_re_t, h0_im_t, rows):
    n = u16.shape[0]
    once = pl.Buffered(1)
    row = pl.BlockSpec((rows, CHUNK_LANES), lambda i: (i, 0), pipeline_mode=once)
    col = pl.BlockSpec((HALF_COLS, rows), lambda i: (0, i), pipeline_mode=once)
    state = jax.ShapeDtypeStruct((HALF_COLS, n), F32)
    return pl.pallas_call(
        _s5_rows_kernel,
        grid=(n // rows,),
        in_specs=[row, _const_spec(m.shape), _const_spec(ws.shape), _const_spec(why.shape),
                  _const_spec(an.shape), _const_spec(d_tiled.shape), col, col],
        out_specs=[row, col, col],
        out_shape=[jax.ShapeDtypeStruct(u16.shape, BF16), state, state],
        scratch_shapes=[pltpu.VMEM((rows, TILE_COLS), F32)],
        compiler_params=_params("arbitrary"),
        name="s5_rows",
    )(u16, m, ws, why, an, d_tiled, h0_re_t, h0_im_t)


def _post_kernel(g_ref, ma_ref, gs_ref, x_ref, permt_ref, wglu_ref, wout_ref, gffn_ref, wup_ref, wdown_ref,
                 gfin_ref, out_ref, *, n_tok):
    g = _load_chunk_rows(g_ref, permt_ref, n_tok)
    glu = jnp.dot(g, wglu_ref[...], preferred_element_type=F32)
    s_out = glu[:, :D_MODEL] * jax.nn.sigmoid(glu[:, D_MODEL:])
    merged = ma_ref[...] + gs_ref[...].astype(F32) * s_out
    x1 = x_ref[...] + jnp.dot(merged.astype(BF16), wout_ref[...], preferred_element_type=F32)
    h2 = _rmsnorm(x1, gffn_ref[...]).astype(BF16)
    x2 = x1
    for c in range(0, D_FF, FF_CHUNK):
        up = jnp.dot(h2, wup_ref[:, c:c + FF_CHUNK], preferred_element_type=F32)
        act = jnp.square(jnp.maximum(up, 0.0)).astype(BF16)
        x2 = x2 + jnp.dot(act, wdown_ref[c:c + FF_CHUNK, :], preferred_element_type=F32)
    out_ref[...] = _rmsnorm(x2, gfin_ref[...])


def _post(g, ma, gs, x, permt, w_glu, w_out, g_ffn, w_up, w_down, g_final, tm, n_tok):
    n = x.shape[0]
    row = pl.BlockSpec((tm, D_MODEL), lambda i: (i, 0))
    g_spec = pl.BlockSpec((tm // n_tok, CHUNK_LANES), lambda i: (i, 0))
    return pl.pallas_call(
        functools.partial(_post_kernel, n_tok=n_tok),
        grid=(n // tm,),
        in_specs=[g_spec, row, row, row, _const_spec(permt.shape), _const_spec(w_glu.shape),
                  _const_spec(w_out.shape), _const_spec((1, D_MODEL)), _const_spec(w_up.shape),
                  _const_spec(w_down.shape), _const_spec((1, D_MODEL))],
        out_specs=row,
        out_shape=jax.ShapeDtypeStruct((n, D_MODEL), F32),
        compiler_params=_params("arbitrary"),
        name="post",
    )(g, ma, gs, x, permt, w_glu, w_out, g_ffn, w_up, w_down, g_final)


def _tile(n, pref):
    t = pref
    while n % t:
        t //= 2
    return t


def kernel(x_prompt, x_sample, cache_k, cache_v, state_ssm_re, state_ssm_im, g_mix, w_in, attn_sinks,
           w_attn_o, ssm_lambda_re, ssm_lambda_im, ssm_log_dt, ssm_b_re, ssm_b_im, ssm_c_re, ssm_c_im,
           ssm_d, w_glu, w_out, g_ffn, w_up, w_down, g_final):
    batch, seq, _ = x_prompt.shape
    db, dec_seq, _ = x_sample.shape
    assert w_in.shape[0] == 1, "one layer"
    assert seq % PERM_ROWS == 0 and dec_seq in (1, 2, 4, 8) and (db * dec_seq) % PERM_ROWS == 0

    vec = lambda v: v.reshape(1, D_MODEL).astype(F32)
    sinks = attn_sinks[0].astype(F32)
    s5p = (ssm_lambda_re[0], ssm_lambda_im[0], ssm_log_dt[0], ssm_b_re[0], ssm_b_im[0], ssm_c_re[0], ssm_c_im[0])
    d_skip = jnp.tile(ssm_d[0].astype(F32).reshape(SSM_GROUPS, 1, SSM_CH), (1, 1, CHUNK))
    perm = _chunk_perm(CHUNK)
    m, ws, ws_n, whyt, mk, pw, an, w_in_b = _s5_prep(*s5p, dec_seq, w_in[0])

    xp = x_prompt.reshape(batch * seq, D_MODEL)
    q, kv, u16, ga, gs, wo_b, w_glu_b, w_out_b, w_up_b, w_down_b = _proj(
        xp, vec(g_mix[0]), w_in_b, perm, _tile(batch * seq, 1024), CHUNK,
        cast=(w_attn_o[0], w_glu[0], w_out[0], w_up[0], w_down[0]))
    post_w = (w_glu_b, w_out_b, vec(g_ffn[0]), w_up_b, w_down_b, vec(g_final))
    ma = _attn_prompt(sinks, q, kv, ga, wo_b, batch, seq, _tile(seq, 512))
    n_rows = seq // CHUNK
    g16, hT = _s5_chain(u16, m, ws, whyt, mk, pw, d_skip, jnp.zeros((batch, 1, STATE_COLS), F32),
                        batch, n_rows, _tile(n_rows, 128))
    y_prompt = _post(g16, ma, gs, xp, perm.T, *post_w, _tile(batch * seq, 512), CHUNK).reshape(batch, seq, D_MODEL)
    kv_last = kv.reshape(batch, seq, 2 * D_KV)[:, seq - WINDOW:]
    k_prompt = kv_last[..., :D_KV].reshape(1, batch, WINDOW, N_KV_HEADS, HEAD_DIM)
    v_prompt = kv_last[..., D_KV:].reshape(1, batch, WINDOW, N_KV_HEADS, HEAD_DIM)
    hr, hi = _state_split(hT[:, 0])
    ssm_re_prompt, ssm_im_prompt = hr[None], hi[None]

    ns = db * dec_seq
    xs = x_sample.reshape(ns, D_MODEL)
    perm_s = _chunk_perm(dec_seq)
    q, kv, u16, ga, gs = _proj(xs, vec(g_mix[0]), w_in_b, perm_s, _tile(ns, 512), dec_seq)
    tpad = SUBLANES
    pad3 = lambda v: jnp.pad(v.reshape(db, dec_seq, -1).astype(F32), ((0, 0), (0, tpad - dec_seq), (0, 0)))
    ga_pad = pad3(ga).astype(BF16).reshape(db * tpad, D_MODEL)
    ma_pad, k_new, v_new = _attn_sample(
        sinks, pad3(q), pad3(kv), cache_k[0].reshape(db, WINDOW, D_KV), cache_v[0].reshape(db, WINDOW, D_KV),
        ga_pad, wo_b, _tile(db, 16), dec_seq)
    ma = ma_pad.reshape(db, tpad, D_MODEL)[:, :dec_seq].reshape(ns, D_MODEL)
    to_cols = lambda s: s[0].astype(F32).transpose(1, 2, 0).reshape(HALF_COLS, db)
    from_cols = lambda h: h.reshape(SSM_GROUPS, SSM_STATE, db).transpose(2, 0, 1)[None]
    g16, hT_re, hT_im = _s5_rows(u16, m, ws_n, whyt, an, d_skip, to_cols(state_ssm_re), to_cols(state_ssm_im),
                                 _tile(db, LANES))
    y_sample = _post(g16, ma, gs, xs, perm_s.T, *post_w, _tile(ns, 512), dec_seq).reshape(db, dec_seq, D_MODEL)
    k_sample = k_new.reshape(1, db, WINDOW, N_KV_HEADS, HEAD_DIM)
    v_sample = v_new.reshape(1, db, WINDOW, N_KV_HEADS, HEAD_DIM)
    ssm_re_sample, ssm_im_sample = from_cols(hT_re), from_cols(hT_im)

    return (y_prompt, y_sample, k_prompt, v_prompt, ssm_re_prompt, ssm_im_prompt,
            k_sample, v_sample, ssm_re_sample, ssm_im_sample)
```

```python
import functools

import jax
import jax.numpy as jnp
from jax import lax
from jax.experimental import pallas as pl
from jax.experimental.pallas import tpu as pltpu

F32 = jnp.float32
BF16 = jnp.bfloat16

D_MODEL = 1024
HEAD_DIM = 64
N_HEADS = 16
N_KV_HEADS = 4
WINDOW = 128
D_KV = N_KV_HEADS * HEAD_DIM
SSM_CH = 16
SSM_GROUPS = 64
SSM_STATE = 64
D_FF = 4 * D_MODEL
FF_CHUNK = 1024
RMS_EPS = 1e-5
LOG2E = 1.4426950408889634
Q_SCALE = HEAD_DIM ** -0.5 * LOG2E

LANES = 128
SUBLANES = 8
BF16_ROWS = 16
N_LANE_TILES = D_MODEL // LANES
GROUPS_PER_TILE = LANES // SSM_CH
CHUNK = 16
CHUNK_LANES = CHUNK * D_MODEL
GROUP_IO = CHUNK * SSM_CH
N_PAIRS = SSM_GROUPS // 2
PAIRS_PER_TILE = GROUPS_PER_TILE // 2
PAIR_COLS = 2 * LANES
TILE_COLS = PAIRS_PER_TILE * PAIR_COLS
STATE_COLS = N_PAIRS * PAIR_COLS
HALF_COLS = STATE_COLS // 2
PERM_ROWS = CHUNK * BF16_ROWS
PREP_PAIRS = 4
VMEM_LIMIT = 56 * 1024 * 1024

_Q0, _KV0, _U0, _GA0, _GS0, _END = 0, 1024, 1536, 2560, 3584, 4608


def _rmsnorm(x, g):
    return x * lax.rsqrt(jnp.mean(x * x, axis=-1, keepdims=True) + RMS_EPS) * g


def _params(*sem):
    return pltpu.CompilerParams(dimension_semantics=sem, vmem_limit_bytes=VMEM_LIMIT)


def _const_spec(shape):
    nd = len(shape)
    return pl.BlockSpec(shape, lambda *_: (0,) * nd, pipeline_mode=pl.Buffered(1))


def _chunk_perm(n_tok):
    n_chunks = PERM_ROWS // n_tok
    r = jnp.arange(PERM_ROWS)
    tok = (r % n_chunks) * n_tok + r // n_chunks
    return (tok[:, None] == jnp.arange(PERM_ROWS)[None, :]).astype(BF16)


def _piece_transpose(cols, masks):
    for d, msk in zip((4, 2, 1), masks):
        new = list(cols)
        for v in range(GROUPS_PER_TILE):
            if v & d == 0:
                a, b = cols[v], cols[v + d]
                new[v] = jnp.where(msk, pltpu.roll(b, SSM_CH * d, 1), a)
                new[v + d] = jnp.where(msk, b, pltpu.roll(a, LANES - SSM_CH * d, 1))
        cols = new
    return cols


def _piece_masks(rows):
    piece = lax.broadcasted_iota(jnp.int32, (rows, LANES), 1) // SSM_CH
    return [(piece & d) != 0 for d in (4, 2, 1)]


def _store_chunk_rows(x, perm_ref, out_ref, n_tok):
    n_chunks = PERM_ROWS // n_tok
    masks = _piece_masks(n_chunks)
    zero = jnp.zeros((n_chunks, LANES), F32)
    for hb in range(x.shape[0] // PERM_ROWS):
        rows = slice(hb * n_chunks, (hb + 1) * n_chunks)
        xp = jnp.dot(perm_ref[...], x[hb * PERM_ROWS:(hb + 1) * PERM_ROWS],
                     preferred_element_type=F32)
        for j in range(N_LANE_TILES):
            for hf in range(2):
                live = 8 * hf < n_tok
                cols = [xp[t * n_chunks:(t + 1) * n_chunks, j * LANES:(j + 1) * LANES] if t < n_tok else zero
                        for t in range(8 * hf, 8 * hf + 8)]
                for gl, col in enumerate(_piece_transpose(cols, masks) if live else cols):
                    c0 = (GROUPS_PER_TILE * j + gl) * GROUP_IO + hf * LANES
                    out_ref[rows, c0:c0 + LANES] = col.astype(BF16)


def _load_chunk_rows(in_ref, permt_ref, n_tok):
    n_chunks = PERM_ROWS // n_tok
    masks = _piece_masks(n_chunks)
    blocks = []
    for hb in range(in_ref.shape[0] // n_chunks):
        rows = slice(hb * n_chunks, (hb + 1) * n_chunks)
        tiles = [[None] * N_LANE_TILES for _ in range(n_tok)]
        for j in range(N_LANE_TILES):
            for hf in range(-(-n_tok // 8)):
                cols = []
                for gl in range(GROUPS_PER_TILE):
                    c0 = (GROUPS_PER_TILE * j + gl) * GROUP_IO + hf * LANES
                    cols.append(in_ref[rows, c0:c0 + LANES].astype(F32))
                for k, col in enumerate(_piece_transpose(cols, masks)):
                    if 8 * hf + k < n_tok:
                        tiles[8 * hf + k][j] = col.astype(BF16)
        xp = jnp.concatenate([jnp.concatenate(row, axis=1) for row in tiles], axis=0)
        blocks.append(jnp.dot(permt_ref[...], xp, preferred_element_type=F32).astype(BF16))
    return blocks[0] if len(blocks) == 1 else jnp.concatenate(blocks, axis=0)


def _proj_kernel(x_ref, g_ref, w_ref, perm_ref, *refs, n_tok, n_cast):
    q_ref, kv_ref, u_ref, ga_ref, gs_ref = refs[n_cast:n_cast + 5]
    h = _rmsnorm(x_ref[...], g_ref[...]).astype(BF16)

    def seg(lo, hi):
        return jnp.dot(h, w_ref[:, lo:hi], preferred_element_type=F32)

    q_ref[...] = (seg(_Q0, _KV0) * Q_SCALE).astype(BF16)
    kv_ref[...] = seg(_KV0, _U0)
    _store_chunk_rows(seg(_U0, _GA0).astype(BF16), perm_ref, u_ref, n_tok)
    ga_ref[...] = jax.nn.sigmoid(seg(_GA0, _GS0)).astype(BF16)
    gs_ref[...] = jax.nn.sigmoid(seg(_GS0, _END)).astype(BF16)
    for src, dst in zip(refs[:n_cast], refs[n_cast + 5:]):
        dst[...] = src[...].astype(BF16)


def _proj(x, g_mix, w_in, perm, tm, n_tok, cast=()):
    n = x.shape[0]
    steps = n // tm
    row = lambda w: pl.BlockSpec((tm, w), lambda i: (i, 0))
    slab = lambda w: pl.BlockSpec((w.shape[0] // steps, w.shape[1]), lambda i: (i, 0))
    assert all(w.shape[0] % (steps * BF16_ROWS) == 0 for w in cast)
    return pl.pallas_call(
        functools.partial(_proj_kernel, n_tok=n_tok, n_cast=len(cast)),
        grid=(steps,),
        in_specs=[row(D_MODEL), _const_spec((1, D_MODEL)), _const_spec(w_in.shape), _const_spec(perm.shape)]
        + [slab(w) for w in cast],
        out_specs=[row(D_MODEL), row(2 * D_KV), pl.BlockSpec((tm // n_tok, CHUNK_LANES), lambda i: (i, 0)),
                   row(D_MODEL), row(D_MODEL)] + [slab(w) for w in cast],
        out_shape=[jax.ShapeDtypeStruct((n, D_MODEL), BF16),
                   jax.ShapeDtypeStruct((n, 2 * D_KV), F32),
                   jax.ShapeDtypeStruct((n // n_tok, CHUNK_LANES), BF16),
                   jax.ShapeDtypeStruct((n, D_MODEL), BF16),
                   jax.ShapeDtypeStruct((n, D_MODEL), BF16)]
        + [jax.ShapeDtypeStruct(w.shape, BF16) for w in cast],
        compiler_params=_params("arbitrary"),
        name="proj",
    )(x, g_mix, w_in, perm, *cast)


def _dup_heads(tile):
    lo = lax.broadcasted_iota(jnp.int32, tile.shape, tile.ndim - 1) < HEAD_DIM
    rolled = pltpu.roll(tile, HEAD_DIM, tile.ndim - 1)
    return (jnp.where(lo, tile, rolled).astype(BF16), jnp.where(lo, rolled, tile).astype(BF16))


def _sink_softmax(s, sink):
    sink2 = sink * LOG2E
    m = jnp.maximum(jnp.max(s, axis=-1, keepdims=True), sink2)
    p = jnp.exp2(s - m)
    denom = jnp.sum(p, axis=-1, keepdims=True) + jnp.exp2(sink2 - m)
    return p.astype(BF16), 1.0 / denom


def _attn_prompt_kernel(sinks_ref, q_ref, kvc_ref, kvp_ref, ga_ref, wo_ref, out_ref,
                        kd_scr, vd_scr, bias_scr, attn_scr, *, tq):
    i = pl.program_id(1)
    kv_full = jnp.concatenate([kvp_ref[...], kvc_ref[...]], axis=0)
    for t in range(2):
        ke, ko = _dup_heads(kv_full[:, t * LANES:(t + 1) * LANES])
        ve, vo = _dup_heads(kv_full[:, D_KV + t * LANES:D_KV + (t + 1) * LANES])
        kd_scr[2 * t], kd_scr[2 * t + 1] = ke, ko
        vd_scr[2 * t], vd_scr[2 * t + 1] = ve, vo

    qi = lax.broadcasted_iota(jnp.int32, (WINDOW, 2 * WINDOW), 0)
    si = lax.broadcasted_iota(jnp.int32, (WINDOW, 2 * WINDOW), 1)
    band = (si > qi) & (si <= qi + WINDOW)
    in_block = si >= WINDOW
    lo = lax.broadcasted_iota(jnp.int32, (WINDOW, LANES), 1) < HEAD_DIM

    def attend(jb):
        q0 = jb * WINDOW
        has_prev = (i > 0) | (jb > 0)
        bias_scr[...] = jnp.where(band & (in_block | has_prev), 0.0, -jnp.inf)

        def head_pair(hp, c):
            j = hp // 2
            c0 = pl.multiple_of(hp * LANES, LANES)
            kd = kd_scr[j, pl.ds(q0, 2 * WINDOW), :]
            vd = vd_scr[j, pl.ds(q0, 2 * WINDOW), :]
            qp = q_ref[pl.ds(q0, WINDOW), pl.ds(c0, LANES)]
            outs = []
            for par in range(2):
                qm = jnp.where(lo if par == 0 else ~lo, qp, jnp.zeros_like(qp))
                s = lax.dot_general(qm, kd, (((1,), (1,)), ((), ())), preferred_element_type=F32)
                p, inv = _sink_softmax(s + bias_scr[...], sinks_ref[2 * hp + par])
                outs.append(jnp.dot(p, vd, preferred_element_type=F32) * inv)
            attn_scr[pl.ds(q0, WINDOW), pl.ds(c0, LANES)] = jnp.where(lo, outs[0], outs[1]).astype(BF16)
            return c

        lax.fori_loop(0, N_HEADS // 2, head_pair, 0, unroll=8)

    def project(jb):
        rows = pl.ds(jb * WINDOW, WINDOW)
        a_out = jnp.dot(attn_scr[rows, :], wo_ref[...], preferred_element_type=F32)
        out_ref[rows, :] = ga_ref[rows, :].astype(F32) * a_out

    for jb in range(tq // WINDOW):
        attend(jb)
        project(jb)


def _attn_prompt(sinks, q, kv, ga, wo, batch, seq, tq):
    nq = seq // tq
    bpt = tq // WINDOW
    row = lambda w: pl.BlockSpec((tq, w), lambda b, i: (b * nq + i, 0))
    prev = pl.BlockSpec((WINDOW, 2 * D_KV),
                        lambda b, i: (jnp.maximum((b * nq + i) * bpt - 1, 0), 0))
    return pl.pallas_call(
        functools.partial(_attn_prompt_kernel, tq=tq),
        grid=(batch, nq),
        in_specs=[pl.BlockSpec(memory_space=pltpu.SMEM), row(D_MODEL), row(2 * D_KV), prev,
                  row(D_MODEL), _const_spec(wo.shape)],
        out_specs=row(D_MODEL),
        out_shape=jax.ShapeDtypeStruct((batch * seq, D_MODEL), F32),
        scratch_shapes=[pltpu.VMEM((N_KV_HEADS, WINDOW + tq, LANES), BF16),
                        pltpu.VMEM((N_KV_HEADS, WINDOW + tq, LANES), BF16),
                        pltpu.VMEM((WINDOW, 2 * WINDOW), F32),
                        pltpu.VMEM((tq, D_MODEL), BF16)],
        compiler_params=_params("arbitrary", "arbitrary"),
        name="attn_prompt",
    )(sinks, q, kv, kv, ga, wo)


def _attn_sample_kernel(sinks_ref, q_ref, kvn_ref, ck_ref, cv_ref, ga_ref, wo_ref,
                        out_ref, ko_ref, vo_ref, attn_scr, *, bb, tpad, dec_seq):
    nk = WINDOW + tpad
    lo3 = lax.broadcasted_iota(jnp.int32, (bb, tpad, LANES), 2) < HEAD_DIM
    row = lax.broadcasted_iota(jnp.int32, (4 * tpad, nk), 0)
    si = lax.broadcasted_iota(jnp.int32, (4 * tpad, nk), 1)
    tq = row % tpad
    valid = (si > tq) & (si <= tq + WINDOW) & (si < WINDOW + dec_seq)
    hrow = lax.broadcasted_iota(jnp.int32, (4 * tpad, 1), 0) // tpad

    for t in range(2):
        sl = slice(t * LANES, (t + 1) * LANES)
        kk = jnp.concatenate([ck_ref[:, :, sl], kvn_ref[:, :, sl]], axis=1)
        vv = jnp.concatenate([cv_ref[:, :, sl],
                              kvn_ref[:, :, D_KV + t * LANES:D_KV + (t + 1) * LANES]], axis=1)
        ko_ref[:, :, sl] = kk[:, dec_seq:dec_seq + WINDOW, :]
        vo_ref[:, :, sl] = vv[:, dec_seq:dec_seq + WINDOW, :]
        kds = _dup_heads(kk)
        vds = _dup_heads(vv)
        for par_kv in range(2):
            j = 2 * t + par_kv
            kd, vd = kds[par_kv], vds[par_kv]
            parts = []
            for r in range(2):
                c0 = j * 2 * LANES + r * LANES
                qp = q_ref[:, :, c0:c0 + LANES]
                parts += [jnp.where(lo3, qp, 0.0), jnp.where(lo3, 0.0, qp)]
            lhs = jnp.concatenate(parts, axis=1).astype(BF16)
            s = jnp.einsum('bqd,bkd->bqk', lhs, kd, preferred_element_type=F32)
            sink = jnp.zeros((4 * tpad, 1), F32)
            for g in range(4):
                sink = jnp.where(hrow == g, sinks_ref[4 * j + g], sink)
            p, inv = _sink_softmax(jnp.where(valid[None], s, -jnp.inf), sink[None])
            o = jnp.einsum('bqk,bkd->bqd', p, vd, preferred_element_type=F32) * inv
            for r in range(2):
                c0 = j * 2 * LANES + r * LANES
                o_even = o[:, (2 * r) * tpad:(2 * r + 1) * tpad, :]
                o_odd = o[:, (2 * r + 1) * tpad:(2 * r + 2) * tpad, :]
                attn_scr[:, :, c0:c0 + LANES] = jnp.where(lo3, o_even, o_odd)

    attn = attn_scr[...].reshape(bb * tpad, D_MODEL).astype(BF16)
    a_out = jnp.dot(attn, wo_ref[...], preferred_element_type=F32)
    out_ref[...] = ga_ref[...].astype(F32) * a_out


def _attn_sample(sinks, q3, kvn3, ck, cv, ga2, wo, bb, dec_seq):
    db, tpad, _ = q3.shape
    blk3 = lambda r, w: pl.BlockSpec((bb, r, w), lambda i: (i, 0, 0))
    row = pl.BlockSpec((bb * tpad, D_MODEL), lambda i: (i, 0))
    return pl.pallas_call(
        functools.partial(_attn_sample_kernel, bb=bb, tpad=tpad, dec_seq=dec_seq),
        grid=(db // bb,),
        in_specs=[pl.BlockSpec(memory_space=pltpu.SMEM), blk3(tpad, D_MODEL), blk3(tpad, 2 * D_KV),
                  blk3(WINDOW, D_KV), blk3(WINDOW, D_KV), row, _const_spec(wo.shape)],
        out_specs=[row, blk3(WINDOW, D_KV), blk3(WINDOW, D_KV)],
        out_shape=[jax.ShapeDtypeStruct((db * tpad, D_MODEL), F32),
                   jax.ShapeDtypeStruct((db, WINDOW, D_KV), F32),
                   jax.ShapeDtypeStruct((db, WINDOW, D_KV), F32)],
        scratch_shapes=[pltpu.VMEM((bb, tpad, D_MODEL), F32)],
        compiler_params=_params("arbitrary"),
        name="attn_sample",
    )(sinks, q3, kvn3, ck, cv, ga2, wo)


def _s5_discretize(lam_re, lam_im, log_dt, b_re, b_im):
    dt = jnp.exp(log_dt)[:, None]
    decay = jnp.exp(lam_re * dt)
    ab_re = decay * jnp.cos(lam_im * dt)
    ab_im = decay * jnp.sin(lam_im * dt)
    nr, ni = ab_re - 1.0, ab_im
    den = lam_re * lam_re + lam_im * lam_im
    f_re = ((nr * lam_re + ni * lam_im) / den)[..., None]
    f_im = ((ni * lam_re - nr * lam_im) / den)[..., None]
    return ab_re, ab_im, f_re * b_re - f_im * b_im, f_re * b_im + f_im * b_re


def _state_split(h):
    h = h.reshape(h.shape[:-1] + (N_PAIRS, 2, 2, SSM_STATE))
    unpair = lambda a: a.reshape(a.shape[:-3] + (SSM_GROUPS, SSM_STATE))
    return unpair(h[..., 0, :, :]), unpair(h[..., 1, :, :])


def _shift_lanes(x, n):
    a, b = x[:, :LANES], x[:, LANES:]
    lane = lax.broadcasted_iota(jnp.int32, a.shape, 1)
    if n == 0:
        return x
    if n >= LANES:
        r = n - LANES
        hi = a if r == 0 else jnp.where(lane < r, 0.0, pltpu.roll(a, r, 1))
        return jnp.concatenate([jnp.zeros_like(a), hi], axis=1)
    ra, rb = pltpu.roll(a, n, 1), pltpu.roll(b, n, 1)
    return jnp.concatenate([jnp.where(lane < n, 0.0, ra), jnp.where(lane < n, ra, rb)], axis=1)


def _dot_nt_split(lhs, rhs):
    nt = lambda a, b: lax.dot_general(a, b, (((1,), (1,)), ((), ())), preferred_element_type=F32)
    l_hi, r_hi = lhs.astype(BF16), rhs.astype(BF16)
    l_lo = (lhs - l_hi.astype(F32)).astype(BF16)
    r_lo = (rhs - r_hi.astype(F32)).astype(BF16)
    return nt(l_hi, r_hi) + nt(l_hi, r_lo) + nt(l_lo, r_hi)


def _s5_prep_kernel(a_ref, c_ref, bt_ref, m_ref, ws_ref, wsn_ref, whyt_ref, mk_ref, pw_ref, an_ref, *, n_tok):
    for p in range(PREP_PAIRS):
        one, two, lanes = pl.ds(p, 1), pl.ds(2 * p, 2), pl.ds(p * LANES, LANES)
        _s5_prep_pair(a_ref.at[one], c_ref.at[one], bt_ref.at[one], m_ref.at[two], ws_ref.at[two],
                      wsn_ref.at[two], whyt_ref.at[two], mk_ref.at[:, :, :, lanes], pw_ref.at[:, :, lanes],
                      an_ref.at[:, :, lanes], n_tok)


def _s5_prep_pair(a_ref, c_ref, bt_ref, m_ref, ws_ref, wsn_ref, whyt_ref, mk_ref, pw_ref, an_ref, n_tok):
    ar, ai = a_ref[0, 0], a_ref[0, 1]
    cr, ci = c_ref[0, 0], c_ref[0, 1]
    br, bi = bt_ref[0, 0], bt_ref[0, 1]
    pr, pi = [jnp.ones_like(ar)], [jnp.zeros_like(ai)]
    for _ in range(CHUNK):
        pr, pi = pr + [pr[-1] * ar - pi[-1] * ai], pi + [pr[-1] * ai + pi[-1] * ar]
    first = lax.broadcasted_iota(jnp.int32, (SSM_CH, LANES), 1) < SSM_STATE
    ca = [(cr * pr[t] - ci * pi[t], -(cr * pi[t] + ci * pr[t])) for t in range(CHUNK + 1)]
    ca_full = jnp.concatenate([jnp.concatenate(ca[t], axis=1) for t in range(CHUNK)], axis=0)
    zero = jnp.zeros((SSM_CH, LANES), F32)
    for e in range(2):
        own = first if e == 0 else ~first
        pick = lambda v: jnp.where(own, v, zero)
        for t in range(CHUNK):
            rows = slice(t * SSM_CH, (t + 1) * SSM_CH)
            whyt_ref[e, rows, :LANES] = pick(ca[t + 1][0]).astype(BF16)
            whyt_ref[e, rows, LANES:] = pick(ca[t + 1][1]).astype(BF16)
            for ref, k in ((ws_ref, CHUNK - 1 - t), (wsn_ref, n_tok - 1 - t)):
                if k >= 0:
                    ref[e, rows, :LANES] = pick(br * pr[k] - bi * pi[k]).astype(BF16)
                    ref[e, rows, LANES:] = pick(br * pi[k] + bi * pr[k]).astype(BF16)
                else:
                    ref[e, rows, :] = jnp.zeros((SSM_CH, PAIR_COLS), BF16)
        k_row = _dot_nt_split(jnp.concatenate([pick(br), pick(bi)], axis=1), ca_full)
        for s in range(CHUNK):
            m_ref[e, s * SSM_CH:(s + 1) * SSM_CH, :] = _shift_lanes(k_row, s * SSM_CH).astype(BF16)
    sub = lax.broadcasted_iota(jnp.int32, (SUBLANES, LANES), 0)
    qr, qi = [pr[CHUNK]], [pi[CHUNK]]
    for _ in range(SUBLANES - 1):
        qr, qi = qr + [qr[-1] * pr[CHUNK] - qi[-1] * pi[CHUNK]], qi + [qr[-1] * pi[CHUNK] + qi[-1] * pr[CHUNK]]
    for part, q in enumerate((qr, qi)):
        rows = jnp.zeros((SUBLANES, LANES), F32)
        for s in range(SUBLANES):
            rows = jnp.where(sub == s, q[s], rows)
        pw_ref[part] = rows
        for k, shift in enumerate((1, 2, 4)):
            mk_ref[k, part] = jnp.where(sub >= shift, q[shift - 1], 0.0)
    an_ref[0] = jnp.broadcast_to(pr[n_tok], (SUBLANES, LANES))
    an_ref[1] = jnp.broadcast_to(pi[n_tok], (SUBLANES, LANES))


def _s5_prep(lam_re, lam_im, log_dt, b_re, b_im, c_re, c_im, n_tok):
    ab_re, ab_im, bb_re, bb_im = _s5_discretize(lam_re, lam_im, log_dt, b_re, b_im)
    pair = lambda v: v.reshape(N_PAIRS, 2, v.shape[1], SSM_STATE).transpose(0, 2, 1, 3).reshape(
        N_PAIRS, v.shape[1], LANES)
    a = jnp.stack([pair(ab_re[:, None, :]), pair(ab_im[:, None, :])], axis=1)
    c = jnp.stack([pair(c_re), pair(c_im)], axis=1)
    bt = jnp.stack([pair(bb_re.transpose(0, 2, 1)), pair(bb_im.transpose(0, 2, 1))], axis=1)
    blk4 = lambda r: pl.BlockSpec((PREP_PAIRS, 2, r, LANES), lambda q: (q, 0, 0, 0))
    w_spec = pl.BlockSpec((2 * PREP_PAIRS, GROUP_IO, PAIR_COLS), lambda q: (q, 0, 0))
    w_shape = jax.ShapeDtypeStruct((SSM_GROUPS, GROUP_IO, PAIR_COLS), BF16)
    return pl.pallas_call(
        functools.partial(_s5_prep_kernel, n_tok=n_tok),
        grid=(N_PAIRS // PREP_PAIRS,),
        in_specs=[blk4(1), blk4(SSM_CH), blk4(SSM_CH)],
        out_specs=[w_spec, w_spec, w_spec, w_spec,
                   pl.BlockSpec((3, 2, SUBLANES, PREP_PAIRS * LANES), lambda q: (0, 0, 0, q)),
                   pl.BlockSpec((2, SUBLANES, PREP_PAIRS * LANES), lambda q: (0, 0, q)),
                   pl.BlockSpec((2, SUBLANES, PREP_PAIRS * LANES), lambda q: (0, 0, q))],
        out_shape=[w_shape, w_shape, w_shape, w_shape,
                   jax.ShapeDtypeStruct((3, 2, SUBLANES, HALF_COLS), F32),
                   jax.ShapeDtypeStruct((2, SUBLANES, HALF_COLS), F32),
                   jax.ShapeDtypeStruct((2, SUBLANES, HALF_COLS), F32)],
        compiler_params=_params("arbitrary"),
        name="s5_prep",
    )(a, c, bt)


def _group_io(ref, g):
    return ref.at[:, g * GROUP_IO:(g + 1) * GROUP_IO]


def _s5_emit_tile(u_ref, st_scr, m_ref, whyt_ref, dt_ref, g_ref, j):
    for gl in range(GROUPS_PER_TILE):
        g = GROUPS_PER_TILE * j + gl
        q0 = (gl // 2) * PAIR_COLS
        u_g = _group_io(u_ref, g)[...]
        y = (jnp.dot(u_g, m_ref[g], preferred_element_type=F32)
             + lax.dot_general(st_scr[:, q0:q0 + PAIR_COLS].astype(BF16), whyt_ref[g],
                               (((1,), (1,)), ((), ())), preferred_element_type=F32)
             + dt_ref[g] * u_g.astype(F32))
        _group_io(g_ref, g)[...] = jax.nn.gelu(y).astype(BF16)


def _s5_local_states(u_ref, ws_ref, j, q):
    g = GROUPS_PER_TILE * j + 2 * q
    return (jnp.dot(_group_io(u_ref, g)[...], ws_ref[g], preferred_element_type=F32)
            + jnp.dot(_group_io(u_ref, g + 1)[...], ws_ref[g + 1], preferred_element_type=F32))


def _cmul_add(ar, ai, hr, hi, xr, xi):
    return ar * hr - ai * hi + xr, ar * hi + ai * hr + xi


def _s5_chain_kernel(u_ref, m_ref, ws_ref, whyt_ref, mk_ref, pw_ref, dt_ref, h0_ref, g_ref, hT_ref,
                     carry_scr, st_scr, *, rows):
    @pl.when(pl.program_id(1) == 0)
    def _():
        carry_scr[...] = jnp.broadcast_to(h0_ref[0], (SUBLANES, STATE_COLS))

    row0 = lax.broadcasted_iota(jnp.int32, (SUBLANES, LANES), 0) == 0
    last = lambda h: jnp.broadcast_to(h[SUBLANES - 1:, :], (SUBLANES, LANES))

    for j in range(N_LANE_TILES):
        for q in range(PAIRS_PER_TILE):
            st_scr[:, q * PAIR_COLS:(q + 1) * PAIR_COLS] = _s5_local_states(u_ref, ws_ref, j, q)

        def step(r, carry):
            r0 = pl.multiple_of(r * SUBLANES, SUBLANES)
            out = []
            for q in range(PAIRS_PER_TILE):
                re_c = slice(q * PAIR_COLS, q * PAIR_COLS + LANES)
                im_c = slice(q * PAIR_COLS + LANES, (q + 1) * PAIR_COLS)
                hc = slice((j * PAIRS_PER_TILE + q) * LANES, (j * PAIRS_PER_TILE + q + 1) * LANES)
                xr = st_scr[pl.ds(r0, SUBLANES), re_c]
                xi = st_scr[pl.ds(r0, SUBLANES), im_c]
                for k, shift in enumerate((1, 2, 4)):
                    xr, xi = _cmul_add(mk_ref[k, 0, :, hc], mk_ref[k, 1, :, hc],
                                       pltpu.roll(xr, shift, 0), pltpu.roll(xi, shift, 0), xr, xi)
                cr, ci = carry[q]
                hr, hi = _cmul_add(pw_ref[0, :, hc], pw_ref[1, :, hc], cr, ci, xr, xi)
                st_scr[pl.ds(r0, SUBLANES), re_c] = jnp.where(row0, cr, pltpu.roll(hr, 1, 0))
                st_scr[pl.ds(r0, SUBLANES), im_c] = jnp.where(row0, ci, pltpu.roll(hi, 1, 0))
                out.append((last(hr), last(hi)))
            return tuple(out)

        c0 = j * TILE_COLS
        init = tuple((carry_scr[:, c0 + q * PAIR_COLS:c0 + q * PAIR_COLS + LANES],
                      carry_scr[:, c0 + q * PAIR_COLS + LANES:c0 + (q + 1) * PAIR_COLS])
                     for q in range(PAIRS_PER_TILE))
        final = lax.fori_loop(0, rows // SUBLANES, step, init, unroll=True)
        for q in range(PAIRS_PER_TILE):
            carry_scr[:, c0 + q * PAIR_COLS:c0 + q * PAIR_COLS + LANES] = final[q][0]
            carry_scr[:, c0 + q * PAIR_COLS + LANES:c0 + (q + 1) * PAIR_COLS] = final[q][1]

        _s5_emit_tile(u_ref, st_scr, m_ref, whyt_ref, dt_ref, g_ref, j)

    hT_ref[0] = carry_scr[...]


def _s5_chain(u16, m, ws, why, mk, pw, d_tiled, h0, batch, n_rows, rows):
    nblk = n_rows // rows
    row = pl.BlockSpec((rows, CHUNK_LANES), lambda b, i: (b * nblk + i, 0))
    return pl.pallas_call(
        functools.partial(_s5_chain_kernel, rows=rows),
        grid=(batch, nblk),
        in_specs=[row, _const_spec(m.shape), _const_spec(ws.shape), _const_spec(why.shape),
                  _const_spec(mk.shape), _const_spec(pw.shape), _const_spec(d_tiled.shape),
                  pl.BlockSpec((1, 1, STATE_COLS), lambda b, i: (b, 0, 0))],
        out_specs=[row, pl.BlockSpec((1, SUBLANES, STATE_COLS), lambda b, i: (b, 0, 0))],
        out_shape=[jax.ShapeDtypeStruct(u16.shape, BF16),
                   jax.ShapeDtypeStruct((batch, SUBLANES, STATE_COLS), F32)],
        scratch_shapes=[pltpu.VMEM((SUBLANES, STATE_COLS), F32), pltpu.VMEM((rows, TILE_COLS), F32)],
        compiler_params=_params("arbitrary", "arbitrary"),
        name="s5_chain",
    )(u16, m, ws, why, mk, pw, d_tiled, h0)


def _s5_rows_kernel(u_ref, m_ref, ws_ref, whyt_ref, an_ref, dt_ref, h0r_ref, h0i_ref, g_ref, hTr_ref, hTi_ref,
                    st_scr):
    for j in range(N_LANE_TILES):
        for q in range(PAIRS_PER_TILE):
            hc = slice((j * PAIRS_PER_TILE + q) * LANES, (j * PAIRS_PER_TILE + q + 1) * LANES)
            h0r, h0i = h0r_ref[hc, :].T, h0i_ref[hc, :].T
            local = _s5_local_states(u_ref, ws_ref, j, q)
            hr, hi = _cmul_add(an_ref[0, :1, hc], an_ref[1, :1, hc], h0r, h0i, local[:, :LANES], local[:, LANES:])
            hTr_ref[hc, :] = hr.T
            hTi_ref[hc, :] = hi.T
            st_scr[:, q * PAIR_COLS:q * PAIR_COLS + LANES] = h0r
            st_scr[:, q * PAIR_COLS + LANES:(q + 1) * PAIR_COLS] = h0i
        _s5_emit_tile(u_ref, st_scr, m_ref, whyt_ref, dt_ref, g_ref, j)


def _s5_rows(u16, m, ws, why, an, d_tiled, h0_re_t, h0_im_t, rows):
    n = u16.shape[0]
    once = pl.Buffered(1)
    row = pl.BlockSpec((rows, CHUNK_LANES), lambda i: (i, 0), pipeline_mode=once)
    col = pl.BlockSpec((HALF_COLS, rows), lambda i: (0, i), pipeline_mode=once)
    state = jax.ShapeDtypeStruct((HALF_COLS, n), F32)
    return pl.pallas_call(
        _s5_rows_kernel,
        grid=(n // rows,),
        in_specs=[row, _const_spec(m.shape), _const_spec(ws.shape), _const_spec(why.shape),
                  _const_spec(an.shape), _const_spec(d_tiled.shape), col, col],
        out_specs=[row, col, col],
        out_shape=[jax.ShapeDtypeStruct(u16.shape, BF16), state, state],
        scratch_shapes=[pltpu.VMEM((rows, TILE_COLS), F32)],
        compiler_params=_params("arbitrary"),
        name="s5_rows",
    )(u16, m, ws, why, an, d_tiled, h0_re_t, h0_im_t)


def _post_kernel(g_ref, ma_ref, gs_ref, x_ref, permt_ref, wglu_ref, wout_ref, gffn_ref, wup_ref, wdown_ref,
                 gfin_ref, out_ref, *, n_tok):
    g = _load_chunk_rows(g_ref, permt_ref, n_tok)
    glu = jnp.dot(g, wglu_ref[...], preferred_element_type=F32)
    s_out = glu[:, :D_MODEL] * jax.nn.sigmoid(glu[:, D_MODEL:])
    merged = ma_ref[...] + gs_ref[...].astype(F32) * s_out
    x1 = x_ref[...] + jnp.dot(merged.astype(BF16), wout_ref[...], preferred_element_type=F32)
    h2 = _rmsnorm(x1, gffn_ref[...]).astype(BF16)
    x2 = x1
    for c in range(0, D_FF, FF_CHUNK):
        up = jnp.dot(h2, wup_ref[:, c:c + FF_CHUNK], preferred_element_type=F32)
        act = jnp.square(jnp.maximum(up, 0.0)).astype(BF16)
        x2 = x2 + jnp.dot(act, wdown_ref[c:c + FF_CHUNK, :], preferred_element_type=F32)
    out_ref[...] = _rmsnorm(x2, gfin_ref[...])


def _post(g, ma, gs, x, permt, w_glu, w_out, g_ffn, w_up, w_down, g_final, tm, n_tok):
    n = x.shape[0]
    row = pl.BlockSpec((tm, D_MODEL), lambda i: (i, 0))
    g_spec = pl.BlockSpec((tm // n_tok, CHUNK_LANES), lambda i: (i, 0))
    return pl.pallas_call(
        functools.partial(_post_kernel, n_tok=n_tok),
        grid=(n // tm,),
        in_specs=[g_spec, row, row, row, _const_spec(permt.shape), _const_spec(w_glu.shape),
                  _const_spec(w_out.shape), _const_spec((1, D_MODEL)), _const_spec(w_up.shape),
                  _const_spec(w_down.shape), _const_spec((1, D_MODEL))],
        out_specs=row,
        out_shape=jax.ShapeDtypeStruct((n, D_MODEL), F32),
        compiler_params=_params("arbitrary"),
        name="post",
    )(g, ma, gs, x, permt, w_glu, w_out, g_ffn, w_up, w_down, g_final)


def _tile(n, pref):
    t = pref
    while n % t:
        t //= 2
    return t


def kernel(x_prompt, x_sample, cache_k, cache_v, state_ssm_re, state_ssm_im, g_mix, w_in, attn_sinks,
           w_attn_o, ssm_lambda_re, ssm_lambda_im, ssm_log_dt, ssm_b_re, ssm_b_im, ssm_c_re, ssm_c_im,
           ssm_d, w_glu, w_out, g_ffn, w_up, w_down, g_final):
    batch, seq, _ = x_prompt.shape
    db, dec_seq, _ = x_sample.shape
    assert w_in.shape[0] == 1, "one layer"
    assert seq % PERM_ROWS == 0 and dec_seq in (1, 2, 4, 8) and (db * dec_seq) % PERM_ROWS == 0

    vec = lambda v: v.reshape(1, D_MODEL).astype(F32)
    w_in_b = w_in[0].astype(BF16)
    sinks = attn_sinks[0].astype(F32)
    s5p = (ssm_lambda_re[0], ssm_lambda_im[0], ssm_log_dt[0], ssm_b_re[0], ssm_b_im[0], ssm_c_re[0], ssm_c_im[0])
    d_skip = jnp.tile(ssm_d[0].astype(F32).reshape(SSM_GROUPS, 1, SSM_CH), (1, 1, CHUNK))
    perm = _chunk_perm(CHUNK)

    xp = x_prompt.reshape(batch * seq, D_MODEL)
    q, kv, u16, ga, gs, wo_b, w_glu_b, w_out_b, w_up_b, w_down_b = _proj(
        xp, vec(g_mix[0]), w_in_b, perm, _tile(batch * seq, 1024), CHUNK,
        cast=(w_attn_o[0], w_glu[0], w_out[0], w_up[0], w_down[0]))
    post_w = (w_glu_b, w_out_b, vec(g_ffn[0]), w_up_b, w_down_b, vec(g_final))
    ma = _attn_prompt(sinks, q, kv, ga, wo_b, batch, seq, _tile(seq, 1024))
    m, ws, ws_n, whyt, mk, pw, an = _s5_prep(*s5p, dec_seq)
    n_rows = seq // CHUNK
    g16, hT = _s5_chain(u16, m, ws, whyt, mk, pw, d_skip, jnp.zeros((batch, 1, STATE_COLS), F32),
                        batch, n_rows, _tile(n_rows, 128))
    y_prompt = _post(g16, ma, gs, xp, perm.T, *post_w, _tile(batch * seq, 512), CHUNK).reshape(batch, seq, D_MODEL)
    kv_last = kv.reshape(batch, seq, 2 * D_KV)[:, seq - WINDOW:]
    k_prompt = kv_last[..., :D_KV].reshape(1, batch, WINDOW, N_KV_HEADS, HEAD_DIM)
    v_prompt = kv_last[..., D_KV:].reshape(1, batch, WINDOW, N_KV_HEADS, HEAD_DIM)
    hr, hi = _state_split(hT[:, 0])
    ssm_re_prompt, ssm_im_prompt = hr[None], hi[None]

    ns = db * dec_seq
    xs = x_sample.reshape(ns, D_MODEL)
    perm_s = _chunk_perm(dec_seq)
    q, kv, u16, ga, gs = _proj(xs, vec(g_mix[0]), w_in_b, perm_s, _tile(ns, 512), dec_seq)
    tpad = SUBLANES
    pad3 = lambda v: jnp.pad(v.reshape(db, dec_seq, -1).astype(F32), ((0, 0), (0, tpad - dec_seq), (0, 0)))
    ga_pad = pad3(ga).astype(BF16).reshape(db * tpad, D_MODEL)
    ma_pad, k_new, v_new = _attn_sample(
        sinks, pad3(q), pad3(kv), cache_k[0].reshape(db, WINDOW, D_KV), cache_v[0].reshape(db, WINDOW, D_KV),
        ga_pad, wo_b, _tile(db, 16), dec_seq)
    ma = ma_pad.reshape(db, tpad, D_MODEL)[:, :dec_seq].reshape(ns, D_MODEL)
    to_cols = lambda s: s[0].astype(F32).transpose(1, 2, 0).reshape(HALF_COLS, db)
    from_cols = lambda h: h.reshape(SSM_GROUPS, SSM_STATE, db).transpose(2, 0, 1)[None]
    g16, hT_re, hT_im = _s5_rows(u16, m, ws_n, whyt, an, d_skip, to_cols(state_ssm_re), to_cols(state_ssm_im),
                                 _tile(db, LANES))
    y_sample = _post(g16, ma, gs, xs, perm_s.T, *post_w, PERM_ROWS, dec_seq).reshape(db, dec_seq, D_MODEL)
    k_sample = k_new.reshape(1, db, WINDOW, N_KV_HEADS, HEAD_DIM)
    v_sample = v_new.reshape(1, db, WINDOW, N_KV_HEADS, HEAD_DIM)
    ssm_re_sample, ssm_im_sample = from_cols(hT_re), from_cols(hT_im)

    return (y_prompt, y_sample, k_prompt, v_prompt, ssm_re_prompt, ssm_im_prompt,
            k_sample, v_sample, ssm_re_sample, ssm_im_sample)
```

```python
import functools

import jax
import jax.numpy as jnp
from jax import lax
from jax.experimental import pallas as pl
from jax.experimental.pallas import tpu as pltpu

F32 = jnp.float32
BF16 = jnp.bfloat16

D_MODEL = 1024
HEAD_DIM = 64
N_HEADS = 16
N_KV_HEADS = 4
WINDOW = 128
D_KV = N_KV_HEADS * HEAD_DIM
SSM_CH = 16
SSM_GROUPS = 64
SSM_STATE = 64
D_FF = 4 * D_MODEL
FF_CHUNK = 1024
RMS_EPS = 1e-5
LOG2E = 1.4426950408889634
Q_SCALE = HEAD_DIM ** -0.5 * LOG2E

LANES = 128
SUBLANES = 8
BF16_ROWS = 16
N_LANE_TILES = D_MODEL // LANES
GROUPS_PER_TILE = LANES // SSM_CH
CHUNK = 16
CHUNK_LANES = CHUNK * D_MODEL
GROUP_IO = CHUNK * SSM_CH
N_PAIRS = SSM_GROUPS // 2
PAIRS_PER_TILE = GROUPS_PER_TILE // 2
PAIR_COLS = 2 * LANES
TILE_COLS = PAIRS_PER_TILE * PAIR_COLS
STATE_COLS = N_PAIRS * PAIR_COLS
HALF_COLS = STATE_COLS // 2
PERM_ROWS = CHUNK * BF16_ROWS
PREP_PAIRS = 4
VMEM_LIMIT = 56 * 1024 * 1024

_Q0, _KV0, _U0, _GA0, _GS0, _END = 0, 1024, 1536, 2560, 3584, 4608


def _rmsnorm(x, g):
    return x * lax.rsqrt(jnp.mean(x * x, axis=-1, keepdims=True) + RMS_EPS) * g


def _params(*sem):
    return pltpu.CompilerParams(dimension_semantics=sem, vmem_limit_bytes=VMEM_LIMIT)


def _const_spec(shape):
    nd = len(shape)
    return pl.BlockSpec(shape, lambda *_: (0,) * nd, pipeline_mode=pl.Buffered(1))


def _chunk_perm(n_tok):
    n_chunks = PERM_ROWS // n_tok
    r = jnp.arange(PERM_ROWS)
    tok = (r % n_chunks) * n_tok + r // n_chunks
    return (tok[:, None] == jnp.arange(PERM_ROWS)[None, :]).astype(BF16)


def _piece_transpose(cols, masks):
    for d, msk in zip((4, 2, 1), masks):
        new = list(cols)
        for v in range(GROUPS_PER_TILE):
            if v & d == 0:
                a, b = cols[v], cols[v + d]
                new[v] = jnp.where(msk, pltpu.roll(b, SSM_CH * d, 1), a)
                new[v + d] = jnp.where(msk, b, pltpu.roll(a, LANES - SSM_CH * d, 1))
        cols = new
    return cols


def _piece_masks(rows):
    piece = lax.broadcasted_iota(jnp.int32, (rows, LANES), 1) // SSM_CH
    return [(piece & d) != 0 for d in (4, 2, 1)]


def _store_chunk_rows(x, perm_ref, out_ref, n_tok):
    n_chunks = PERM_ROWS // n_tok
    masks = _piece_masks(n_chunks)
    zero = jnp.zeros((n_chunks, LANES), F32)
    for hb in range(x.shape[0] // PERM_ROWS):
        rows = slice(hb * n_chunks, (hb + 1) * n_chunks)
        xp = jnp.dot(perm_ref[...], x[hb * PERM_ROWS:(hb + 1) * PERM_ROWS],
                     preferred_element_type=F32)
        for j in range(N_LANE_TILES):
            for hf in range(2):
                live = 8 * hf < n_tok
                cols = [xp[t * n_chunks:(t + 1) * n_chunks, j * LANES:(j + 1) * LANES] if t < n_tok else zero
                        for t in range(8 * hf, 8 * hf + 8)]
                for gl, col in enumerate(_piece_transpose(cols, masks) if live else cols):
                    c0 = (GROUPS_PER_TILE * j + gl) * GROUP_IO + hf * LANES
                    out_ref[rows, c0:c0 + LANES] = col.astype(BF16)


def _load_chunk_rows(in_ref, permt_ref, n_tok):
    n_chunks = PERM_ROWS // n_tok
    masks = _piece_masks(n_chunks)
    blocks = []
    for hb in range(in_ref.shape[0] // n_chunks):
        rows = slice(hb * n_chunks, (hb + 1) * n_chunks)
        tiles = [[None] * N_LANE_TILES for _ in range(n_tok)]
        for j in range(N_LANE_TILES):
            for hf in range(-(-n_tok // 8)):
                cols = []
                for gl in range(GROUPS_PER_TILE):
                    c0 = (GROUPS_PER_TILE * j + gl) * GROUP_IO + hf * LANES
                    cols.append(in_ref[rows, c0:c0 + LANES].astype(F32))
                for k, col in enumerate(_piece_transpose(cols, masks)):
                    if 8 * hf + k < n_tok:
                        tiles[8 * hf + k][j] = col.astype(BF16)
        xp = jnp.concatenate([jnp.concatenate(row, axis=1) for row in tiles], axis=0)
        blocks.append(jnp.dot(permt_ref[...], xp, preferred_element_type=F32).astype(BF16))
    return blocks[0] if len(blocks) == 1 else jnp.concatenate(blocks, axis=0)


def _proj_kernel(x_ref, g_ref, w_ref, perm_ref, *refs, n_tok, n_cast):
    q_ref, kv_ref, u_ref, ga_ref, gs_ref = refs[n_cast:n_cast + 5]
    h = _rmsnorm(x_ref[...], g_ref[...]).astype(BF16)

    def seg(lo, hi):
        return jnp.dot(h, w_ref[:, lo:hi], preferred_element_type=F32)

    q_ref[...] = (seg(_Q0, _KV0) * Q_SCALE).astype(BF16)
    kv_ref[...] = seg(_KV0, _U0)
    _store_chunk_rows(seg(_U0, _GA0).astype(BF16), perm_ref, u_ref, n_tok)
    ga_ref[...] = jax.nn.sigmoid(seg(_GA0, _GS0)).astype(BF16)
    gs_ref[...] = jax.nn.sigmoid(seg(_GS0, _END)).astype(BF16)
    for src, dst in zip(refs[:n_cast], refs[n_cast + 5:]):
        dst[...] = src[...].astype(BF16)


def _proj(x, g_mix, w_in, perm, tm, n_tok, cast=()):
    n = x.shape[0]
    steps = n // tm
    row = lambda w: pl.BlockSpec((tm, w), lambda i: (i, 0))
    slab = lambda w: pl.BlockSpec((w.shape[0] // steps, w.shape[1]), lambda i: (i, 0))
    assert all(w.shape[0] % (steps * BF16_ROWS) == 0 for w in cast)
    return pl.pallas_call(
        functools.partial(_proj_kernel, n_tok=n_tok, n_cast=len(cast)),
        grid=(steps,),
        in_specs=[row(D_MODEL), _const_spec((1, D_MODEL)), _const_spec(w_in.shape), _const_spec(perm.shape)]
        + [slab(w) for w in cast],
        out_specs=[row(D_MODEL), row(2 * D_KV), pl.BlockSpec((tm // n_tok, CHUNK_LANES), lambda i: (i, 0)),
                   row(D_MODEL), row(D_MODEL)] + [slab(w) for w in cast],
        out_shape=[jax.ShapeDtypeStruct((n, D_MODEL), BF16),
                   jax.ShapeDtypeStruct((n, 2 * D_KV), F32),
                   jax.ShapeDtypeStruct((n // n_tok, CHUNK_LANES), BF16),
                   jax.ShapeDtypeStruct((n, D_MODEL), BF16),
                   jax.ShapeDtypeStruct((n, D_MODEL), BF16)]
        + [jax.ShapeDtypeStruct(w.shape, BF16) for w in cast],
        compiler_params=_params("arbitrary"),
        name="proj",
    )(x, g_mix, w_in, perm, *cast)


def _dup_heads(tile):
    lo = lax.broadcasted_iota(jnp.int32, tile.shape, tile.ndim - 1) < HEAD_DIM
    rolled = pltpu.roll(tile, HEAD_DIM, tile.ndim - 1)
    return (jnp.where(lo, tile, rolled).astype(BF16), jnp.where(lo, rolled, tile).astype(BF16))


def _sink_softmax(s, sink):
    sink2 = sink * LOG2E
    m = jnp.maximum(jnp.max(s, axis=-1, keepdims=True), sink2)
    p = jnp.exp2(s - m)
    denom = jnp.sum(p, axis=-1, keepdims=True) + jnp.exp2(sink2 - m)
    return p.astype(BF16), 1.0 / denom


def _banded_attention(first_tile, sinks_ref, q_ref, kv_full, ga_ref, wo_ref, out_ref,
                      kd_scr, vd_scr, bias_scr, attn_scr, tq):
    for t in range(2):
        ke, ko = _dup_heads(kv_full[:, t * LANES:(t + 1) * LANES])
        ve, vo = _dup_heads(kv_full[:, D_KV + t * LANES:D_KV + (t + 1) * LANES])
        kd_scr[2 * t], kd_scr[2 * t + 1] = ke, ko
        vd_scr[2 * t], vd_scr[2 * t + 1] = ve, vo

    qi = lax.broadcasted_iota(jnp.int32, (WINDOW, 2 * WINDOW), 0)
    si = lax.broadcasted_iota(jnp.int32, (WINDOW, 2 * WINDOW), 1)
    band = (si > qi) & (si <= qi + WINDOW)
    in_block = si >= WINDOW
    lo = lax.broadcasted_iota(jnp.int32, (WINDOW, LANES), 1) < HEAD_DIM

    def attend(jb):
        q0 = jb * WINDOW
        has_prev = jnp.logical_not(first_tile) | (jb > 0)
        bias_scr[...] = jnp.where(band & (in_block | has_prev), 0.0, -jnp.inf)

        def head_pair(hp, c):
            j = hp // 2
            c0 = pl.multiple_of(hp * LANES, LANES)
            kd = kd_scr[j, pl.ds(q0, 2 * WINDOW), :]
            vd = vd_scr[j, pl.ds(q0, 2 * WINDOW), :]
            qp = q_ref[pl.ds(q0, WINDOW), pl.ds(c0, LANES)]
            outs = []
            for par in range(2):
                qm = jnp.where(lo if par == 0 else ~lo, qp, jnp.zeros_like(qp))
                s = lax.dot_general(qm, kd, (((1,), (1,)), ((), ())), preferred_element_type=F32)
                p, inv = _sink_softmax(s + bias_scr[...], sinks_ref[2 * hp + par])
                outs.append(jnp.dot(p, vd, preferred_element_type=F32) * inv)
            attn_scr[pl.ds(q0, WINDOW), pl.ds(c0, LANES)] = jnp.where(lo, outs[0], outs[1]).astype(BF16)
            return c

        lax.fori_loop(0, N_HEADS // 2, head_pair, 0, unroll=8)

    def project(jb):
        rows = pl.ds(jb * WINDOW, WINDOW)
        a_out = jnp.dot(attn_scr[rows, :], wo_ref[...], preferred_element_type=F32)
        out_ref[rows, :] = ga_ref[rows, :].astype(F32) * a_out

    for jb in range(tq // WINDOW):
        attend(jb)
        project(jb)


def _mix_in_kernel(sinks_ref, x_ref, g_ref, w_ref, perm_ref, wo_ref, *refs, tq, n_cast):
    u_ref, gs_ref, ma_ref, kvl_ref = refs[n_cast:n_cast + 4]
    kvp_scr, q_scr, ga_scr, kd_scr, vd_scr, bias_scr, attn_scr = refs[2 * n_cast + 4:]
    first_tile = pl.program_id(1) == 0

    @pl.when(first_tile)
    def _():
        kvp_scr[...] = jnp.zeros_like(kvp_scr)

    h = _rmsnorm(x_ref[...], g_ref[...]).astype(BF16)

    def seg(lo, hi):
        return jnp.dot(h, w_ref[:, lo:hi], preferred_element_type=F32)

    q_scr[...] = (seg(_Q0, _KV0) * Q_SCALE).astype(BF16)
    kv = seg(_KV0, _U0)
    kv_full = jnp.concatenate([kvp_scr[...], kv], axis=0)
    kvp_scr[...] = kv[tq - WINDOW:]
    kvl_ref[...] = kv[tq - WINDOW:]
    _store_chunk_rows(seg(_U0, _GA0).astype(BF16), perm_ref, u_ref, CHUNK)
    ga_scr[...] = jax.nn.sigmoid(seg(_GA0, _GS0)).astype(BF16)
    gs_ref[...] = jax.nn.sigmoid(seg(_GS0, _END)).astype(BF16)
    for src, dst in zip(refs[:n_cast], refs[n_cast + 4:2 * n_cast + 4]):
        dst[...] = src[...].astype(BF16)
    _banded_attention(first_tile, sinks_ref, q_scr, kv_full, ga_scr, wo_ref, ma_ref,
                      kd_scr, vd_scr, bias_scr, attn_scr, tq)


def _mix_in(sinks, x, g_mix, w_in, perm, wo, batch, seq, tq, cast):
    nq = seq // tq
    steps = batch * nq
    row = lambda w: pl.BlockSpec((tq, w), lambda b, i: (b * nq + i, 0))
    slab = lambda w: pl.BlockSpec((w.shape[0] // steps, w.shape[1]), lambda b, i: (b * nq + i, 0))
    assert all(w.shape[0] % (steps * BF16_ROWS) == 0 for w in cast)
    return pl.pallas_call(
        functools.partial(_mix_in_kernel, tq=tq, n_cast=len(cast)),
        grid=(batch, nq),
        in_specs=[pl.BlockSpec(memory_space=pltpu.SMEM), row(D_MODEL), _const_spec((1, D_MODEL)),
                  _const_spec(w_in.shape), _const_spec(perm.shape), _const_spec(wo.shape)]
        + [slab(w) for w in cast],
        out_specs=[pl.BlockSpec((tq // CHUNK, CHUNK_LANES), lambda b, i: (b * nq + i, 0)), row(D_MODEL),
                   row(D_MODEL), pl.BlockSpec((WINDOW, 2 * D_KV), lambda b, i: (b, 0))]
        + [slab(w) for w in cast],
        out_shape=[jax.ShapeDtypeStruct((batch * seq // CHUNK, CHUNK_LANES), BF16),
                   jax.ShapeDtypeStruct((batch * seq, D_MODEL), BF16),
                   jax.ShapeDtypeStruct((batch * seq, D_MODEL), F32),
                   jax.ShapeDtypeStruct((batch * WINDOW, 2 * D_KV), F32)]
        + [jax.ShapeDtypeStruct(w.shape, BF16) for w in cast],
        scratch_shapes=[pltpu.VMEM((WINDOW, 2 * D_KV), F32),
                        pltpu.VMEM((tq, D_MODEL), BF16),
                        pltpu.VMEM((tq, D_MODEL), BF16),
                        pltpu.VMEM((N_KV_HEADS, WINDOW + tq, LANES), BF16),
                        pltpu.VMEM((N_KV_HEADS, WINDOW + tq, LANES), BF16),
                        pltpu.VMEM((WINDOW, 2 * WINDOW), F32),
                        pltpu.VMEM((tq, D_MODEL), BF16)],
        compiler_params=_params("arbitrary", "arbitrary"),
        name="mix_in",
    )(sinks, x, g_mix, w_in, perm, wo, *cast)


def _attn_sample_kernel(sinks_ref, q_ref, kvn_ref, ck_ref, cv_ref, ga_ref, wo_ref,
                        out_ref, ko_ref, vo_ref, attn_scr, *, bb, tpad, dec_seq):
    nk = WINDOW + tpad
    lo3 = lax.broadcasted_iota(jnp.int32, (bb, tpad, LANES), 2) < HEAD_DIM
    row = lax.broadcasted_iota(jnp.int32, (4 * tpad, nk), 0)
    si = lax.broadcasted_iota(jnp.int32, (4 * tpad, nk), 1)
    tq = row % tpad
    valid = (si > tq) & (si <= tq + WINDOW) & (si < WINDOW + dec_seq)
    hrow = lax.broadcasted_iota(jnp.int32, (4 * tpad, 1), 0) // tpad

    for t in range(2):
        sl = slice(t * LANES, (t + 1) * LANES)
        kk = jnp.concatenate([ck_ref[:, :, sl], kvn_ref[:, :, sl]], axis=1)
        vv = jnp.concatenate([cv_ref[:, :, sl],
                              kvn_ref[:, :, D_KV + t * LANES:D_KV + (t + 1) * LANES]], axis=1)
        ko_ref[:, :, sl] = kk[:, dec_seq:dec_seq + WINDOW, :]
        vo_ref[:, :, sl] = vv[:, dec_seq:dec_seq + WINDOW, :]
        kds = _dup_heads(kk)
        vds = _dup_heads(vv)
        for par_kv in range(2):
            j = 2 * t + par_kv
            kd, vd = kds[par_kv], vds[par_kv]
            parts = []
            for r in range(2):
                c0 = j * 2 * LANES + r * LANES
                qp = q_ref[:, :, c0:c0 + LANES]
                parts += [jnp.where(lo3, qp, 0.0), jnp.where(lo3, 0.0, qp)]
            lhs = jnp.concatenate(parts, axis=1).astype(BF16)
            s = jnp.einsum('bqd,bkd->bqk', lhs, kd, preferred_element_type=F32)
            sink = jnp.zeros((4 * tpad, 1), F32)
            for g in range(4):
                sink = jnp.where(hrow == g, sinks_ref[4 * j + g], sink)
            p, inv = _sink_softmax(jnp.where(valid[None], s, -jnp.inf), sink[None])
            o = jnp.einsum('bqk,bkd->bqd', p, vd, preferred_element_type=F32) * inv
            for r in range(2):
                c0 = j * 2 * LANES + r * LANES
                o_even = o[:, (2 * r) * tpad:(2 * r + 1) * tpad, :]
                o_odd = o[:, (2 * r + 1) * tpad:(2 * r + 2) * tpad, :]
                attn_scr[:, :, c0:c0 + LANES] = jnp.where(lo3, o_even, o_odd)

    attn = attn_scr[...].reshape(bb * tpad, D_MODEL).astype(BF16)
    a_out = jnp.dot(attn, wo_ref[...], preferred_element_type=F32)
    out_ref[...] = ga_ref[...].astype(F32) * a_out


def _attn_sample(sinks, q3, kvn3, ck, cv, ga2, wo, bb, dec_seq):
    db, tpad, _ = q3.shape
    blk3 = lambda r, w: pl.BlockSpec((bb, r, w), lambda i: (i, 0, 0))
    row = pl.BlockSpec((bb * tpad, D_MODEL), lambda i: (i, 0))
    return pl.pallas_call(
        functools.partial(_attn_sample_kernel, bb=bb, tpad=tpad, dec_seq=dec_seq),
        grid=(db // bb,),
        in_specs=[pl.BlockSpec(memory_space=pltpu.SMEM), blk3(tpad, D_MODEL), blk3(tpad, 2 * D_KV),
                  blk3(WINDOW, D_KV), blk3(WINDOW, D_KV), row, _const_spec(wo.shape)],
        out_specs=[row, blk3(WINDOW, D_KV), blk3(WINDOW, D_KV)],
        out_shape=[jax.ShapeDtypeStruct((db * tpad, D_MODEL), F32),
                   jax.ShapeDtypeStruct((db, WINDOW, D_KV), F32),
                   jax.ShapeDtypeStruct((db, WINDOW, D_KV), F32)],
        scratch_shapes=[pltpu.VMEM((bb, tpad, D_MODEL), F32)],
        compiler_params=_params("arbitrary"),
        name="attn_sample",
    )(sinks, q3, kvn3, ck, cv, ga2, wo)


def _s5_discretize(lam_re, lam_im, log_dt, b_re, b_im):
    dt = jnp.exp(log_dt)[:, None]
    decay = jnp.exp(lam_re * dt)
    ab_re = decay * jnp.cos(lam_im * dt)
    ab_im = decay * jnp.sin(lam_im * dt)
    nr, ni = ab_re - 1.0, ab_im
    den = lam_re * lam_re + lam_im * lam_im
    f_re = ((nr * lam_re + ni * lam_im) / den)[..., None]
    f_im = ((ni * lam_re - nr * lam_im) / den)[..., None]
    return ab_re, ab_im, f_re * b_re - f_im * b_im, f_re * b_im + f_im * b_re


def _state_split(h):
    h = h.reshape(h.shape[:-1] + (N_PAIRS, 2, 2, SSM_STATE))
    unpair = lambda a: a.reshape(a.shape[:-3] + (SSM_GROUPS, SSM_STATE))
    return unpair(h[..., 0, :, :]), unpair(h[..., 1, :, :])


def _shift_lanes(x, n):
    a, b = x[:, :LANES], x[:, LANES:]
    lane = lax.broadcasted_iota(jnp.int32, a.shape, 1)
    if n == 0:
        return x
    if n >= LANES:
        r = n - LANES
        hi = a if r == 0 else jnp.where(lane < r, 0.0, pltpu.roll(a, r, 1))
        return jnp.concatenate([jnp.zeros_like(a), hi], axis=1)
    ra, rb = pltpu.roll(a, n, 1), pltpu.roll(b, n, 1)
    return jnp.concatenate([jnp.where(lane < n, 0.0, ra), jnp.where(lane < n, ra, rb)], axis=1)


def _dot_nt_split(lhs, rhs):
    nt = lambda a, b: lax.dot_general(a, b, (((1,), (1,)), ((), ())), preferred_element_type=F32)
    l_hi, r_hi = lhs.astype(BF16), rhs.astype(BF16)
    l_lo = (lhs - l_hi.astype(F32)).astype(BF16)
    r_lo = (rhs - r_hi.astype(F32)).astype(BF16)
    return nt(l_hi, r_hi) + nt(l_hi, r_lo) + nt(l_lo, r_hi)


def _s5_prep_kernel(a_ref, c_ref, bt_ref, m_ref, ws_ref, wsn_ref, whyt_ref, mk_ref, pw_ref, an_ref, *, n_tok):
    for p in range(PREP_PAIRS):
        one, two, lanes = pl.ds(p, 1), pl.ds(2 * p, 2), pl.ds(p * LANES, LANES)
        _s5_prep_pair(a_ref.at[one], c_ref.at[one], bt_ref.at[one], m_ref.at[two], ws_ref.at[two],
                      wsn_ref.at[two], whyt_ref.at[two], mk_ref.at[:, :, :, lanes], pw_ref.at[:, :, lanes],
                      an_ref.at[:, :, lanes], n_tok)


def _s5_prep_pair(a_ref, c_ref, bt_ref, m_ref, ws_ref, wsn_ref, whyt_ref, mk_ref, pw_ref, an_ref, n_tok):
    ar, ai = a_ref[0, 0], a_ref[0, 1]
    cr, ci = c_ref[0, 0], c_ref[0, 1]
    br, bi = bt_ref[0, 0], bt_ref[0, 1]
    pr, pi = [jnp.ones_like(ar)], [jnp.zeros_like(ai)]
    for _ in range(CHUNK):
        pr, pi = pr + [pr[-1] * ar - pi[-1] * ai], pi + [pr[-1] * ai + pi[-1] * ar]
    first = lax.broadcasted_iota(jnp.int32, (SSM_CH, LANES), 1) < SSM_STATE
    ca = [(cr * pr[t] - ci * pi[t], -(cr * pi[t] + ci * pr[t])) for t in range(CHUNK + 1)]
    ca_full = jnp.concatenate([jnp.concatenate(ca[t], axis=1) for t in range(CHUNK)], axis=0)
    zero = jnp.zeros((SSM_CH, LANES), F32)
    for e in range(2):
        own = first if e == 0 else ~first
        pick = lambda v: jnp.where(own, v, zero)
        for t in range(CHUNK):
            rows = slice(t * SSM_CH, (t + 1) * SSM_CH)
            whyt_ref[e, rows, :LANES] = pick(ca[t + 1][0]).astype(BF16)
            whyt_ref[e, rows, LANES:] = pick(ca[t + 1][1]).astype(BF16)
            for ref, k in ((ws_ref, CHUNK - 1 - t), (wsn_ref, n_tok - 1 - t)):
                if k >= 0:
                    ref[e, rows, :LANES] = pick(br * pr[k] - bi * pi[k]).astype(BF16)
                    ref[e, rows, LANES:] = pick(br * pi[k] + bi * pr[k]).astype(BF16)
                else:
                    ref[e, rows, :] = jnp.zeros((SSM_CH, PAIR_COLS), BF16)
        k_row = _dot_nt_split(jnp.concatenate([pick(br), pick(bi)], axis=1), ca_full)
        for s in range(CHUNK):
            m_ref[e, s * SSM_CH:(s + 1) * SSM_CH, :] = _shift_lanes(k_row, s * SSM_CH).astype(BF16)
    sub = lax.broadcasted_iota(jnp.int32, (SUBLANES, LANES), 0)
    qr, qi = [pr[CHUNK]], [pi[CHUNK]]
    for _ in range(SUBLANES - 1):
        qr, qi = qr + [qr[-1] * pr[CHUNK] - qi[-1] * pi[CHUNK]], qi + [qr[-1] * pi[CHUNK] + qi[-1] * pr[CHUNK]]
    for part, q in enumerate((qr, qi)):
        rows = jnp.zeros((SUBLANES, LANES), F32)
        for s in range(SUBLANES):
            rows = jnp.where(sub == s, q[s], rows)
        pw_ref[part] = rows
        for k, shift in enumerate((1, 2, 4)):
            mk_ref[k, part] = jnp.where(sub >= shift, q[shift - 1], 0.0)
    an_ref[0] = jnp.broadcast_to(pr[n_tok], (SUBLANES, LANES))
    an_ref[1] = jnp.broadcast_to(pi[n_tok], (SUBLANES, LANES))


def _s5_prep(lam_re, lam_im, log_dt, b_re, b_im, c_re, c_im, n_tok):
    ab_re, ab_im, bb_re, bb_im = _s5_discretize(lam_re, lam_im, log_dt, b_re, b_im)
    pair = lambda v: v.reshape(N_PAIRS, 2, v.shape[1], SSM_STATE).transpose(0, 2, 1, 3).reshape(
        N_PAIRS, v.shape[1], LANES)
    a = jnp.stack([pair(ab_re[:, None, :]), pair(ab_im[:, None, :])], axis=1)
    c = jnp.stack([pair(c_re), pair(c_im)], axis=1)
    bt = jnp.stack([pair(bb_re.transpose(0, 2, 1)), pair(bb_im.transpose(0, 2, 1))], axis=1)
    blk4 = lambda r: pl.BlockSpec((PREP_PAIRS, 2, r, LANES), lambda q: (q, 0, 0, 0))
    w_spec = pl.BlockSpec((2 * PREP_PAIRS, GROUP_IO, PAIR_COLS), lambda q: (q, 0, 0))
    w_shape = jax.ShapeDtypeStruct((SSM_GROUPS, GROUP_IO, PAIR_COLS), BF16)
    return pl.pallas_call(
        functools.partial(_s5_prep_kernel, n_tok=n_tok),
        grid=(N_PAIRS // PREP_PAIRS,),
        in_specs=[blk4(1), blk4(SSM_CH), blk4(SSM_CH)],
        out_specs=[w_spec, w_spec, w_spec, w_spec,
                   pl.BlockSpec((3, 2, SUBLANES, PREP_PAIRS * LANES), lambda q: (0, 0, 0, q)),
                   pl.BlockSpec((2, SUBLANES, PREP_PAIRS * LANES), lambda q: (0, 0, q)),
                   pl.BlockSpec((2, SUBLANES, PREP_PAIRS * LANES), lambda q: (0, 0, q))],
        out_shape=[w_shape, w_shape, w_shape, w_shape,
                   jax.ShapeDtypeStruct((3, 2, SUBLANES, HALF_COLS), F32),
                   jax.ShapeDtypeStruct((2, SUBLANES, HALF_COLS), F32),
                   jax.ShapeDtypeStruct((2, SUBLANES, HALF_COLS), F32)],
        compiler_params=_params("arbitrary"),
        name="s5_prep",
    )(a, c, bt)


def _group_io(ref, g):
    return ref.at[:, g * GROUP_IO:(g + 1) * GROUP_IO]


def _s5_emit_tile(u_ref, st_scr, m_ref, whyt_ref, dt_ref, g_ref, j):
    for gl in range(GROUPS_PER_TILE):
        g = GROUPS_PER_TILE * j + gl
        q0 = (gl // 2) * PAIR_COLS
        u_g = _group_io(u_ref, g)[...]
        y = (jnp.dot(u_g, m_ref[g], preferred_element_type=F32)
             + lax.dot_general(st_scr[:, q0:q0 + PAIR_COLS].astype(BF16), whyt_ref[g],
                               (((1,), (1,)), ((), ())), preferred_element_type=F32)
             + dt_ref[g] * u_g.astype(F32))
        _group_io(g_ref, g)[...] = jax.nn.gelu(y).astype(BF16)


def _s5_local_states(u_ref, ws_ref, j, q):
    g = GROUPS_PER_TILE * j + 2 * q
    return (jnp.dot(_group_io(u_ref, g)[...], ws_ref[g], preferred_element_type=F32)
            + jnp.dot(_group_io(u_ref, g + 1)[...], ws_ref[g + 1], preferred_element_type=F32))


def _cmul_add(ar, ai, hr, hi, xr, xi):
    return ar * hr - ai * hi + xr, ar * hi + ai * hr + xi


def _s5_chain_kernel(u_ref, m_ref, ws_ref, whyt_ref, mk_ref, pw_ref, dt_ref, h0_ref, g_ref, hT_ref,
                     carry_scr, st_scr, *, rows):
    @pl.when(pl.program_id(1) == 0)
    def _():
        carry_scr[...] = jnp.broadcast_to(h0_ref[0], (SUBLANES, STATE_COLS))

    row0 = lax.broadcasted_iota(jnp.int32, (SUBLANES, LANES), 0) == 0
    last = lambda h: jnp.broadcast_to(h[SUBLANES - 1:, :], (SUBLANES, LANES))

    for j in range(N_LANE_TILES):
        for q in range(PAIRS_PER_TILE):
            st_scr[:, q * PAIR_COLS:(q + 1) * PAIR_COLS] = _s5_local_states(u_ref, ws_ref, j, q)

        def step(r, carry):
            r0 = pl.multiple_of(r * SUBLANES, SUBLANES)
            out = []
            for q in range(PAIRS_PER_TILE):
                re_c = slice(q * PAIR_COLS, q * PAIR_COLS + LANES)
                im_c = slice(q * PAIR_COLS + LANES, (q + 1) * PAIR_COLS)
                hc = slice((j * PAIRS_PER_TILE + q) * LANES, (j * PAIRS_PER_TILE + q + 1) * LANES)
                xr = st_scr[pl.ds(r0, SUBLANES), re_c]
                xi = st_scr[pl.ds(r0, SUBLANES), im_c]
                for k, shift in enumerate((1, 2, 4)):
                    xr, xi = _cmul_add(mk_ref[k, 0, :, hc], mk_ref[k, 1, :, hc],
                                       pltpu.roll(xr, shift, 0), pltpu.roll(xi, shift, 0), xr, xi)
                cr, ci = carry[q]
                hr, hi = _cmul_add(pw_ref[0, :, hc], pw_ref[1, :, hc], cr, ci, xr, xi)
                st_scr[pl.ds(r0, SUBLANES), re_c] = jnp.where(row0, cr, pltpu.roll(hr, 1, 0))
                st_scr[pl.ds(r0, SUBLANES), im_c] = jnp.where(row0, ci, pltpu.roll(hi, 1, 0))
                out.append((last(hr), last(hi)))
            return tuple(out)

        c0 = j * TILE_COLS
        init = tuple((carry_scr[:, c0 + q * PAIR_COLS:c0 + q * PAIR_COLS + LANES],
                      carry_scr[:, c0 + q * PAIR_COLS + LANES:c0 + (q + 1) * PAIR_COLS])
                     for q in range(PAIRS_PER_TILE))
        final = lax.fori_loop(0, rows // SUBLANES, step, init, unroll=True)
        for q in range(PAIRS_PER_TILE):
            carry_scr[:, c0 + q * PAIR_COLS:c0 + q * PAIR_COLS + LANES] = final[q][0]
            carry_scr[:, c0 + q * PAIR_COLS + LANES:c0 + (q + 1) * PAIR_COLS] = final[q][1]

        _s5_emit_tile(u_ref, st_scr, m_ref, whyt_ref, dt_ref, g_ref, j)

    hT_ref[0] = carry_scr[...]


def _s5_chain(u16, m, ws, why, mk, pw, d_tiled, h0, batch, n_rows, rows):
    nblk = n_rows // rows
    row = pl.BlockSpec((rows, CHUNK_LANES), lambda b, i: (b * nblk + i, 0))
    return pl.pallas_call(
        functools.partial(_s5_chain_kernel, rows=rows),
        grid=(batch, nblk),
        in_specs=[row, _const_spec(m.shape), _const_spec(ws.shape), _const_spec(why.shape),
                  _const_spec(mk.shape), _const_spec(pw.shape), _const_spec(d_tiled.shape),
                  pl.BlockSpec((1, 1, STATE_COLS), lambda b, i: (b, 0, 0))],
        out_specs=[row, pl.BlockSpec((1, SUBLANES, STATE_COLS), lambda b, i: (b, 0, 0))],
        out_shape=[jax.ShapeDtypeStruct(u16.shape, BF16),
                   jax.ShapeDtypeStruct((batch, SUBLANES, STATE_COLS), F32)],
        scratch_shapes=[pltpu.VMEM((SUBLANES, STATE_COLS), F32), pltpu.VMEM((rows, TILE_COLS), F32)],
        compiler_params=_params("arbitrary", "arbitrary"),
        name="s5_chain",
    )(u16, m, ws, why, mk, pw, d_tiled, h0)


def _s5_rows_kernel(u_ref, m_ref, ws_ref, whyt_ref, an_ref, dt_ref, h0r_ref, h0i_ref, g_ref, hTr_ref, hTi_ref,
                    st_scr):
    for j in range(N_LANE_TILES):
        for q in range(PAIRS_PER_TILE):
            hc = slice((j * PAIRS_PER_TILE + q) * LANES, (j * PAIRS_PER_TILE + q + 1) * LANES)
            h0r, h0i = h0r_ref[hc, :].T, h0i_ref[hc, :].T
            local = _s5_local_states(u_ref, ws_ref, j, q)
            hr, hi = _cmul_add(an_ref[0, :1, hc], an_ref[1, :1, hc], h0r, h0i, local[:, :LANES], local[:, LANES:])
            hTr_ref[hc, :] = hr.T
            hTi_ref[hc, :] = hi.T
            st_scr[:, q * PAIR_COLS:q * PAIR_COLS + LANES] = h0r
            st_scr[:, q * PAIR_COLS + LANES:(q + 1) * PAIR_COLS] = h0i
        _s5_emit_tile(u_ref, st_scr, m_ref, whyt_ref, dt_ref, g_ref, j)


def _s5_rows(u16, m, ws, why, an, d_tiled, h0_re_t, h0_im_t, rows):
    n = u16.shape[0]
    once = pl.Buffered(1)
    row = pl.BlockSpec((rows, CHUNK_LANES), lambda i: (i, 0), pipeline_mode=once)
    col = pl.BlockSpec((HALF_COLS, rows), lambda i: (0, i), pipeline_mode=once)
    state = jax.ShapeDtypeStruct((HALF_COLS, n), F32)
    return pl.pallas_call(
        _s5_rows_kernel,
        grid=(n // rows,),
        in_specs=[row, _const_spec(m.shape), _const_spec(ws.shape), _const_spec(why.shape),
                  _const_spec(an.shape), _const_spec(d_tiled.shape), col, col],
        out_specs=[row, col, col],
        out_shape=[jax.ShapeDtypeStruct(u16.shape, BF16), state, state],
        scratch_shapes=[pltpu.VMEM((rows, TILE_COLS), F32)],
        compiler_params=_params("arbitrary"),
        name="s5_rows",
    )(u16, m, ws, why, an, d_tiled, h0_re_t, h0_im_t)


def _post_kernel(g_ref, ma_ref, gs_ref, x_ref, permt_ref, wglu_ref, wout_ref, gffn_ref, wup_ref, wdown_ref,
                 gfin_ref, out_ref, *, n_tok):
    g = _load_chunk_rows(g_ref, permt_ref, n_tok)
    glu = jnp.dot(g, wglu_ref[...], preferred_element_type=F32)
    s_out = glu[:, :D_MODEL] * jax.nn.sigmoid(glu[:, D_MODEL:])
    merged = ma_ref[...] + gs_ref[...].astype(F32) * s_out
    x1 = x_ref[...] + jnp.dot(merged.astype(BF16), wout_ref[...], preferred_element_type=F32)
    h2 = _rmsnorm(x1, gffn_ref[...]).astype(BF16)
    x2 = x1
    for c in range(0, D_FF, FF_CHUNK):
        up = jnp.dot(h2, wup_ref[:, c:c + FF_CHUNK], preferred_element_type=F32)
        act = jnp.square(jnp.maximum(up, 0.0)).astype(BF16)
        x2 = x2 + jnp.dot(act, wdown_ref[c:c + FF_CHUNK, :], preferred_element_type=F32)
    out_ref[...] = _rmsnorm(x2, gfin_ref[...])


def _post(g, ma, gs, x, permt, w_glu, w_out, g_ffn, w_up, w_down, g_final, tm, n_tok):
    n = x.shape[0]
    row = pl.BlockSpec((tm, D_MODEL), lambda i: (i, 0))
    g_spec = pl.BlockSpec((tm // n_tok, CHUNK_LANES), lambda i: (i, 0))
    return pl.pallas_call(
        functools.partial(_post_kernel, n_tok=n_tok),
        grid=(n // tm,),
        in_specs=[g_spec, row, row, row, _const_spec(permt.shape), _const_spec(w_glu.shape),
                  _const_spec(w_out.shape), _const_spec((1, D_MODEL)), _const_spec(w_up.shape),
                  _const_spec(w_down.shape), _const_spec((1, D_MODEL))],
        out_specs=row,
        out_shape=jax.ShapeDtypeStruct((n, D_MODEL), F32),
        compiler_params=_params("arbitrary"),
        name="post",
    )(g, ma, gs, x, permt, w_glu, w_out, g_ffn, w_up, w_down, g_final)


def _tile(n, pref):
    t = pref
    while n % t:
        t //= 2
    return t


def kernel(x_prompt, x_sample, cache_k, cache_v, state_ssm_re, state_ssm_im, g_mix, w_in, attn_sinks,
           w_attn_o, ssm_lambda_re, ssm_lambda_im, ssm_log_dt, ssm_b_re, ssm_b_im, ssm_c_re, ssm_c_im,
           ssm_d, w_glu, w_out, g_ffn, w_up, w_down, g_final):
    batch, seq, _ = x_prompt.shape
    db, dec_seq, _ = x_sample.shape
    assert w_in.shape[0] == 1, "one layer"
    assert seq % PERM_ROWS == 0 and dec_seq in (1, 2, 4, 8) and (db * dec_seq) % PERM_ROWS == 0

    vec = lambda v: v.reshape(1, D_MODEL).astype(F32)
    w_in_b, wo_b = w_in[0].astype(BF16), w_attn_o[0].astype(BF16)
    sinks = attn_sinks[0].astype(F32)
    s5p = (ssm_lambda_re[0], ssm_lambda_im[0], ssm_log_dt[0], ssm_b_re[0], ssm_b_im[0], ssm_c_re[0], ssm_c_im[0])
    d_skip = jnp.tile(ssm_d[0].astype(F32).reshape(SSM_GROUPS, 1, SSM_CH), (1, 1, CHUNK))
    perm = _chunk_perm(CHUNK)

    xp = x_prompt.reshape(batch * seq, D_MODEL)
    u16, gs, ma, kv_last, w_glu_b, w_out_b, w_up_b, w_down_b = _mix_in(
        sinks, xp, vec(g_mix[0]), w_in_b, perm, wo_b, batch, seq, _tile(seq, 512),
        cast=(w_glu[0], w_out[0], w_up[0], w_down[0]))
    post_w = (w_glu_b, w_out_b, vec(g_ffn[0]), w_up_b, w_down_b, vec(g_final))
    m, ws, ws_n, whyt, mk, pw, an = _s5_prep(*s5p, dec_seq)
    n_rows = seq // CHUNK
    g16, hT = _s5_chain(u16, m, ws, whyt, mk, pw, d_skip, jnp.zeros((batch, 1, STATE_COLS), F32),
                        batch, n_rows, _tile(n_rows, 128))
    y_prompt = _post(g16, ma, gs, xp, perm.T, *post_w, _tile(batch * seq, 512), CHUNK).reshape(batch, seq, D_MODEL)
    kv_last = kv_last.reshape(batch, WINDOW, 2 * D_KV)
    k_prompt = kv_last[..., :D_KV].reshape(1, batch, WINDOW, N_KV_HEADS, HEAD_DIM)
    v_prompt = kv_last[..., D_KV:].reshape(1, batch, WINDOW, N_KV_HEADS, HEAD_DIM)
    hr, hi = _state_split(hT[:, 0])
    ssm_re_prompt, ssm_im_prompt = hr[None], hi[None]

    ns = db * dec_seq
    xs = x_sample.reshape(ns, D_MODEL)
    perm_s = _chunk_perm(dec_seq)
    q, kv, u16, ga, gs = _proj(xs, vec(g_mix[0]), w_in_b, perm_s, _tile(ns, 512), dec_seq)
    tpad = SUBLANES
    pad3 = lambda v: jnp.pad(v.reshape(db, dec_seq, -1).astype(F32), ((0, 0), (0, tpad - dec_seq), (0, 0)))
    ga_pad = pad3(ga).astype(BF16).reshape(db * tpad, D_MODEL)
    ma_pad, k_new, v_new = _attn_sample(
        sinks, pad3(q), pad3(kv), cache_k[0].reshape(db, WINDOW, D_KV), cache_v[0].reshape(db, WINDOW, D_KV),
        ga_pad, wo_b, _tile(db, 16), dec_seq)
    ma = ma_pad.reshape(db, tpad, D_MODEL)[:, :dec_seq].reshape(ns, D_MODEL)
    to_cols = lambda s: s[0].astype(F32).transpose(1, 2, 0).reshape(HALF_COLS, db)
    from_cols = lambda h: h.reshape(SSM_GROUPS, SSM_STATE, db).transpose(2, 0, 1)[None]
    g16, hT_re, hT_im = _s5_rows(u16, m, ws_n, whyt, an, d_skip, to_cols(state_ssm_re), to_cols(state_ssm_im),
                                 _tile(db, LANES))
    y_sample = _post(g16, ma, gs, xs, perm_s.T, *post_w, PERM_ROWS, dec_seq).reshape(db, dec_seq, D_MODEL)
    k_sample = k_new.reshape(1, db, WINDOW, N_KV_HEADS, HEAD_DIM)
    v_sample = v_new.reshape(1, db, WINDOW, N_KV_HEADS, HEAD_DIM)
    ssm_re_sample, ssm_im_sample = from_cols(hT_re), from_cols(hT_im)

    return (y_prompt, y_sample, k_prompt, v_prompt, ssm_re_prompt, ssm_im_prompt,
            k_sample, v_sample, ssm_re_sample, ssm_im_sample)
```

```python
import functools
import math

import jax
import jax.numpy as jnp
from jax import lax
from jax.experimental import pallas as pl
from jax.experimental.pallas import tpu as pltpu

F32 = jnp.float32
BF16 = jnp.bfloat16

D_MODEL = 1024
HEAD_DIM = 64
N_HEADS = 16
N_KV_HEADS = 4
WINDOW = 128
D_KV = N_KV_HEADS * HEAD_DIM
SSM_CH = 16
SSM_GROUPS = 64
SSM_STATE = 64
D_FF = 4 * D_MODEL
FF_CHUNK = 1024
RMS_EPS = 1e-5
LOG2E = 1.4426950408889634
Q_SCALE = HEAD_DIM ** -0.5 * LOG2E

LANES = 128
SUBLANES = 8
BF16_ROWS = 16
N_LANE_TILES = D_MODEL // LANES
GROUPS_PER_TILE = LANES // SSM_CH
CHUNK = 16
CHUNK_LANES = CHUNK * D_MODEL
GROUP_IO = CHUNK * SSM_CH
N_PAIRS = SSM_GROUPS // 2
PAIRS_PER_TILE = GROUPS_PER_TILE // 2
PAIR_COLS = 2 * LANES
TILE_COLS = PAIRS_PER_TILE * PAIR_COLS
STATE_COLS = N_PAIRS * PAIR_COLS
HALF_COLS = STATE_COLS // 2
PERM_ROWS = CHUNK * BF16_ROWS
PREP_PAIRS = 4
VMEM_LIMIT = 56 * 1024 * 1024

_Q0, _KV0, _U0, _GA0, _GS0, _END = 0, 1024, 1536, 2560, 3584, 4608


def _gelu_tanh(x):
    k1 = -2.0 * math.sqrt(2.0 / math.pi) * LOG2E
    return x / (1.0 + jnp.exp2(x * (k1 + (0.044715 * k1) * (x * x))))


def _rmsnorm(x, g):
    return x * lax.rsqrt(jnp.mean(x * x, axis=-1, keepdims=True) + RMS_EPS) * g


def _params(*sem):
    return pltpu.CompilerParams(dimension_semantics=sem, vmem_limit_bytes=VMEM_LIMIT)


def _const_spec(shape):
    nd = len(shape)
    return pl.BlockSpec(shape, lambda *_: (0,) * nd, pipeline_mode=pl.Buffered(1))


def _chunk_perm(n_tok):
    n_chunks = PERM_ROWS // n_tok
    r = jnp.arange(PERM_ROWS)
    tok = (r % n_chunks) * n_tok + r // n_chunks
    return (tok[:, None] == jnp.arange(PERM_ROWS)[None, :]).astype(BF16)


def _piece_transpose(cols, masks):
    for d, msk in zip((4, 2, 1), masks):
        new = list(cols)
        for v in range(GROUPS_PER_TILE):
            if v & d == 0:
                a, b = cols[v], cols[v + d]
                new[v] = jnp.where(msk, pltpu.roll(b, SSM_CH * d, 1), a)
                new[v + d] = jnp.where(msk, b, pltpu.roll(a, LANES - SSM_CH * d, 1))
        cols = new
    return cols


def _piece_masks(rows):
    piece = lax.broadcasted_iota(jnp.int32, (rows, LANES), 1) // SSM_CH
    return [(piece & d) != 0 for d in (4, 2, 1)]


def _store_chunk_rows(x, perm_ref, out_ref, n_tok):
    n_chunks = PERM_ROWS // n_tok
    masks = _piece_masks(n_chunks)
    zero = jnp.zeros((n_chunks, LANES), F32)
    for hb in range(x.shape[0] // PERM_ROWS):
        rows = slice(hb * n_chunks, (hb + 1) * n_chunks)
        xp = jnp.dot(perm_ref[...], x[hb * PERM_ROWS:(hb + 1) * PERM_ROWS],
                     preferred_element_type=F32)
        for j in range(N_LANE_TILES):
            for hf in range(2):
                live = 8 * hf < n_tok
                cols = [xp[t * n_chunks:(t + 1) * n_chunks, j * LANES:(j + 1) * LANES] if t < n_tok else zero
                        for t in range(8 * hf, 8 * hf + 8)]
                for gl, col in enumerate(_piece_transpose(cols, masks) if live else cols):
                    c0 = (GROUPS_PER_TILE * j + gl) * GROUP_IO + hf * LANES
                    out_ref[rows, c0:c0 + LANES] = col.astype(BF16)


def _load_chunk_rows(in_ref, permt_ref, n_tok):
    n_chunks = PERM_ROWS // n_tok
    masks = _piece_masks(n_chunks)
    blocks = []
    for hb in range(in_ref.shape[0] // n_chunks):
        rows = slice(hb * n_chunks, (hb + 1) * n_chunks)
        tiles = [[None] * N_LANE_TILES for _ in range(n_tok)]
        for j in range(N_LANE_TILES):
            for hf in range(-(-n_tok // 8)):
                cols = []
                for gl in range(GROUPS_PER_TILE):
                    c0 = (GROUPS_PER_TILE * j + gl) * GROUP_IO + hf * LANES
                    cols.append(in_ref[rows, c0:c0 + LANES].astype(F32))
                for k, col in enumerate(_piece_transpose(cols, masks)):
                    if 8 * hf + k < n_tok:
                        tiles[8 * hf + k][j] = col.astype(BF16)
        xp = jnp.concatenate([jnp.concatenate(row, axis=1) for row in tiles], axis=0)
        blocks.append(jnp.dot(permt_ref[...], xp, preferred_element_type=F32).astype(BF16))
    return blocks[0] if len(blocks) == 1 else jnp.concatenate(blocks, axis=0)


def _proj_kernel(x_ref, g_ref, w_ref, perm_ref, *refs, n_tok, n_cast):
    q_ref, kv_ref, u_ref, ga_ref, gs_ref = refs[n_cast:n_cast + 5]
    h = _rmsnorm(x_ref[...], g_ref[...]).astype(BF16)

    def seg(lo, hi):
        return jnp.dot(h, w_ref[:, lo:hi], preferred_element_type=F32)

    q_ref[...] = (seg(_Q0, _KV0) * Q_SCALE).astype(BF16)
    kv_ref[...] = seg(_KV0, _U0)
    _store_chunk_rows(seg(_U0, _GA0).astype(BF16), perm_ref, u_ref, n_tok)
    ga_ref[...] = jax.nn.sigmoid(seg(_GA0, _GS0)).astype(BF16)
    gs_ref[...] = jax.nn.sigmoid(seg(_GS0, _END)).astype(BF16)
    for src, dst in zip(refs[:n_cast], refs[n_cast + 5:]):
        dst[...] = src[...].astype(BF16)


def _proj(x, g_mix, w_in, perm, tm, n_tok, cast=()):
    n = x.shape[0]
    steps = n // tm
    row = lambda w: pl.BlockSpec((tm, w), lambda i: (i, 0))
    slab = lambda w: pl.BlockSpec((w.shape[0] // steps, w.shape[1]), lambda i: (i, 0))
    assert all(w.shape[0] % (steps * BF16_ROWS) == 0 for w in cast)
    return pl.pallas_call(
        functools.partial(_proj_kernel, n_tok=n_tok, n_cast=len(cast)),
        grid=(steps,),
        in_specs=[row(D_MODEL), _const_spec((1, D_MODEL)), _const_spec(w_in.shape), _const_spec(perm.shape)]
        + [slab(w) for w in cast],
        out_specs=[row(D_MODEL), row(2 * D_KV), pl.BlockSpec((tm // n_tok, CHUNK_LANES), lambda i: (i, 0)),
                   row(D_MODEL), row(D_MODEL)] + [slab(w) for w in cast],
        out_shape=[jax.ShapeDtypeStruct((n, D_MODEL), BF16),
                   jax.ShapeDtypeStruct((n, 2 * D_KV), F32),
                   jax.ShapeDtypeStruct((n // n_tok, CHUNK_LANES), BF16),
                   jax.ShapeDtypeStruct((n, D_MODEL), BF16),
                   jax.ShapeDtypeStruct((n, D_MODEL), BF16)]
        + [jax.ShapeDtypeStruct(w.shape, BF16) for w in cast],
        compiler_params=_params("arbitrary"),
        name="proj",
    )(x, g_mix, w_in, perm, *cast)


def _dup_heads(tile):
    lo = lax.broadcasted_iota(jnp.int32, tile.shape, tile.ndim - 1) < HEAD_DIM
    rolled = pltpu.roll(tile, HEAD_DIM, tile.ndim - 1)
    return (jnp.where(lo, tile, rolled).astype(BF16), jnp.where(lo, rolled, tile).astype(BF16))


def _sink_softmax(s, sink):
    sink2 = sink * LOG2E
    m = jnp.maximum(jnp.max(s, axis=-1, keepdims=True), sink2)
    p = jnp.exp2(s - m)
    denom = jnp.sum(p, axis=-1, keepdims=True) + jnp.exp2(sink2 - m)
    return p.astype(BF16), 1.0 / denom


def _attn_prompt_kernel(sinks_ref, q_ref, kvc_ref, kvp_ref, ga_ref, wo_ref, out_ref,
                        kd_scr, vd_scr, bias_scr, attn_scr, *, tq):
    i = pl.program_id(1)
    kv_full = jnp.concatenate([kvp_ref[...], kvc_ref[...]], axis=0)
    for t in range(2):
        ke, ko = _dup_heads(kv_full[:, t * LANES:(t + 1) * LANES])
        ve, vo = _dup_heads(kv_full[:, D_KV + t * LANES:D_KV + (t + 1) * LANES])
        kd_scr[2 * t], kd_scr[2 * t + 1] = ke, ko
        vd_scr[2 * t], vd_scr[2 * t + 1] = ve, vo

    qi = lax.broadcasted_iota(jnp.int32, (WINDOW, 2 * WINDOW), 0)
    si = lax.broadcasted_iota(jnp.int32, (WINDOW, 2 * WINDOW), 1)
    band = (si > qi) & (si <= qi + WINDOW)
    in_block = si >= WINDOW
    lo = lax.broadcasted_iota(jnp.int32, (WINDOW, LANES), 1) < HEAD_DIM

    def attend(jb):
        q0 = jb * WINDOW
        has_prev = (i > 0) | (jb > 0)
        bias_scr[...] = jnp.where(band & (in_block | has_prev), 0.0, -jnp.inf)

        def head_pair(hp, c):
            j = hp // 2
            c0 = pl.multiple_of(hp * LANES, LANES)
            kd = kd_scr[j, pl.ds(q0, 2 * WINDOW), :]
            vd = vd_scr[j, pl.ds(q0, 2 * WINDOW), :]
            qp = q_ref[pl.ds(q0, WINDOW), pl.ds(c0, LANES)]
            outs = []
            for par in range(2):
                qm = jnp.where(lo if par == 0 else ~lo, qp, jnp.zeros_like(qp))
                s = lax.dot_general(qm, kd, (((1,), (1,)), ((), ())), preferred_element_type=F32)
                p, inv = _sink_softmax(s + bias_scr[...], sinks_ref[2 * hp + par])
                outs.append(jnp.dot(p, vd, preferred_element_type=F32) * inv)
            attn_scr[pl.ds(q0, WINDOW), pl.ds(c0, LANES)] = jnp.where(lo, outs[0], outs[1]).astype(BF16)
            return c

        lax.fori_loop(0, N_HEADS // 2, head_pair, 0, unroll=8)

    def project(jb):
        rows = pl.ds(jb * WINDOW, WINDOW)
        a_out = jnp.dot(attn_scr[rows, :], wo_ref[...], preferred_element_type=F32)
        out_ref[rows, :] = ga_ref[rows, :].astype(F32) * a_out

    for jb in range(tq // WINDOW):
        attend(jb)
        project(jb)


def _attn_prompt(sinks, q, kv, ga, wo, batch, seq, tq):
    nq = seq // tq
    bpt = tq // WINDOW
    row = lambda w: pl.BlockSpec((tq, w), lambda b, i: (b * nq + i, 0))
    prev = pl.BlockSpec((WINDOW, 2 * D_KV),
                        lambda b, i: (jnp.maximum((b * nq + i) * bpt - 1, 0), 0))
    return pl.pallas_call(
        functools.partial(_attn_prompt_kernel, tq=tq),
        grid=(batch, nq),
        in_specs=[pl.BlockSpec(memory_space=pltpu.SMEM), row(D_MODEL), row(2 * D_KV), prev,
                  row(D_MODEL), _const_spec(wo.shape)],
        out_specs=row(D_MODEL),
        out_shape=jax.ShapeDtypeStruct((batch * seq, D_MODEL), F32),
        scratch_shapes=[pltpu.VMEM((N_KV_HEADS, WINDOW + tq, LANES), BF16),
                        pltpu.VMEM((N_KV_HEADS, WINDOW + tq, LANES), BF16),
                        pltpu.VMEM((WINDOW, 2 * WINDOW), F32),
                        pltpu.VMEM((tq, D_MODEL), BF16)],
        compiler_params=_params("arbitrary", "arbitrary"),
        name="attn_prompt",
    )(sinks, q, kv, kv, ga, wo)


def _attn_sample_kernel(sinks_ref, q_ref, kvn_ref, ck_ref, cv_ref, ga_ref, wo_ref,
                        out_ref, ko_ref, vo_ref, attn_scr, *, bb, tpad, dec_seq):
    nk = WINDOW + tpad
    lo3 = lax.broadcasted_iota(jnp.int32, (bb, tpad, LANES), 2) < HEAD_DIM
    row = lax.broadcasted_iota(jnp.int32, (4 * tpad, nk), 0)
    si = lax.broadcasted_iota(jnp.int32, (4 * tpad, nk), 1)
    tq = row % tpad
    valid = (si > tq) & (si <= tq + WINDOW) & (si < WINDOW + dec_seq)
    hrow = lax.broadcasted_iota(jnp.int32, (4 * tpad, 1), 0) // tpad

    for t in range(2):
        sl = slice(t * LANES, (t + 1) * LANES)
        kk = jnp.concatenate([ck_ref[:, :, sl], kvn_ref[:, :, sl]], axis=1)
        vv = jnp.concatenate([cv_ref[:, :, sl],
                              kvn_ref[:, :, D_KV + t * LANES:D_KV + (t + 1) * LANES]], axis=1)
        ko_ref[:, :, sl] = kk[:, dec_seq:dec_seq + WINDOW, :]
        vo_ref[:, :, sl] = vv[:, dec_seq:dec_seq + WINDOW, :]
        kds = _dup_heads(kk)
        vds = _dup_heads(vv)
        for par_kv in range(2):
            j = 2 * t + par_kv
            kd, vd = kds[par_kv], vds[par_kv]
            parts = []
            for r in range(2):
                c0 = j * 2 * LANES + r * LANES
                qp = q_ref[:, :, c0:c0 + LANES]
                parts += [jnp.where(lo3, qp, 0.0), jnp.where(lo3, 0.0, qp)]
            lhs = jnp.concatenate(parts, axis=1).astype(BF16)
            s = jnp.einsum('bqd,bkd->bqk', lhs, kd, preferred_element_type=F32)
            sink = jnp.zeros((4 * tpad, 1), F32)
            for g in range(4):
                sink = jnp.where(hrow == g, sinks_ref[4 * j + g], sink)
            p, inv = _sink_softmax(jnp.where(valid[None], s, -jnp.inf), sink[None])
            o = jnp.einsum('bqk,bkd->bqd', p, vd, preferred_element_type=F32) * inv
            for r in range(2):
                c0 = j * 2 * LANES + r * LANES
                o_even = o[:, (2 * r) * tpad:(2 * r + 1) * tpad, :]
                o_odd = o[:, (2 * r + 1) * tpad:(2 * r + 2) * tpad, :]
                attn_scr[:, :, c0:c0 + LANES] = jnp.where(lo3, o_even, o_odd)

    attn = attn_scr[...].reshape(bb * tpad, D_MODEL).astype(BF16)
    a_out = jnp.dot(attn, wo_ref[...], preferred_element_type=F32)
    out_ref[...] = ga_ref[...].astype(F32) * a_out


def _attn_sample(sinks, q3, kvn3, ck, cv, ga2, wo, bb, dec_seq):
    db, tpad, _ = q3.shape
    blk3 = lambda r, w: pl.BlockSpec((bb, r, w), lambda i: (i, 0, 0))
    row = pl.BlockSpec((bb * tpad, D_MODEL), lambda i: (i, 0))
    return pl.pallas_call(
        functools.partial(_attn_sample_kernel, bb=bb, tpad=tpad, dec_seq=dec_seq),
        grid=(db // bb,),
        in_specs=[pl.BlockSpec(memory_space=pltpu.SMEM), blk3(tpad, D_MODEL), blk3(tpad, 2 * D_KV),
                  blk3(WINDOW, D_KV), blk3(WINDOW, D_KV), row, _const_spec(wo.shape)],
        out_specs=[row, blk3(WINDOW, D_KV), blk3(WINDOW, D_KV)],
        out_shape=[jax.ShapeDtypeStruct((db * tpad, D_MODEL), F32),
                   jax.ShapeDtypeStruct((db, WINDOW, D_KV), F32),
                   jax.ShapeDtypeStruct((db, WINDOW, D_KV), F32)],
        scratch_shapes=[pltpu.VMEM((bb, tpad, D_MODEL), F32)],
        compiler_params=_params("arbitrary"),
        name="attn_sample",
    )(sinks, q3, kvn3, ck, cv, ga2, wo)


def _s5_discretize(lam_re, lam_im, log_dt, b_re, b_im):
    dt = jnp.exp(log_dt)[:, None]
    decay = jnp.exp(lam_re * dt)
    ab_re = decay * jnp.cos(lam_im * dt)
    ab_im = decay * jnp.sin(lam_im * dt)
    nr, ni = ab_re - 1.0, ab_im
    den = lam_re * lam_re + lam_im * lam_im
    f_re = ((nr * lam_re + ni * lam_im) / den)[..., None]
    f_im = ((ni * lam_re - nr * lam_im) / den)[..., None]
    return ab_re, ab_im, f_re * b_re - f_im * b_im, f_re * b_im + f_im * b_re


def _state_split(h):
    h = h.reshape(h.shape[:-1] + (N_PAIRS, 2, 2, SSM_STATE))
    unpair = lambda a: a.reshape(a.shape[:-3] + (SSM_GROUPS, SSM_STATE))
    return unpair(h[..., 0, :, :]), unpair(h[..., 1, :, :])


def _shift_lanes(x, n):
    a, b = x[:, :LANES], x[:, LANES:]
    lane = lax.broadcasted_iota(jnp.int32, a.shape, 1)
    if n == 0:
        return x
    if n >= LANES:
        r = n - LANES
        hi = a if r == 0 else jnp.where(lane < r, 0.0, pltpu.roll(a, r, 1))
        return jnp.concatenate([jnp.zeros_like(a), hi], axis=1)
    ra, rb = pltpu.roll(a, n, 1), pltpu.roll(b, n, 1)
    return jnp.concatenate([jnp.where(lane < n, 0.0, ra), jnp.where(lane < n, ra, rb)], axis=1)


def _dot_nt_split(lhs, rhs):
    nt = lambda a, b: lax.dot_general(a, b, (((1,), (1,)), ((), ())), preferred_element_type=F32)
    l_hi, r_hi = lhs.astype(BF16), rhs.astype(BF16)
    l_lo = (lhs - l_hi.astype(F32)).astype(BF16)
    r_lo = (rhs - r_hi.astype(F32)).astype(BF16)
    return nt(l_hi, r_hi) + nt(l_hi, r_lo) + nt(l_lo, r_hi)


def _s5_prep_kernel(a_ref, c_ref, bt_ref, m_ref, ws_ref, wsn_ref, whyt_ref, mk_ref, pw_ref, an_ref, *, n_tok):
    for p in range(PREP_PAIRS):
        one, two, lanes = pl.ds(p, 1), pl.ds(2 * p, 2), pl.ds(p * LANES, LANES)
        _s5_prep_pair(a_ref.at[one], c_ref.at[one], bt_ref.at[one], m_ref.at[two], ws_ref.at[two],
                      wsn_ref.at[two], whyt_ref.at[two], mk_ref.at[:, :, :, lanes], pw_ref.at[:, :, lanes],
                      an_ref.at[:, :, lanes], n_tok)


def _s5_prep_pair(a_ref, c_ref, bt_ref, m_ref, ws_ref, wsn_ref, whyt_ref, mk_ref, pw_ref, an_ref, n_tok):
    ar, ai = a_ref[0, 0], a_ref[0, 1]
    cr, ci = c_ref[0, 0], c_ref[0, 1]
    br, bi = bt_ref[0, 0], bt_ref[0, 1]
    pr, pi = [jnp.ones_like(ar)], [jnp.zeros_like(ai)]
    for _ in range(CHUNK):
        pr, pi = pr + [pr[-1] * ar - pi[-1] * ai], pi + [pr[-1] * ai + pi[-1] * ar]
    first = lax.broadcasted_iota(jnp.int32, (SSM_CH, LANES), 1) < SSM_STATE
    ca = [(cr * pr[t] - ci * pi[t], -(cr * pi[t] + ci * pr[t])) for t in range(CHUNK + 1)]
    ca_full = jnp.concatenate([jnp.concatenate(ca[t], axis=1) for t in range(CHUNK)], axis=0)
    zero = jnp.zeros((SSM_CH, LANES), F32)
    for e in range(2):
        own = first if e == 0 else ~first
        pick = lambda v: jnp.where(own, v, zero)
        for t in range(CHUNK):
            rows = slice(t * SSM_CH, (t + 1) * SSM_CH)
            whyt_ref[e, rows, :LANES] = pick(ca[t + 1][0]).astype(BF16)
            whyt_ref[e, rows, LANES:] = pick(ca[t + 1][1]).astype(BF16)
            for ref, k in ((ws_ref, CHUNK - 1 - t), (wsn_ref, n_tok - 1 - t)):
                if k >= 0:
                    ref[e, rows, :LANES] = pick(br * pr[k] - bi * pi[k]).astype(BF16)
                    ref[e, rows, LANES:] = pick(br * pi[k] + bi * pr[k]).astype(BF16)
                else:
                    ref[e, rows, :] = jnp.zeros((SSM_CH, PAIR_COLS), BF16)
        k_row = _dot_nt_split(jnp.concatenate([pick(br), pick(bi)], axis=1), ca_full)
        for s in range(CHUNK):
            m_ref[e, s * SSM_CH:(s + 1) * SSM_CH, :] = _shift_lanes(k_row, s * SSM_CH).astype(BF16)
    sub = lax.broadcasted_iota(jnp.int32, (SUBLANES, LANES), 0)
    qr, qi = [pr[CHUNK]], [pi[CHUNK]]
    for _ in range(SUBLANES - 1):
        qr, qi = qr + [qr[-1] * pr[CHUNK] - qi[-1] * pi[CHUNK]], qi + [qr[-1] * pi[CHUNK] + qi[-1] * pr[CHUNK]]
    for part, q in enumerate((qr, qi)):
        rows = jnp.zeros((SUBLANES, LANES), F32)
        for s in range(SUBLANES):
            rows = jnp.where(sub == s, q[s], rows)
        pw_ref[part] = rows
        for k, shift in enumerate((1, 2, 4)):
            mk_ref[k, part] = jnp.where(sub >= shift, q[shift - 1], 0.0)
    an_ref[0] = jnp.broadcast_to(pr[n_tok], (SUBLANES, LANES))
    an_ref[1] = jnp.broadcast_to(pi[n_tok], (SUBLANES, LANES))


def _s5_prep(lam_re, lam_im, log_dt, b_re, b_im, c_re, c_im, n_tok):
    ab_re, ab_im, bb_re, bb_im = _s5_discretize(lam_re, lam_im, log_dt, b_re, b_im)
    pair = lambda v: v.reshape(N_PAIRS, 2, v.shape[1], SSM_STATE).transpose(0, 2, 1, 3).reshape(
        N_PAIRS, v.shape[1], LANES)
    a = jnp.stack([pair(ab_re[:, None, :]), pair(ab_im[:, None, :])], axis=1)
    c = jnp.stack([pair(c_re), pair(c_im)], axis=1)
    bt = jnp.stack([pair(bb_re.transpose(0, 2, 1)), pair(bb_im.transpose(0, 2, 1))], axis=1)
    blk4 = lambda r: pl.BlockSpec((PREP_PAIRS, 2, r, LANES), lambda q: (q, 0, 0, 0))
    w_spec = pl.BlockSpec((2 * PREP_PAIRS, GROUP_IO, PAIR_COLS), lambda q: (q, 0, 0))
    w_shape = jax.ShapeDtypeStruct((SSM_GROUPS, GROUP_IO, PAIR_COLS), BF16)
    return pl.pallas_call(
        functools.partial(_s5_prep_kernel, n_tok=n_tok),
        grid=(N_PAIRS // PREP_PAIRS,),
        in_specs=[blk4(1), blk4(SSM_CH), blk4(SSM_CH)],
        out_specs=[w_spec, w_spec, w_spec, w_spec,
                   pl.BlockSpec((3, 2, SUBLANES, PREP_PAIRS * LANES), lambda q: (0, 0, 0, q)),
                   pl.BlockSpec((2, SUBLANES, PREP_PAIRS * LANES), lambda q: (0, 0, q)),
                   pl.BlockSpec((2, SUBLANES, PREP_PAIRS * LANES), lambda q: (0, 0, q))],
        out_shape=[w_shape, w_shape, w_shape, w_shape,
                   jax.ShapeDtypeStruct((3, 2, SUBLANES, HALF_COLS), F32),
                   jax.ShapeDtypeStruct((2, SUBLANES, HALF_COLS), F32),
                   jax.ShapeDtypeStruct((2, SUBLANES, HALF_COLS), F32)],
        compiler_params=_params("arbitrary"),
        name="s5_prep",
    )(a, c, bt)


def _group_io(ref, g):
    return ref.at[:, g * GROUP_IO:(g + 1) * GROUP_IO]


def _s5_emit_tile(u_ref, st_scr, m_ref, whyt_ref, dt_ref, g_ref, j):
    for gl in range(GROUPS_PER_TILE):
        g = GROUPS_PER_TILE * j + gl
        q0 = (gl // 2) * PAIR_COLS
        u_g = _group_io(u_ref, g)[...]
        y = (jnp.dot(u_g, m_ref[g], preferred_element_type=F32)
             + lax.dot_general(st_scr[:, q0:q0 + PAIR_COLS].astype(BF16), whyt_ref[g],
                               (((1,), (1,)), ((), ())), preferred_element_type=F32)
             + dt_ref[g] * u_g.astype(F32))
        _group_io(g_ref, g)[...] = _gelu_tanh(y).astype(BF16)


def _s5_local_states(u_ref, ws_ref, j, q):
    g = GROUPS_PER_TILE * j + 2 * q
    return (jnp.dot(_group_io(u_ref, g)[...], ws_ref[g], preferred_element_type=F32)
            + jnp.dot(_group_io(u_ref, g + 1)[...], ws_ref[g + 1], preferred_element_type=F32))


def _cmul_add(ar, ai, hr, hi, xr, xi):
    return ar * hr - ai * hi + xr, ar * hi + ai * hr + xi


def _s5_chain_kernel(u_ref, m_ref, ws_ref, whyt_ref, mk_ref, pw_ref, dt_ref, h0_ref, g_ref, hT_ref,
                     carry_scr, st_scr, *, rows):
    @pl.when(pl.program_id(1) == 0)
    def _():
        carry_scr[...] = jnp.broadcast_to(h0_ref[0], (SUBLANES, STATE_COLS))

    row0 = lax.broadcasted_iota(jnp.int32, (SUBLANES, LANES), 0) == 0
    last = lambda h: jnp.broadcast_to(h[SUBLANES - 1:, :], (SUBLANES, LANES))

    for j in range(N_LANE_TILES):
        for q in range(PAIRS_PER_TILE):
            st_scr[:, q * PAIR_COLS:(q + 1) * PAIR_COLS] = _s5_local_states(u_ref, ws_ref, j, q)

        def step(r, carry):
            r0 = pl.multiple_of(r * SUBLANES, SUBLANES)
            out = []
            for q in range(PAIRS_PER_TILE):
                re_c = slice(q * PAIR_COLS, q * PAIR_COLS + LANES)
                im_c = slice(q * PAIR_COLS + LANES, (q + 1) * PAIR_COLS)
                hc = slice((j * PAIRS_PER_TILE + q) * LANES, (j * PAIRS_PER_TILE + q + 1) * LANES)
                xr = st_scr[pl.ds(r0, SUBLANES), re_c]
                xi = st_scr[pl.ds(r0, SUBLANES), im_c]
                for k, shift in enumerate((1, 2, 4)):
                    xr, xi = _cmul_add(mk_ref[k, 0, :, hc], mk_ref[k, 1, :, hc],
                                       pltpu.roll(xr, shift, 0), pltpu.roll(xi, shift, 0), xr, xi)
                cr, ci = carry[q]
                hr, hi = _cmul_add(pw_ref[0, :, hc], pw_ref[1, :, hc], cr, ci, xr, xi)
                st_scr[pl.ds(r0, SUBLANES), re_c] = jnp.where(row0, cr, pltpu.roll(hr, 1, 0))
                st_scr[pl.ds(r0, SUBLANES), im_c] = jnp.where(row0, ci, pltpu.roll(hi, 1, 0))
                out.append((last(hr), last(hi)))
            return tuple(out)

        c0 = j * TILE_COLS
        init = tuple((carry_scr[:, c0 + q * PAIR_COLS:c0 + q * PAIR_COLS + LANES],
                      carry_scr[:, c0 + q * PAIR_COLS + LANES:c0 + (q + 1) * PAIR_COLS])
                     for q in range(PAIRS_PER_TILE))
        final = lax.fori_loop(0, rows // SUBLANES, step, init, unroll=True)
        for q in range(PAIRS_PER_TILE):
            carry_scr[:, c0 + q * PAIR_COLS:c0 + q * PAIR_COLS + LANES] = final[q][0]
            carry_scr[:, c0 + q * PAIR_COLS + LANES:c0 + (q + 1) * PAIR_COLS] = final[q][1]

        _s5_emit_tile(u_ref, st_scr, m_ref, whyt_ref, dt_ref, g_ref, j)

    hT_ref[0] = carry_scr[...]


def _s5_chain(u16, m, ws, why, mk, pw, d_tiled, h0, batch, n_rows, rows):
    nblk = n_rows // rows
    row = pl.BlockSpec((rows, CHUNK_LANES), lambda b, i: (b * nblk + i, 0))
    return pl.pallas_call(
        functools.partial(_s5_chain_kernel, rows=rows),
        grid=(batch, nblk),
        in_specs=[row, _const_spec(m.shape), _const_spec(ws.shape), _const_spec(why.shape),
                  _const_spec(mk.shape), _const_spec(pw.shape), _const_spec(d_tiled.shape),
                  pl.BlockSpec((1, 1, STATE_COLS), lambda b, i: (b, 0, 0))],
        out_specs=[row, pl.BlockSpec((1, SUBLANES, STATE_COLS), lambda b, i: (b, 0, 0))],
        out_shape=[jax.ShapeDtypeStruct(u16.shape, BF16),
                   jax.ShapeDtypeStruct((batch, SUBLANES, STATE_COLS), F32)],
        scratch_shapes=[pltpu.VMEM((SUBLANES, STATE_COLS), F32), pltpu.VMEM((rows, TILE_COLS), F32)],
        compiler_params=_params("arbitrary", "arbitrary"),
        name="s5_chain",
    )(u16, m, ws, why, mk, pw, d_tiled, h0)


def _s5_rows_kernel(u_ref, m_ref, ws_ref, whyt_ref, an_ref, dt_ref, h0r_ref, h0i_ref, g_ref, hTr_ref, hTi_ref,
                    st_scr):
    for j in range(N_LANE_TILES):
        for q in range(PAIRS_PER_TILE):
            hc = slice((j * PAIRS_PER_TILE + q) * LANES, (j * PAIRS_PER_TILE + q + 1) * LANES)
            h0r, h0i = h0r_ref[hc, :].T, h0i_ref[hc, :].T
            local = _s5_local_states(u_ref, ws_ref, j, q)
            hr, hi = _cmul_add(an_ref[0, :1, hc], an_ref[1, :1, hc], h0r, h0i, local[:, :LANES], local[:, LANES:])
            hTr_ref[hc, :] = hr.T
            hTi_ref[hc, :] = hi.T
            st_scr[:, q * PAIR_COLS:q * PAIR_COLS + LANES] = h0r
            st_scr[:, q * PAIR_COLS + LANES:(q + 1) * PAIR_COLS] = h0i
        _s5_emit_tile(u_ref, st_scr, m_ref, whyt_ref, dt_ref, g_ref, j)


def _s5_rows(u16, m, ws, why, an, d_tiled, h0_re_t, h0_im_t, rows):
    n = u16.shape[0]
    once = pl.Buffered(1)
    row = pl.BlockSpec((rows, CHUNK_LANES), lambda i: (i, 0), pipeline_mode=once)
    col = pl.BlockSpec((HALF_COLS, rows), lambda i: (0, i), pipeline_mode=once)
    state = jax.ShapeDtypeStruct((HALF_COLS, n), F32)
    return pl.pallas_call(
        _s5_rows_kernel,
        grid=(n // rows,),
        in_specs=[row, _const_spec(m.shape), _const_spec(ws.shape), _const_spec(why.shape),
                  _const_spec(an.shape), _const_spec(d_tiled.shape), col, col],
        out_specs=[row, col, col],
        out_shape=[jax.ShapeDtypeStruct(u16.shape, BF16), state, state],
        scratch_shapes=[pltpu.VMEM((rows, TILE_COLS), F32)],
        compiler_params=_params("arbitrary"),
        name="s5_rows",
    )(u16, m, ws, why, an, d_tiled, h0_re_t, h0_im_t)


def _post_kernel(g_ref, ma_ref, gs_ref, x_ref, permt_ref, wglu_ref, wout_ref, gffn_ref, wup_ref, wdown_ref,
                 gfin_ref, out_ref, *, n_tok):
    g = _load_chunk_rows(g_ref, permt_ref, n_tok)
    glu = jnp.dot(g, wglu_ref[...], preferred_element_type=F32)
    s_out = glu[:, :D_MODEL] * jax.nn.sigmoid(glu[:, D_MODEL:])
    merged = ma_ref[...] + gs_ref[...].astype(F32) * s_out
    x1 = x_ref[...] + jnp.dot(merged.astype(BF16), wout_ref[...], preferred_element_type=F32)
    h2 = _rmsnorm(x1, gffn_ref[...]).astype(BF16)
    x2 = x1
    for c in range(0, D_FF, FF_CHUNK):
        up = jnp.dot(h2, wup_ref[:, c:c + FF_CHUNK], preferred_element_type=F32)
        act = jnp.square(jnp.maximum(up, 0.0)).astype(BF16)
        x2 = x2 + jnp.dot(act, wdown_ref[c:c + FF_CHUNK, :], preferred_element_type=F32)
    out_ref[...] = _rmsnorm(x2, gfin_ref[...])


def _post(g, ma, gs, x, permt, w_glu, w_out, g_ffn, w_up, w_down, g_final, tm, n_tok):
    n = x.shape[0]
    row = pl.BlockSpec((tm, D_MODEL), lambda i: (i, 0))
    g_spec = pl.BlockSpec((tm // n_tok, CHUNK_LANES), lambda i: (i, 0))
    return pl.pallas_call(
        functools.partial(_post_kernel, n_tok=n_tok),
        grid=(n // tm,),
        in_specs=[g_spec, row, row, row, _const_spec(permt.shape), _const_spec(w_glu.shape),
                  _const_spec(w_out.shape), _const_spec((1, D_MODEL)), _const_spec(w_up.shape),
                  _const_spec(w_down.shape), _const_spec((1, D_MODEL))],
        out_specs=row,
        out_shape=jax.ShapeDtypeStruct((n, D_MODEL), F32),
        compiler_params=_params("arbitrary"),
        name="post",
    )(g, ma, gs, x, permt, w_glu, w_out, g_ffn, w_up, w_down, g_final)


def _tile(n, pref):
    t = pref
    while n % t:
        t //= 2
    return t


def kernel(x_prompt, x_sample, cache_k, cache_v, state_ssm_re, state_ssm_im, g_mix, w_in, attn_sinks,
           w_attn_o, ssm_lambda_re, ssm_lambda_im, ssm_log_dt, ssm_b_re, ssm_b_im, ssm_c_re, ssm_c_im,
           ssm_d, w_glu, w_out, g_ffn, w_up, w_down, g_final):
    batch, seq, _ = x_prompt.shape
    db, dec_seq, _ = x_sample.shape
    assert w_in.shape[0] == 1, "one layer"
    assert seq % PERM_ROWS == 0 and dec_seq in (1, 2, 4, 8) and (db * dec_seq) % PERM_ROWS == 0

    vec = lambda v: v.reshape(1, D_MODEL).astype(F32)
    w_in_b = w_in[0].astype(BF16)
    sinks = attn_sinks[0].astype(F32)
    s5p = (ssm_lambda_re[0], ssm_lambda_im[0], ssm_log_dt[0], ssm_b_re[0], ssm_b_im[0], ssm_c_re[0], ssm_c_im[0])
    d_skip = jnp.tile(ssm_d[0].astype(F32).reshape(SSM_GROUPS, 1, SSM_CH), (1, 1, CHUNK))
    perm = _chunk_perm(CHUNK)

    xp = x_prompt.reshape(batch * seq, D_MODEL)
    q, kv, u16, ga, gs, wo_b, w_glu_b, w_out_b, w_up_b, w_down_b = _proj(
        xp, vec(g_mix[0]), w_in_b, perm, _tile(batch * seq, 1024), CHUNK,
        cast=(w_attn_o[0], w_glu[0], w_out[0], w_up[0], w_down[0]))
    post_w = (w_glu_b, w_out_b, vec(g_ffn[0]), w_up_b, w_down_b, vec(g_final))
    ma = _attn_prompt(sinks, q, kv, ga, wo_b, batch, seq, _tile(seq, 2048))
    m, ws, ws_n, whyt, mk, pw, an = _s5_prep(*s5p, dec_seq)
    n_rows = seq // CHUNK
    g16, hT = _s5_chain(u16, m, ws, whyt, mk, pw, d_skip, jnp.zeros((batch, 1, STATE_COLS), F32),
                        batch, n_rows, _tile(n_rows, 128))
    y_prompt = _post(g16, ma, gs, xp, perm.T, *post_w, _tile(batch * seq, 512), CHUNK).reshape(batch, seq, D_MODEL)
    kv_last = kv.reshape(batch, seq, 2 * D_KV)[:, seq - WINDOW:]
    k_prompt = kv_last[..., :D_KV].reshape(1, batch, WINDOW, N_KV_HEADS, HEAD_DIM)
    v_prompt = kv_last[..., D_KV:].reshape(1, batch, WINDOW, N_KV_HEADS, HEAD_DIM)
    hr, hi = _state_split(hT[:, 0])
    ssm_re_prompt, ssm_im_prompt = hr[None], hi[None]

    ns = db * dec_seq
    xs = x_sample.reshape(ns, D_MODEL)
    perm_s = _chunk_perm(dec_seq)
    q, kv, u16, ga, gs = _proj(xs, vec(g_mix[0]), w_in_b, perm_s, _tile(ns, 512), dec_seq)
    tpad = SUBLANES
    pad3 = lambda v: jnp.pad(v.reshape(db, dec_seq, -1).astype(F32), ((0, 0), (0, tpad - dec_seq), (0, 0)))
    ga_pad = pad3(ga).astype(BF16).reshape(db * tpad, D_MODEL)
    ma_pad, k_new, v_new = _attn_sample(
        sinks, pad3(q), pad3(kv), cache_k[0].reshape(db, WINDOW, D_KV), cache_v[0].reshape(db, WINDOW, D_KV),
        ga_pad, wo_b, _tile(db, 16), dec_seq)
    ma = ma_pad.reshape(db, tpad, D_MODEL)[:, :dec_seq].reshape(ns, D_MODEL)
    to_cols = lambda s: s[0].astype(F32).transpose(1, 2, 0).reshape(HALF_COLS, db)
    from_cols = lambda h: h.reshape(SSM_GROUPS, SSM_STATE, db).transpose(2, 0, 1)[None]
    g16, hT_re, hT_im = _s5_rows(u16, m, ws_n, whyt, an, d_skip, to_cols(state_ssm_re), to_cols(state_ssm_im),
                                 _tile(db, LANES))
    y_sample = _post(g16, ma, gs, xs, perm_s.T, *post_w, PERM_ROWS, dec_seq).reshape(db, dec_seq, D_MODEL)
    k_sample = k_new.reshape(1, db, WINDOW, N_KV_HEADS, HEAD_DIM)
    v_sample = v_new.reshape(1, db, WINDOW, N_KV_HEADS, HEAD_DIM)
    ssm_re_sample, ssm_im_sample = from_cols(hT_re), from_cols(hT_im)

    return (y_prompt, y_sample, k_prompt, v_prompt, ssm_re_prompt, ssm_im_prompt,
            k_sample, v_sample, ssm_re_sample, ssm_im_sample)
```

```python
import functools
import math

import jax
import jax.numpy as jnp
from jax import lax
from jax.experimental import pallas as pl
from jax.experimental.pallas import tpu as pltpu

F32 = jnp.float32
BF16 = jnp.bfloat16

D_MODEL = 1024
HEAD_DIM = 64
N_HEADS = 16
N_KV_HEADS = 4
WINDOW = 128
D_KV = N_KV_HEADS * HEAD_DIM
SSM_CH = 16
SSM_GROUPS = 64
SSM_STATE = 64
D_FF = 4 * D_MODEL
FF_CHUNK = 1024
RMS_EPS = 1e-5
LOG2E = 1.4426950408889634
Q_SCALE = HEAD_DIM ** -0.5 * LOG2E

LANES = 128
SUBLANES = 8
BF16_ROWS = 16
N_LANE_TILES = D_MODEL // LANES
GROUPS_PER_TILE = LANES // SSM_CH
CHUNK = 16
CHUNK_LANES = CHUNK * D_MODEL
GROUP_IO = CHUNK * SSM_CH
N_PAIRS = SSM_GROUPS // 2
PAIRS_PER_TILE = GROUPS_PER_TILE // 2
PAIR_COLS = 2 * LANES
TILE_COLS = PAIRS_PER_TILE * PAIR_COLS
STATE_COLS = N_PAIRS * PAIR_COLS
HALF_COLS = STATE_COLS // 2
PERM_ROWS = CHUNK * BF16_ROWS
PREP_PAIRS = 4
VMEM_LIMIT = 56 * 1024 * 1024

_Q0, _KV0, _U0, _GA0, _GS0, _END = 0, 1024, 1536, 2560, 3584, 4608


def _gelu_tanh(x):
    k1 = -2.0 * math.sqrt(2.0 / math.pi) * LOG2E
    return x / (1.0 + jnp.exp2(x * (k1 + (0.044715 * k1) * (x * x))))


def _rmsnorm(x, g):
    return x * lax.rsqrt(jnp.mean(x * x, axis=-1, keepdims=True) + RMS_EPS) * g


def _params(*sem):
    return pltpu.CompilerParams(dimension_semantics=sem, vmem_limit_bytes=VMEM_LIMIT)


def _const_spec(shape):
    nd = len(shape)
    return pl.BlockSpec(shape, lambda *_: (0,) * nd, pipeline_mode=pl.Buffered(1))


def _chunk_perm(n_tok):
    n_chunks = PERM_ROWS // n_tok
    r = jnp.arange(PERM_ROWS)
    tok = (r % n_chunks) * n_tok + r // n_chunks
    return (tok[:, None] == jnp.arange(PERM_ROWS)[None, :]).astype(BF16)


def _piece_transpose(cols, masks):
    for d, msk in zip((4, 2, 1), masks):
        new = list(cols)
        for v in range(GROUPS_PER_TILE):
            if v & d == 0:
                a, b = cols[v], cols[v + d]
                new[v] = jnp.where(msk, pltpu.roll(b, SSM_CH * d, 1), a)
                new[v + d] = jnp.where(msk, b, pltpu.roll(a, LANES - SSM_CH * d, 1))
        cols = new
    return cols


def _piece_masks(rows):
    piece = lax.broadcasted_iota(jnp.int32, (rows, LANES), 1) // SSM_CH
    return [(piece & d) != 0 for d in (4, 2, 1)]


def _store_chunk_rows(x, perm_ref, out_ref, n_tok):
    n_chunks = PERM_ROWS // n_tok
    masks = _piece_masks(n_chunks)
    zero = jnp.zeros((n_chunks, LANES), F32)
    for hb in range(x.shape[0] // PERM_ROWS):
        rows = slice(hb * n_chunks, (hb + 1) * n_chunks)
        xp = jnp.dot(perm_ref[...], x[hb * PERM_ROWS:(hb + 1) * PERM_ROWS],
                     preferred_element_type=F32)
        for j in range(N_LANE_TILES):
            for hf in range(2):
                live = 8 * hf < n_tok
                cols = [xp[t * n_chunks:(t + 1) * n_chunks, j * LANES:(j + 1) * LANES] if t < n_tok else zero
                        for t in range(8 * hf, 8 * hf + 8)]
                for gl, col in enumerate(_piece_transpose(cols, masks) if live else cols):
                    c0 = (GROUPS_PER_TILE * j + gl) * GROUP_IO + hf * LANES
                    out_ref[rows, c0:c0 + LANES] = col.astype(BF16)


def _load_chunk_rows(in_ref, permt_ref, n_tok):
    n_chunks = PERM_ROWS // n_tok
    masks = _piece_masks(n_chunks)
    blocks = []
    for hb in range(in_ref.shape[0] // n_chunks):
        rows = slice(hb * n_chunks, (hb + 1) * n_chunks)
        tiles = [[None] * N_LANE_TILES for _ in range(n_tok)]
        for j in range(N_LANE_TILES):
            for hf in range(-(-n_tok // 8)):
                cols = []
                for gl in range(GROUPS_PER_TILE):
                    c0 = (GROUPS_PER_TILE * j + gl) * GROUP_IO + hf * LANES
                    cols.append(in_ref[rows, c0:c0 + LANES].astype(F32))
                for k, col in enumerate(_piece_transpose(cols, masks)):
                    if 8 * hf + k < n_tok:
                        tiles[8 * hf + k][j] = col.astype(BF16)
        xp = jnp.concatenate([jnp.concatenate(row, axis=1) for row in tiles], axis=0)
        blocks.append(jnp.dot(permt_ref[...], xp, preferred_element_type=F32).astype(BF16))
    return blocks[0] if len(blocks) == 1 else jnp.concatenate(blocks, axis=0)


def _proj_kernel(x_ref, g_ref, w_ref, perm_ref, *refs, n_tok, n_cast):
    q_ref, kv_ref, u_ref, ga_ref, gs_ref = refs[n_cast:n_cast + 5]
    h = _rmsnorm(x_ref[...], g_ref[...]).astype(BF16)

    def seg(lo, hi):
        return jnp.dot(h, w_ref[:, lo:hi], preferred_element_type=F32)

    q_ref[...] = (seg(_Q0, _KV0) * Q_SCALE).astype(BF16)
    kv_ref[...] = seg(_KV0, _U0)
    _store_chunk_rows(seg(_U0, _GA0).astype(BF16), perm_ref, u_ref, n_tok)
    ga_ref[...] = jax.nn.sigmoid(seg(_GA0, _GS0)).astype(BF16)
    gs_ref[...] = jax.nn.sigmoid(seg(_GS0, _END)).astype(BF16)
    for src, dst in zip(refs[:n_cast], refs[n_cast + 5:]):
        dst[...] = src[...].astype(BF16)


def _proj(x, g_mix, w_in, perm, tm, n_tok, cast=()):
    n = x.shape[0]
    steps = n // tm
    row = lambda w: pl.BlockSpec((tm, w), lambda i: (i, 0))
    slab = lambda w: pl.BlockSpec((w.shape[0] // steps, w.shape[1]), lambda i: (i, 0))
    assert all(w.shape[0] % (steps * BF16_ROWS) == 0 for w in cast)
    return pl.pallas_call(
        functools.partial(_proj_kernel, n_tok=n_tok, n_cast=len(cast)),
        grid=(steps,),
        in_specs=[row(D_MODEL), _const_spec((1, D_MODEL)), _const_spec(w_in.shape), _const_spec(perm.shape)]
        + [slab(w) for w in cast],
        out_specs=[row(D_MODEL), row(2 * D_KV), pl.BlockSpec((tm // n_tok, CHUNK_LANES), lambda i: (i, 0)),
                   row(D_MODEL), row(D_MODEL)] + [slab(w) for w in cast],
        out_shape=[jax.ShapeDtypeStruct((n, D_MODEL), BF16),
                   jax.ShapeDtypeStruct((n, 2 * D_KV), F32),
                   jax.ShapeDtypeStruct((n // n_tok, CHUNK_LANES), BF16),
                   jax.ShapeDtypeStruct((n, D_MODEL), BF16),
                   jax.ShapeDtypeStruct((n, D_MODEL), BF16)]
        + [jax.ShapeDtypeStruct(w.shape, BF16) for w in cast],
        compiler_params=_params("arbitrary"),
        name="proj",
    )(x, g_mix, w_in, perm, *cast)


def _dup_heads(tile):
    lo = lax.broadcasted_iota(jnp.int32, tile.shape, tile.ndim - 1) < HEAD_DIM
    rolled = pltpu.roll(tile, HEAD_DIM, tile.ndim - 1)
    return (jnp.where(lo, tile, rolled).astype(BF16), jnp.where(lo, rolled, tile).astype(BF16))


def _sink_softmax(s, sink):
    sink2 = sink * LOG2E
    m = jnp.maximum(jnp.max(s, axis=-1, keepdims=True), sink2)
    p = jnp.exp2(s - m)
    denom = jnp.sum(p, axis=-1, keepdims=True) + jnp.exp2(sink2 - m)
    return p.astype(BF16), 1.0 / denom


def _attn_prompt_kernel(sinks_ref, q_ref, kvc_ref, kvp_ref, ga_ref, wo_ref, out_ref,
                        kd_scr, vd_scr, bias_scr, attn_scr, *, tq):
    i = pl.program_id(1)
    kv_full = jnp.concatenate([kvp_ref[...], kvc_ref[...]], axis=0)
    for t in range(2):
        ke, ko = _dup_heads(kv_full[:, t * LANES:(t + 1) * LANES])
        ve, vo = _dup_heads(kv_full[:, D_KV + t * LANES:D_KV + (t + 1) * LANES])
        kd_scr[2 * t], kd_scr[2 * t + 1] = ke, ko
        vd_scr[2 * t], vd_scr[2 * t + 1] = ve, vo

    qi = lax.broadcasted_iota(jnp.int32, (WINDOW, 2 * WINDOW), 0)
    si = lax.broadcasted_iota(jnp.int32, (WINDOW, 2 * WINDOW), 1)
    band = (si > qi) & (si <= qi + WINDOW)
    in_block = si >= WINDOW
    lo = lax.broadcasted_iota(jnp.int32, (WINDOW, LANES), 1) < HEAD_DIM

    def attend(jb):
        q0 = jb * WINDOW
        has_prev = (i > 0) | (jb > 0)
        bias_scr[...] = jnp.where(band & (in_block | has_prev), 0.0, -jnp.inf)

        def head_pair(hp, c):
            j = hp // 2
            c0 = pl.multiple_of(hp * LANES, LANES)
            kd = kd_scr[j, pl.ds(q0, 2 * WINDOW), :]
            vd = vd_scr[j, pl.ds(q0, 2 * WINDOW), :]
            qp = q_ref[pl.ds(q0, WINDOW), pl.ds(c0, LANES)]
            outs = []
            for par in range(2):
                qm = jnp.where(lo if par == 0 else ~lo, qp, jnp.zeros_like(qp))
                s = lax.dot_general(qm, kd, (((1,), (1,)), ((), ())), preferred_element_type=F32)
                p, inv = _sink_softmax(s + bias_scr[...], sinks_ref[2 * hp + par])
                outs.append(jnp.dot(p, vd, preferred_element_type=F32) * inv)
            attn_scr[pl.ds(q0, WINDOW), pl.ds(c0, LANES)] = jnp.where(lo, outs[0], outs[1]).astype(BF16)
            return c

        lax.fori_loop(0, N_HEADS // 2, head_pair, 0, unroll=8)

    def project(jb):
        rows = pl.ds(jb * WINDOW, WINDOW)
        a_out = jnp.dot(attn_scr[rows, :], wo_ref[...], preferred_element_type=F32)
        out_ref[rows, :] = ga_ref[rows, :].astype(F32) * a_out

    for jb in range(tq // WINDOW):
        attend(jb)
        project(jb)


def _attn_prompt(sinks, q, kv, ga, wo, batch, seq, tq):
    nq = seq // tq
    bpt = tq // WINDOW
    row = lambda w: pl.BlockSpec((tq, w), lambda b, i: (b * nq + i, 0))
    prev = pl.BlockSpec((WINDOW, 2 * D_KV),
                        lambda b, i: (jnp.maximum((b * nq + i) * bpt - 1, 0), 0))
    return pl.pallas_call(
        functools.partial(_attn_prompt_kernel, tq=tq),
        grid=(batch, nq),
        in_specs=[pl.BlockSpec(memory_space=pltpu.SMEM), row(D_MODEL), row(2 * D_KV), prev,
                  row(D_MODEL), _const_spec(wo.shape)],
        out_specs=row(D_MODEL),
        out_shape=jax.ShapeDtypeStruct((batch * seq, D_MODEL), F32),
        scratch_shapes=[pltpu.VMEM((N_KV_HEADS, WINDOW + tq, LANES), BF16),
                        pltpu.VMEM((N_KV_HEADS, WINDOW + tq, LANES), BF16),
                        pltpu.VMEM((WINDOW, 2 * WINDOW), F32),
                        pltpu.VMEM((tq, D_MODEL), BF16)],
        compiler_params=_params("arbitrary", "arbitrary"),
        name="attn_prompt",
    )(sinks, q, kv, kv, ga, wo)


def _attn_sample_kernel(sinks_ref, q_ref, kvn_ref, ck_ref, cv_ref, ga_ref, wo_ref,
                        out_ref, ko_ref, vo_ref, attn_scr, *, bb, tpad, dec_seq):
    nk = WINDOW + tpad
    lo3 = lax.broadcasted_iota(jnp.int32, (bb, tpad, LANES), 2) < HEAD_DIM
    row = lax.broadcasted_iota(jnp.int32, (4 * tpad, nk), 0)
    si = lax.broadcasted_iota(jnp.int32, (4 * tpad, nk), 1)
    tq = row % tpad
    valid = (si > tq) & (si <= tq + WINDOW) & (si < WINDOW + dec_seq)
    hrow = lax.broadcasted_iota(jnp.int32, (4 * tpad, 1), 0) // tpad

    keys_first = lambda ref, sl: jnp.stack([ref[b, sl, :].T for b in range(bb)])
    for t in range(2):
        sl = slice(t * LANES, (t + 1) * LANES)
        kk = jnp.concatenate([keys_first(ck_ref, sl), kvn_ref[:, :, sl]], axis=1)
        vv = jnp.concatenate([keys_first(cv_ref, sl),
                              kvn_ref[:, :, D_KV + t * LANES:D_KV + (t + 1) * LANES]], axis=1)
        ko_ref[:, :, sl] = kk[:, dec_seq:dec_seq + WINDOW, :]
        vo_ref[:, :, sl] = vv[:, dec_seq:dec_seq + WINDOW, :]
        kds = _dup_heads(kk)
        vds = _dup_heads(vv)
        for par_kv in range(2):
            j = 2 * t + par_kv
            kd, vd = kds[par_kv], vds[par_kv]
            parts = []
            for r in range(2):
                c0 = j * 2 * LANES + r * LANES
                qp = q_ref[:, :, c0:c0 + LANES]
                parts += [jnp.where(lo3, qp, 0.0), jnp.where(lo3, 0.0, qp)]
            lhs = jnp.concatenate(parts, axis=1).astype(BF16)
            s = jnp.einsum('bqd,bkd->bqk', lhs, kd, preferred_element_type=F32)
            sink = jnp.zeros((4 * tpad, 1), F32)
            for g in range(4):
                sink = jnp.where(hrow == g, sinks_ref[4 * j + g], sink)
            p, inv = _sink_softmax(jnp.where(valid[None], s, -jnp.inf), sink[None])
            o = jnp.einsum('bqk,bkd->bqd', p, vd, preferred_element_type=F32) * inv
            for r in range(2):
                c0 = j * 2 * LANES + r * LANES
                o_even = o[:, (2 * r) * tpad:(2 * r + 1) * tpad, :]
                o_odd = o[:, (2 * r + 1) * tpad:(2 * r + 2) * tpad, :]
                attn_scr[:, :, c0:c0 + LANES] = jnp.where(lo3, o_even, o_odd)

    attn = attn_scr[...].reshape(bb * tpad, D_MODEL).astype(BF16)
    a_out = jnp.dot(attn, wo_ref[...], preferred_element_type=F32)
    out_ref[...] = ga_ref[...].astype(F32) * a_out


def _attn_sample(sinks, q3, kvn3, ck, cv, ga2, wo, bb, dec_seq):
    db, tpad, _ = q3.shape
    blk3 = lambda r, w: pl.BlockSpec((bb, r, w), lambda i: (i, 0, 0))
    row = pl.BlockSpec((bb * tpad, D_MODEL), lambda i: (i, 0))
    return pl.pallas_call(
        functools.partial(_attn_sample_kernel, bb=bb, tpad=tpad, dec_seq=dec_seq),
        grid=(db // bb,),
        in_specs=[pl.BlockSpec(memory_space=pltpu.SMEM), blk3(tpad, D_MODEL), blk3(tpad, 2 * D_KV),
                  blk3(D_KV, WINDOW), blk3(D_KV, WINDOW), row, _const_spec(wo.shape)],
        out_specs=[row, blk3(WINDOW, D_KV), blk3(WINDOW, D_KV)],
        out_shape=[jax.ShapeDtypeStruct((db * tpad, D_MODEL), F32),
                   jax.ShapeDtypeStruct((db, WINDOW, D_KV), F32),
                   jax.ShapeDtypeStruct((db, WINDOW, D_KV), F32)],
        scratch_shapes=[pltpu.VMEM((bb, tpad, D_MODEL), F32)],
        compiler_params=_params("arbitrary"),
        name="attn_sample",
    )(sinks, q3, kvn3, ck, cv, ga2, wo)


def _s5_discretize(lam_re, lam_im, log_dt, b_re, b_im):
    dt = jnp.exp(log_dt)[:, None]
    decay = jnp.exp(lam_re * dt)
    ab_re = decay * jnp.cos(lam_im * dt)
    ab_im = decay * jnp.sin(lam_im * dt)
    nr, ni = ab_re - 1.0, ab_im
    den = lam_re * lam_re + lam_im * lam_im
    f_re = ((nr * lam_re + ni * lam_im) / den)[..., None]
    f_im = ((ni * lam_re - nr * lam_im) / den)[..., None]
    return ab_re, ab_im, f_re * b_re - f_im * b_im, f_re * b_im + f_im * b_re


def _state_split(h):
    h = h.reshape(h.shape[:-1] + (N_PAIRS, 2, 2, SSM_STATE))
    unpair = lambda a: a.reshape(a.shape[:-3] + (SSM_GROUPS, SSM_STATE))
    return unpair(h[..., 0, :, :]), unpair(h[..., 1, :, :])


def _shift_lanes(x, n):
    a, b = x[:, :LANES], x[:, LANES:]
    lane = lax.broadcasted_iota(jnp.int32, a.shape, 1)
    if n == 0:
        return x
    if n >= LANES:
        r = n - LANES
        hi = a if r == 0 else jnp.where(lane < r, 0.0, pltpu.roll(a, r, 1))
        return jnp.concatenate([jnp.zeros_like(a), hi], axis=1)
    ra, rb = pltpu.roll(a, n, 1), pltpu.roll(b, n, 1)
    return jnp.concatenate([jnp.where(lane < n, 0.0, ra), jnp.where(lane < n, ra, rb)], axis=1)


def _dot_nt_split(lhs, rhs):
    nt = lambda a, b: lax.dot_general(a, b, (((1,), (1,)), ((), ())), preferred_element_type=F32)
    l_hi, r_hi = lhs.astype(BF16), rhs.astype(BF16)
    l_lo = (lhs - l_hi.astype(F32)).astype(BF16)
    r_lo = (rhs - r_hi.astype(F32)).astype(BF16)
    return nt(l_hi, r_hi) + nt(l_hi, r_lo) + nt(l_lo, r_hi)


def _s5_prep_kernel(a_ref, c_ref, bt_ref, m_ref, ws_ref, wsn_ref, whyt_ref, mk_ref, pw_ref, an_ref, *, n_tok):
    for p in range(PREP_PAIRS):
        one, two, lanes = pl.ds(p, 1), pl.ds(2 * p, 2), pl.ds(p * LANES, LANES)
        _s5_prep_pair(a_ref.at[one], c_ref.at[one], bt_ref.at[one], m_ref.at[two], ws_ref.at[two],
                      wsn_ref.at[two], whyt_ref.at[two], mk_ref.at[:, :, :, lanes], pw_ref.at[:, :, lanes],
                      an_ref.at[:, :, lanes], n_tok)


def _s5_prep_pair(a_ref, c_ref, bt_ref, m_ref, ws_ref, wsn_ref, whyt_ref, mk_ref, pw_ref, an_ref, n_tok):
    ar, ai = a_ref[0, 0], a_ref[0, 1]
    cr, ci = c_ref[0, 0], c_ref[0, 1]
    br, bi = bt_ref[0, 0], bt_ref[0, 1]
    pr, pi = [jnp.ones_like(ar)], [jnp.zeros_like(ai)]
    for _ in range(CHUNK):
        pr, pi = pr + [pr[-1] * ar - pi[-1] * ai], pi + [pr[-1] * ai + pi[-1] * ar]
    first = lax.broadcasted_iota(jnp.int32, (SSM_CH, LANES), 1) < SSM_STATE
    ca = [(cr * pr[t] - ci * pi[t], -(cr * pi[t] + ci * pr[t])) for t in range(CHUNK + 1)]
    ca_full = jnp.concatenate([jnp.concatenate(ca[t], axis=1) for t in range(CHUNK)], axis=0)
    zero = jnp.zeros((SSM_CH, LANES), F32)
    for e in range(2):
        own = first if e == 0 else ~first
        pick = lambda v: jnp.where(own, v, zero)
        for t in range(CHUNK):
            rows = slice(t * SSM_CH, (t + 1) * SSM_CH)
            whyt_ref[e, rows, :LANES] = pick(ca[t + 1][0]).astype(BF16)
            whyt_ref[e, rows, LANES:] = pick(ca[t + 1][1]).astype(BF16)
            for ref, k in ((ws_ref, CHUNK - 1 - t), (wsn_ref, n_tok - 1 - t)):
                if k >= 0:
                    ref[e, rows, :LANES] = pick(br * pr[k] - bi * pi[k]).astype(BF16)
                    ref[e, rows, LANES:] = pick(br * pi[k] + bi * pr[k]).astype(BF16)
                else:
                    ref[e, rows, :] = jnp.zeros((SSM_CH, PAIR_COLS), BF16)
        k_row = _dot_nt_split(jnp.concatenate([pick(br), pick(bi)], axis=1), ca_full)
        for s in range(CHUNK):
            m_ref[e, s * SSM_CH:(s + 1) * SSM_CH, :] = _shift_lanes(k_row, s * SSM_CH).astype(BF16)
    sub = lax.broadcasted_iota(jnp.int32, (SUBLANES, LANES), 0)
    qr, qi = [pr[CHUNK]], [pi[CHUNK]]
    for _ in range(SUBLANES - 1):
        qr, qi = qr + [qr[-1] * pr[CHUNK] - qi[-1] * pi[CHUNK]], qi + [qr[-1] * pi[CHUNK] + qi[-1] * pr[CHUNK]]
    for part, q in enumerate((qr, qi)):
        rows = jnp.zeros((SUBLANES, LANES), F32)
        for s in range(SUBLANES):
            rows = jnp.where(sub == s, q[s], rows)
        pw_ref[part] = rows
        for k, shift in enumerate((1, 2, 4)):
            mk_ref[k, part] = jnp.where(sub >= shift, q[shift - 1], 0.0)
    an_ref[0] = jnp.broadcast_to(pr[n_tok], (SUBLANES, LANES))
    an_ref[1] = jnp.broadcast_to(pi[n_tok], (SUBLANES, LANES))


def _s5_prep(lam_re, lam_im, log_dt, b_re, b_im, c_re, c_im, n_tok):
    ab_re, ab_im, bb_re, bb_im = _s5_discretize(lam_re, lam_im, log_dt, b_re, b_im)
    pair = lambda v: v.reshape(N_PAIRS, 2, v.shape[1], SSM_STATE).transpose(0, 2, 1, 3).reshape(
        N_PAIRS, v.shape[1], LANES)
    a = jnp.stack([pair(ab_re[:, None, :]), pair(ab_im[:, None, :])], axis=1)
    c = jnp.stack([pair(c_re), pair(c_im)], axis=1)
    bt = jnp.stack([pair(bb_re.transpose(0, 2, 1)), pair(bb_im.transpose(0, 2, 1))], axis=1)
    blk4 = lambda r: pl.BlockSpec((PREP_PAIRS, 2, r, LANES), lambda q: (q, 0, 0, 0))
    w_spec = pl.BlockSpec((2 * PREP_PAIRS, GROUP_IO, PAIR_COLS), lambda q: (q, 0, 0))
    w_shape = jax.ShapeDtypeStruct((SSM_GROUPS, GROUP_IO, PAIR_COLS), BF16)
    return pl.pallas_call(
        functools.partial(_s5_prep_kernel, n_tok=n_tok),
        grid=(N_PAIRS // PREP_PAIRS,),
        in_specs=[blk4(1), blk4(SSM_CH), blk4(SSM_CH)],
        out_specs=[w_spec, w_spec, w_spec, w_spec,
                   pl.BlockSpec((3, 2, SUBLANES, PREP_PAIRS * LANES), lambda q: (0, 0, 0, q)),
                   pl.BlockSpec((2, SUBLANES, PREP_PAIRS * LANES), lambda q: (0, 0, q)),
                   pl.BlockSpec((2, SUBLANES, PREP_PAIRS * LANES), lambda q: (0, 0, q))],
        out_shape=[w_shape, w_shape, w_shape, w_shape,
                   jax.ShapeDtypeStruct((3, 2, SUBLANES, HALF_COLS), F32),
                   jax.ShapeDtypeStruct((2, SUBLANES, HALF_COLS), F32),
                   jax.ShapeDtypeStruct((2, SUBLANES, HALF_COLS), F32)],
        compiler_params=_params("arbitrary"),
        name="s5_prep",
    )(a, c, bt)


def _group_io(ref, g):
    return ref.at[:, g * GROUP_IO:(g + 1) * GROUP_IO]


def _s5_emit_tile(u_ref, st_scr, m_ref, whyt_ref, dt_ref, g_ref, j):
    for gl in range(GROUPS_PER_TILE):
        g = GROUPS_PER_TILE * j + gl
        q0 = (gl // 2) * PAIR_COLS
        u_g = _group_io(u_ref, g)[...]
        y = (jnp.dot(u_g, m_ref[g], preferred_element_type=F32)
             + lax.dot_general(st_scr[:, q0:q0 + PAIR_COLS].astype(BF16), whyt_ref[g],
                               (((1,), (1,)), ((), ())), preferred_element_type=F32)
             + dt_ref[g] * u_g.astype(F32))
        _group_io(g_ref, g)[...] = _gelu_tanh(y).astype(BF16)


def _s5_local_states(u_ref, ws_ref, j, q):
    g = GROUPS_PER_TILE * j + 2 * q
    return (jnp.dot(_group_io(u_ref, g)[...], ws_ref[g], preferred_element_type=F32)
            + jnp.dot(_group_io(u_ref, g + 1)[...], ws_ref[g + 1], preferred_element_type=F32))


def _cmul_add(ar, ai, hr, hi, xr, xi):
    return ar * hr - ai * hi + xr, ar * hi + ai * hr + xi


def _s5_chain_kernel(u_ref, m_ref, ws_ref, whyt_ref, mk_ref, pw_ref, dt_ref, h0_ref, g_ref, hT_ref,
                     carry_scr, st_scr, *, rows):
    @pl.when(pl.program_id(1) == 0)
    def _():
        carry_scr[...] = jnp.broadcast_to(h0_ref[0], (SUBLANES, STATE_COLS))

    row0 = lax.broadcasted_iota(jnp.int32, (SUBLANES, LANES), 0) == 0
    last = lambda h: jnp.broadcast_to(h[SUBLANES - 1:, :], (SUBLANES, LANES))

    for j in range(N_LANE_TILES):
        for q in range(PAIRS_PER_TILE):
            st_scr[:, q * PAIR_COLS:(q + 1) * PAIR_COLS] = _s5_local_states(u_ref, ws_ref, j, q)

        def step(r, carry):
            r0 = pl.multiple_of(r * SUBLANES, SUBLANES)
            out = []
            for q in range(PAIRS_PER_TILE):
                re_c = slice(q * PAIR_COLS, q * PAIR_COLS + LANES)
                im_c = slice(q * PAIR_COLS + LANES, (q + 1) * PAIR_COLS)
                hc = slice((j * PAIRS_PER_TILE + q) * LANES, (j * PAIRS_PER_TILE + q + 1) * LANES)
                xr = st_scr[pl.ds(r0, SUBLANES), re_c]
                xi = st_scr[pl.ds(r0, SUBLANES), im_c]
                for k, shift in enumerate((1, 2, 4)):
                    xr, xi = _cmul_add(mk_ref[k, 0, :, hc], mk_ref[k, 1, :, hc],
                                       pltpu.roll(xr, shift, 0), pltpu.roll(xi, shift, 0), xr, xi)
                cr, ci = carry[q]
                hr, hi = _cmul_add(pw_ref[0, :, hc], pw_ref[1, :, hc], cr, ci, xr, xi)
                st_scr[pl.ds(r0, SUBLANES), re_c] = jnp.where(row0, cr, pltpu.roll(hr, 1, 0))
                st_scr[pl.ds(r0, SUBLANES), im_c] = jnp.where(row0, ci, pltpu.roll(hi, 1, 0))
                out.append((last(hr), last(hi)))
            return tuple(out)

        c0 = j * TILE_COLS
        init = tuple((carry_scr[:, c0 + q * PAIR_COLS:c0 + q * PAIR_COLS + LANES],
                      carry_scr[:, c0 + q * PAIR_COLS + LANES:c0 + (q + 1) * PAIR_COLS])
                     for q in range(PAIRS_PER_TILE))
        final = lax.fori_loop(0, rows // SUBLANES, step, init, unroll=True)
        for q in range(PAIRS_PER_TILE):
            carry_scr[:, c0 + q * PAIR_COLS:c0 + q * PAIR_COLS + LANES] = final[q][0]
            carry_scr[:, c0 + q * PAIR_COLS + LANES:c0 + (q + 1) * PAIR_COLS] = final[q][1]

        _s5_emit_tile(u_ref, st_scr, m_ref, whyt_ref, dt_ref, g_ref, j)

    hT_ref[0] = carry_scr[...]


def _s5_chain(u16, m, ws, why, mk, pw, d_tiled, h0, batch, n_rows, rows):
    nblk = n_rows // rows
    row = pl.BlockSpec((rows, CHUNK_LANES), lambda b, i: (b * nblk + i, 0))
    return pl.pallas_call(
        functools.partial(_s5_chain_kernel, rows=rows),
        grid=(batch, nblk),
        in_specs=[row, _const_spec(m.shape), _const_spec(ws.shape), _const_spec(why.shape),
                  _const_spec(mk.shape), _const_spec(pw.shape), _const_spec(d_tiled.shape),
                  pl.BlockSpec((1, 1, STATE_COLS), lambda b, i: (b, 0, 0))],
        out_specs=[row, pl.BlockSpec((1, SUBLANES, STATE_COLS), lambda b, i: (b, 0, 0))],
        out_shape=[jax.ShapeDtypeStruct(u16.shape, BF16),
                   jax.ShapeDtypeStruct((batch, SUBLANES, STATE_COLS), F32)],
        scratch_shapes=[pltpu.VMEM((SUBLANES, STATE_COLS), F32), pltpu.VMEM((rows, TILE_COLS), F32)],
        compiler_params=_params("arbitrary", "arbitrary"),
        name="s5_chain",
    )(u16, m, ws, why, mk, pw, d_tiled, h0)


def _s5_rows_kernel(u_ref, m_ref, ws_ref, whyt_ref, an_ref, dt_ref, h0r_ref, h0i_ref, g_ref, hTr_ref, hTi_ref,
                    st_scr):
    for j in range(N_LANE_TILES):
        for q in range(PAIRS_PER_TILE):
            hc = slice((j * PAIRS_PER_TILE + q) * LANES, (j * PAIRS_PER_TILE + q + 1) * LANES)
            h0r, h0i = h0r_ref[hc, :].T, h0i_ref[hc, :].T
            local = _s5_local_states(u_ref, ws_ref, j, q)
            hr, hi = _cmul_add(an_ref[0, :1, hc], an_ref[1, :1, hc], h0r, h0i, local[:, :LANES], local[:, LANES:])
            hTr_ref[hc, :] = hr.T
            hTi_ref[hc, :] = hi.T
            st_scr[:, q * PAIR_COLS:q * PAIR_COLS + LANES] = h0r
            st_scr[:, q * PAIR_COLS + LANES:(q + 1) * PAIR_COLS] = h0i
        _s5_emit_tile(u_ref, st_scr, m_ref, whyt_ref, dt_ref, g_ref, j)


def _s5_rows(u16, m, ws, why, an, d_tiled, h0_re_t, h0_im_t, rows):
    n = u16.shape[0]
    once = pl.Buffered(1)
    row = pl.BlockSpec((rows, CHUNK_LANES), lambda i: (i, 0), pipeline_mode=once)
    col = pl.BlockSpec((HALF_COLS, rows), lambda i: (0, i), pipeline_mode=once)
    state = jax.ShapeDtypeStruct((HALF_COLS, n), F32)
    return pl.pallas_call(
        _s5_rows_kernel,
        grid=(n // rows,),
        in_specs=[row, _const_spec(m.shape), _const_spec(ws.shape), _const_spec(why.shape),
                  _const_spec(an.shape), _const_spec(d_tiled.shape), col, col],
        out_specs=[row, col, col],
        out_shape=[jax.ShapeDtypeStruct(u16.shape, BF16), state, state],
        scratch_shapes=[pltpu.VMEM((rows, TILE_COLS), F32)],
        compiler_params=_params("arbitrary"),
        name="s5_rows",
    )(u16, m, ws, why, an, d_tiled, h0_re_t, h0_im_t)


def _post_kernel(g_ref, ma_ref, gs_ref, x_ref, permt_ref, wglu_ref, wout_ref, gffn_ref, wup_ref, wdown_ref,
                 gfin_ref, out_ref, *, n_tok):
    g = _load_chunk_rows(g_ref, permt_ref, n_tok)
    glu = jnp.dot(g, wglu_ref[...], preferred_element_type=F32)
    s_out = glu[:, :D_MODEL] * jax.nn.sigmoid(glu[:, D_MODEL:])
    merged = ma_ref[...] + gs_ref[...].astype(F32) * s_out
    x1 = x_ref[...] + jnp.dot(merged.astype(BF16), wout_ref[...], preferred_element_type=F32)
    h2 = _rmsnorm(x1, gffn_ref[...]).astype(BF16)
    x2 = x1
    for c in range(0, D_FF, FF_CHUNK):
        up = jnp.dot(h2, wup_ref[:, c:c + FF_CHUNK], preferred_element_type=F32)
        act = jnp.square(jnp.maximum(up, 0.0)).astype(BF16)
        x2 = x2 + jnp.dot(act, wdown_ref[c:c + FF_CHUNK, :], preferred_element_type=F32)
    out_ref[...] = _rmsnorm(x2, gfin_ref[...])


def _post(g, ma, gs, x, permt, w_glu, w_out, g_ffn, w_up, w_down, g_final, tm, n_tok):
    n = x.shape[0]
    row = pl.BlockSpec((tm, D_MODEL), lambda i: (i, 0))
    g_spec = pl.BlockSpec((tm // n_tok, CHUNK_LANES), lambda i: (i, 0))
    return pl.pallas_call(
        functools.partial(_post_kernel, n_tok=n_tok),
        grid=(n // tm,),
        in_specs=[g_spec, row, row, row, _const_spec(permt.shape), _const_spec(w_glu.shape),
                  _const_spec(w_out.shape), _const_spec((1, D_MODEL)), _const_spec(w_up.shape),
                  _const_spec(w_down.shape), _const_spec((1, D_MODEL))],
        out_specs=row,
        out_shape=jax.ShapeDtypeStruct((n, D_MODEL), F32),
        compiler_params=_params("arbitrary"),
        name="post",
    )(g, ma, gs, x, permt, w_glu, w_out, g_ffn, w_up, w_down, g_final)


def _tile(n, pref):
    t = pref
    while n % t:
        t //= 2
    return t


def kernel(x_prompt, x_sample, cache_k, cache_v, state_ssm_re, state_ssm_im, g_mix, w_in, attn_sinks,
           w_attn_o, ssm_lambda_re, ssm_lambda_im, ssm_log_dt, ssm_b_re, ssm_b_im, ssm_c_re, ssm_c_im,
           ssm_d, w_glu, w_out, g_ffn, w_up, w_down, g_final):
    batch, seq, _ = x_prompt.shape
    db, dec_seq, _ = x_sample.shape
    assert w_in.shape[0] == 1, "one layer"
    assert seq % PERM_ROWS == 0 and dec_seq in (1, 2, 4, 8) and (db * dec_seq) % PERM_ROWS == 0

    vec = lambda v: v.reshape(1, D_MODEL).astype(F32)
    w_in_b = w_in[0].astype(BF16)
    sinks = attn_sinks[0].astype(F32)
    s5p = (ssm_lambda_re[0], ssm_lambda_im[0], ssm_log_dt[0], ssm_b_re[0], ssm_b_im[0], ssm_c_re[0], ssm_c_im[0])
    d_skip = jnp.tile(ssm_d[0].astype(F32).reshape(SSM_GROUPS, 1, SSM_CH), (1, 1, CHUNK))
    perm = _chunk_perm(CHUNK)

    xp = x_prompt.reshape(batch * seq, D_MODEL)
    q, kv, u16, ga, gs, wo_b, w_glu_b, w_out_b, w_up_b, w_down_b = _proj(
        xp, vec(g_mix[0]), w_in_b, perm, _tile(batch * seq, 1024), CHUNK,
        cast=(w_attn_o[0], w_glu[0], w_out[0], w_up[0], w_down[0]))
    post_w = (w_glu_b, w_out_b, vec(g_ffn[0]), w_up_b, w_down_b, vec(g_final))
    ma = _attn_prompt(sinks, q, kv, ga, wo_b, batch, seq, _tile(seq, 2048))
    m, ws, ws_n, whyt, mk, pw, an = _s5_prep(*s5p, dec_seq)
    n_rows = seq // CHUNK
    g16, hT = _s5_chain(u16, m, ws, whyt, mk, pw, d_skip, jnp.zeros((batch, 1, STATE_COLS), F32),
                        batch, n_rows, _tile(n_rows, 128))
    y_prompt = _post(g16, ma, gs, xp, perm.T, *post_w, _tile(batch * seq, 512), CHUNK).reshape(batch, seq, D_MODEL)
    kv_last = kv.reshape(batch, seq, 2 * D_KV)[:, seq - WINDOW:]
    k_prompt = kv_last[..., :D_KV].reshape(1, batch, WINDOW, N_KV_HEADS, HEAD_DIM)
    v_prompt = kv_last[..., D_KV:].reshape(1, batch, WINDOW, N_KV_HEADS, HEAD_DIM)
    hr, hi = _state_split(hT[:, 0])
    ssm_re_prompt, ssm_im_prompt = hr[None], hi[None]

    ns = db * dec_seq
    xs = x_sample.reshape(ns, D_MODEL)
    perm_s = _chunk_perm(dec_seq)
    q, kv, u16, ga, gs = _proj(xs, vec(g_mix[0]), w_in_b, perm_s, _tile(ns, 512), dec_seq)
    tpad = SUBLANES
    pad3 = lambda v: jnp.pad(v.reshape(db, dec_seq, -1).astype(F32), ((0, 0), (0, tpad - dec_seq), (0, 0)))
    ga_pad = pad3(ga).astype(BF16).reshape(db * tpad, D_MODEL)
    window_last = lambda c: c[0].astype(F32).transpose(0, 2, 3, 1).reshape(db, D_KV, WINDOW)
    ma_pad, k_new, v_new = _attn_sample(
        sinks, pad3(q), pad3(kv), window_last(cache_k), window_last(cache_v), ga_pad, wo_b, _tile(db, 16), dec_seq)
    ma = ma_pad.reshape(db, tpad, D_MODEL)[:, :dec_seq].reshape(ns, D_MODEL)
    to_cols = lambda s: s[0].astype(F32).transpose(1, 2, 0).reshape(HALF_COLS, db)
    from_cols = lambda h: h.reshape(SSM_GROUPS, SSM_STATE, db).transpose(2, 0, 1)[None]
    g16, hT_re, hT_im = _s5_rows(u16, m, ws_n, whyt, an, d_skip, to_cols(state_ssm_re), to_cols(state_ssm_im),
                                 _tile(db, LANES))
    y_sample = _post(g16, ma, gs, xs, perm_s.T, *post_w, PERM_ROWS, dec_seq).reshape(db, dec_seq, D_MODEL)
    k_sample = k_new.reshape(1, db, WINDOW, N_KV_HEADS, HEAD_DIM)
    v_sample = v_new.reshape(1, db, WINDOW, N_KV_HEADS, HEAD_DIM)
    ssm_re_sample, ssm_im_sample = from_cols(hT_re), from_cols(hT_im)

    return (y_prompt, y_sample, k_prompt, v_prompt, ssm_re_prompt, ssm_im_prompt,
            k_sample, v_sample, ssm_re_sample, ssm_im_sample)
```

```python
import functools
import math

import jax
import jax.numpy as jnp
from jax import lax
from jax.experimental import pallas as pl
from jax.experimental.pallas import tpu as pltpu

F32 = jnp.float32
BF16 = jnp.bfloat16

D_MODEL = 1024
HEAD_DIM = 64
N_HEADS = 16
N_KV_HEADS = 4
WINDOW = 128
D_KV = N_KV_HEADS * HEAD_DIM
SSM_CH = 16
SSM_GROUPS = 64
SSM_STATE = 64
D_FF = 4 * D_MODEL
FF_CHUNK = 1024
RMS_EPS = 1e-5
LOG2E = 1.4426950408889634
Q_SCALE = HEAD_DIM ** -0.5 * LOG2E

LANES = 128
SUBLANES = 8
BF16_ROWS = 16
N_LANE_TILES = D_MODEL // LANES
GROUPS_PER_TILE = LANES // SSM_CH
CHUNK = 16
CHUNK_LANES = CHUNK * D_MODEL
GROUP_IO = CHUNK * SSM_CH
N_PAIRS = SSM_GROUPS // 2
PAIRS_PER_TILE = GROUPS_PER_TILE // 2
PAIR_COLS = 2 * LANES
TILE_COLS = PAIRS_PER_TILE * PAIR_COLS
STATE_COLS = N_PAIRS * PAIR_COLS
HALF_COLS = STATE_COLS // 2
PERM_ROWS = CHUNK * BF16_ROWS
PREP_PAIRS = 4
VMEM_LIMIT = 56 * 1024 * 1024

_Q0, _KV0, _U0, _GA0, _GS0, _END = 0, 1024, 1536, 2560, 3584, 4608


def _gelu_tanh(x):
    k1 = -2.0 * math.sqrt(2.0 / math.pi) * LOG2E
    return x / (1.0 + jnp.exp2(x * (k1 + (0.044715 * k1) * (x * x))))


def _rmsnorm(x, g):
    return x * lax.rsqrt(jnp.mean(x * x, axis=-1, keepdims=True) + RMS_EPS) * g


def _params(*sem):
    return pltpu.CompilerParams(dimension_semantics=sem, vmem_limit_bytes=VMEM_LIMIT)


def _const_spec(shape):
    nd = len(shape)
    return pl.BlockSpec(shape, lambda *_: (0,) * nd, pipeline_mode=pl.Buffered(1))


def _chunk_perm(n_tok):
    n_chunks = PERM_ROWS // n_tok
    r = jnp.arange(PERM_ROWS)
    tok = (r % n_chunks) * n_tok + r // n_chunks
    return (tok[:, None] == jnp.arange(PERM_ROWS)[None, :]).astype(BF16)


def _piece_transpose(cols, masks):
    for d, msk in zip((4, 2, 1), masks):
        new = list(cols)
        for v in range(GROUPS_PER_TILE):
            if v & d == 0:
                a, b = cols[v], cols[v + d]
                new[v] = jnp.where(msk, pltpu.roll(b, SSM_CH * d, 1), a)
                new[v + d] = jnp.where(msk, b, pltpu.roll(a, LANES - SSM_CH * d, 1))
        cols = new
    return cols


def _piece_masks(rows):
    piece = lax.broadcasted_iota(jnp.int32, (rows, LANES), 1) // SSM_CH
    return [(piece & d) != 0 for d in (4, 2, 1)]


def _store_chunk_rows(x, perm_ref, out_ref, n_tok):
    n_chunks = PERM_ROWS // n_tok
    masks = _piece_masks(n_chunks)
    zero = jnp.zeros((n_chunks, LANES), F32)
    for hb in range(x.shape[0] // PERM_ROWS):
        rows = slice(hb * n_chunks, (hb + 1) * n_chunks)
        xp = jnp.dot(perm_ref[...], x[hb * PERM_ROWS:(hb + 1) * PERM_ROWS],
                     preferred_element_type=F32)
        for j in range(N_LANE_TILES):
            for hf in range(2):
                live = 8 * hf < n_tok
                cols = [xp[t * n_chunks:(t + 1) * n_chunks, j * LANES:(j + 1) * LANES] if t < n_tok else zero
                        for t in range(8 * hf, 8 * hf + 8)]
                for gl, col in enumerate(_piece_transpose(cols, masks) if live else cols):
                    c0 = (GROUPS_PER_TILE * j + gl) * GROUP_IO + hf * LANES
                    out_ref[rows, c0:c0 + LANES] = col.astype(BF16)


def _load_chunk_rows(in_ref, permt_ref, n_tok):
    n_chunks = PERM_ROWS // n_tok
    masks = _piece_masks(n_chunks)
    blocks = []
    for hb in range(in_ref.shape[0] // n_chunks):
        rows = slice(hb * n_chunks, (hb + 1) * n_chunks)
        tiles = [[None] * N_LANE_TILES for _ in range(n_tok)]
        for j in range(N_LANE_TILES):
            for hf in range(-(-n_tok // 8)):
                cols = []
                for gl in range(GROUPS_PER_TILE):
                    c0 = (GROUPS_PER_TILE * j + gl) * GROUP_IO + hf * LANES
                    cols.append(in_ref[rows, c0:c0 + LANES].astype(F32))
                for k, col in enumerate(_piece_transpose(cols, masks)):
                    if 8 * hf + k < n_tok:
                        tiles[8 * hf + k][j] = col.astype(BF16)
        xp = jnp.concatenate([jnp.concatenate(row, axis=1) for row in tiles], axis=0)
        blocks.append(jnp.dot(permt_ref[...], xp, preferred_element_type=F32).astype(BF16))
    return blocks[0] if len(blocks) == 1 else jnp.concatenate(blocks, axis=0)


def _proj_kernel(x_ref, g_ref, w_ref, perm_ref, *refs, n_tok, n_cast):
    q_ref, kv_ref, u_ref, ga_ref, gs_ref = refs[n_cast:n_cast + 5]
    h = _rmsnorm(x_ref[...], g_ref[...]).astype(BF16)

    def seg(lo, hi):
        return jnp.dot(h, w_ref[:, lo:hi], preferred_element_type=F32)

    q_ref[...] = (seg(_Q0, _KV0) * Q_SCALE).astype(BF16)
    kv_ref[...] = seg(_KV0, _U0)
    _store_chunk_rows(seg(_U0, _GA0).astype(BF16), perm_ref, u_ref, n_tok)
    ga_ref[...] = jax.nn.sigmoid(seg(_GA0, _GS0)).astype(BF16)
    gs_ref[...] = jax.nn.sigmoid(seg(_GS0, _END)).astype(BF16)
    for src, dst in zip(refs[:n_cast], refs[n_cast + 5:]):
        dst[...] = src[...].astype(BF16)


def _proj(x, g_mix, w_in, perm, tm, n_tok, cast=()):
    n = x.shape[0]
    steps = n // tm
    row = lambda w: pl.BlockSpec((tm, w), lambda i: (i, 0))
    slab = lambda w: pl.BlockSpec((w.shape[0] // steps, w.shape[1]), lambda i: (i, 0))
    assert all(w.shape[0] % (steps * BF16_ROWS) == 0 for w in cast)
    return pl.pallas_call(
        functools.partial(_proj_kernel, n_tok=n_tok, n_cast=len(cast)),
        grid=(steps,),
        in_specs=[row(D_MODEL), _const_spec((1, D_MODEL)), _const_spec(w_in.shape), _const_spec(perm.shape)]
        + [slab(w) for w in cast],
        out_specs=[row(D_MODEL), row(2 * D_KV), pl.BlockSpec((tm // n_tok, CHUNK_LANES), lambda i: (i, 0)),
                   row(D_MODEL), row(D_MODEL)] + [slab(w) for w in cast],
        out_shape=[jax.ShapeDtypeStruct((n, D_MODEL), BF16),
                   jax.ShapeDtypeStruct((n, 2 * D_KV), F32),
                   jax.ShapeDtypeStruct((n // n_tok, CHUNK_LANES), BF16),
                   jax.ShapeDtypeStruct((n, D_MODEL), BF16),
                   jax.ShapeDtypeStruct((n, D_MODEL), BF16)]
        + [jax.ShapeDtypeStruct(w.shape, BF16) for w in cast],
        compiler_params=_params("arbitrary"),
        name="proj",
    )(x, g_mix, w_in, perm, *cast)


def _dup_heads(tile):
    lo = lax.broadcasted_iota(jnp.int32, tile.shape, tile.ndim - 1) < HEAD_DIM
    rolled = pltpu.roll(tile, HEAD_DIM, tile.ndim - 1)
    return (jnp.where(lo, tile, rolled).astype(BF16), jnp.where(lo, rolled, tile).astype(BF16))


def _sink_softmax(s, sink):
    sink2 = sink * LOG2E
    m = jnp.maximum(jnp.max(s, axis=-1, keepdims=True), sink2)
    p = jnp.exp2(s - m)
    denom = jnp.sum(p, axis=-1, keepdims=True) + jnp.exp2(sink2 - m)
    return p.astype(BF16), 1.0 / denom


def _attn_prompt_kernel(sinks_ref, q_ref, kvc_ref, kvp_ref, ga_ref, wo_ref, out_ref,
                        kd_scr, vd_scr, bias_scr, attn_scr, *, tq):
    i = pl.program_id(1)
    kv_full = jnp.concatenate([kvp_ref[...], kvc_ref[...]], axis=0)
    for t in range(2):
        ke, ko = _dup_heads(kv_full[:, t * LANES:(t + 1) * LANES])
        ve, vo = _dup_heads(kv_full[:, D_KV + t * LANES:D_KV + (t + 1) * LANES])
        kd_scr[2 * t], kd_scr[2 * t + 1] = ke, ko
        vd_scr[2 * t], vd_scr[2 * t + 1] = ve, vo

    qi = lax.broadcasted_iota(jnp.int32, (WINDOW, 2 * WINDOW), 0)
    si = lax.broadcasted_iota(jnp.int32, (WINDOW, 2 * WINDOW), 1)
    band = (si > qi) & (si <= qi + WINDOW)
    in_block = si >= WINDOW
    lo = lax.broadcasted_iota(jnp.int32, (WINDOW, LANES), 1) < HEAD_DIM

    def attend(jb):
        q0 = jb * WINDOW
        has_prev = (i > 0) | (jb > 0)
        bias_scr[...] = jnp.where(band & (in_block | has_prev), 0.0, -jnp.inf)

        def head_pair(hp, c):
            j = hp // 2
            c0 = pl.multiple_of(hp * LANES, LANES)
            kd = kd_scr[j, pl.ds(q0, 2 * WINDOW), :]
            vd = vd_scr[j, pl.ds(q0, 2 * WINDOW), :]
            qp = q_ref[pl.ds(q0, WINDOW), pl.ds(c0, LANES)]
            outs = []
            for par in range(2):
                qm = jnp.where(lo if par == 0 else ~lo, qp, jnp.zeros_like(qp))
                s = lax.dot_general(qm, kd, (((1,), (1,)), ((), ())), preferred_element_type=F32)
                p, inv = _sink_softmax(s + bias_scr[...], sinks_ref[2 * hp + par])
                outs.append(jnp.dot(p, vd, preferred_element_type=F32) * inv)
            attn_scr[pl.ds(q0, WINDOW), pl.ds(c0, LANES)] = jnp.where(lo, outs[0], outs[1]).astype(BF16)
            return c

        lax.fori_loop(0, N_HEADS // 2, head_pair, 0, unroll=8)

    def project(jb):
        rows = pl.ds(jb * WINDOW, WINDOW)
        a_out = jnp.dot(attn_scr[rows, :], wo_ref[...], preferred_element_type=F32)
        out_ref[rows, :] = ga_ref[rows, :].astype(F32) * a_out

    for jb in range(tq // WINDOW):
        attend(jb)
        project(jb)


def _attn_prompt(sinks, q, kv, ga, wo, batch, seq, tq):
    nq = seq // tq
    bpt = tq // WINDOW
    row = lambda w: pl.BlockSpec((tq, w), lambda b, i: (b * nq + i, 0))
    prev = pl.BlockSpec((WINDOW, 2 * D_KV),
                        lambda b, i: (jnp.maximum((b * nq + i) * bpt - 1, 0), 0))
    return pl.pallas_call(
        functools.partial(_attn_prompt_kernel, tq=tq),
        grid=(batch, nq),
        in_specs=[pl.BlockSpec(memory_space=pltpu.SMEM), row(D_MODEL), row(2 * D_KV), prev,
                  row(D_MODEL), _const_spec(wo.shape)],
        out_specs=row(D_MODEL),
        out_shape=jax.ShapeDtypeStruct((batch * seq, D_MODEL), F32),
        scratch_shapes=[pltpu.VMEM((N_KV_HEADS, WINDOW + tq, LANES), BF16),
                        pltpu.VMEM((N_KV_HEADS, WINDOW + tq, LANES), BF16),
                        pltpu.VMEM((WINDOW, 2 * WINDOW), F32),
                        pltpu.VMEM((tq, D_MODEL), BF16)],
        compiler_params=_params("arbitrary", "arbitrary"),
        name="attn_prompt",
    )(sinks, q, kv, kv, ga, wo)


def _attn_sample_kernel(sinks_ref, q_ref, kvn_ref, ck_ref, cv_ref, ga_ref, wo_ref,
                        out_ref, ko_ref, vo_ref, attn_scr, *, bb, tpad, dec_seq):
    nk = WINDOW + tpad
    lo3 = lax.broadcasted_iota(jnp.int32, (bb, tpad, LANES), 2) < HEAD_DIM
    row = lax.broadcasted_iota(jnp.int32, (4 * tpad, nk), 0)
    si = lax.broadcasted_iota(jnp.int32, (4 * tpad, nk), 1)
    tq = row % tpad
    valid = (si > tq) & (si <= tq + WINDOW) & (si < WINDOW + dec_seq)
    hrow = lax.broadcasted_iota(jnp.int32, (4 * tpad, 1), 0) // tpad

    keys_first = lambda ref, sl: jnp.stack([ref[b, sl, :].T for b in range(bb)])
    for t in range(2):
        sl = slice(t * LANES, (t + 1) * LANES)
        kk = jnp.concatenate([keys_first(ck_ref, sl), kvn_ref[:, :, sl]], axis=1)
        vv = jnp.concatenate([keys_first(cv_ref, sl),
                              kvn_ref[:, :, D_KV + t * LANES:D_KV + (t + 1) * LANES]], axis=1)
        ko_ref[:, :, sl] = kk[:, dec_seq:dec_seq + WINDOW, :]
        vo_ref[:, :, sl] = vv[:, dec_seq:dec_seq + WINDOW, :]
        kds = _dup_heads(kk)
        vds = _dup_heads(vv)
        for par_kv in range(2):
            j = 2 * t + par_kv
            kd, vd = kds[par_kv], vds[par_kv]
            parts = []
            for r in range(2):
                c0 = j * 2 * LANES + r * LANES
                qp = q_ref[:, :, c0:c0 + LANES]
                parts += [jnp.where(lo3, qp, 0.0), jnp.where(lo3, 0.0, qp)]
            lhs = jnp.concatenate(parts, axis=1).astype(BF16)
            s = jnp.einsum('bqd,bkd->bqk', lhs, kd, preferred_element_type=F32)
            sink = jnp.zeros((4 * tpad, 1), F32)
            for g in range(4):
                sink = jnp.where(hrow == g, sinks_ref[4 * j + g], sink)
            p, inv = _sink_softmax(jnp.where(valid[None], s, -jnp.inf), sink[None])
            o = jnp.einsum('bqk,bkd->bqd', p, vd, preferred_element_type=F32) * inv
            for r in range(2):
                c0 = j * 2 * LANES + r * LANES
                o_even = o[:, (2 * r) * tpad:(2 * r + 1) * tpad, :]
                o_odd = o[:, (2 * r + 1) * tpad:(2 * r + 2) * tpad, :]
                attn_scr[:, :, c0:c0 + LANES] = jnp.where(lo3, o_even, o_odd)

    attn = attn_scr[...].reshape(bb * tpad, D_MODEL).astype(BF16)
    a_out = jnp.dot(attn, wo_ref[...], preferred_element_type=F32)
    out_ref[...] = ga_ref[...].astype(F32) * a_out


def _attn_sample(sinks, q3, kvn3, ck, cv, ga2, wo, bb, dec_seq):
    db, tpad, _ = q3.shape
    blk3 = lambda r, w: pl.BlockSpec((bb, r, w), lambda i: (i, 0, 0))
    row = pl.BlockSpec((bb * tpad, D_MODEL), lambda i: (i, 0))
    return pl.pallas_call(
        functools.partial(_attn_sample_kernel, bb=bb, tpad=tpad, dec_seq=dec_seq),
        grid=(db // bb,),
        in_specs=[pl.BlockSpec(memory_space=pltpu.SMEM), blk3(tpad, D_MODEL), blk3(tpad, 2 * D_KV),
                  blk3(D_KV, WINDOW), blk3(D_KV, WINDOW), row, _const_spec(wo.shape)],
        out_specs=[row, blk3(WINDOW, D_KV), blk3(WINDOW, D_KV)],
        out_shape=[jax.ShapeDtypeStruct((db * tpad, D_MODEL), F32),
                   jax.ShapeDtypeStruct((db, WINDOW, D_KV), F32),
                   jax.ShapeDtypeStruct((db, WINDOW, D_KV), F32)],
        scratch_shapes=[pltpu.VMEM((bb, tpad, D_MODEL), F32)],
        compiler_params=_params("arbitrary"),
        name="attn_sample",
    )(sinks, q3, kvn3, ck, cv, ga2, wo)


def _s5_discretize(lam_re, lam_im, log_dt, b_re, b_im):
    dt = jnp.exp(log_dt)[:, None]
    decay = jnp.exp(lam_re * dt)
    ab_re = decay * jnp.cos(lam_im * dt)
    ab_im = decay * jnp.sin(lam_im * dt)
    nr, ni = ab_re - 1.0, ab_im
    den = lam_re * lam_re + lam_im * lam_im
    f_re = ((nr * lam_re + ni * lam_im) / den)[..., None]
    f_im = ((ni * lam_re - nr * lam_im) / den)[..., None]
    return ab_re, ab_im, f_re * b_re - f_im * b_im, f_re * b_im + f_im * b_re


def _state_split(h):
    h = h.reshape(h.shape[:-1] + (N_PAIRS, 2, 2, SSM_STATE))
    unpair = lambda a: a.reshape(a.shape[:-3] + (SSM_GROUPS, SSM_STATE))
    return unpair(h[..., 0, :, :]), unpair(h[..., 1, :, :])


def _shift_lanes(x, n):
    a, b = x[:, :LANES], x[:, LANES:]
    lane = lax.broadcasted_iota(jnp.int32, a.shape, 1)
    if n == 0:
        return x
    if n >= LANES:
        r = n - LANES
        hi = a if r == 0 else jnp.where(lane < r, 0.0, pltpu.roll(a, r, 1))
        return jnp.concatenate([jnp.zeros_like(a), hi], axis=1)
    ra, rb = pltpu.roll(a, n, 1), pltpu.roll(b, n, 1)
    return jnp.concatenate([jnp.where(lane < n, 0.0, ra), jnp.where(lane < n, ra, rb)], axis=1)


def _dot_nt_split(lhs, rhs):
    nt = lambda a, b: lax.dot_general(a, b, (((1,), (1,)), ((), ())), preferred_element_type=F32)
    l_hi, r_hi = lhs.astype(BF16), rhs.astype(BF16)
    l_lo = (lhs - l_hi.astype(F32)).astype(BF16)
    r_lo = (rhs - r_hi.astype(F32)).astype(BF16)
    return nt(l_hi, r_hi) + nt(l_hi, r_lo) + nt(l_lo, r_hi)


def _s5_prep_kernel(a_ref, c_ref, bt_ref, m_ref, ws_ref, wsn_ref, whyt_ref, mk_ref, pw_ref, an_ref, *, n_tok):
    for p in range(PREP_PAIRS):
        one, two, lanes = pl.ds(p, 1), pl.ds(2 * p, 2), pl.ds(p * LANES, LANES)
        _s5_prep_pair(a_ref.at[one], c_ref.at[one], bt_ref.at[one], m_ref.at[two], ws_ref.at[two],
                      wsn_ref.at[two], whyt_ref.at[two], mk_ref.at[:, :, :, lanes], pw_ref.at[:, :, lanes],
                      an_ref.at[:, :, lanes], n_tok)


def _s5_prep_pair(a_ref, c_ref, bt_ref, m_ref, ws_ref, wsn_ref, whyt_ref, mk_ref, pw_ref, an_ref, n_tok):
    ar, ai = a_ref[0, 0], a_ref[0, 1]
    cr, ci = c_ref[0, 0], c_ref[0, 1]
    br, bi = bt_ref[0, 0], bt_ref[0, 1]
    pr, pi = [jnp.ones_like(ar)], [jnp.zeros_like(ai)]
    for _ in range(CHUNK):
        pr, pi = pr + [pr[-1] * ar - pi[-1] * ai], pi + [pr[-1] * ai + pi[-1] * ar]
    first = lax.broadcasted_iota(jnp.int32, (SSM_CH, LANES), 1) < SSM_STATE
    ca = [(cr * pr[t] - ci * pi[t], -(cr * pi[t] + ci * pr[t])) for t in range(CHUNK + 1)]
    ca_full = jnp.concatenate([jnp.concatenate(ca[t], axis=1) for t in range(CHUNK)], axis=0)
    zero = jnp.zeros((SSM_CH, LANES), F32)
    for e in range(2):
        own = first if e == 0 else ~first
        pick = lambda v: jnp.where(own, v, zero)
        for t in range(CHUNK):
            rows = slice(t * SSM_CH, (t + 1) * SSM_CH)
            whyt_ref[e, rows, :LANES] = pick(ca[t + 1][0]).astype(BF16)
            whyt_ref[e, rows, LANES:] = pick(ca[t + 1][1]).astype(BF16)
            for ref, k in ((ws_ref, CHUNK - 1 - t), (wsn_ref, n_tok - 1 - t)):
                if k >= 0:
                    ref[e, rows, :LANES] = pick(br * pr[k] - bi * pi[k]).astype(BF16)
                    ref[e, rows, LANES:] = pick(br * pi[k] + bi * pr[k]).astype(BF16)
                else:
                    ref[e, rows, :] = jnp.zeros((SSM_CH, PAIR_COLS), BF16)
        k_row = _dot_nt_split(jnp.concatenate([pick(br), pick(bi)], axis=1), ca_full)
        for s in range(CHUNK):
            m_ref[e, s * SSM_CH:(s + 1) * SSM_CH, :] = _shift_lanes(k_row, s * SSM_CH).astype(BF16)
    sub = lax.broadcasted_iota(jnp.int32, (SUBLANES, LANES), 0)
    qr, qi = [pr[CHUNK]], [pi[CHUNK]]
    for _ in range(SUBLANES - 1):
        qr, qi = qr + [qr[-1] * pr[CHUNK] - qi[-1] * pi[CHUNK]], qi + [qr[-1] * pi[CHUNK] + qi[-1] * pr[CHUNK]]
    for part, q in enumerate((qr, qi)):
        rows = jnp.zeros((SUBLANES, LANES), F32)
        for s in range(SUBLANES):
            rows = jnp.where(sub == s, q[s], rows)
        pw_ref[part] = rows
        for k, shift in enumerate((1, 2, 4)):
            mk_ref[k, part] = jnp.where(sub >= shift, q[shift - 1], 0.0)
    an_ref[0] = jnp.broadcast_to(pr[n_tok], (SUBLANES, LANES))
    an_ref[1] = jnp.broadcast_to(pi[n_tok], (SUBLANES, LANES))


def _s5_prep(lam_re, lam_im, log_dt, b_re, b_im, c_re, c_im, n_tok):
    ab_re, ab_im, bb_re, bb_im = _s5_discretize(lam_re, lam_im, log_dt, b_re, b_im)
    pair = lambda v: v.reshape(N_PAIRS, 2, v.shape[1], SSM_STATE).transpose(0, 2, 1, 3).reshape(
        N_PAIRS, v.shape[1], LANES)
    a = jnp.stack([pair(ab_re[:, None, :]), pair(ab_im[:, None, :])], axis=1)
    c = jnp.stack([pair(c_re), pair(c_im)], axis=1)
    bt = jnp.stack([pair(bb_re.transpose(0, 2, 1)), pair(bb_im.transpose(0, 2, 1))], axis=1)
    blk4 = lambda r: pl.BlockSpec((PREP_PAIRS, 2, r, LANES), lambda q: (q, 0, 0, 0))
    w_spec = pl.BlockSpec((2 * PREP_PAIRS, GROUP_IO, PAIR_COLS), lambda q: (q, 0, 0))
    w_shape = jax.ShapeDtypeStruct((SSM_GROUPS, GROUP_IO, PAIR_COLS), BF16)
    return pl.pallas_call(
        functools.partial(_s5_prep_kernel, n_tok=n_tok),
        grid=(N_PAIRS // PREP_PAIRS,),
        in_specs=[blk4(1), blk4(SSM_CH), blk4(SSM_CH)],
        out_specs=[w_spec, w_spec, w_spec, w_spec,
                   pl.BlockSpec((3, 2, SUBLANES, PREP_PAIRS * LANES), lambda q: (0, 0, 0, q)),
                   pl.BlockSpec((2, SUBLANES, PREP_PAIRS * LANES), lambda q: (0, 0, q)),
                   pl.BlockSpec((2, SUBLANES, PREP_PAIRS * LANES), lambda q: (0, 0, q))],
        out_shape=[w_shape, w_shape, w_shape, w_shape,
                   jax.ShapeDtypeStruct((3, 2, SUBLANES, HALF_COLS), F32),
                   jax.ShapeDtypeStruct((2, SUBLANES, HALF_COLS), F32),
                   jax.ShapeDtypeStruct((2, SUBLANES, HALF_COLS), F32)],
        compiler_params=_params("arbitrary"),
        name="s5_prep",
    )(a, c, bt)


def _group_io(ref, g):
    return ref.at[:, g * GROUP_IO:(g + 1) * GROUP_IO]


def _s5_emit_tile(u_ref, st_scr, m_ref, whyt_ref, dt_ref, g_ref, j):
    for gl in range(GROUPS_PER_TILE):
        g = GROUPS_PER_TILE * j + gl
        q0 = (gl // 2) * PAIR_COLS
        u_g = _group_io(u_ref, g)[...]
        y = (jnp.dot(u_g, m_ref[g], preferred_element_type=F32)
             + lax.dot_general(st_scr[:, q0:q0 + PAIR_COLS].astype(BF16), whyt_ref[g],
                               (((1,), (1,)), ((), ())), preferred_element_type=F32)
             + dt_ref[g] * u_g.astype(F32))
        _group_io(g_ref, g)[...] = _gelu_tanh(y).astype(BF16)


def _s5_local_states(u_ref, ws_ref, j, q):
    g = GROUPS_PER_TILE * j + 2 * q
    return (jnp.dot(_group_io(u_ref, g)[...], ws_ref[g], preferred_element_type=F32)
            + jnp.dot(_group_io(u_ref, g + 1)[...], ws_ref[g + 1], preferred_element_type=F32))


def _cmul_add(ar, ai, hr, hi, xr, xi):
    return ar * hr - ai * hi + xr, ar * hi + ai * hr + xi


def _s5_chain_kernel(u_ref, m_ref, ws_ref, whyt_ref, mk_ref, pw_ref, dt_ref, h0_ref, g_ref, hT_ref,
                     carry_scr, st_scr, *, rows):
    @pl.when(pl.program_id(1) == 0)
    def _():
        carry_scr[...] = jnp.broadcast_to(h0_ref[0], (SUBLANES, STATE_COLS))

    row0 = lax.broadcasted_iota(jnp.int32, (SUBLANES, LANES), 0) == 0
    last = lambda h: jnp.broadcast_to(h[SUBLANES - 1:, :], (SUBLANES, LANES))

    for j in range(N_LANE_TILES):
        for q in range(PAIRS_PER_TILE):
            st_scr[:, q * PAIR_COLS:(q + 1) * PAIR_COLS] = _s5_local_states(u_ref, ws_ref, j, q)

        def step(r, carry):
            r0 = pl.multiple_of(r * SUBLANES, SUBLANES)
            out = []
            for q in range(PAIRS_PER_TILE):
                re_c = slice(q * PAIR_COLS, q * PAIR_COLS + LANES)
                im_c = slice(q * PAIR_COLS + LANES, (q + 1) * PAIR_COLS)
                hc = slice((j * PAIRS_PER_TILE + q) * LANES, (j * PAIRS_PER_TILE + q + 1) * LANES)
                xr = st_scr[pl.ds(r0, SUBLANES), re_c]
                xi = st_scr[pl.ds(r0, SUBLANES), im_c]
                for k, shift in enumerate((1, 2, 4)):
                    xr, xi = _cmul_add(mk_ref[k, 0, :, hc], mk_ref[k, 1, :, hc],
                                       pltpu.roll(xr, shift, 0), pltpu.roll(xi, shift, 0), xr, xi)
                cr, ci = carry[q]
                hr, hi = _cmul_add(pw_ref[0, :, hc], pw_ref[1, :, hc], cr, ci, xr, xi)
                st_scr[pl.ds(r0, SUBLANES), re_c] = jnp.where(row0, cr, pltpu.roll(hr, 1, 0))
                st_scr[pl.ds(r0, SUBLANES), im_c] = jnp.where(row0, ci, pltpu.roll(hi, 1, 0))
                out.append((last(hr), last(hi)))
            return tuple(out)

        c0 = j * TILE_COLS
        init = tuple((carry_scr[:, c0 + q * PAIR_COLS:c0 + q * PAIR_COLS + LANES],
                      carry_scr[:, c0 + q * PAIR_COLS + LANES:c0 + (q + 1) * PAIR_COLS])
                     for q in range(PAIRS_PER_TILE))
        final = lax.fori_loop(0, rows // SUBLANES, step, init, unroll=True)
        for q in range(PAIRS_PER_TILE):
            carry_scr[:, c0 + q * PAIR_COLS:c0 + q * PAIR_COLS + LANES] = final[q][0]
            carry_scr[:, c0 + q * PAIR_COLS + LANES:c0 + (q + 1) * PAIR_COLS] = final[q][1]

        _s5_emit_tile(u_ref, st_scr, m_ref, whyt_ref, dt_ref, g_ref, j)

    hT_ref[0] = carry_scr[...]


def _s5_chain(u16, m, ws, why, mk, pw, d_tiled, h0, batch, n_rows, rows):
    nblk = n_rows // rows
    row = pl.BlockSpec((rows, CHUNK_LANES), lambda b, i: (b * nblk + i, 0))
    return pl.pallas_call(
        functools.partial(_s5_chain_kernel, rows=rows),
        grid=(batch, nblk),
        in_specs=[row, _const_spec(m.shape), _const_spec(ws.shape), _const_spec(why.shape),
                  _const_spec(mk.shape), _const_spec(pw.shape), _const_spec(d_tiled.shape),
                  pl.BlockSpec((1, 1, STATE_COLS), lambda b, i: (b, 0, 0))],
        out_specs=[row, pl.BlockSpec((1, SUBLANES, STATE_COLS), lambda b, i: (b, 0, 0))],
        out_shape=[jax.ShapeDtypeStruct(u16.shape, BF16),
                   jax.ShapeDtypeStruct((batch, SUBLANES, STATE_COLS), F32)],
        scratch_shapes=[pltpu.VMEM((SUBLANES, STATE_COLS), F32), pltpu.VMEM((rows, TILE_COLS), F32)],
        compiler_params=_params("arbitrary", "arbitrary"),
        name="s5_chain",
    )(u16, m, ws, why, mk, pw, d_tiled, h0)


def _s5_rows_kernel(u_ref, m_ref, ws_ref, whyt_ref, an_ref, dt_ref, h0r_ref, h0i_ref, g_ref, hTr_ref, hTi_ref,
                    st_scr):
    for j in range(N_LANE_TILES):
        for q in range(PAIRS_PER_TILE):
            hc = slice((j * PAIRS_PER_TILE + q) * LANES, (j * PAIRS_PER_TILE + q + 1) * LANES)
            h0r, h0i = h0r_ref[hc, :].T, h0i_ref[hc, :].T
            local = _s5_local_states(u_ref, ws_ref, j, q)
            hr, hi = _cmul_add(an_ref[0, :1, hc], an_ref[1, :1, hc], h0r, h0i, local[:, :LANES], local[:, LANES:])
            hTr_ref[hc, :] = hr.T
            hTi_ref[hc, :] = hi.T
            st_scr[:, q * PAIR_COLS:q * PAIR_COLS + LANES] = h0r
            st_scr[:, q * PAIR_COLS + LANES:(q + 1) * PAIR_COLS] = h0i
        _s5_emit_tile(u_ref, st_scr, m_ref, whyt_ref, dt_ref, g_ref, j)


def _s5_rows(u16, m, ws, why, an, d_tiled, h0_re_t, h0_im_t, rows):
    n = u16.shape[0]
    once = pl.Buffered(1)
    row = pl.BlockSpec((rows, CHUNK_LANES), lambda i: (i, 0), pipeline_mode=once)
    col = pl.BlockSpec((HALF_COLS, rows), lambda i: (0, i), pipeline_mode=once)
    state = jax.ShapeDtypeStruct((HALF_COLS, n), F32)
    return pl.pallas_call(
        _s5_rows_kernel,
        grid=(n // rows,),
        in_specs=[row, _const_spec(m.shape), _const_spec(ws.shape), _const_spec(why.shape),
                  _const_spec(an.shape), _const_spec(d_tiled.shape), col, col],
        out_specs=[row, col, col],
        out_shape=[jax.ShapeDtypeStruct(u16.shape, BF16), state, state],
        scratch_shapes=[pltpu.VMEM((rows, TILE_COLS), F32)],
        compiler_params=_params("arbitrary"),
        name="s5_rows",
    )(u16, m, ws, why, an, d_tiled, h0_re_t, h0_im_t)


def _post_kernel(g_ref, ma_ref, gs_ref, x_ref, permt_ref, wglu_ref, wout_ref, gffn_ref, wup_ref, wdown_ref,
                 gfin_ref, out_ref, *, n_tok):
    g = _load_chunk_rows(g_ref, permt_ref, n_tok)
    glu = jnp.dot(g, wglu_ref[...], preferred_element_type=F32)
    s_out = glu[:, :D_MODEL] * jax.nn.sigmoid(glu[:, D_MODEL:])
    merged = ma_ref[...] + gs_ref[...].astype(F32) * s_out
    x1 = x_ref[...] + jnp.dot(merged.astype(BF16), wout_ref[...], preferred_element_type=F32)
    h2 = _rmsnorm(x1, gffn_ref[...]).astype(BF16)
    x2 = x1
    for c in range(0, D_FF, FF_CHUNK):
        up = jnp.dot(h2, wup_ref[:, c:c + FF_CHUNK], preferred_element_type=F32)
        act = jnp.square(jnp.maximum(up, 0.0)).astype(BF16)
        x2 = x2 + jnp.dot(act, wdown_ref[c:c + FF_CHUNK, :], preferred_element_type=F32)
    out_ref[...] = _rmsnorm(x2, gfin_ref[...])


def _post(g, ma, gs, x, permt, w_glu, w_out, g_ffn, w_up, w_down, g_final, tm, n_tok):
    n = x.shape[0]
    row = pl.BlockSpec((tm, D_MODEL), lambda i: (i, 0))
    g_spec = pl.BlockSpec((tm // n_tok, CHUNK_LANES), lambda i: (i, 0))
    return pl.pallas_call(
        functools.partial(_post_kernel, n_tok=n_tok),
        grid=(n // tm,),
        in_specs=[g_spec, row, row, row, _const_spec(permt.shape), _const_spec(w_glu.shape),
                  _const_spec(w_out.shape), _const_spec((1, D_MODEL)), _const_spec(w_up.shape),
                  _const_spec(w_down.shape), _const_spec((1, D_MODEL))],
        out_specs=row,
        out_shape=jax.ShapeDtypeStruct((n, D_MODEL), F32),
        compiler_params=_params("arbitrary"),
        name="post",
    )(g, ma, gs, x, permt, w_glu, w_out, g_ffn, w_up, w_down, g_final)


def _tile(n, pref):
    t = pref
    while n % t:
        t //= 2
    return t


def kernel(x_prompt, x_sample, cache_k, cache_v, state_ssm_re, state_ssm_im, g_mix, w_in, attn_sinks,
           w_attn_o, ssm_lambda_re, ssm_lambda_im, ssm_log_dt, ssm_b_re, ssm_b_im, ssm_c_re, ssm_c_im,
           ssm_d, w_glu, w_out, g_ffn, w_up, w_down, g_final):
    batch, seq, _ = x_prompt.shape
    db, dec_seq, _ = x_sample.shape
    assert w_in.shape[0] == 1, "one layer"
    assert seq % PERM_ROWS == 0 and dec_seq in (1, 2, 4, 8) and (db * dec_seq) % PERM_ROWS == 0

    vec = lambda v: v.reshape(1, D_MODEL).astype(F32)
    w_in_b = w_in[0].astype(BF16)
    sinks = attn_sinks[0].astype(F32)
    s5p = (ssm_lambda_re[0], ssm_lambda_im[0], ssm_log_dt[0], ssm_b_re[0], ssm_b_im[0], ssm_c_re[0], ssm_c_im[0])
    d_skip = jnp.tile(ssm_d[0].astype(F32).reshape(SSM_GROUPS, 1, SSM_CH), (1, 1, CHUNK))
    perm = _chunk_perm(CHUNK)

    xp = x_prompt.reshape(batch * seq, D_MODEL)
    q, kv, u16, ga, gs, wo_b, w_glu_b, w_out_b, w_up_b, w_down_b = _proj(
        xp, vec(g_mix[0]), w_in_b, perm, _tile(batch * seq, 1024), CHUNK,
        cast=(w_attn_o[0], w_glu[0], w_out[0], w_up[0], w_down[0]))
    post_w = (w_glu_b, w_out_b, vec(g_ffn[0]), w_up_b, w_down_b, vec(g_final))
    ma = _attn_prompt(sinks, q, kv, ga, wo_b, batch, seq, _tile(seq, 2048))
    m, ws, ws_n, whyt, mk, pw, an = _s5_prep(*s5p, dec_seq)
    n_rows = seq // CHUNK
    g16, hT = _s5_chain(u16, m, ws, whyt, mk, pw, d_skip, jnp.zeros((batch, 1, STATE_COLS), F32),
                        batch, n_rows, _tile(n_rows, 128))
    y_prompt = _post(g16, ma, gs, xp, perm.T, *post_w, _tile(batch * seq, 512), CHUNK).reshape(batch, seq, D_MODEL)
    kv_last = kv.reshape(batch, seq, 2 * D_KV)[:, seq - WINDOW:]
    k_prompt = kv_last[..., :D_KV].reshape(1, batch, WINDOW, N_KV_HEADS, HEAD_DIM)
    v_prompt = kv_last[..., D_KV:].reshape(1, batch, WINDOW, N_KV_HEADS, HEAD_DIM)
    hr, hi = _state_split(hT[:, 0])
    ssm_re_prompt, ssm_im_prompt = hr[None], hi[None]

    ns = db * dec_seq
    xs = x_sample.reshape(ns, D_MODEL)
    perm_s = _chunk_perm(dec_seq)
    q, kv, u16, ga, gs = _proj(xs, vec(g_mix[0]), w_in_b, perm_s, _tile(ns, 512), dec_seq)
    tpad = SUBLANES
    pad3 = lambda v: jnp.pad(v.reshape(db, dec_seq, -1).astype(F32), ((0, 0), (0, tpad - dec_seq), (0, 0)))
    ga_pad = pad3(ga).astype(BF16).reshape(db * tpad, D_MODEL)
    window_last = lambda c: c[0].astype(F32).transpose(0, 2, 3, 1).reshape(db, D_KV, WINDOW)
    ma_pad, k_new, v_new = _attn_sample(
        sinks, pad3(q), pad3(kv), window_last(cache_k), window_last(cache_v), ga_pad, wo_b, _tile(db, 32), dec_seq)
    ma = ma_pad.reshape(db, tpad, D_MODEL)[:, :dec_seq].reshape(ns, D_MODEL)
    to_cols = lambda s: s[0].astype(F32).transpose(1, 2, 0).reshape(HALF_COLS, db)
    from_cols = lambda h: h.reshape(SSM_GROUPS, SSM_STATE, db).transpose(2, 0, 1)[None]
    g16, hT_re, hT_im = _s5_rows(u16, m, ws_n, whyt, an, d_skip, to_cols(state_ssm_re), to_cols(state_ssm_im),
                                 _tile(db, LANES))
    y_sample = _post(g16, ma, gs, xs, perm_s.T, *post_w, PERM_ROWS, dec_seq).reshape(db, dec_seq, D_MODEL)
    k_sample = k_new.reshape(1, db, WINDOW, N_KV_HEADS, HEAD_DIM)
    v_sample = v_new.reshape(1, db, WINDOW, N_KV_HEADS, HEAD_DIM)
    ssm_re_sample, ssm_im_sample = from_cols(hT_re), from_cols(hT_im)

    return (y_prompt, y_sample, k_prompt, v_prompt, ssm_re_prompt, ssm_im_prompt,
            k_sample, v_sample, ssm_re_sample, ssm_im_sample)
```

```python
import functools
import math

import jax
import jax.numpy as jnp
from jax import lax
from jax.experimental import pallas as pl
from jax.experimental.pallas import tpu as pltpu

F32 = jnp.float32
BF16 = jnp.bfloat16

D_MODEL = 1024
HEAD_DIM = 64
N_HEADS = 16
N_KV_HEADS = 4
WINDOW = 128
D_KV = N_KV_HEADS * HEAD_DIM
SSM_CH = 16
SSM_GROUPS = 64
SSM_STATE = 64
D_FF = 4 * D_MODEL
FF_CHUNK = 1024
RMS_EPS = 1e-5
LOG2E = 1.4426950408889634
Q_SCALE = HEAD_DIM ** -0.5 * LOG2E

LANES = 128
SUBLANES = 8
BF16_ROWS = 16
N_LANE_TILES = D_MODEL // LANES
GROUPS_PER_TILE = LANES // SSM_CH
CHUNK = 16
CHUNK_LANES = CHUNK * D_MODEL
GROUP_IO = CHUNK * SSM_CH
N_PAIRS = SSM_GROUPS // 2
PAIRS_PER_TILE = GROUPS_PER_TILE // 2
PAIR_COLS = 2 * LANES
TILE_COLS = PAIRS_PER_TILE * PAIR_COLS
STATE_COLS = N_PAIRS * PAIR_COLS
HALF_COLS = STATE_COLS // 2
PERM_ROWS = CHUNK * BF16_ROWS
PREP_PAIRS = 4
PROJ_ROWS = 1024
ATTN_ROWS = 2048
POST_ROWS = 512
S5_BLOCK_ROWS = 128
SAMPLE_SEQS = 32
VMEM_LIMIT = 56 * 1024 * 1024

_Q0, _KV0, _U0, _GA0, _GS0, _END = 0, 1024, 1536, 2560, 3584, 4608


def _gelu_tanh(x):
    k1 = -2.0 * math.sqrt(2.0 / math.pi) * LOG2E
    return x / (1.0 + jnp.exp2(x * (k1 + (0.044715 * k1) * (x * x))))


def _rmsnorm(x, g):
    return x * lax.rsqrt(jnp.mean(x * x, axis=-1, keepdims=True) + RMS_EPS) * g


def _params(*sem):
    return pltpu.CompilerParams(dimension_semantics=sem, vmem_limit_bytes=VMEM_LIMIT)


def _const_spec(shape):
    nd = len(shape)
    return pl.BlockSpec(shape, lambda *_: (0,) * nd, pipeline_mode=pl.Buffered(1))


def _chunk_perm(n_tok):
    n_chunks = PERM_ROWS // n_tok
    r = jnp.arange(PERM_ROWS)
    tok = (r % n_chunks) * n_tok + r // n_chunks
    return (tok[:, None] == jnp.arange(PERM_ROWS)[None, :]).astype(BF16)


def _piece_transpose(cols, masks):
    for d, msk in zip((4, 2, 1), masks):
        new = list(cols)
        for v in range(GROUPS_PER_TILE):
            if v & d == 0:
                a, b = cols[v], cols[v + d]
                new[v] = jnp.where(msk, pltpu.roll(b, SSM_CH * d, 1), a)
                new[v + d] = jnp.where(msk, b, pltpu.roll(a, LANES - SSM_CH * d, 1))
        cols = new
    return cols


def _piece_masks(rows):
    piece = lax.broadcasted_iota(jnp.int32, (rows, LANES), 1) // SSM_CH
    return [(piece & d) != 0 for d in (4, 2, 1)]


def _store_chunk_rows(x, perm_ref, out_ref, n_tok):
    n_chunks = PERM_ROWS // n_tok
    masks = _piece_masks(n_chunks)
    zero = jnp.zeros((n_chunks, LANES), F32)
    for hb in range(x.shape[0] // PERM_ROWS):
        rows = slice(hb * n_chunks, (hb + 1) * n_chunks)
        xp = jnp.dot(perm_ref[...], x[hb * PERM_ROWS:(hb + 1) * PERM_ROWS],
                     preferred_element_type=F32)
        for j in range(N_LANE_TILES):
            for hf in range(2):
                live = 8 * hf < n_tok
                cols = [xp[t * n_chunks:(t + 1) * n_chunks, j * LANES:(j + 1) * LANES] if t < n_tok else zero
                        for t in range(8 * hf, 8 * hf + 8)]
                for gl, col in enumerate(_piece_transpose(cols, masks) if live else cols):
                    c0 = (GROUPS_PER_TILE * j + gl) * GROUP_IO + hf * LANES
                    out_ref[rows, c0:c0 + LANES] = col.astype(BF16)


def _load_chunk_rows(in_ref, permt_ref, n_tok):
    n_chunks = PERM_ROWS // n_tok
    masks = _piece_masks(n_chunks)
    blocks = []
    for hb in range(in_ref.shape[0] // n_chunks):
        rows = slice(hb * n_chunks, (hb + 1) * n_chunks)
        tiles = [[None] * N_LANE_TILES for _ in range(n_tok)]
        for j in range(N_LANE_TILES):
            for hf in range(-(-n_tok // 8)):
                cols = []
                for gl in range(GROUPS_PER_TILE):
                    c0 = (GROUPS_PER_TILE * j + gl) * GROUP_IO + hf * LANES
                    cols.append(in_ref[rows, c0:c0 + LANES].astype(F32))
                for k, col in enumerate(_piece_transpose(cols, masks)):
                    if 8 * hf + k < n_tok:
                        tiles[8 * hf + k][j] = col.astype(BF16)
        xp = jnp.concatenate([jnp.concatenate(row, axis=1) for row in tiles], axis=0)
        blocks.append(jnp.dot(permt_ref[...], xp, preferred_element_type=F32).astype(BF16))
    return blocks[0] if len(blocks) == 1 else jnp.concatenate(blocks, axis=0)


def _proj_kernel(x_ref, g_ref, w_ref, perm_ref, *refs, n_tok, n_cast):
    q_ref, kv_ref, u_ref, ga_ref, gs_ref = refs[n_cast:n_cast + 5]
    h = _rmsnorm(x_ref[...], g_ref[...]).astype(BF16)

    def seg(lo, hi):
        return jnp.dot(h, w_ref[:, lo:hi], preferred_element_type=F32)

    q_ref[...] = (seg(_Q0, _KV0) * Q_SCALE).astype(BF16)
    kv_ref[...] = seg(_KV0, _U0)
    _store_chunk_rows(seg(_U0, _GA0).astype(BF16), perm_ref, u_ref, n_tok)
    ga_ref[...] = jax.nn.sigmoid(seg(_GA0, _GS0)).astype(BF16)
    gs_ref[...] = jax.nn.sigmoid(seg(_GS0, _END)).astype(BF16)
    for src, dst in zip(refs[:n_cast], refs[n_cast + 5:]):
        dst[...] = src[...].astype(BF16)


def _proj(x, g_mix, w_in, perm, tm, n_tok, cast=()):
    n = x.shape[0]
    steps = n // tm
    row = lambda w: pl.BlockSpec((tm, w), lambda i: (i, 0))
    slab = lambda w: pl.BlockSpec((w.shape[0] // steps, w.shape[1]), lambda i: (i, 0))
    assert all(w.shape[0] % (steps * BF16_ROWS) == 0 for w in cast)
    return pl.pallas_call(
        functools.partial(_proj_kernel, n_tok=n_tok, n_cast=len(cast)),
        grid=(steps,),
        in_specs=[row(D_MODEL), _const_spec((1, D_MODEL)), _const_spec(w_in.shape), _const_spec(perm.shape)]
        + [slab(w) for w in cast],
        out_specs=[row(D_MODEL), row(2 * D_KV), pl.BlockSpec((tm // n_tok, CHUNK_LANES), lambda i: (i, 0)),
                   row(D_MODEL), row(D_MODEL)] + [slab(w) for w in cast],
        out_shape=[jax.ShapeDtypeStruct((n, D_MODEL), BF16),
                   jax.ShapeDtypeStruct((n, 2 * D_KV), F32),
                   jax.ShapeDtypeStruct((n // n_tok, CHUNK_LANES), BF16),
                   jax.ShapeDtypeStruct((n, D_MODEL), BF16),
                   jax.ShapeDtypeStruct((n, D_MODEL), BF16)]
        + [jax.ShapeDtypeStruct(w.shape, BF16) for w in cast],
        compiler_params=_params("arbitrary"),
        name="proj",
    )(x, g_mix, w_in, perm, *cast)


def _dup_heads(tile):
    lo = lax.broadcasted_iota(jnp.int32, tile.shape, tile.ndim - 1) < HEAD_DIM
    rolled = pltpu.roll(tile, HEAD_DIM, tile.ndim - 1)
    return (jnp.where(lo, tile, rolled).astype(BF16), jnp.where(lo, rolled, tile).astype(BF16))


def _sink_softmax(s, sink):
    sink2 = sink * LOG2E
    m = jnp.maximum(jnp.max(s, axis=-1, keepdims=True), sink2)
    p = jnp.exp2(s - m)
    denom = jnp.sum(p, axis=-1, keepdims=True) + jnp.exp2(sink2 - m)
    return p.astype(BF16), 1.0 / denom


def _attn_prompt_kernel(sinks_ref, q_ref, kvc_ref, kvp_ref, ga_ref, wo_ref, out_ref,
                        kd_scr, vd_scr, bias_scr, attn_scr, *, tq):
    i = pl.program_id(1)
    kv_full = jnp.concatenate([kvp_ref[...], kvc_ref[...]], axis=0)
    for t in range(2):
        ke, ko = _dup_heads(kv_full[:, t * LANES:(t + 1) * LANES])
        ve, vo = _dup_heads(kv_full[:, D_KV + t * LANES:D_KV + (t + 1) * LANES])
        kd_scr[2 * t], kd_scr[2 * t + 1] = ke, ko
        vd_scr[2 * t], vd_scr[2 * t + 1] = ve, vo

    qi = lax.broadcasted_iota(jnp.int32, (WINDOW, 2 * WINDOW), 0)
    si = lax.broadcasted_iota(jnp.int32, (WINDOW, 2 * WINDOW), 1)
    band = (si > qi) & (si <= qi + WINDOW)
    in_block = si >= WINDOW
    lo = lax.broadcasted_iota(jnp.int32, (WINDOW, LANES), 1) < HEAD_DIM

    def attend(jb):
        q0 = jb * WINDOW
        has_prev = (i > 0) | (jb > 0)
        bias_scr[...] = jnp.where(band & (in_block | has_prev), 0.0, -jnp.inf)

        def head_pair(hp, c):
            j = hp // 2
            c0 = pl.multiple_of(hp * LANES, LANES)
            kd = kd_scr[j, pl.ds(q0, 2 * WINDOW), :]
            vd = vd_scr[j, pl.ds(q0, 2 * WINDOW), :]
            qp = q_ref[pl.ds(q0, WINDOW), pl.ds(c0, LANES)]
            outs = []
            for par in range(2):
                qm = jnp.where(lo if par == 0 else ~lo, qp, jnp.zeros_like(qp))
                s = lax.dot_general(qm, kd, (((1,), (1,)), ((), ())), preferred_element_type=F32)
                p, inv = _sink_softmax(s + bias_scr[...], sinks_ref[2 * hp + par])
                outs.append(jnp.dot(p, vd, preferred_element_type=F32) * inv)
            attn_scr[pl.ds(q0, WINDOW), pl.ds(c0, LANES)] = jnp.where(lo, outs[0], outs[1]).astype(BF16)
            return c

        lax.fori_loop(0, N_HEADS // 2, head_pair, 0, unroll=8)

    def project(jb):
        rows = pl.ds(jb * WINDOW, WINDOW)
        a_out = jnp.dot(attn_scr[rows, :], wo_ref[...], preferred_element_type=F32)
        out_ref[rows, :] = ga_ref[rows, :].astype(F32) * a_out

    for jb in range(tq // WINDOW):
        attend(jb)
        project(jb)


def _attn_prompt(sinks, q, kv, ga, wo, batch, seq, tq):
    nq = seq // tq
    bpt = tq // WINDOW
    row = lambda w: pl.BlockSpec((tq, w), lambda b, i: (b * nq + i, 0))
    prev = pl.BlockSpec((WINDOW, 2 * D_KV),
                        lambda b, i: (jnp.maximum((b * nq + i) * bpt - 1, 0), 0))
    return pl.pallas_call(
        functools.partial(_attn_prompt_kernel, tq=tq),
        grid=(batch, nq),
        in_specs=[pl.BlockSpec(memory_space=pltpu.SMEM), row(D_MODEL), row(2 * D_KV), prev,
                  row(D_MODEL), _const_spec(wo.shape)],
        out_specs=row(D_MODEL),
        out_shape=jax.ShapeDtypeStruct((batch * seq, D_MODEL), F32),
        scratch_shapes=[pltpu.VMEM((N_KV_HEADS, WINDOW + tq, LANES), BF16),
                        pltpu.VMEM((N_KV_HEADS, WINDOW + tq, LANES), BF16),
                        pltpu.VMEM((WINDOW, 2 * WINDOW), F32),
                        pltpu.VMEM((tq, D_MODEL), BF16)],
        compiler_params=_params("arbitrary", "arbitrary"),
        name="attn_prompt",
    )(sinks, q, kv, kv, ga, wo)


def _attn_sample_kernel(sinks_ref, q_ref, kvn_ref, ck_ref, cv_ref, ga_ref, wo_ref,
                        out_ref, ko_ref, vo_ref, attn_scr, *, bb, tpad, dec_seq):
    nk = WINDOW + tpad
    lo3 = lax.broadcasted_iota(jnp.int32, (bb, tpad, LANES), 2) < HEAD_DIM
    row = lax.broadcasted_iota(jnp.int32, (4 * tpad, nk), 0)
    si = lax.broadcasted_iota(jnp.int32, (4 * tpad, nk), 1)
    tq = row % tpad
    valid = (si > tq) & (si <= tq + WINDOW) & (si < WINDOW + dec_seq)
    hrow = lax.broadcasted_iota(jnp.int32, (4 * tpad, 1), 0) // tpad

    keys_first = lambda ref, sl: jnp.stack([ref[b, sl, :].T for b in range(bb)])
    for t in range(2):
        sl = slice(t * LANES, (t + 1) * LANES)
        kk = jnp.concatenate([keys_first(ck_ref, sl), kvn_ref[:, :, sl]], axis=1)
        vv = jnp.concatenate([keys_first(cv_ref, sl),
                              kvn_ref[:, :, D_KV + t * LANES:D_KV + (t + 1) * LANES]], axis=1)
        ko_ref[:, :, sl] = kk[:, dec_seq:dec_seq + WINDOW, :]
        vo_ref[:, :, sl] = vv[:, dec_seq:dec_seq + WINDOW, :]
        kds = _dup_heads(kk)
        vds = _dup_heads(vv)
        for par_kv in range(2):
            j = 2 * t + par_kv
            kd, vd = kds[par_kv], vds[par_kv]
            parts = []
            for r in range(2):
                c0 = j * 2 * LANES + r * LANES
                qp = q_ref[:, :, c0:c0 + LANES]
                parts += [jnp.where(lo3, qp, 0.0), jnp.where(lo3, 0.0, qp)]
            lhs = jnp.concatenate(parts, axis=1).astype(BF16)
            s = jnp.einsum('bqd,bkd->bqk', lhs, kd, preferred_element_type=F32)
            sink = jnp.zeros((4 * tpad, 1), F32)
            for g in range(4):
                sink = jnp.where(hrow == g, sinks_ref[4 * j + g], sink)
            p, inv = _sink_softmax(jnp.where(valid[None], s, -jnp.inf), sink[None])
            o = jnp.einsum('bqk,bkd->bqd', p, vd, preferred_element_type=F32) * inv
            for r in range(2):
                c0 = j * 2 * LANES + r * LANES
                o_even = o[:, (2 * r) * tpad:(2 * r + 1) * tpad, :]
                o_odd = o[:, (2 * r + 1) * tpad:(2 * r + 2) * tpad, :]
                attn_scr[:, :, c0:c0 + LANES] = jnp.where(lo3, o_even, o_odd)

    attn = attn_scr[...].reshape(bb * tpad, D_MODEL).astype(BF16)
    a_out = jnp.dot(attn, wo_ref[...], preferred_element_type=F32)
    out_ref[...] = ga_ref[...].astype(F32) * a_out


def _attn_sample(sinks, q3, kvn3, ck, cv, ga2, wo, bb, dec_seq):
    db, tpad, _ = q3.shape
    blk3 = lambda r, w: pl.BlockSpec((bb, r, w), lambda i: (i, 0, 0))
    row = pl.BlockSpec((bb * tpad, D_MODEL), lambda i: (i, 0))
    return pl.pallas_call(
        functools.partial(_attn_sample_kernel, bb=bb, tpad=tpad, dec_seq=dec_seq),
        grid=(db // bb,),
        in_specs=[pl.BlockSpec(memory_space=pltpu.SMEM), blk3(tpad, D_MODEL), blk3(tpad, 2 * D_KV),
                  blk3(D_KV, WINDOW), blk3(D_KV, WINDOW), row, _const_spec(wo.shape)],
        out_specs=[row, blk3(WINDOW, D_KV), blk3(WINDOW, D_KV)],
        out_shape=[jax.ShapeDtypeStruct((db * tpad, D_MODEL), F32),
                   jax.ShapeDtypeStruct((db, WINDOW, D_KV), F32),
                   jax.ShapeDtypeStruct((db, WINDOW, D_KV), F32)],
        scratch_shapes=[pltpu.VMEM((bb, tpad, D_MODEL), F32)],
        compiler_params=_params("arbitrary"),
        name="attn_sample",
    )(sinks, q3, kvn3, ck, cv, ga2, wo)


def _s5_discretize(lam_re, lam_im, log_dt, b_re, b_im):
    dt = jnp.exp(log_dt)[:, None]
    decay = jnp.exp(lam_re * dt)
    ab_re = decay * jnp.cos(lam_im * dt)
    ab_im = decay * jnp.sin(lam_im * dt)
    nr, ni = ab_re - 1.0, ab_im
    den = lam_re * lam_re + lam_im * lam_im
    f_re = ((nr * lam_re + ni * lam_im) / den)[..., None]
    f_im = ((ni * lam_re - nr * lam_im) / den)[..., None]
    return ab_re, ab_im, f_re * b_re - f_im * b_im, f_re * b_im + f_im * b_re


def _state_split(h):
    h = h.reshape(h.shape[:-1] + (N_PAIRS, 2, 2, SSM_STATE))
    unpair = lambda a: a.reshape(a.shape[:-3] + (SSM_GROUPS, SSM_STATE))
    return unpair(h[..., 0, :, :]), unpair(h[..., 1, :, :])


def _shift_lanes(x, n):
    a, b = x[:, :LANES], x[:, LANES:]
    lane = lax.broadcasted_iota(jnp.int32, a.shape, 1)
    if n == 0:
        return x
    if n >= LANES:
        r = n - LANES
        hi = a if r == 0 else jnp.where(lane < r, 0.0, pltpu.roll(a, r, 1))
        return jnp.concatenate([jnp.zeros_like(a), hi], axis=1)
    ra, rb = pltpu.roll(a, n, 1), pltpu.roll(b, n, 1)
    return jnp.concatenate([jnp.where(lane < n, 0.0, ra), jnp.where(lane < n, ra, rb)], axis=1)


def _dot_nt_split(lhs, rhs):
    nt = lambda a, b: lax.dot_general(a, b, (((1,), (1,)), ((), ())), preferred_element_type=F32)
    l_hi, r_hi = lhs.astype(BF16), rhs.astype(BF16)
    l_lo = (lhs - l_hi.astype(F32)).astype(BF16)
    r_lo = (rhs - r_hi.astype(F32)).astype(BF16)
    return nt(l_hi, r_hi) + nt(l_hi, r_lo) + nt(l_lo, r_hi)


def _s5_prep_kernel(a_ref, c_ref, bt_ref, m_ref, ws_ref, wsn_ref, whyt_ref, mk_ref, pw_ref, an_ref, *, n_tok):
    for p in range(PREP_PAIRS):
        one, two, lanes = pl.ds(p, 1), pl.ds(2 * p, 2), pl.ds(p * LANES, LANES)
        _s5_prep_pair(a_ref.at[one], c_ref.at[one], bt_ref.at[one], m_ref.at[two], ws_ref.at[two],
                      wsn_ref.at[two], whyt_ref.at[two], mk_ref.at[:, :, :, lanes], pw_ref.at[:, :, lanes],
                      an_ref.at[:, :, lanes], n_tok)


def _s5_prep_pair(a_ref, c_ref, bt_ref, m_ref, ws_ref, wsn_ref, whyt_ref, mk_ref, pw_ref, an_ref, n_tok):
    ar, ai = a_ref[0, 0], a_ref[0, 1]
    cr, ci = c_ref[0, 0], c_ref[0, 1]
    br, bi = bt_ref[0, 0], bt_ref[0, 1]
    pr, pi = [jnp.ones_like(ar)], [jnp.zeros_like(ai)]
    for _ in range(CHUNK):
        pr, pi = pr + [pr[-1] * ar - pi[-1] * ai], pi + [pr[-1] * ai + pi[-1] * ar]
    first = lax.broadcasted_iota(jnp.int32, (SSM_CH, LANES), 1) < SSM_STATE
    ca = [(cr * pr[t] - ci * pi[t], -(cr * pi[t] + ci * pr[t])) for t in range(CHUNK + 1)]
    ca_full = jnp.concatenate([jnp.concatenate(ca[t], axis=1) for t in range(CHUNK)], axis=0)
    zero = jnp.zeros((SSM_CH, LANES), F32)
    for e in range(2):
        own = first if e == 0 else ~first
        pick = lambda v: jnp.where(own, v, zero)
        for t in range(CHUNK):
            rows = slice(t * SSM_CH, (t + 1) * SSM_CH)
            whyt_ref[e, rows, :LANES] = pick(ca[t + 1][0]).astype(BF16)
            whyt_ref[e, rows, LANES:] = pick(ca[t + 1][1]).astype(BF16)
            for ref, k in ((ws_ref, CHUNK - 1 - t), (wsn_ref, n_tok - 1 - t)):
                if k >= 0:
                    ref[e, rows, :LANES] = pick(br * pr[k] - bi * pi[k]).astype(BF16)
                    ref[e, rows, LANES:] = pick(br * pi[k] + bi * pr[k]).astype(BF16)
                else:
                    ref[e, rows, :] = jnp.zeros((SSM_CH, PAIR_COLS), BF16)
        k_row = _dot_nt_split(jnp.concatenate([pick(br), pick(bi)], axis=1), ca_full)
        for s in range(CHUNK):
            m_ref[e, s * SSM_CH:(s + 1) * SSM_CH, :] = _shift_lanes(k_row, s * SSM_CH).astype(BF16)
    sub = lax.broadcasted_iota(jnp.int32, (SUBLANES, LANES), 0)
    qr, qi = [pr[CHUNK]], [pi[CHUNK]]
    for _ in range(SUBLANES - 1):
        qr, qi = qr + [qr[-1] * pr[CHUNK] - qi[-1] * pi[CHUNK]], qi + [qr[-1] * pi[CHUNK] + qi[-1] * pr[CHUNK]]
    for part, q in enumerate((qr, qi)):
        rows = jnp.zeros((SUBLANES, LANES), F32)
        for s in range(SUBLANES):
            rows = jnp.where(sub == s, q[s], rows)
        pw_ref[part] = rows
        for k, shift in enumerate((1, 2, 4)):
            mk_ref[k, part] = jnp.where(sub >= shift, q[shift - 1], 0.0)
    an_ref[0] = jnp.broadcast_to(pr[n_tok], (SUBLANES, LANES))
    an_ref[1] = jnp.broadcast_to(pi[n_tok], (SUBLANES, LANES))


def _s5_prep(lam_re, lam_im, log_dt, b_re, b_im, c_re, c_im, n_tok):
    ab_re, ab_im, bb_re, bb_im = _s5_discretize(lam_re, lam_im, log_dt, b_re, b_im)
    pair = lambda v: v.reshape(N_PAIRS, 2, v.shape[1], SSM_STATE).transpose(0, 2, 1, 3).reshape(
        N_PAIRS, v.shape[1], LANES)
    a = jnp.stack([pair(ab_re[:, None, :]), pair(ab_im[:, None, :])], axis=1)
    c = jnp.stack([pair(c_re), pair(c_im)], axis=1)
    bt = jnp.stack([pair(bb_re.transpose(0, 2, 1)), pair(bb_im.transpose(0, 2, 1))], axis=1)
    blk4 = lambda r: pl.BlockSpec((PREP_PAIRS, 2, r, LANES), lambda q: (q, 0, 0, 0))
    w_spec = pl.BlockSpec((2 * PREP_PAIRS, GROUP_IO, PAIR_COLS), lambda q: (q, 0, 0))
    w_shape = jax.ShapeDtypeStruct((SSM_GROUPS, GROUP_IO, PAIR_COLS), BF16)
    return pl.pallas_call(
        functools.partial(_s5_prep_kernel, n_tok=n_tok),
        grid=(N_PAIRS // PREP_PAIRS,),
        in_specs=[blk4(1), blk4(SSM_CH), blk4(SSM_CH)],
        out_specs=[w_spec, w_spec, w_spec, w_spec,
                   pl.BlockSpec((3, 2, SUBLANES, PREP_PAIRS * LANES), lambda q: (0, 0, 0, q)),
                   pl.BlockSpec((2, SUBLANES, PREP_PAIRS * LANES), lambda q: (0, 0, q)),
                   pl.BlockSpec((2, SUBLANES, PREP_PAIRS * LANES), lambda q: (0, 0, q))],
        out_shape=[w_shape, w_shape, w_shape, w_shape,
                   jax.ShapeDtypeStruct((3, 2, SUBLANES, HALF_COLS), F32),
                   jax.ShapeDtypeStruct((2, SUBLANES, HALF_COLS), F32),
                   jax.ShapeDtypeStruct((2, SUBLANES, HALF_COLS), F32)],
        compiler_params=_params("arbitrary"),
        name="s5_prep",
    )(a, c, bt)


def _group_io(ref, g):
    return ref.at[:, g * GROUP_IO:(g + 1) * GROUP_IO]


def _s5_emit_tile(u_ref, st_scr, m_ref, whyt_ref, dt_ref, g_ref, j):
    for gl in range(GROUPS_PER_TILE):
        g = GROUPS_PER_TILE * j + gl
        q0 = (gl // 2) * PAIR_COLS
        u_g = _group_io(u_ref, g)[...]
        y = (jnp.dot(u_g, m_ref[g], preferred_element_type=F32)
             + lax.dot_general(st_scr[:, q0:q0 + PAIR_COLS].astype(BF16), whyt_ref[g],
                               (((1,), (1,)), ((), ())), preferred_element_type=F32)
             + dt_ref[g] * u_g.astype(F32))
        _group_io(g_ref, g)[...] = _gelu_tanh(y).astype(BF16)


def _s5_local_states(u_ref, ws_ref, j, q):
    g = GROUPS_PER_TILE * j + 2 * q
    return (jnp.dot(_group_io(u_ref, g)[...], ws_ref[g], preferred_element_type=F32)
            + jnp.dot(_group_io(u_ref, g + 1)[...], ws_ref[g + 1], preferred_element_type=F32))


def _cmul_add(ar, ai, hr, hi, xr, xi):
    return ar * hr - ai * hi + xr, ar * hi + ai * hr + xi


def _s5_chain_kernel(u_ref, m_ref, ws_ref, whyt_ref, mk_ref, pw_ref, dt_ref, h0_ref, g_ref, hT_ref,
                     carry_scr, st_scr, *, rows):
    @pl.when(pl.program_id(1) == 0)
    def _():
        carry_scr[...] = jnp.broadcast_to(h0_ref[0], (SUBLANES, STATE_COLS))

    row0 = lax.broadcasted_iota(jnp.int32, (SUBLANES, LANES), 0) == 0
    last = lambda h: jnp.broadcast_to(h[SUBLANES - 1:, :], (SUBLANES, LANES))

    for j in range(N_LANE_TILES):
        for q in range(PAIRS_PER_TILE):
            st_scr[:, q * PAIR_COLS:(q + 1) * PAIR_COLS] = _s5_local_states(u_ref, ws_ref, j, q)

        def step(r, carry):
            r0 = pl.multiple_of(r * SUBLANES, SUBLANES)
            out = []
            for q in range(PAIRS_PER_TILE):
                re_c = slice(q * PAIR_COLS, q * PAIR_COLS + LANES)
                im_c = slice(q * PAIR_COLS + LANES, (q + 1) * PAIR_COLS)
                hc = slice((j * PAIRS_PER_TILE + q) * LANES, (j * PAIRS_PER_TILE + q + 1) * LANES)
                xr = st_scr[pl.ds(r0, SUBLANES), re_c]
                xi = st_scr[pl.ds(r0, SUBLANES), im_c]
                for k, shift in enumerate((1, 2, 4)):
                    xr, xi = _cmul_add(mk_ref[k, 0, :, hc], mk_ref[k, 1, :, hc],
                                       pltpu.roll(xr, shift, 0), pltpu.roll(xi, shift, 0), xr, xi)
                cr, ci = carry[q]
                hr, hi = _cmul_add(pw_ref[0, :, hc], pw_ref[1, :, hc], cr, ci, xr, xi)
                st_scr[pl.ds(r0, SUBLANES), re_c] = jnp.where(row0, cr, pltpu.roll(hr, 1, 0))
                st_scr[pl.ds(r0, SUBLANES), im_c] = jnp.where(row0, ci, pltpu.roll(hi, 1, 0))
                out.append((last(hr), last(hi)))
            return tuple(out)

        c0 = j * TILE_COLS
        init = tuple((carry_scr[:, c0 + q * PAIR_COLS:c0 + q * PAIR_COLS + LANES],
                      carry_scr[:, c0 + q * PAIR_COLS + LANES:c0 + (q + 1) * PAIR_COLS])
                     for q in range(PAIRS_PER_TILE))
        final = lax.fori_loop(0, rows // SUBLANES, step, init, unroll=True)
        for q in range(PAIRS_PER_TILE):
            carry_scr[:, c0 + q * PAIR_COLS:c0 + q * PAIR_COLS + LANES] = final[q][0]
            carry_scr[:, c0 + q * PAIR_COLS + LANES:c0 + (q + 1) * PAIR_COLS] = final[q][1]

        _s5_emit_tile(u_ref, st_scr, m_ref, whyt_ref, dt_ref, g_ref, j)

    hT_ref[0] = carry_scr[...]


def _s5_chain(u16, m, ws, why, mk, pw, d_tiled, h0, batch, n_rows, rows):
    nblk = n_rows // rows
    row = pl.BlockSpec((rows, CHUNK_LANES), lambda b, i: (b * nblk + i, 0))
    return pl.pallas_call(
        functools.partial(_s5_chain_kernel, rows=rows),
        grid=(batch, nblk),
        in_specs=[row, _const_spec(m.shape), _const_spec(ws.shape), _const_spec(why.shape),
                  _const_spec(mk.shape), _const_spec(pw.shape), _const_spec(d_tiled.shape),
                  pl.BlockSpec((1, 1, STATE_COLS), lambda b, i: (b, 0, 0))],
        out_specs=[row, pl.BlockSpec((1, SUBLANES, STATE_COLS), lambda b, i: (b, 0, 0))],
        out_shape=[jax.ShapeDtypeStruct(u16.shape, BF16),
                   jax.ShapeDtypeStruct((batch, SUBLANES, STATE_COLS), F32)],
        scratch_shapes=[pltpu.VMEM((SUBLANES, STATE_COLS), F32), pltpu.VMEM((rows, TILE_COLS), F32)],
        compiler_params=_params("arbitrary", "arbitrary"),
        name="s5_chain",
    )(u16, m, ws, why, mk, pw, d_tiled, h0)


def _s5_rows_kernel(u_ref, m_ref, ws_ref, whyt_ref, an_ref, dt_ref, h0r_ref, h0i_ref, g_ref, hTr_ref, hTi_ref,
                    st_scr):
    for j in range(N_LANE_TILES):
        for q in range(PAIRS_PER_TILE):
            hc = slice((j * PAIRS_PER_TILE + q) * LANES, (j * PAIRS_PER_TILE + q + 1) * LANES)
            h0r, h0i = h0r_ref[hc, :].T, h0i_ref[hc, :].T
            local = _s5_local_states(u_ref, ws_ref, j, q)
            hr, hi = _cmul_add(an_ref[0, :1, hc], an_ref[1, :1, hc], h0r, h0i, local[:, :LANES], local[:, LANES:])
            hTr_ref[hc, :] = hr.T
            hTi_ref[hc, :] = hi.T
            st_scr[:, q * PAIR_COLS:q * PAIR_COLS + LANES] = h0r
            st_scr[:, q * PAIR_COLS + LANES:(q + 1) * PAIR_COLS] = h0i
        _s5_emit_tile(u_ref, st_scr, m_ref, whyt_ref, dt_ref, g_ref, j)


def _s5_rows(u16, m, ws, why, an, d_tiled, h0_re_t, h0_im_t, rows):
    n = u16.shape[0]
    once = pl.Buffered(1)
    row = pl.BlockSpec((rows, CHUNK_LANES), lambda i: (i, 0), pipeline_mode=once)
    col = pl.BlockSpec((HALF_COLS, rows), lambda i: (0, i), pipeline_mode=once)
    state = jax.ShapeDtypeStruct((HALF_COLS, n), F32)
    return pl.pallas_call(
        _s5_rows_kernel,
        grid=(n // rows,),
        in_specs=[row, _const_spec(m.shape), _const_spec(ws.shape), _const_spec(why.shape),
                  _const_spec(an.shape), _const_spec(d_tiled.shape), col, col],
        out_specs=[row, col, col],
        out_shape=[jax.ShapeDtypeStruct(u16.shape, BF16), state, state],
        scratch_shapes=[pltpu.VMEM((rows, TILE_COLS), F32)],
        compiler_params=_params("arbitrary"),
        name="s5_rows",
    )(u16, m, ws, why, an, d_tiled, h0_re_t, h0_im_t)


def _post_kernel(g_ref, ma_ref, gs_ref, x_ref, permt_ref, wglu_ref, wout_ref, gffn_ref, wup_ref, wdown_ref,
                 gfin_ref, out_ref, *, n_tok):
    g = _load_chunk_rows(g_ref, permt_ref, n_tok)
    glu = jnp.dot(g, wglu_ref[...], preferred_element_type=F32)
    s_out = glu[:, :D_MODEL] * jax.nn.sigmoid(glu[:, D_MODEL:])
    merged = ma_ref[...] + gs_ref[...].astype(F32) * s_out
    x1 = x_ref[...] + jnp.dot(merged.astype(BF16), wout_ref[...], preferred_element_type=F32)
    h2 = _rmsnorm(x1, gffn_ref[...]).astype(BF16)
    x2 = x1
    for c in range(0, D_FF, FF_CHUNK):
        up = jnp.dot(h2, wup_ref[:, c:c + FF_CHUNK], preferred_element_type=F32)
        act = jnp.square(jnp.maximum(up, 0.0)).astype(BF16)
        x2 = x2 + jnp.dot(act, wdown_ref[c:c + FF_CHUNK, :], preferred_element_type=F32)
    out_ref[...] = _rmsnorm(x2, gfin_ref[...])


def _post(g, ma, gs, x, permt, w_glu, w_out, g_ffn, w_up, w_down, g_final, tm, n_tok):
    n = x.shape[0]
    row = pl.BlockSpec((tm, D_MODEL), lambda i: (i, 0))
    g_spec = pl.BlockSpec((tm // n_tok, CHUNK_LANES), lambda i: (i, 0))
    return pl.pallas_call(
        functools.partial(_post_kernel, n_tok=n_tok),
        grid=(n // tm,),
        in_specs=[g_spec, row, row, row, _const_spec(permt.shape), _const_spec(w_glu.shape),
                  _const_spec(w_out.shape), _const_spec((1, D_MODEL)), _const_spec(w_up.shape),
                  _const_spec(w_down.shape), _const_spec((1, D_MODEL))],
        out_specs=row,
        out_shape=jax.ShapeDtypeStruct((n, D_MODEL), F32),
        compiler_params=_params("arbitrary"),
        name="post",
    )(g, ma, gs, x, permt, w_glu, w_out, g_ffn, w_up, w_down, g_final)


def _tile(n, pref):
    t = pref
    while n % t:
        t //= 2
    return t


def kernel(x_prompt, x_sample, cache_k, cache_v, state_ssm_re, state_ssm_im, g_mix, w_in, attn_sinks,
           w_attn_o, ssm_lambda_re, ssm_lambda_im, ssm_log_dt, ssm_b_re, ssm_b_im, ssm_c_re, ssm_c_im,
           ssm_d, w_glu, w_out, g_ffn, w_up, w_down, g_final):
    batch, seq, _ = x_prompt.shape
    db, dec_seq, _ = x_sample.shape
    assert w_in.shape[0] == 1, "one layer"
    assert seq % PERM_ROWS == 0 and dec_seq in (1, 2, 4, 8) and (db * dec_seq) % PERM_ROWS == 0

    vec = lambda v: v.reshape(1, D_MODEL).astype(F32)
    w_in_b = w_in[0].astype(BF16)
    sinks = attn_sinks[0].astype(F32)
    s5p = (ssm_lambda_re[0], ssm_lambda_im[0], ssm_log_dt[0], ssm_b_re[0], ssm_b_im[0], ssm_c_re[0], ssm_c_im[0])
    d_skip = jnp.tile(ssm_d[0].astype(F32).reshape(SSM_GROUPS, 1, SSM_CH), (1, 1, CHUNK))
    perm = _chunk_perm(CHUNK)

    xp = x_prompt.reshape(batch * seq, D_MODEL)
    q, kv, u16, ga, gs, wo_b, w_glu_b, w_out_b, w_up_b, w_down_b = _proj(
        xp, vec(g_mix[0]), w_in_b, perm, _tile(batch * seq, PROJ_ROWS), CHUNK,
        cast=(w_attn_o[0], w_glu[0], w_out[0], w_up[0], w_down[0]))
    post_w = (w_glu_b, w_out_b, vec(g_ffn[0]), w_up_b, w_down_b, vec(g_final))
    ma = _attn_prompt(sinks, q, kv, ga, wo_b, batch, seq, _tile(seq, ATTN_ROWS))
    m, ws, ws_n, whyt, mk, pw, an = _s5_prep(*s5p, dec_seq)
    n_rows = seq // CHUNK
    g16, hT = _s5_chain(u16, m, ws, whyt, mk, pw, d_skip, jnp.zeros((batch, 1, STATE_COLS), F32),
                        batch, n_rows, _tile(n_rows, S5_BLOCK_ROWS))
    y_prompt = _post(g16, ma, gs, xp, perm.T, *post_w, _tile(batch * seq, POST_ROWS), CHUNK).reshape(
        batch, seq, D_MODEL)
    kv_last = kv.reshape(batch, seq, 2 * D_KV)[:, seq - WINDOW:]
    k_prompt = kv_last[..., :D_KV].reshape(1, batch, WINDOW, N_KV_HEADS, HEAD_DIM)
    v_prompt = kv_last[..., D_KV:].reshape(1, batch, WINDOW, N_KV_HEADS, HEAD_DIM)
    hr, hi = _state_split(hT[:, 0])
    ssm_re_prompt, ssm_im_prompt = hr[None], hi[None]

    ns = db * dec_seq
    xs = x_sample.reshape(ns, D_MODEL)
    perm_s = _chunk_perm(dec_seq)
    q, kv, u16, ga, gs = _proj(xs, vec(g_mix[0]), w_in_b, perm_s, _tile(ns, PROJ_ROWS), dec_seq)
    tpad = SUBLANES
    pad3 = lambda v: jnp.pad(v.reshape(db, dec_seq, -1).astype(F32), ((0, 0), (0, tpad - dec_seq), (0, 0)))
    ga_pad = pad3(ga).astype(BF16).reshape(db * tpad, D_MODEL)
    window_last = lambda c: c[0].astype(F32).transpose(0, 2, 3, 1).reshape(db, D_KV, WINDOW)
    ma_pad, k_new, v_new = _attn_sample(
        sinks, pad3(q), pad3(kv), window_last(cache_k), window_last(cache_v), ga_pad, wo_b,
        _tile(db, SAMPLE_SEQS), dec_seq)
    ma = ma_pad.reshape(db, tpad, D_MODEL)[:, :dec_seq].reshape(ns, D_MODEL)
    to_cols = lambda s: s[0].astype(F32).transpose(1, 2, 0).reshape(HALF_COLS, db)
    from_cols = lambda h: h.reshape(SSM_GROUPS, SSM_STATE, db).transpose(2, 0, 1)[None]
    g16, hT_re, hT_im = _s5_rows(u16, m, ws_n, whyt, an, d_skip, to_cols(state_ssm_re), to_cols(state_ssm_im),
                                 _tile(db, LANES))
    y_sample = _post(g16, ma, gs, xs, perm_s.T, *post_w, PERM_ROWS, dec_seq).reshape(db, dec_seq, D_MODEL)
    k_sample = k_new.reshape(1, db, WINDOW, N_KV_HEADS, HEAD_DIM)
    v_sample = v_new.reshape(1, db, WINDOW, N_KV_HEADS, HEAD_DIM)
    ssm_re_sample, ssm_im_sample = from_cols(hT_re), from_cols(hT_im)

    return (y_prompt, y_sample, k_prompt, v_prompt, ssm_re_prompt, ssm_im_prompt,
            k_sample, v_sample, ssm_re_sample, ssm_im_sample)
```

```python
import functools
import math

import jax
import jax.numpy as jnp
from jax import lax
from jax.experimental import pallas as pl
from jax.experimental.pallas import tpu as pltpu

F32 = jnp.float32
BF16 = jnp.bfloat16

D_MODEL = 1024
HEAD_DIM = 64
N_HEADS = 16
N_KV_HEADS = 4
WINDOW = 128
D_KV = N_KV_HEADS * HEAD_DIM
SSM_CH = 16
SSM_GROUPS = 64
SSM_STATE = 64
D_FF = 4 * D_MODEL
FF_CHUNK = 1024
RMS_EPS = 1e-5
LOG2E = 1.4426950408889634
Q_SCALE = HEAD_DIM ** -0.5 * LOG2E

LANES = 128
SUBLANES = 8
BF16_ROWS = 16
N_LANE_TILES = D_MODEL // LANES
GROUPS_PER_TILE = LANES // SSM_CH
CHUNK = 16
CHUNK_LANES = CHUNK * D_MODEL
GROUP_IO = CHUNK * SSM_CH
N_PAIRS = SSM_GROUPS // 2
PAIRS_PER_TILE = GROUPS_PER_TILE // 2
PAIR_COLS = 2 * LANES
TILE_COLS = PAIRS_PER_TILE * PAIR_COLS
STATE_COLS = N_PAIRS * PAIR_COLS
HALF_COLS = STATE_COLS // 2
PERM_ROWS = CHUNK * BF16_ROWS
PREP_PAIRS = 4
PROJ_ROWS = 1024
ATTN_ROWS = 2048
POST_ROWS = 512
S5_BLOCK_ROWS = 128
SAMPLE_SEQS = 32
VMEM_LIMIT = 56 * 1024 * 1024

_Q0, _KV0, _U0, _GA0, _GS0, _END = 0, 1024, 1536, 2560, 3584, 4608


def _gelu_tanh(x):
    k1 = -2.0 * math.sqrt(2.0 / math.pi) * LOG2E
    return x / (1.0 + jnp.exp2(x * (k1 + (0.044715 * k1) * (x * x))))


def _rmsnorm(x, g):
    return x * lax.rsqrt(jnp.mean(x * x, axis=-1, keepdims=True) + RMS_EPS) * g


def _params(*sem):
    return pltpu.CompilerParams(dimension_semantics=sem, vmem_limit_bytes=VMEM_LIMIT)


def _const_spec(shape):
    nd = len(shape)
    return pl.BlockSpec(shape, lambda *_: (0,) * nd, pipeline_mode=pl.Buffered(1))


def _chunk_perm(n_tok):
    n_chunks = PERM_ROWS // n_tok
    r = jnp.arange(PERM_ROWS)
    tok = (r % n_chunks) * n_tok + r // n_chunks
    return (tok[:, None] == jnp.arange(PERM_ROWS)[None, :]).astype(BF16)


def _piece_transpose(cols, masks):
    for d, msk in zip((4, 2, 1), masks):
        new = list(cols)
        for v in range(GROUPS_PER_TILE):
            if v & d == 0:
                a, b = cols[v], cols[v + d]
                new[v] = jnp.where(msk, pltpu.roll(b, SSM_CH * d, 1), a)
                new[v + d] = jnp.where(msk, b, pltpu.roll(a, LANES - SSM_CH * d, 1))
        cols = new
    return cols


def _piece_masks(rows):
    piece = lax.broadcasted_iota(jnp.int32, (rows, LANES), 1) // SSM_CH
    return [(piece & d) != 0 for d in (4, 2, 1)]


def _store_chunk_rows(x, perm_ref, out_ref, n_tok):
    n_chunks = PERM_ROWS // n_tok
    masks = _piece_masks(n_chunks)
    zero = jnp.zeros((n_chunks, LANES), F32)
    for hb in range(x.shape[0] // PERM_ROWS):
        rows = slice(hb * n_chunks, (hb + 1) * n_chunks)
        xp = jnp.dot(perm_ref[...], x[hb * PERM_ROWS:(hb + 1) * PERM_ROWS],
                     preferred_element_type=F32)
        for j in range(N_LANE_TILES):
            for hf in range(2):
                live = 8 * hf < n_tok
                cols = [xp[t * n_chunks:(t + 1) * n_chunks, j * LANES:(j + 1) * LANES] if t < n_tok else zero
                        for t in range(8 * hf, 8 * hf + 8)]
                for gl, col in enumerate(_piece_transpose(cols, masks) if live else cols):
                    c0 = (GROUPS_PER_TILE * j + gl) * GROUP_IO + hf * LANES
                    out_ref[rows, c0:c0 + LANES] = col.astype(BF16)


def _load_chunk_rows(in_ref, permt_ref, n_tok):
    n_chunks = PERM_ROWS // n_tok
    masks = _piece_masks(n_chunks)
    blocks = []
    for hb in range(in_ref.shape[0] // n_chunks):
        rows = slice(hb * n_chunks, (hb + 1) * n_chunks)
        tiles = [[None] * N_LANE_TILES for _ in range(n_tok)]
        for j in range(N_LANE_TILES):
            for hf in range(-(-n_tok // 8)):
                cols = []
                for gl in range(GROUPS_PER_TILE):
                    c0 = (GROUPS_PER_TILE * j + gl) * GROUP_IO + hf * LANES
                    cols.append(in_ref[rows, c0:c0 + LANES].astype(F32))
                for k, col in enumerate(_piece_transpose(cols, masks)):
                    if 8 * hf + k < n_tok:
                        tiles[8 * hf + k][j] = col.astype(BF16)
        xp = jnp.concatenate([jnp.concatenate(row, axis=1) for row in tiles], axis=0)
        blocks.append(jnp.dot(permt_ref[...], xp, preferred_element_type=F32).astype(BF16))
    return blocks[0] if len(blocks) == 1 else jnp.concatenate(blocks, axis=0)


def _proj_kernel(x_ref, g_ref, w_ref, perm_ref, *refs, n_tok, n_cast):
    q_ref, kv_ref, u_ref, ga_ref, gs_ref = refs[n_cast:n_cast + 5]
    h = _rmsnorm(x_ref[...], g_ref[...]).astype(BF16)

    def seg(lo, hi):
        return jnp.dot(h, w_ref[:, lo:hi], preferred_element_type=F32)

    q_ref[...] = (seg(_Q0, _KV0) * Q_SCALE).astype(BF16)
    kv_ref[...] = seg(_KV0, _U0)
    _store_chunk_rows(seg(_U0, _GA0).astype(BF16), perm_ref, u_ref, n_tok)
    ga_ref[...] = jax.nn.sigmoid(seg(_GA0, _GS0)).astype(BF16)
    gs_ref[...] = jax.nn.sigmoid(seg(_GS0, _END)).astype(BF16)
    for src, dst in zip(refs[:n_cast], refs[n_cast + 5:]):
        dst[...] = src[...].astype(BF16)


def _proj(x, g_mix, w_in, perm, tm, n_tok, cast=()):
    n = x.shape[0]
    steps = n // tm
    row = lambda w: pl.BlockSpec((tm, w), lambda i: (i, 0))
    slab = lambda w: pl.BlockSpec((w.shape[0] // steps, w.shape[1]), lambda i: (i, 0))
    assert all(w.shape[0] % (steps * BF16_ROWS) == 0 for w in cast)
    return pl.pallas_call(
        functools.partial(_proj_kernel, n_tok=n_tok, n_cast=len(cast)),
        grid=(steps,),
        in_specs=[row(D_MODEL), _const_spec((1, D_MODEL)), _const_spec(w_in.shape), _const_spec(perm.shape)]
        + [slab(w) for w in cast],
        out_specs=[row(D_MODEL), row(2 * D_KV), pl.BlockSpec((tm // n_tok, CHUNK_LANES), lambda i: (i, 0)),
                   row(D_MODEL), row(D_MODEL)] + [slab(w) for w in cast],
        out_shape=[jax.ShapeDtypeStruct((n, D_MODEL), BF16),
                   jax.ShapeDtypeStruct((n, 2 * D_KV), F32),
                   jax.ShapeDtypeStruct((n // n_tok, CHUNK_LANES), BF16),
                   jax.ShapeDtypeStruct((n, D_MODEL), BF16),
                   jax.ShapeDtypeStruct((n, D_MODEL), BF16)]
        + [jax.ShapeDtypeStruct(w.shape, BF16) for w in cast],
        compiler_params=_params("arbitrary"),
        name="proj",
    )(x, g_mix, w_in, perm, *cast)


def _dup_heads(tile):
    lo = lax.broadcasted_iota(jnp.int32, tile.shape, tile.ndim - 1) < HEAD_DIM
    rolled = pltpu.roll(tile, HEAD_DIM, tile.ndim - 1)
    return (jnp.where(lo, tile, rolled).astype(BF16), jnp.where(lo, rolled, tile).astype(BF16))


def _sink_softmax(s, sink):
    sink2 = sink * LOG2E
    m = jnp.maximum(jnp.max(s, axis=-1, keepdims=True), sink2)
    p = jnp.exp2(s - m)
    denom = jnp.sum(p, axis=-1, keepdims=True) + jnp.exp2(sink2 - m)
    return p.astype(BF16), 1.0 / denom


def _attn_prompt_kernel(sinks_ref, q_ref, kvc_ref, kvp_ref, ga_ref, wo_ref, out_ref,
                        kd_scr, vd_scr, bias_scr, attn_scr, *, tq):
    i = pl.program_id(1)
    kv_full = jnp.concatenate([kvp_ref[...], kvc_ref[...]], axis=0)
    for t in range(2):
        ke, ko = _dup_heads(kv_full[:, t * LANES:(t + 1) * LANES])
        ve, vo = _dup_heads(kv_full[:, D_KV + t * LANES:D_KV + (t + 1) * LANES])
        kd_scr[2 * t], kd_scr[2 * t + 1] = ke, ko
        vd_scr[2 * t], vd_scr[2 * t + 1] = ve, vo

    qi = lax.broadcasted_iota(jnp.int32, (WINDOW, 2 * WINDOW), 0)
    si = lax.broadcasted_iota(jnp.int32, (WINDOW, 2 * WINDOW), 1)
    band = (si > qi) & (si <= qi + WINDOW)
    in_block = si >= WINDOW
    lo = lax.broadcasted_iota(jnp.int32, (WINDOW, LANES), 1) < HEAD_DIM

    def attend(jb):
        q0 = jb * WINDOW
        has_prev = (i > 0) | (jb > 0)
        bias_scr[...] = jnp.where(band & (in_block | has_prev), 0.0, -jnp.inf)

        def head_pair(hp, c):
            j = hp // 2
            c0 = pl.multiple_of(hp * LANES, LANES)
            kd = kd_scr[j, pl.ds(q0, 2 * WINDOW), :]
            vd = vd_scr[j, pl.ds(q0, 2 * WINDOW), :]
            qp = q_ref[pl.ds(q0, WINDOW), pl.ds(c0, LANES)]
            outs = []
            for par in range(2):
                qm = jnp.where(lo if par == 0 else ~lo, qp, jnp.zeros_like(qp))
                s = lax.dot_general(qm, kd, (((1,), (1,)), ((), ())), preferred_element_type=F32)
                p, inv = _sink_softmax(s + bias_scr[...], sinks_ref[2 * hp + par])
                outs.append(jnp.dot(p, vd, preferred_element_type=F32) * inv)
            attn_scr[pl.ds(q0, WINDOW), pl.ds(c0, LANES)] = jnp.where(lo, outs[0], outs[1]).astype(BF16)
            return c

        lax.fori_loop(0, N_HEADS // 2, head_pair, 0, unroll=8)

    def project(jb):
        rows = pl.ds(jb * WINDOW, WINDOW)
        a_out = jnp.dot(attn_scr[rows, :], wo_ref[...], preferred_element_type=F32)
        out_ref[rows, :] = ga_ref[rows, :].astype(F32) * a_out

    for jb in range(tq // WINDOW):
        attend(jb)
        project(jb)


def _attn_prompt(sinks, q, kv, ga, wo, batch, seq, tq):
    nq = seq // tq
    bpt = tq // WINDOW
    row = lambda w: pl.BlockSpec((tq, w), lambda b, i: (b * nq + i, 0))
    prev = pl.BlockSpec((WINDOW, 2 * D_KV),
                        lambda b, i: (jnp.maximum((b * nq + i) * bpt - 1, 0), 0))
    return pl.pallas_call(
        functools.partial(_attn_prompt_kernel, tq=tq),
        grid=(batch, nq),
        in_specs=[pl.BlockSpec(memory_space=pltpu.SMEM), row(D_MODEL), row(2 * D_KV), prev,
                  row(D_MODEL), _const_spec(wo.shape)],
        out_specs=row(D_MODEL),
        out_shape=jax.ShapeDtypeStruct((batch * seq, D_MODEL), F32),
        scratch_shapes=[pltpu.VMEM((N_KV_HEADS, WINDOW + tq, LANES), BF16),
                        pltpu.VMEM((N_KV_HEADS, WINDOW + tq, LANES), BF16),
                        pltpu.VMEM((WINDOW, 2 * WINDOW), F32),
                        pltpu.VMEM((tq, D_MODEL), BF16)],
        compiler_params=_params("arbitrary", "arbitrary"),
        name="attn_prompt",
    )(sinks, q, kv, kv, ga, wo)


def _attn_sample_kernel(sinks_ref, q_ref, kvn_ref, ck_ref, cv_ref, ga_ref, wo_ref,
                        out_ref, ko_ref, vo_ref, attn_scr, *, bb, tpad, dec_seq):
    nk = WINDOW + tpad
    lo3 = lax.broadcasted_iota(jnp.int32, (bb, tpad, LANES), 2) < HEAD_DIM
    row = lax.broadcasted_iota(jnp.int32, (4 * tpad, nk), 0)
    si = lax.broadcasted_iota(jnp.int32, (4 * tpad, nk), 1)
    tq = row % tpad
    valid = (si > tq) & (si <= tq + WINDOW) & (si < WINDOW + dec_seq)
    hrow = lax.broadcasted_iota(jnp.int32, (4 * tpad, 1), 0) // tpad

    keys_first = lambda ref, sl: jnp.stack([ref[b, sl, :].T for b in range(bb)])
    for t in range(2):
        sl = slice(t * LANES, (t + 1) * LANES)
        kk = jnp.concatenate([keys_first(ck_ref, sl), kvn_ref[:, :, sl]], axis=1)
        vv = jnp.concatenate([keys_first(cv_ref, sl),
                              kvn_ref[:, :, D_KV + t * LANES:D_KV + (t + 1) * LANES]], axis=1)
        ko_ref[:, :, sl] = kk[:, dec_seq:dec_seq + WINDOW, :]
        vo_ref[:, :, sl] = vv[:, dec_seq:dec_seq + WINDOW, :]
        kds = _dup_heads(kk)
        vds = _dup_heads(vv)
        for par_kv in range(2):
            j = 2 * t + par_kv
            kd, vd = kds[par_kv], vds[par_kv]
            parts = []
            for r in range(2):
                c0 = j * 2 * LANES + r * LANES
                qp = q_ref[:, :, c0:c0 + LANES]
                parts += [jnp.where(lo3, qp, 0.0), jnp.where(lo3, 0.0, qp)]
            lhs = jnp.concatenate(parts, axis=1).astype(BF16)
            s = jnp.einsum('bqd,bkd->bqk', lhs, kd, preferred_element_type=F32)
            sink = jnp.zeros((4 * tpad, 1), F32)
            for g in range(4):
                sink = jnp.where(hrow == g, sinks_ref[4 * j + g], sink)
            p, inv = _sink_softmax(jnp.where(valid[None], s, -jnp.inf), sink[None])
            o = jnp.einsum('bqk,bkd->bqd', p, vd, preferred_element_type=F32) * inv
            for r in range(2):
                c0 = j * 2 * LANES + r * LANES
                o_even = o[:, (2 * r) * tpad:(2 * r + 1) * tpad, :]
                o_odd = o[:, (2 * r + 1) * tpad:(2 * r + 2) * tpad, :]
                attn_scr[:, :, c0:c0 + LANES] = jnp.where(lo3, o_even, o_odd)

    attn = attn_scr[...].reshape(bb * tpad, D_MODEL).astype(BF16)
    a_out = jnp.dot(attn, wo_ref[...], preferred_element_type=F32)
    out_ref[...] = ga_ref[...].astype(F32) * a_out


def _attn_sample(sinks, q3, kvn3, ck, cv, ga2, wo, bb, dec_seq):
    db, tpad, _ = q3.shape
    blk3 = lambda r, w: pl.BlockSpec((bb, r, w), lambda i: (i, 0, 0))
    row = pl.BlockSpec((bb * tpad, D_MODEL), lambda i: (i, 0))
    return pl.pallas_call(
        functools.partial(_attn_sample_kernel, bb=bb, tpad=tpad, dec_seq=dec_seq),
        grid=(db // bb,),
        in_specs=[pl.BlockSpec(memory_space=pltpu.SMEM), blk3(tpad, D_MODEL), blk3(tpad, 2 * D_KV),
                  blk3(D_KV, WINDOW), blk3(D_KV, WINDOW), row, _const_spec(wo.shape)],
        out_specs=[row, blk3(WINDOW, D_KV), blk3(WINDOW, D_KV)],
        out_shape=[jax.ShapeDtypeStruct((db * tpad, D_MODEL), F32),
                   jax.ShapeDtypeStruct((db, WINDOW, D_KV), F32),
                   jax.ShapeDtypeStruct((db, WINDOW, D_KV), F32)],
        scratch_shapes=[pltpu.VMEM((bb, tpad, D_MODEL), F32)],
        compiler_params=_params("arbitrary"),
        name="attn_sample",
    )(sinks, q3, kvn3, ck, cv, ga2, wo)


def _s5_discretize(lam_re, lam_im, log_dt, b_re, b_im):
    dt = jnp.exp(log_dt)[:, None]
    decay = jnp.exp(lam_re * dt)
    ab_re = decay * jnp.cos(lam_im * dt)
    ab_im = decay * jnp.sin(lam_im * dt)
    nr, ni = ab_re - 1.0, ab_im
    den = lam_re * lam_re + lam_im * lam_im
    f_re = ((nr * lam_re + ni * lam_im) / den)[..., None]
    f_im = ((ni * lam_re - nr * lam_im) / den)[..., None]
    return ab_re, ab_im, f_re * b_re - f_im * b_im, f_re * b_im + f_im * b_re


def _state_split(h):
    h = h.reshape(h.shape[:-1] + (N_PAIRS, 2, 2, SSM_STATE))
    unpair = lambda a: a.reshape(a.shape[:-3] + (SSM_GROUPS, SSM_STATE))
    return unpair(h[..., 0, :, :]), unpair(h[..., 1, :, :])


def _shift_lanes(x, n):
    a, b = x[:, :LANES], x[:, LANES:]
    lane = lax.broadcasted_iota(jnp.int32, a.shape, 1)
    if n == 0:
        return x
    if n >= LANES:
        r = n - LANES
        hi = a if r == 0 else jnp.where(lane < r, 0.0, pltpu.roll(a, r, 1))
        return jnp.concatenate([jnp.zeros_like(a), hi], axis=1)
    ra, rb = pltpu.roll(a, n, 1), pltpu.roll(b, n, 1)
    return jnp.concatenate([jnp.where(lane < n, 0.0, ra), jnp.where(lane < n, ra, rb)], axis=1)


def _dot_nt_split(lhs, rhs):
    nt = lambda a, b: lax.dot_general(a, b, (((1,), (1,)), ((), ())), preferred_element_type=F32)
    l_hi, r_hi = lhs.astype(BF16), rhs.astype(BF16)
    l_lo = (lhs - l_hi.astype(F32)).astype(BF16)
    r_lo = (rhs - r_hi.astype(F32)).astype(BF16)
    return nt(l_hi, r_hi) + nt(l_hi, r_lo) + nt(l_lo, r_hi)


def _s5_prep_kernel(a_ref, c_ref, bt_ref, w_ref, m_ref, ws_ref, wsn_ref, whyt_ref, mk_ref, pw_ref, an_ref,
                    wb_ref, *, n_tok):
    wb_ref[...] = w_ref[...].astype(BF16)
    for p in range(PREP_PAIRS):
        one, two, lanes = pl.ds(p, 1), pl.ds(2 * p, 2), pl.ds(p * LANES, LANES)
        _s5_prep_pair(a_ref.at[one], c_ref.at[one], bt_ref.at[one], m_ref.at[two], ws_ref.at[two],
                      wsn_ref.at[two], whyt_ref.at[two], mk_ref.at[:, :, :, lanes], pw_ref.at[:, :, lanes],
                      an_ref.at[:, :, lanes], n_tok)


def _s5_prep_pair(a_ref, c_ref, bt_ref, m_ref, ws_ref, wsn_ref, whyt_ref, mk_ref, pw_ref, an_ref, n_tok):
    ar, ai = a_ref[0, 0], a_ref[0, 1]
    cr, ci = c_ref[0, 0], c_ref[0, 1]
    br, bi = bt_ref[0, 0], bt_ref[0, 1]
    pr, pi = [jnp.ones_like(ar)], [jnp.zeros_like(ai)]
    for _ in range(CHUNK):
        pr, pi = pr + [pr[-1] * ar - pi[-1] * ai], pi + [pr[-1] * ai + pi[-1] * ar]
    first = lax.broadcasted_iota(jnp.int32, (SSM_CH, LANES), 1) < SSM_STATE
    ca = [(cr * pr[t] - ci * pi[t], -(cr * pi[t] + ci * pr[t])) for t in range(CHUNK + 1)]
    ca_full = jnp.concatenate([jnp.concatenate(ca[t], axis=1) for t in range(CHUNK)], axis=0)
    zero = jnp.zeros((SSM_CH, LANES), F32)
    for e in range(2):
        own = first if e == 0 else ~first
        pick = lambda v: jnp.where(own, v, zero)
        for t in range(CHUNK):
            rows = slice(t * SSM_CH, (t + 1) * SSM_CH)
            whyt_ref[e, rows, :LANES] = pick(ca[t + 1][0]).astype(BF16)
            whyt_ref[e, rows, LANES:] = pick(ca[t + 1][1]).astype(BF16)
            for ref, k in ((ws_ref, CHUNK - 1 - t), (wsn_ref, n_tok - 1 - t)):
                if k >= 0:
                    ref[e, rows, :LANES] = pick(br * pr[k] - bi * pi[k]).astype(BF16)
                    ref[e, rows, LANES:] = pick(br * pi[k] + bi * pr[k]).astype(BF16)
                else:
                    ref[e, rows, :] = jnp.zeros((SSM_CH, PAIR_COLS), BF16)
        k_row = _dot_nt_split(jnp.concatenate([pick(br), pick(bi)], axis=1), ca_full)
        for s in range(CHUNK):
            m_ref[e, s * SSM_CH:(s + 1) * SSM_CH, :] = _shift_lanes(k_row, s * SSM_CH).astype(BF16)
    sub = lax.broadcasted_iota(jnp.int32, (SUBLANES, LANES), 0)
    qr, qi = [pr[CHUNK]], [pi[CHUNK]]
    for _ in range(SUBLANES - 1):
        qr, qi = qr + [qr[-1] * pr[CHUNK] - qi[-1] * pi[CHUNK]], qi + [qr[-1] * pi[CHUNK] + qi[-1] * pr[CHUNK]]
    for part, q in enumerate((qr, qi)):
        rows = jnp.zeros((SUBLANES, LANES), F32)
        for s in range(SUBLANES):
            rows = jnp.where(sub == s, q[s], rows)
        pw_ref[part] = rows
        for k, shift in enumerate((1, 2, 4)):
            mk_ref[k, part] = jnp.where(sub >= shift, q[shift - 1], 0.0)
    an_ref[0] = jnp.broadcast_to(pr[n_tok], (SUBLANES, LANES))
    an_ref[1] = jnp.broadcast_to(pi[n_tok], (SUBLANES, LANES))


def _s5_prep(lam_re, lam_im, log_dt, b_re, b_im, c_re, c_im, n_tok, w_in):
    ab_re, ab_im, bb_re, bb_im = _s5_discretize(lam_re, lam_im, log_dt, b_re, b_im)
    pair = lambda v: v.reshape(N_PAIRS, 2, v.shape[1], SSM_STATE).transpose(0, 2, 1, 3).reshape(
        N_PAIRS, v.shape[1], LANES)
    a = jnp.stack([pair(ab_re[:, None, :]), pair(ab_im[:, None, :])], axis=1)
    c = jnp.stack([pair(c_re), pair(c_im)], axis=1)
    bt = jnp.stack([pair(bb_re.transpose(0, 2, 1)), pair(bb_im.transpose(0, 2, 1))], axis=1)
    blk4 = lambda r: pl.BlockSpec((PREP_PAIRS, 2, r, LANES), lambda q: (q, 0, 0, 0))
    w_spec = pl.BlockSpec((2 * PREP_PAIRS, GROUP_IO, PAIR_COLS), lambda q: (q, 0, 0))
    w_shape = jax.ShapeDtypeStruct((SSM_GROUPS, GROUP_IO, PAIR_COLS), BF16)
    steps = N_PAIRS // PREP_PAIRS
    assert w_in.shape[0] % (steps * BF16_ROWS) == 0
    slab = pl.BlockSpec((w_in.shape[0] // steps, w_in.shape[1]), lambda q: (q, 0))
    return pl.pallas_call(
        functools.partial(_s5_prep_kernel, n_tok=n_tok),
        grid=(steps,),
        in_specs=[blk4(1), blk4(SSM_CH), blk4(SSM_CH), slab],
        out_specs=[w_spec, w_spec, w_spec, w_spec,
                   pl.BlockSpec((3, 2, SUBLANES, PREP_PAIRS * LANES), lambda q: (0, 0, 0, q)),
                   pl.BlockSpec((2, SUBLANES, PREP_PAIRS * LANES), lambda q: (0, 0, q)),
                   pl.BlockSpec((2, SUBLANES, PREP_PAIRS * LANES), lambda q: (0, 0, q)), slab],
        out_shape=[w_shape, w_shape, w_shape, w_shape,
                   jax.ShapeDtypeStruct((3, 2, SUBLANES, HALF_COLS), F32),
                   jax.ShapeDtypeStruct((2, SUBLANES, HALF_COLS), F32),
                   jax.ShapeDtypeStruct((2, SUBLANES, HALF_COLS), F32),
                   jax.ShapeDtypeStruct(w_in.shape, BF16)],
        compiler_params=_params("arbitrary"),
        name="s5_prep",
    )(a, c, bt, w_in)


def _group_io(ref, g):
    return ref.at[:, g * GROUP_IO:(g + 1) * GROUP_IO]


def _s5_emit_tile(u_ref, st_scr, m_ref, whyt_ref, dt_ref, g_ref, j):
    for gl in range(GROUPS_PER_TILE):
        g = GROUPS_PER_TILE * j + gl
        q0 = (gl // 2) * PAIR_COLS
        u_g = _group_io(u_ref, g)[...]
        y = (jnp.dot(u_g, m_ref[g], preferred_element_type=F32)
             + lax.dot_general(st_scr[:, q0:q0 + PAIR_COLS].astype(BF16), whyt_ref[g],
                               (((1,), (1,)), ((), ())), preferred_element_type=F32)
             + dt_ref[g] * u_g.astype(F32))
        _group_io(g_ref, g)[...] = _gelu_tanh(y).astype(BF16)


def _s5_local_states(u_ref, ws_ref, j, q):
    g = GROUPS_PER_TILE * j + 2 * q
    return (jnp.dot(_group_io(u_ref, g)[...], ws_ref[g], preferred_element_type=F32)
            + jnp.dot(_group_io(u_ref, g + 1)[...], ws_ref[g + 1], preferred_element_type=F32))


def _cmul_add(ar, ai, hr, hi, xr, xi):
    return ar * hr - ai * hi + xr, ar * hi + ai * hr + xi


def _s5_chain_kernel(u_ref, m_ref, ws_ref, whyt_ref, mk_ref, pw_ref, dt_ref, h0_ref, g_ref, hT_ref,
                     carry_scr, st_scr, *, rows):
    @pl.when(pl.program_id(1) == 0)
    def _():
        carry_scr[...] = jnp.broadcast_to(h0_ref[0], (SUBLANES, STATE_COLS))

    row0 = lax.broadcasted_iota(jnp.int32, (SUBLANES, LANES), 0) == 0
    last = lambda h: jnp.broadcast_to(h[SUBLANES - 1:, :], (SUBLANES, LANES))

    for j in range(N_LANE_TILES):
        for q in range(PAIRS_PER_TILE):
            st_scr[:, q * PAIR_COLS:(q + 1) * PAIR_COLS] = _s5_local_states(u_ref, ws_ref, j, q)

        def step(r, carry):
            r0 = pl.multiple_of(r * SUBLANES, SUBLANES)
            out = []
            for q in range(PAIRS_PER_TILE):
                re_c = slice(q * PAIR_COLS, q * PAIR_COLS + LANES)
                im_c = slice(q * PAIR_COLS + LANES, (q + 1) * PAIR_COLS)
                hc = slice((j * PAIRS_PER_TILE + q) * LANES, (j * PAIRS_PER_TILE + q + 1) * LANES)
                xr = st_scr[pl.ds(r0, SUBLANES), re_c]
                xi = st_scr[pl.ds(r0, SUBLANES), im_c]
                for k, shift in enumerate((1, 2, 4)):
                    xr, xi = _cmul_add(mk_ref[k, 0, :, hc], mk_ref[k, 1, :, hc],
                                       pltpu.roll(xr, shift, 0), pltpu.roll(xi, shift, 0), xr, xi)
                cr, ci = carry[q]
                hr, hi = _cmul_add(pw_ref[0, :, hc], pw_ref[1, :, hc], cr, ci, xr, xi)
                st_scr[pl.ds(r0, SUBLANES), re_c] = jnp.where(row0, cr, pltpu.roll(hr, 1, 0))
                st_scr[pl.ds(r0, SUBLANES), im_c] = jnp.where(row0, ci, pltpu.roll(hi, 1, 0))
                out.append((last(hr), last(hi)))
            return tuple(out)

        c0 = j * TILE_COLS
        init = tuple((carry_scr[:, c0 + q * PAIR_COLS:c0 + q * PAIR_COLS + LANES],
                      carry_scr[:, c0 + q * PAIR_COLS + LANES:c0 + (q + 1) * PAIR_COLS])
                     for q in range(PAIRS_PER_TILE))
        final = lax.fori_loop(0, rows // SUBLANES, step, init, unroll=True)
        for q in range(PAIRS_PER_TILE):
            carry_scr[:, c0 + q * PAIR_COLS:c0 + q * PAIR_COLS + LANES] = final[q][0]
            carry_scr[:, c0 + q * PAIR_COLS + LANES:c0 + (q + 1) * PAIR_COLS] = final[q][1]

        _s5_emit_tile(u_ref, st_scr, m_ref, whyt_ref, dt_ref, g_ref, j)

    hT_ref[0] = carry_scr[...]


def _s5_chain(u16, m, ws, why, mk, pw, d_tiled, h0, batch, n_rows, rows):
    nblk = n_rows // rows
    row = pl.BlockSpec((rows, CHUNK_LANES), lambda b, i: (b * nblk + i, 0))
    return pl.pallas_call(
        functools.partial(_s5_chain_kernel, rows=rows),
        grid=(batch, nblk),
        in_specs=[row, _const_spec(m.shape), _const_spec(ws.shape), _const_spec(why.shape),
                  _const_spec(mk.shape), _const_spec(pw.shape), _const_spec(d_tiled.shape),
                  pl.BlockSpec((1, 1, STATE_COLS), lambda b, i: (b, 0, 0))],
        out_specs=[row, pl.BlockSpec((1, SUBLANES, STATE_COLS), lambda b, i: (b, 0, 0))],
        out_shape=[jax.ShapeDtypeStruct(u16.shape, BF16),
                   jax.ShapeDtypeStruct((batch, SUBLANES, STATE_COLS), F32)],
        scratch_shapes=[pltpu.VMEM((SUBLANES, STATE_COLS), F32), pltpu.VMEM((rows, TILE_COLS), F32)],
        compiler_params=_params("arbitrary", "arbitrary"),
        name="s5_chain",
    )(u16, m, ws, why, mk, pw, d_tiled, h0)


def _s5_rows_kernel(u_ref, m_ref, ws_ref, whyt_ref, an_ref, dt_ref, h0r_ref, h0i_ref, g_ref, hTr_ref, hTi_ref,
                    st_scr):
    for j in range(N_LANE_TILES):
        for q in range(PAIRS_PER_TILE):
            hc = slice((j * PAIRS_PER_TILE + q) * LANES, (j * PAIRS_PER_TILE + q + 1) * LANES)
            h0r, h0i = h0r_ref[hc, :].T, h0i_ref[hc, :].T
            local = _s5_local_states(u_ref, ws_ref, j, q)
            hr, hi = _cmul_add(an_ref[0, :1, hc], an_ref[1, :1, hc], h0r, h0i, local[:, :LANES], local[:, LANES:])
            hTr_ref[hc, :] = hr.T
            hTi_ref[hc, :] = hi.T
            st_scr[:, q * PAIR_COLS:q * PAIR_COLS + LANES] = h0r
            st_scr[:, q * PAIR_COLS + LANES:(q + 1) * PAIR_COLS] = h0i
        _s5_emit_tile(u_ref, st_scr, m_ref, whyt_ref, dt_ref, g_ref, j)


def _s5_rows(u16, m, ws, why, an, d_tiled, h0_re_t, h0_im_t, rows):
    n = u16.shape[0]
    once = pl.Buffered(1)
    row = pl.BlockSpec((rows, CHUNK_LANES), lambda i: (i, 0), pipeline_mode=once)
    col = pl.BlockSpec((HALF_COLS, rows), lambda i: (0, i), pipeline_mode=once)
    state = jax.ShapeDtypeStruct((HALF_COLS, n), F32)
    return pl.pallas_call(
        _s5_rows_kernel,
        grid=(n // rows,),
        in_specs=[row, _const_spec(m.shape), _const_spec(ws.shape), _const_spec(why.shape),
                  _const_spec(an.shape), _const_spec(d_tiled.shape), col, col],
        out_specs=[row, col, col],
        out_shape=[jax.ShapeDtypeStruct(u16.shape, BF16), state, state],
        scratch_shapes=[pltpu.VMEM((rows, TILE_COLS), F32)],
        compiler_params=_params("arbitrary"),
        name="s5_rows",
    )(u16, m, ws, why, an, d_tiled, h0_re_t, h0_im_t)


def _post_kernel(g_ref, ma_ref, gs_ref, x_ref, permt_ref, wglu_ref, wout_ref, gffn_ref, wup_ref, wdown_ref,
                 gfin_ref, out_ref, *, n_tok):
    g = _load_chunk_rows(g_ref, permt_ref, n_tok)
    glu = jnp.dot(g, wglu_ref[...], preferred_element_type=F32)
    s_out = glu[:, :D_MODEL] * jax.nn.sigmoid(glu[:, D_MODEL:])
    merged = ma_ref[...] + gs_ref[...].astype(F32) * s_out
    x1 = x_ref[...] + jnp.dot(merged.astype(BF16), wout_ref[...], preferred_element_type=F32)
    h2 = _rmsnorm(x1, gffn_ref[...]).astype(BF16)
    x2 = x1
    for c in range(0, D_FF, FF_CHUNK):
        up = jnp.dot(h2, wup_ref[:, c:c + FF_CHUNK], preferred_element_type=F32)
        act = jnp.square(jnp.maximum(up, 0.0)).astype(BF16)
        x2 = x2 + jnp.dot(act, wdown_ref[c:c + FF_CHUNK, :], preferred_element_type=F32)
    out_ref[...] = _rmsnorm(x2, gfin_ref[...])


def _post(g, ma, gs, x, permt, w_glu, w_out, g_ffn, w_up, w_down, g_final, tm, n_tok):
    n = x.shape[0]
    row = pl.BlockSpec((tm, D_MODEL), lambda i: (i, 0))
    g_spec = pl.BlockSpec((tm // n_tok, CHUNK_LANES), lambda i: (i, 0))
    return pl.pallas_call(
        functools.partial(_post_kernel, n_tok=n_tok),
        grid=(n // tm,),
        in_specs=[g_spec, row, row, row, _const_spec(permt.shape), _const_spec(w_glu.shape),
                  _const_spec(w_out.shape), _const_spec((1, D_MODEL)), _const_spec(w_up.shape),
                  _const_spec(w_down.shape), _const_spec((1, D_MODEL))],
        out_specs=row,
        out_shape=jax.ShapeDtypeStruct((n, D_MODEL), F32),
        compiler_params=_params("arbitrary"),
        name="post",
    )(g, ma, gs, x, permt, w_glu, w_out, g_ffn, w_up, w_down, g_final)


def _tile(n, pref):
    t = pref
    while n % t:
        t //= 2
    return t


def kernel(x_prompt, x_sample, cache_k, cache_v, state_ssm_re, state_ssm_im, g_mix, w_in, attn_sinks,
           w_attn_o, ssm_lambda_re, ssm_lambda_im, ssm_log_dt, ssm_b_re, ssm_b_im, ssm_c_re, ssm_c_im,
           ssm_d, w_glu, w_out, g_ffn, w_up, w_down, g_final):
    batch, seq, _ = x_prompt.shape
    db, dec_seq, _ = x_sample.shape
    assert w_in.shape[0] == 1, "one layer"
    assert seq % PERM_ROWS == 0 and dec_seq in (1, 2, 4, 8) and (db * dec_seq) % PERM_ROWS == 0

    vec = lambda v: v.reshape(1, D_MODEL).astype(F32)
    sinks = attn_sinks[0].astype(F32)
    s5p = (ssm_lambda_re[0], ssm_lambda_im[0], ssm_log_dt[0], ssm_b_re[0], ssm_b_im[0], ssm_c_re[0], ssm_c_im[0])
    d_skip = jnp.tile(ssm_d[0].astype(F32).reshape(SSM_GROUPS, 1, SSM_CH), (1, 1, CHUNK))
    perm = _chunk_perm(CHUNK)
    m, ws, ws_n, whyt, mk, pw, an, w_in_b = _s5_prep(*s5p, dec_seq, w_in[0])

    xp = x_prompt.reshape(batch * seq, D_MODEL)
    q, kv, u16, ga, gs, wo_b, w_glu_b, w_out_b, w_up_b, w_down_b = _proj(
        xp, vec(g_mix[0]), w_in_b, perm, _tile(batch * seq, PROJ_ROWS), CHUNK,
        cast=(w_attn_o[0], w_glu[0], w_out[0], w_up[0], w_down[0]))
    post_w = (w_glu_b, w_out_b, vec(g_ffn[0]), w_up_b, w_down_b, vec(g_final))
    ma = _attn_prompt(sinks, q, kv, ga, wo_b, batch, seq, _tile(seq, ATTN_ROWS))
    n_rows = seq // CHUNK
    g16, hT = _s5_chain(u16, m, ws, whyt, mk, pw, d_skip, jnp.zeros((batch, 1, STATE_COLS), F32),
                        batch, n_rows, _tile(n_rows, S5_BLOCK_ROWS))
    y_prompt = _post(g16, ma, gs, xp, perm.T, *post_w, _tile(batch * seq, POST_ROWS), CHUNK).reshape(
        batch, seq, D_MODEL)
    kv_last = kv.reshape(batch, seq, 2 * D_KV)[:, seq - WINDOW:]
    k_prompt = kv_last[..., :D_KV].reshape(1, batch, WINDOW, N_KV_HEADS, HEAD_DIM)
    v_prompt = kv_last[..., D_KV:].reshape(1, batch, WINDOW, N_KV_HEADS, HEAD_DIM)
    hr, hi = _state_split(hT[:, 0])
    ssm_re_prompt, ssm_im_prompt = hr[None], hi[None]

    ns = db * dec_seq
    xs = x_sample.reshape(ns, D_MODEL)
    perm_s = _chunk_perm(dec_seq)
    q, kv, u16, ga, gs = _proj(xs, vec(g_mix[0]), w_in_b, perm_s, _tile(ns, PROJ_ROWS), dec_seq)
    tpad = SUBLANES
    pad3 = lambda v: jnp.pad(v.reshape(db, dec_seq, -1).astype(F32), ((0, 0), (0, tpad - dec_seq), (0, 0)))
    ga_pad = pad3(ga).astype(BF16).reshape(db * tpad, D_MODEL)
    window_last = lambda c: c[0].astype(F32).transpose(0, 2, 3, 1).reshape(db, D_KV, WINDOW)
    ma_pad, k_new, v_new = _attn_sample(
        sinks, pad3(q), pad3(kv), window_last(cache_k), window_last(cache_v), ga_pad, wo_b,
        _tile(db, SAMPLE_SEQS), dec_seq)
    ma = ma_pad.reshape(db, tpad, D_MODEL)[:, :dec_seq].reshape(ns, D_MODEL)
    to_cols = lambda s: s[0].astype(F32).transpose(1, 2, 0).reshape(HALF_COLS, db)
    from_cols = lambda h: h.reshape(SSM_GROUPS, SSM_STATE, db).transpose(2, 0, 1)[None]
    g16, hT_re, hT_im = _s5_rows(u16, m, ws_n, whyt, an, d_skip, to_cols(state_ssm_re), to_cols(state_ssm_im),
                                 _tile(db, LANES))
    y_sample = _post(g16, ma, gs, xs, perm_s.T, *post_w, PERM_ROWS, dec_seq).reshape(db, dec_seq, D_MODEL)
    k_sample = k_new.reshape(1, db, WINDOW, N_KV_HEADS, HEAD_DIM)
    v_sample = v_new.reshape(1, db, WINDOW, N_KV_HEADS, HEAD_DIM)
    ssm_re_sample, ssm_im_sample = from_cols(hT_re), from_cols(hT_im)

    return (y_prompt, y_sample, k_prompt, v_prompt, ssm_re_prompt, ssm_im_prompt,
            k_sample, v_sample, ssm_re_sample, ssm_im_sample)
```

```python
import functools
import math

import jax
import jax.numpy as jnp
from jax import lax
from jax.experimental import pallas as pl
from jax.experimental.pallas import tpu as pltpu

F32 = jnp.float32
BF16 = jnp.bfloat16

D_MODEL = 1024
HEAD_DIM = 64
N_HEADS = 16
N_KV_HEADS = 4
WINDOW = 128
D_KV = N_KV_HEADS * HEAD_DIM
SSM_CH = 16
SSM_GROUPS = 64
SSM_STATE = 64
D_FF = 4 * D_MODEL
FF_CHUNK = 1024
RMS_EPS = 1e-5
LOG2E = 1.4426950408889634
Q_SCALE = HEAD_DIM ** -0.5 * LOG2E

LANES = 128
SUBLANES = 8
BF16_ROWS = 16
N_LANE_TILES = D_MODEL // LANES
GROUPS_PER_TILE = LANES // SSM_CH
CHUNK = 16
CHUNK_LANES = CHUNK * D_MODEL
GROUP_IO = CHUNK * SSM_CH
N_PAIRS = SSM_GROUPS // 2
PAIRS_PER_TILE = GROUPS_PER_TILE // 2
PAIR_COLS = 2 * LANES
TILE_COLS = PAIRS_PER_TILE * PAIR_COLS
STATE_COLS = N_PAIRS * PAIR_COLS
HALF_COLS = STATE_COLS // 2
PERM_ROWS = CHUNK * BF16_ROWS
PREP_PAIRS = 4
PROJ_ROWS = 1024
ATTN_ROWS = 2048
POST_ROWS = 512
S5_BLOCK_ROWS = 128
SAMPLE_SEQS = 32
VMEM_LIMIT = 56 * 1024 * 1024

_Q0, _KV0, _U0, _GA0, _GS0, _END = 0, 1024, 1536, 2560, 3584, 4608


def _gelu_tanh(x):
    k1 = -2.0 * math.sqrt(2.0 / math.pi) * LOG2E
    return x / (1.0 + jnp.exp2(x * (k1 + (0.044715 * k1) * (x * x))))


def _rmsnorm(x, g):
    return x * lax.rsqrt(jnp.mean(x * x, axis=-1, keepdims=True) + RMS_EPS) * g


def _params(*sem):
    return pltpu.CompilerParams(dimension_semantics=sem, vmem_limit_bytes=VMEM_LIMIT)


def _const_spec(shape):
    nd = len(shape)
    return pl.BlockSpec(shape, lambda *_: (0,) * nd, pipeline_mode=pl.Buffered(1))


def _chunk_perm(n_tok):
    n_chunks = PERM_ROWS // n_tok
    r = jnp.arange(PERM_ROWS)
    tok = (r % n_chunks) * n_tok + r // n_chunks
    return (tok[:, None] == jnp.arange(PERM_ROWS)[None, :]).astype(BF16)


def _piece_transpose(cols, masks):
    for d, msk in zip((4, 2, 1), masks):
        new = list(cols)
        for v in range(GROUPS_PER_TILE):
            if v & d == 0:
                a, b = cols[v], cols[v + d]
                new[v] = jnp.where(msk, pltpu.roll(b, SSM_CH * d, 1), a)
                new[v + d] = jnp.where(msk, b, pltpu.roll(a, LANES - SSM_CH * d, 1))
        cols = new
    return cols


def _piece_masks(rows):
    piece = lax.broadcasted_iota(jnp.int32, (rows, LANES), 1) // SSM_CH
    return [(piece & d) != 0 for d in (4, 2, 1)]


def _store_chunk_rows(x, perm_ref, out_ref, n_tok):
    n_chunks = PERM_ROWS // n_tok
    masks = _piece_masks(n_chunks)
    zero = jnp.zeros((n_chunks, LANES), F32)
    for hb in range(x.shape[0] // PERM_ROWS):
        rows = slice(hb * n_chunks, (hb + 1) * n_chunks)
        xp = jnp.dot(perm_ref[...], x[hb * PERM_ROWS:(hb + 1) * PERM_ROWS],
                     preferred_element_type=F32)
        for j in range(N_LANE_TILES):
            for hf in range(2):
                live = 8 * hf < n_tok
                cols = [xp[t * n_chunks:(t + 1) * n_chunks, j * LANES:(j + 1) * LANES] if t < n_tok else zero
                        for t in range(8 * hf, 8 * hf + 8)]
                for gl, col in enumerate(_piece_transpose(cols, masks) if live else cols):
                    c0 = (GROUPS_PER_TILE * j + gl) * GROUP_IO + hf * LANES
                    out_ref[rows, c0:c0 + LANES] = col.astype(BF16)


def _load_chunk_rows(in_ref, permt_ref, n_tok):
    n_chunks = PERM_ROWS // n_tok
    masks = _piece_masks(n_chunks)
    blocks = []
    for hb in range(in_ref.shape[0] // n_chunks):
        rows = slice(hb * n_chunks, (hb + 1) * n_chunks)
        tiles = [[None] * N_LANE_TILES for _ in range(n_tok)]
        for j in range(N_LANE_TILES):
            for hf in range(-(-n_tok // 8)):
                cols = []
                for gl in range(GROUPS_PER_TILE):
                    c0 = (GROUPS_PER_TILE * j + gl) * GROUP_IO + hf * LANES
                    cols.append(in_ref[rows, c0:c0 + LANES].astype(F32))
                for k, col in enumerate(_piece_transpose(cols, masks)):
                    if 8 * hf + k < n_tok:
                        tiles[8 * hf + k][j] = col.astype(BF16)
        xp = jnp.concatenate([jnp.concatenate(row, axis=1) for row in tiles], axis=0)
        blocks.append(jnp.dot(permt_ref[...], xp, preferred_element_type=F32).astype(BF16))
    return blocks[0] if len(blocks) == 1 else jnp.concatenate(blocks, axis=0)


def _proj_kernel(x_ref, g_ref, w_ref, perm_ref, *refs, n_tok, n_cast):
    q_ref, kv_ref, u_ref, ga_ref, gs_ref = refs[n_cast:n_cast + 5]
    h = _rmsnorm(x_ref[...], g_ref[...]).astype(BF16)

    def seg(lo, hi):
        return jnp.dot(h, w_ref[:, lo:hi], preferred_element_type=F32)

    q_ref[...] = (seg(_Q0, _KV0) * Q_SCALE).astype(BF16)
    kv_ref[...] = seg(_KV0, _U0)
    _store_chunk_rows(seg(_U0, _GA0).astype(BF16), perm_ref, u_ref, n_tok)
    ga_ref[...] = jax.nn.sigmoid(seg(_GA0, _GS0)).astype(BF16)
    gs_ref[...] = jax.nn.sigmoid(seg(_GS0, _END)).astype(BF16)
    for src, dst in zip(refs[:n_cast], refs[n_cast + 5:]):
        dst[...] = src[...].astype(BF16)


def _proj(x, g_mix, w_in, perm, tm, n_tok, cast=()):
    n = x.shape[0]
    steps = n // tm
    row = lambda w: pl.BlockSpec((tm, w), lambda i: (i, 0))
    slab = lambda w: pl.BlockSpec((w.shape[0] // steps, w.shape[1]), lambda i: (i, 0))
    assert all(w.shape[0] % (steps * BF16_ROWS) == 0 for w in cast)
    return pl.pallas_call(
        functools.partial(_proj_kernel, n_tok=n_tok, n_cast=len(cast)),
        grid=(steps,),
        in_specs=[row(D_MODEL), _const_spec((1, D_MODEL)), _const_spec(w_in.shape), _const_spec(perm.shape)]
        + [slab(w) for w in cast],
        out_specs=[row(D_MODEL), row(2 * D_KV), pl.BlockSpec((tm // n_tok, CHUNK_LANES), lambda i: (i, 0)),
                   row(D_MODEL), row(D_MODEL)] + [slab(w) for w in cast],
        out_shape=[jax.ShapeDtypeStruct((n, D_MODEL), BF16),
                   jax.ShapeDtypeStruct((n, 2 * D_KV), F32),
                   jax.ShapeDtypeStruct((n // n_tok, CHUNK_LANES), BF16),
                   jax.ShapeDtypeStruct((n, D_MODEL), BF16),
                   jax.ShapeDtypeStruct((n, D_MODEL), BF16)]
        + [jax.ShapeDtypeStruct(w.shape, BF16) for w in cast],
        compiler_params=_params("arbitrary"),
        name="proj",
    )(x, g_mix, w_in, perm, *cast)


def _dup_heads(tile):
    lo = lax.broadcasted_iota(jnp.int32, tile.shape, tile.ndim - 1) < HEAD_DIM
    rolled = pltpu.roll(tile, HEAD_DIM, tile.ndim - 1)
    return (jnp.where(lo, tile, rolled).astype(BF16), jnp.where(lo, rolled, tile).astype(BF16))


def _sink_softmax(s, sink):
    sink2 = sink * LOG2E
    m = jnp.maximum(jnp.max(s, axis=-1, keepdims=True), sink2)
    p = jnp.exp2(s - m)
    denom = jnp.sum(p, axis=-1, keepdims=True) + jnp.exp2(sink2 - m)
    return p.astype(BF16), 1.0 / denom


def _attn_prompt_kernel(sinks_ref, q_ref, kvc_ref, kvp_ref, ga_ref, wo_ref, out_ref,
                        kd_scr, vd_scr, bias_scr, attn_scr, *, tq):
    i = pl.program_id(1)
    kv_full = jnp.concatenate([kvp_ref[...], kvc_ref[...]], axis=0)
    for t in range(2):
        ke, ko = _dup_heads(kv_full[:, t * LANES:(t + 1) * LANES])
        ve, vo = _dup_heads(kv_full[:, D_KV + t * LANES:D_KV + (t + 1) * LANES])
        kd_scr[2 * t], kd_scr[2 * t + 1] = ke, ko
        vd_scr[2 * t], vd_scr[2 * t + 1] = ve, vo

    qi = lax.broadcasted_iota(jnp.int32, (WINDOW, 2 * WINDOW), 0)
    si = lax.broadcasted_iota(jnp.int32, (WINDOW, 2 * WINDOW), 1)
    band = (si > qi) & (si <= qi + WINDOW)
    in_block = si >= WINDOW
    lo = lax.broadcasted_iota(jnp.int32, (WINDOW, LANES), 1) < HEAD_DIM

    def attend(jb):
        q0 = jb * WINDOW
        has_prev = (i > 0) | (jb > 0)
        bias_scr[...] = jnp.where(band & (in_block | has_prev), 0.0, -jnp.inf)

        def head_pair(hp, c):
            j = hp // 2
            c0 = pl.multiple_of(hp * LANES, LANES)
            kd = kd_scr[j, pl.ds(q0, 2 * WINDOW), :]
            vd = vd_scr[j, pl.ds(q0, 2 * WINDOW), :]
            qp = q_ref[pl.ds(q0, WINDOW), pl.ds(c0, LANES)]
            outs = []
            for par in range(2):
                qm = jnp.where(lo if par == 0 else ~lo, qp, jnp.zeros_like(qp))
                s = lax.dot_general(qm, kd, (((1,), (1,)), ((), ())), preferred_element_type=F32)
                p, inv = _sink_softmax(s + bias_scr[...], sinks_ref[2 * hp + par])
                outs.append(jnp.dot(p, vd, preferred_element_type=F32) * inv)
            attn_scr[pl.ds(q0, WINDOW), pl.ds(c0, LANES)] = jnp.where(lo, outs[0], outs[1]).astype(BF16)
            return c

        lax.fori_loop(0, N_HEADS // 2, head_pair, 0, unroll=8)

    def project(jb):
        rows = pl.ds(jb * WINDOW, WINDOW)
        a_out = jnp.dot(attn_scr[rows, :], wo_ref[...], preferred_element_type=F32)
        out_ref[rows, :] = ga_ref[rows, :].astype(F32) * a_out

    for jb in range(tq // WINDOW):
        attend(jb)
        project(jb)


def _attn_prompt(sinks, q, kv, ga, wo, batch, seq, tq):
    nq = seq // tq
    bpt = tq // WINDOW
    row = lambda w: pl.BlockSpec((tq, w), lambda b, i: (b * nq + i, 0))
    prev = pl.BlockSpec((WINDOW, 2 * D_KV),
                        lambda b, i: (jnp.maximum((b * nq + i) * bpt - 1, 0), 0))
    return pl.pallas_call(
        functools.partial(_attn_prompt_kernel, tq=tq),
        grid=(batch, nq),
        in_specs=[pl.BlockSpec(memory_space=pltpu.SMEM), row(D_MODEL), row(2 * D_KV), prev,
                  row(D_MODEL), _const_spec(wo.shape)],
        out_specs=row(D_MODEL),
        out_shape=jax.ShapeDtypeStruct((batch * seq, D_MODEL), F32),
        scratch_shapes=[pltpu.VMEM((N_KV_HEADS, WINDOW + tq, LANES), BF16),
                        pltpu.VMEM((N_KV_HEADS, WINDOW + tq, LANES), BF16),
                        pltpu.VMEM((WINDOW, 2 * WINDOW), F32),
                        pltpu.VMEM((tq, D_MODEL), BF16)],
        compiler_params=_params("arbitrary", "arbitrary"),
        name="attn_prompt",
    )(sinks, q, kv, kv, ga, wo)


def _attn_sample_kernel(sinks_ref, q_ref, kvn_ref, ck_ref, cv_ref, ga_ref, wo_ref,
                        out_ref, ko_ref, vo_ref, attn_scr, *, bb, tpad, dec_seq):
    nk = WINDOW + tpad
    lo3 = lax.broadcasted_iota(jnp.int32, (bb, tpad, LANES), 2) < HEAD_DIM
    row = lax.broadcasted_iota(jnp.int32, (4 * tpad, nk), 0)
    si = lax.broadcasted_iota(jnp.int32, (4 * tpad, nk), 1)
    tq = row % tpad
    valid = (si > tq) & (si <= tq + WINDOW) & (si < WINDOW + dec_seq)
    hrow = lax.broadcasted_iota(jnp.int32, (4 * tpad, 1), 0) // tpad

    keys_first = lambda ref, sl: jnp.stack([ref[b, sl, :].T for b in range(bb)])
    for t in range(2):
        sl = slice(t * LANES, (t + 1) * LANES)
        kk = jnp.concatenate([keys_first(ck_ref, sl), kvn_ref[:, :, sl]], axis=1)
        vv = jnp.concatenate([keys_first(cv_ref, sl),
                              kvn_ref[:, :, D_KV + t * LANES:D_KV + (t + 1) * LANES]], axis=1)
        ko_ref[:, :, sl] = kk[:, dec_seq:dec_seq + WINDOW, :]
        vo_ref[:, :, sl] = vv[:, dec_seq:dec_seq + WINDOW, :]
        kds = _dup_heads(kk)
        vds = _dup_heads(vv)
        for par_kv in range(2):
            j = 2 * t + par_kv
            kd, vd = kds[par_kv], vds[par_kv]
            parts = []
            for r in range(2):
                c0 = j * 2 * LANES + r * LANES
                qp = q_ref[:, :, c0:c0 + LANES]
                parts += [jnp.where(lo3, qp, 0.0), jnp.where(lo3, 0.0, qp)]
            lhs = jnp.concatenate(parts, axis=1).astype(BF16)
            s = jnp.einsum('bqd,bkd->bqk', lhs, kd, preferred_element_type=F32)
            sink = jnp.zeros((4 * tpad, 1), F32)
            for g in range(4):
                sink = jnp.where(hrow == g, sinks_ref[4 * j + g], sink)
            p, inv = _sink_softmax(jnp.where(valid[None], s, -jnp.inf), sink[None])
            o = jnp.einsum('bqk,bkd->bqd', p, vd, preferred_element_type=F32) * inv
            for r in range(2):
                c0 = j * 2 * LANES + r * LANES
                o_even = o[:, (2 * r) * tpad:(2 * r + 1) * tpad, :]
                o_odd = o[:, (2 * r + 1) * tpad:(2 * r + 2) * tpad, :]
                attn_scr[:, :, c0:c0 + LANES] = jnp.where(lo3, o_even, o_odd)

    attn = attn_scr[...].reshape(bb * tpad, D_MODEL).astype(BF16)
    a_out = jnp.dot(attn, wo_ref[...], preferred_element_type=F32)
    out_ref[...] = ga_ref[...].astype(F32) * a_out


def _attn_sample(sinks, q3, kvn3, ck, cv, ga2, wo, bb, dec_seq):
    db, tpad, _ = q3.shape
    blk3 = lambda r, w: pl.BlockSpec((bb, r, w), lambda i: (i, 0, 0))
    row = pl.BlockSpec((bb * tpad, D_MODEL), lambda i: (i, 0))
    return pl.pallas_call(
        functools.partial(_attn_sample_kernel, bb=bb, tpad=tpad, dec_seq=dec_seq),
        grid=(db // bb,),
        in_specs=[pl.BlockSpec(memory_space=pltpu.SMEM), blk3(tpad, D_MODEL), blk3(tpad, 2 * D_KV),
                  blk3(D_KV, WINDOW), blk3(D_KV, WINDOW), row, _const_spec(wo.shape)],
        out_specs=[row, blk3(WINDOW, D_KV), blk3(WINDOW, D_KV)],
        out_shape=[jax.ShapeDtypeStruct((db * tpad, D_MODEL), F32),
                   jax.ShapeDtypeStruct((db, WINDOW, D_KV), F32),
                   jax.ShapeDtypeStruct((db, WINDOW, D_KV), F32)],
        scratch_shapes=[pltpu.VMEM((bb, tpad, D_MODEL), F32)],
        compiler_params=_params("arbitrary"),
        name="attn_sample",
    )(sinks, q3, kvn3, ck, cv, ga2, wo)


def _s5_discretize(lam_re, lam_im, log_dt, b_re, b_im):
    dt = jnp.exp(log_dt)[:, None]
    decay = jnp.exp(lam_re * dt)
    ab_re = decay * jnp.cos(lam_im * dt)
    ab_im = decay * jnp.sin(lam_im * dt)
    nr, ni = ab_re - 1.0, ab_im
    den = lam_re * lam_re + lam_im * lam_im
    f_re = ((nr * lam_re + ni * lam_im) / den)[..., None]
    f_im = ((ni * lam_re - nr * lam_im) / den)[..., None]
    return ab_re, ab_im, f_re * b_re - f_im * b_im, f_re * b_im + f_im * b_re


def _state_split(h):
    h = h.reshape(h.shape[:-1] + (N_PAIRS, 2, 2, SSM_STATE))
    unpair = lambda a: a.reshape(a.shape[:-3] + (SSM_GROUPS, SSM_STATE))
    return unpair(h[..., 0, :, :]), unpair(h[..., 1, :, :])


def _shift_lanes(x, n):
    a, b = x[:, :LANES], x[:, LANES:]
    lane = lax.broadcasted_iota(jnp.int32, a.shape, 1)
    if n == 0:
        return x
    if n >= LANES:
        r = n - LANES
        hi = a if r == 0 else jnp.where(lane < r, 0.0, pltpu.roll(a, r, 1))
        return jnp.concatenate([jnp.zeros_like(a), hi], axis=1)
    ra, rb = pltpu.roll(a, n, 1), pltpu.roll(b, n, 1)
    return jnp.concatenate([jnp.where(lane < n, 0.0, ra), jnp.where(lane < n, ra, rb)], axis=1)


def _dot_nt_split(lhs, rhs):
    nt = lambda a, b: lax.dot_general(a, b, (((1,), (1,)), ((), ())), preferred_element_type=F32)
    l_hi, r_hi = lhs.astype(BF16), rhs.astype(BF16)
    l_lo = (lhs - l_hi.astype(F32)).astype(BF16)
    r_lo = (rhs - r_hi.astype(F32)).astype(BF16)
    return nt(l_hi, r_hi) + nt(l_hi, r_lo) + nt(l_lo, r_hi)


def _s5_prep_kernel(a_ref, c_ref, bt_ref, w_ref, m_ref, ws_ref, wsn_ref, whyt_ref, mk_ref, pw_ref, an_ref,
                    wb_ref, *, n_tok):
    wb_ref[...] = w_ref[...].astype(BF16)
    for p in range(PREP_PAIRS):
        one, two, lanes = pl.ds(p, 1), pl.ds(2 * p, 2), pl.ds(p * LANES, LANES)
        _s5_prep_pair(a_ref.at[one], c_ref.at[one], bt_ref.at[one], m_ref.at[two], ws_ref.at[two],
                      wsn_ref.at[two], whyt_ref.at[two], mk_ref.at[:, :, :, lanes], pw_ref.at[:, :, lanes],
                      an_ref.at[:, :, lanes], n_tok)


def _s5_prep_pair(a_ref, c_ref, bt_ref, m_ref, ws_ref, wsn_ref, whyt_ref, mk_ref, pw_ref, an_ref, n_tok):
    ar, ai = a_ref[0, 0], a_ref[0, 1]
    cr, ci = c_ref[0, 0], c_ref[0, 1]
    br, bi = bt_ref[0, 0], bt_ref[0, 1]
    pr, pi = [jnp.ones_like(ar)], [jnp.zeros_like(ai)]
    for _ in range(CHUNK):
        pr, pi = pr + [pr[-1] * ar - pi[-1] * ai], pi + [pr[-1] * ai + pi[-1] * ar]
    first = lax.broadcasted_iota(jnp.int32, (SSM_CH, LANES), 1) < SSM_STATE
    ca = [(cr * pr[t] - ci * pi[t], -(cr * pi[t] + ci * pr[t])) for t in range(CHUNK + 1)]
    ca_full = jnp.concatenate([jnp.concatenate(ca[t], axis=1) for t in range(CHUNK)], axis=0)
    zero = jnp.zeros((SSM_CH, LANES), F32)
    for e in range(2):
        own = first if e == 0 else ~first
        pick = lambda v: jnp.where(own, v, zero)
        for t in range(CHUNK):
            rows = slice(t * SSM_CH, (t + 1) * SSM_CH)
            whyt_ref[e, rows, :LANES] = pick(ca[t + 1][0]).astype(BF16)
            whyt_ref[e, rows, LANES:] = pick(ca[t + 1][1]).astype(BF16)
            for ref, k in ((ws_ref, CHUNK - 1 - t), (wsn_ref, n_tok - 1 - t)):
                if k >= 0:
                    ref[e, rows, :LANES] = pick(br * pr[k] - bi * pi[k]).astype(BF16)
                    ref[e, rows, LANES:] = pick(br * pi[k] + bi * pr[k]).astype(BF16)
                else:
                    ref[e, rows, :] = jnp.zeros((SSM_CH, PAIR_COLS), BF16)
        k_row = _dot_nt_split(jnp.concatenate([pick(br), pick(bi)], axis=1), ca_full)
        for s in range(CHUNK):
            m_ref[e, s * SSM_CH:(s + 1) * SSM_CH, :] = _shift_lanes(k_row, s * SSM_CH).astype(BF16)
    sub = lax.broadcasted_iota(jnp.int32, (SUBLANES, LANES), 0)
    qr, qi = [pr[CHUNK]], [pi[CHUNK]]
    for _ in range(SUBLANES - 1):
        qr, qi = qr + [qr[-1] * pr[CHUNK] - qi[-1] * pi[CHUNK]], qi + [qr[-1] * pi[CHUNK] + qi[-1] * pr[CHUNK]]
    for part, q in enumerate((qr, qi)):
        rows = jnp.zeros((SUBLANES, LANES), F32)
        for s in range(SUBLANES):
            rows = jnp.where(sub == s, q[s], rows)
        pw_ref[part] = rows
        for k, shift in enumerate((1, 2, 4)):
            mk_ref[k, part] = jnp.where(sub >= shift, q[shift - 1], 0.0)
    an_ref[0] = jnp.broadcast_to(pr[n_tok], (SUBLANES, LANES))
    an_ref[1] = jnp.broadcast_to(pi[n_tok], (SUBLANES, LANES))


def _s5_prep(lam_re, lam_im, log_dt, b_re, b_im, c_re, c_im, n_tok, w_in):
    ab_re, ab_im, bb_re, bb_im = _s5_discretize(lam_re, lam_im, log_dt, b_re, b_im)
    pair = lambda v: v.reshape(N_PAIRS, 2, v.shape[1], SSM_STATE).transpose(0, 2, 1, 3).reshape(
        N_PAIRS, v.shape[1], LANES)
    a = jnp.stack([pair(ab_re[:, None, :]), pair(ab_im[:, None, :])], axis=1)
    c = jnp.stack([pair(c_re), pair(c_im)], axis=1)
    bt = jnp.stack([pair(bb_re.transpose(0, 2, 1)), pair(bb_im.transpose(0, 2, 1))], axis=1)
    blk4 = lambda r: pl.BlockSpec((PREP_PAIRS, 2, r, LANES), lambda q: (q, 0, 0, 0))
    w_spec = pl.BlockSpec((2 * PREP_PAIRS, GROUP_IO, PAIR_COLS), lambda q: (q, 0, 0))
    w_shape = jax.ShapeDtypeStruct((SSM_GROUPS, GROUP_IO, PAIR_COLS), BF16)
    steps = N_PAIRS // PREP_PAIRS
    assert w_in.shape[0] % (steps * BF16_ROWS) == 0
    slab = pl.BlockSpec((w_in.shape[0] // steps, w_in.shape[1]), lambda q: (q, 0))
    return pl.pallas_call(
        functools.partial(_s5_prep_kernel, n_tok=n_tok),
        grid=(steps,),
        in_specs=[blk4(1), blk4(SSM_CH), blk4(SSM_CH), slab],
        out_specs=[w_spec, w_spec, w_spec, w_spec,
                   pl.BlockSpec((3, 2, SUBLANES, PREP_PAIRS * LANES), lambda q: (0, 0, 0, q)),
                   pl.BlockSpec((2, SUBLANES, PREP_PAIRS * LANES), lambda q: (0, 0, q)),
                   pl.BlockSpec((2, SUBLANES, PREP_PAIRS * LANES), lambda q: (0, 0, q)), slab],
        out_shape=[w_shape, w_shape, w_shape, w_shape,
                   jax.ShapeDtypeStruct((3, 2, SUBLANES, HALF_COLS), F32),
                   jax.ShapeDtypeStruct((2, SUBLANES, HALF_COLS), F32),
                   jax.ShapeDtypeStruct((2, SUBLANES, HALF_COLS), F32),
                   jax.ShapeDtypeStruct(w_in.shape, BF16)],
        compiler_params=_params("arbitrary"),
        name="s5_prep",
    )(a, c, bt, w_in)


def _group_io(ref, g):
    return ref.at[:, g * GROUP_IO:(g + 1) * GROUP_IO]


def _s5_emit_tile(u_ref, st_scr, m_ref, whyt_ref, dt_ref, g_ref, j):
    for gl in range(GROUPS_PER_TILE):
        g = GROUPS_PER_TILE * j + gl
        q0 = (gl // 2) * PAIR_COLS
        u_g = _group_io(u_ref, g)[...]
        y = (jnp.dot(u_g, m_ref[g], preferred_element_type=F32)
             + lax.dot_general(st_scr[:, q0:q0 + PAIR_COLS].astype(BF16), whyt_ref[g],
                               (((1,), (1,)), ((), ())), preferred_element_type=F32)
             + dt_ref[g] * u_g.astype(F32))
        _group_io(g_ref, g)[...] = _gelu_tanh(y).astype(BF16)


def _s5_local_states(u_ref, ws_ref, j, q):
    g = GROUPS_PER_TILE * j + 2 * q
    return (jnp.dot(_group_io(u_ref, g)[...], ws_ref[g], preferred_element_type=F32)
            + jnp.dot(_group_io(u_ref, g + 1)[...], ws_ref[g + 1], preferred_element_type=F32))


def _cmul_add(ar, ai, hr, hi, xr, xi):
    return ar * hr - ai * hi + xr, ar * hi + ai * hr + xi


def _s5_chain_kernel(u_ref, m_ref, ws_ref, whyt_ref, mk_ref, pw_ref, dt_ref, h0_ref, g_ref, hT_ref,
                     carry_scr, st_scr, *, rows):
    @pl.when(pl.program_id(1) == 0)
    def _():
        carry_scr[...] = jnp.broadcast_to(h0_ref[0], (SUBLANES, STATE_COLS))

    row0 = lax.broadcasted_iota(jnp.int32, (SUBLANES, LANES), 0) == 0
    last = lambda h: jnp.broadcast_to(h[SUBLANES - 1:, :], (SUBLANES, LANES))

    for j in range(N_LANE_TILES):
        for q in range(PAIRS_PER_TILE):
            st_scr[:, q * PAIR_COLS:(q + 1) * PAIR_COLS] = _s5_local_states(u_ref, ws_ref, j, q)

        def step(r, carry):
            r0 = pl.multiple_of(r * SUBLANES, SUBLANES)
            out = []
            for q in range(PAIRS_PER_TILE):
                re_c = slice(q * PAIR_COLS, q * PAIR_COLS + LANES)
                im_c = slice(q * PAIR_COLS + LANES, (q + 1) * PAIR_COLS)
                hc = slice((j * PAIRS_PER_TILE + q) * LANES, (j * PAIRS_PER_TILE + q + 1) * LANES)
                xr = st_scr[pl.ds(r0, SUBLANES), re_c]
                xi = st_scr[pl.ds(r0, SUBLANES), im_c]
                for k, shift in enumerate((1, 2, 4)):
                    xr, xi = _cmul_add(mk_ref[k, 0, :, hc], mk_ref[k, 1, :, hc],
                                       pltpu.roll(xr, shift, 0), pltpu.roll(xi, shift, 0), xr, xi)
                cr, ci = carry[q]
                hr, hi = _cmul_add(pw_ref[0, :, hc], pw_ref[1, :, hc], cr, ci, xr, xi)
                st_scr[pl.ds(r0, SUBLANES), re_c] = jnp.where(row0, cr, pltpu.roll(hr, 1, 0))
                st_scr[pl.ds(r0, SUBLANES), im_c] = jnp.where(row0, ci, pltpu.roll(hi, 1, 0))
                out.append((last(hr), last(hi)))
            return tuple(out)

        c0 = j * TILE_COLS
        init = tuple((carry_scr[:, c0 + q * PAIR_COLS:c0 + q * PAIR_COLS + LANES],
                      carry_scr[:, c0 + q * PAIR_COLS + LANES:c0 + (q + 1) * PAIR_COLS])
                     for q in range(PAIRS_PER_TILE))
        final = lax.fori_loop(0, rows // SUBLANES, step, init, unroll=True)
        for q in range(PAIRS_PER_TILE):
            carry_scr[:, c0 + q * PAIR_COLS:c0 + q * PAIR_COLS + LANES] = final[q][0]
            carry_scr[:, c0 + q * PAIR_COLS + LANES:c0 + (q + 1) * PAIR_COLS] = final[q][1]

        _s5_emit_tile(u_ref, st_scr, m_ref, whyt_ref, dt_ref, g_ref, j)

    hT_ref[0] = carry_scr[...]


def _s5_chain(u16, m, ws, why, mk, pw, d_tiled, h0, batch, n_rows, rows):
    nblk = n_rows // rows
    row = pl.BlockSpec((rows, CHUNK_LANES), lambda b, i: (b * nblk + i, 0))
    return pl.pallas_call(
        functools.partial(_s5_chain_kernel, rows=rows),
        grid=(batch, nblk),
        in_specs=[row, _const_spec(m.shape), _const_spec(ws.shape), _const_spec(why.shape),
                  _const_spec(mk.shape), _const_spec(pw.shape), _const_spec(d_tiled.shape),
                  pl.BlockSpec((1, 1, STATE_COLS), lambda b, i: (b, 0, 0))],
        out_specs=[row, pl.BlockSpec((1, SUBLANES, STATE_COLS), lambda b, i: (b, 0, 0))],
        out_shape=[jax.ShapeDtypeStruct(u16.shape, BF16),
                   jax.ShapeDtypeStruct((batch, SUBLANES, STATE_COLS), F32)],
        scratch_shapes=[pltpu.VMEM((SUBLANES, STATE_COLS), F32), pltpu.VMEM((rows, TILE_COLS), F32)],
        compiler_params=_params("arbitrary", "arbitrary"),
        name="s5_chain",
    )(u16, m, ws, why, mk, pw, d_tiled, h0)


def _s5_rows_kernel(u_ref, m_ref, ws_ref, whyt_ref, an_ref, dt_ref, h0r_ref, h0i_ref, g_ref, hTr_ref, hTi_ref,
                    st_scr):
    for j in range(N_LANE_TILES):
        for q in range(PAIRS_PER_TILE):
            hc = slice((j * PAIRS_PER_TILE + q) * LANES, (j * PAIRS_PER_TILE + q + 1) * LANES)
            h0r, h0i = h0r_ref[hc, :].T, h0i_ref[hc, :].T
            local = _s5_local_states(u_ref, ws_ref, j, q)
            hr, hi = _cmul_add(an_ref[0, :1, hc], an_ref[1, :1, hc], h0r, h0i, local[:, :LANES], local[:, LANES:])
            hTr_ref[hc, :] = hr.T
            hTi_ref[hc, :] = hi.T
            st_scr[:, q * PAIR_COLS:q * PAIR_COLS + LANES] = h0r
            st_scr[:, q * PAIR_COLS + LANES:(q + 1) * PAIR_COLS] = h0i
        _s5_emit_tile(u_ref, st_scr, m_ref, whyt_ref, dt_ref, g_ref, j)


def _s5_rows(u16, m, ws, why, an, d_tiled, h0_re_t, h0_im_t, rows):
    n = u16.shape[0]
    once = pl.Buffered(1)
    row = pl.BlockSpec((rows, CHUNK_LANES), lambda i: (i, 0), pipeline_mode=once)
    col = pl.BlockSpec((HALF_COLS, rows), lambda i: (0, i), pipeline_mode=once)
    state = jax.ShapeDtypeStruct((HALF_COLS, n), F32)
    return pl.pallas_call(
        _s5_rows_kernel,
        grid=(n // rows,),
        in_specs=[row, _const_spec(m.shape), _const_spec(ws.shape), _const_spec(why.shape),
                  _const_spec(an.shape), _const_spec(d_tiled.shape), col, col],
        out_specs=[row, col, col],
        out_shape=[jax.ShapeDtypeStruct(u16.shape, BF16), state, state],
        scratch_shapes=[pltpu.VMEM((rows, TILE_COLS), F32)],
        compiler_params=_params("arbitrary"),
        name="s5_rows",
    )(u16, m, ws, why, an, d_tiled, h0_re_t, h0_im_t)


def _post_kernel(g_ref, ma_ref, gs_ref, x_ref, permt_ref, wglu_ref, wout_ref, gffn_ref, wup_ref, wdown_ref,
                 gfin_ref, out_ref, *, n_tok):
    g = _load_chunk_rows(g_ref, permt_ref, n_tok)
    glu = jnp.dot(g, wglu_ref[...], preferred_element_type=F32)
    s_out = glu[:, :D_MODEL] * jax.nn.sigmoid(glu[:, D_MODEL:])
    merged = ma_ref[...] + gs_ref[...].astype(F32) * s_out
    x1 = x_ref[...] + jnp.dot(merged.astype(BF16), wout_ref[...], preferred_element_type=F32)
    h2 = _rmsnorm(x1, gffn_ref[...]).astype(BF16)
    x2 = x1
    for c in range(0, D_FF, FF_CHUNK):
        up = jnp.dot(h2, wup_ref[:, c:c + FF_CHUNK], preferred_element_type=F32)
        act = jnp.square(jnp.maximum(up, 0.0)).astype(BF16)
        x2 = x2 + jnp.dot(act, wdown_ref[c:c + FF_CHUNK, :], preferred_element_type=F32)
    out_ref[...] = _rmsnorm(x2, gfin_ref[...])


def _post(g, ma, gs, x, permt, w_glu, w_out, g_ffn, w_up, w_down, g_final, tm, n_tok):
    n = x.shape[0]
    row = pl.BlockSpec((tm, D_MODEL), lambda i: (i, 0))
    g_spec = pl.BlockSpec((tm // n_tok, CHUNK_LANES), lambda i: (i, 0))
    return pl.pallas_call(
        functools.partial(_post_kernel, n_tok=n_tok),
        grid=(n // tm,),
        in_specs=[g_spec, row, row, row, _const_spec(permt.shape), _const_spec(w_glu.shape),
                  _const_spec(w_out.shape), _const_spec((1, D_MODEL)), _const_spec(w_up.shape),
                  _const_spec(w_down.shape), _const_spec((1, D_MODEL))],
        out_specs=row,
        out_shape=jax.ShapeDtypeStruct((n, D_MODEL), F32),
        compiler_params=_params("arbitrary"),
        name="post",
    )(g, ma, gs, x, permt, w_glu, w_out, g_ffn, w_up, w_down, g_final)


def _tile(n, pref):
    t = pref
    while n % t:
        t //= 2
    return t


def kernel(x_prompt, x_sample, cache_k, cache_v, state_ssm_re, state_ssm_im, g_mix, w_in, attn_sinks,
           w_attn_o, ssm_lambda_re, ssm_lambda_im, ssm_log_dt, ssm_b_re, ssm_b_im, ssm_c_re, ssm_c_im,
           ssm_d, w_glu, w_out, g_ffn, w_up, w_down, g_final):
    batch, seq, _ = x_prompt.shape
    db, dec_seq, _ = x_sample.shape
    assert w_in.shape[0] == 1, "one layer"
    assert seq % PERM_ROWS == 0 and dec_seq in (1, 2, 4, 8) and (db * dec_seq) % PERM_ROWS == 0

    vec = lambda v: v.reshape(1, D_MODEL).astype(F32)
    sinks = attn_sinks[0].astype(F32)
    s5p = (ssm_lambda_re[0], ssm_lambda_im[0], ssm_log_dt[0], ssm_b_re[0], ssm_b_im[0], ssm_c_re[0], ssm_c_im[0])
    d_skip = jnp.tile(ssm_d[0].astype(F32).reshape(SSM_GROUPS, 1, SSM_CH), (1, 1, CHUNK))
    perm = _chunk_perm(CHUNK)
    m, ws, ws_n, whyt, mk, pw, an, w_in_b = _s5_prep(*s5p, dec_seq, w_in[0])

    xp = x_prompt.reshape(batch * seq, D_MODEL)
    q, kv, u16, ga, gs, wo_b, w_glu_b, w_out_b, w_up_b, w_down_b = _proj(
        xp, vec(g_mix[0]), w_in_b, perm, _tile(batch * seq, PROJ_ROWS), CHUNK,
        cast=(w_attn_o[0], w_glu[0], w_out[0], w_up[0], w_down[0]))
    post_w = (w_glu_b, w_out_b, vec(g_ffn[0]), w_up_b, w_down_b, vec(g_final))
    ma = _attn_prompt(sinks, q, kv, ga, wo_b, batch, seq, _tile(seq, ATTN_ROWS))
    n_rows = seq // CHUNK
    g16, hT = _s5_chain(u16, m, ws, whyt, mk, pw, d_skip, jnp.zeros((batch, 1, STATE_COLS), F32),
                        batch, n_rows, _tile(n_rows, S5_BLOCK_ROWS))
    y_prompt = _post(g16, ma, gs, xp, perm.T, *post_w, _tile(batch * seq, POST_ROWS), CHUNK).reshape(
        batch, seq, D_MODEL)
    kv_last = kv.reshape(batch, seq, 2 * D_KV)[:, seq - WINDOW:]
    k_prompt = kv_last[..., :D_KV].reshape(1, batch, WINDOW, N_KV_HEADS, HEAD_DIM)
    v_prompt = kv_last[..., D_KV:].reshape(1, batch, WINDOW, N_KV_HEADS, HEAD_DIM)
    hr, hi = _state_split(hT[:, 0])
    ssm_re_prompt, ssm_im_prompt = hr[None], hi[None]

    ns = db * dec_seq
    xs = x_sample.reshape(ns, D_MODEL)
    perm_s = _chunk_perm(dec_seq)
    q, kv, u16, ga, gs = _proj(xs, vec(g_mix[0]), w_in_b, perm_s, _tile(ns, PROJ_ROWS), dec_seq)
    to_cols = lambda s: s[0].astype(F32).transpose(1, 2, 0).reshape(HALF_COLS, db)
    from_cols = lambda h: h.reshape(SSM_GROUPS, SSM_STATE, db).transpose(2, 0, 1)[None]
    g16, hT_re, hT_im = _s5_rows(u16, m, ws_n, whyt, an, d_skip, to_cols(state_ssm_re), to_cols(state_ssm_im),
                                 _tile(db, LANES))
    tpad = SUBLANES
    pad3 = lambda v: jnp.pad(v.reshape(db, dec_seq, -1).astype(F32), ((0, 0), (0, tpad - dec_seq), (0, 0)))
    ga_pad = pad3(ga).astype(BF16).reshape(db * tpad, D_MODEL)
    window_last = lambda c: c[0].astype(F32).transpose(0, 2, 3, 1).reshape(db, D_KV, WINDOW)
    ma_pad, k_new, v_new = _attn_sample(
        sinks, pad3(q), pad3(kv), window_last(cache_k), window_last(cache_v), ga_pad, wo_b,
        _tile(db, SAMPLE_SEQS), dec_seq)
    ma = ma_pad.reshape(db, tpad, D_MODEL)[:, :dec_seq].reshape(ns, D_MODEL)
    y_sample = _post(g16, ma, gs, xs, perm_s.T, *post_w, PERM_ROWS, dec_seq).reshape(db, dec_seq, D_MODEL)
    k_sample = k_new.reshape(1, db, WINDOW, N_KV_HEADS, HEAD_DIM)
    v_sample = v_new.reshape(1, db, WINDOW, N_KV_HEADS, HEAD_DIM)
    ssm_re_sample, ssm_im_sample = from_cols(hT_re), from_cols(hT_im)

    return (y_prompt, y_sample, k_prompt, v_prompt, ssm_re_prompt, ssm_im_prompt,
            k_sample, v_sample, ssm_re_sample, ssm_im_sample)
```

```python
import functools
import math

import jax
import jax.numpy as jnp
from jax import lax
from jax.experimental import pallas as pl
from jax.experimental.pallas import tpu as pltpu

F32 = jnp.float32
BF16 = jnp.bfloat16

D_MODEL = 1024
HEAD_DIM = 64
N_HEADS = 16
N_KV_HEADS = 4
WINDOW = 128
D_KV = N_KV_HEADS * HEAD_DIM
SSM_CH = 16
SSM_GROUPS = 64
SSM_STATE = 64
D_FF = 4 * D_MODEL
FF_CHUNK = 1024
RMS_EPS = 1e-5
LOG2E = 1.4426950408889634
Q_SCALE = HEAD_DIM ** -0.5 * LOG2E

LANES = 128
SUBLANES = 8
BF16_ROWS = 16
N_LANE_TILES = D_MODEL // LANES
GROUPS_PER_TILE = LANES // SSM_CH
CHUNK = 16
CHUNK_LANES = CHUNK * D_MODEL
GROUP_IO = CHUNK * SSM_CH
N_PAIRS = SSM_GROUPS // 2
PAIRS_PER_TILE = GROUPS_PER_TILE // 2
PAIR_COLS = 2 * LANES
TILE_COLS = PAIRS_PER_TILE * PAIR_COLS
STATE_COLS = N_PAIRS * PAIR_COLS
HALF_COLS = STATE_COLS // 2
PERM_ROWS = CHUNK * BF16_ROWS
PREP_PAIRS = 8
PROJ_ROWS = 1024
ATTN_ROWS = 2048
POST_ROWS = 512
S5_BLOCK_ROWS = 128
SAMPLE_SEQS = 32
VMEM_LIMIT = 56 * 1024 * 1024

_Q0, _KV0, _U0, _GA0, _GS0, _END = 0, 1024, 1536, 2560, 3584, 4608


def _gelu_tanh(x):
    k1 = -2.0 * math.sqrt(2.0 / math.pi) * LOG2E
    return x / (1.0 + jnp.exp2(x * (k1 + (0.044715 * k1) * (x * x))))


def _rmsnorm(x, g):
    return x * lax.rsqrt(jnp.mean(x * x, axis=-1, keepdims=True) + RMS_EPS) * g


def _params(*sem):
    return pltpu.CompilerParams(dimension_semantics=sem, vmem_limit_bytes=VMEM_LIMIT)


def _const_spec(shape):
    nd = len(shape)
    return pl.BlockSpec(shape, lambda *_: (0,) * nd, pipeline_mode=pl.Buffered(1))


def _chunk_perm(n_tok):
    n_chunks = PERM_ROWS // n_tok
    r = jnp.arange(PERM_ROWS)
    tok = (r % n_chunks) * n_tok + r // n_chunks
    return (tok[:, None] == jnp.arange(PERM_ROWS)[None, :]).astype(BF16)


def _piece_transpose(cols, masks):
    for d, msk in zip((4, 2, 1), masks):
        new = list(cols)
        for v in range(GROUPS_PER_TILE):
            if v & d == 0:
                a, b = cols[v], cols[v + d]
                new[v] = jnp.where(msk, pltpu.roll(b, SSM_CH * d, 1), a)
                new[v + d] = jnp.where(msk, b, pltpu.roll(a, LANES - SSM_CH * d, 1))
        cols = new
    return cols


def _piece_masks(rows):
    piece = lax.broadcasted_iota(jnp.int32, (rows, LANES), 1) // SSM_CH
    return [(piece & d) != 0 for d in (4, 2, 1)]


def _store_chunk_rows(x, perm_ref, out_ref, n_tok):
    n_chunks = PERM_ROWS // n_tok
    masks = _piece_masks(n_chunks)
    zero = jnp.zeros((n_chunks, LANES), F32)
    for hb in range(x.shape[0] // PERM_ROWS):
        rows = slice(hb * n_chunks, (hb + 1) * n_chunks)
        xp = jnp.dot(perm_ref[...], x[hb * PERM_ROWS:(hb + 1) * PERM_ROWS],
                     preferred_element_type=F32)
        for j in range(N_LANE_TILES):
            for hf in range(2):
                live = 8 * hf < n_tok
                cols = [xp[t * n_chunks:(t + 1) * n_chunks, j * LANES:(j + 1) * LANES] if t < n_tok else zero
                        for t in range(8 * hf, 8 * hf + 8)]
                for gl, col in enumerate(_piece_transpose(cols, masks) if live else cols):
                    c0 = (GROUPS_PER_TILE * j + gl) * GROUP_IO + hf * LANES
                    out_ref[rows, c0:c0 + LANES] = col.astype(BF16)


def _load_chunk_rows(in_ref, permt_ref, n_tok):
    n_chunks = PERM_ROWS // n_tok
    masks = _piece_masks(n_chunks)
    blocks = []
    for hb in range(in_ref.shape[0] // n_chunks):
        rows = slice(hb * n_chunks, (hb + 1) * n_chunks)
        tiles = [[None] * N_LANE_TILES for _ in range(n_tok)]
        for j in range(N_LANE_TILES):
            for hf in range(-(-n_tok // 8)):
                cols = []
                for gl in range(GROUPS_PER_TILE):
                    c0 = (GROUPS_PER_TILE * j + gl) * GROUP_IO + hf * LANES
                    cols.append(in_ref[rows, c0:c0 + LANES].astype(F32))
                for k, col in enumerate(_piece_transpose(cols, masks)):
                    if 8 * hf + k < n_tok:
                        tiles[8 * hf + k][j] = col.astype(BF16)
        xp = jnp.concatenate([jnp.concatenate(row, axis=1) for row in tiles], axis=0)
        blocks.append(jnp.dot(permt_ref[...], xp, preferred_element_type=F32).astype(BF16))
    return blocks[0] if len(blocks) == 1 else jnp.concatenate(blocks, axis=0)


def _proj_kernel(x_ref, g_ref, w_ref, perm_ref, *refs, n_tok, n_cast):
    q_ref, kv_ref, u_ref, ga_ref, gs_ref = refs[n_cast:n_cast + 5]
    h = _rmsnorm(x_ref[...], g_ref[...]).astype(BF16)

    def seg(lo, hi):
        return jnp.dot(h, w_ref[:, lo:hi], preferred_element_type=F32)

    q_ref[...] = (seg(_Q0, _KV0) * Q_SCALE).astype(BF16)
    kv_ref[...] = seg(_KV0, _U0)
    _store_chunk_rows(seg(_U0, _GA0).astype(BF16), perm_ref, u_ref, n_tok)
    ga_ref[...] = jax.nn.sigmoid(seg(_GA0, _GS0)).astype(BF16)
    gs_ref[...] = jax.nn.sigmoid(seg(_GS0, _END)).astype(BF16)
    for src, dst in zip(refs[:n_cast], refs[n_cast + 5:]):
        dst[...] = src[...].astype(BF16)


def _proj(x, g_mix, w_in, perm, tm, n_tok, cast=()):
    n = x.shape[0]
    steps = n // tm
    row = lambda w: pl.BlockSpec((tm, w), lambda i: (i, 0))
    slab = lambda w: pl.BlockSpec((w.shape[0] // steps, w.shape[1]), lambda i: (i, 0))
    assert all(w.shape[0] % (steps * BF16_ROWS) == 0 for w in cast)
    return pl.pallas_call(
        functools.partial(_proj_kernel, n_tok=n_tok, n_cast=len(cast)),
        grid=(steps,),
        in_specs=[row(D_MODEL), _const_spec((1, D_MODEL)), _const_spec(w_in.shape), _const_spec(perm.shape)]
        + [slab(w) for w in cast],
        out_specs=[row(D_MODEL), row(2 * D_KV), pl.BlockSpec((tm // n_tok, CHUNK_LANES), lambda i: (i, 0)),
                   row(D_MODEL), row(D_MODEL)] + [slab(w) for w in cast],
        out_shape=[jax.ShapeDtypeStruct((n, D_MODEL), BF16),
                   jax.ShapeDtypeStruct((n, 2 * D_KV), F32),
                   jax.ShapeDtypeStruct((n // n_tok, CHUNK_LANES), BF16),
                   jax.ShapeDtypeStruct((n, D_MODEL), BF16),
                   jax.ShapeDtypeStruct((n, D_MODEL), BF16)]
        + [jax.ShapeDtypeStruct(w.shape, BF16) for w in cast],
        compiler_params=_params("arbitrary"),
        name="proj",
    )(x, g_mix, w_in, perm, *cast)


def _dup_heads(tile):
    lo = lax.broadcasted_iota(jnp.int32, tile.shape, tile.ndim - 1) < HEAD_DIM
    rolled = pltpu.roll(tile, HEAD_DIM, tile.ndim - 1)
    return (jnp.where(lo, tile, rolled).astype(BF16), jnp.where(lo, rolled, tile).astype(BF16))


def _sink_softmax(s, sink):
    sink2 = sink * LOG2E
    m = jnp.maximum(jnp.max(s, axis=-1, keepdims=True), sink2)
    p = jnp.exp2(s - m)
    denom = jnp.sum(p, axis=-1, keepdims=True) + jnp.exp2(sink2 - m)
    return p.astype(BF16), 1.0 / denom


def _attn_prompt_kernel(sinks_ref, q_ref, kvc_ref, kvp_ref, ga_ref, wo_ref, out_ref,
                        kd_scr, vd_scr, bias_scr, attn_scr, *, tq):
    i = pl.program_id(1)
    kv_full = jnp.concatenate([kvp_ref[...], kvc_ref[...]], axis=0)
    for t in range(2):
        ke, ko = _dup_heads(kv_full[:, t * LANES:(t + 1) * LANES])
        ve, vo = _dup_heads(kv_full[:, D_KV + t * LANES:D_KV + (t + 1) * LANES])
        kd_scr[2 * t], kd_scr[2 * t + 1] = ke, ko
        vd_scr[2 * t], vd_scr[2 * t + 1] = ve, vo

    qi = lax.broadcasted_iota(jnp.int32, (WINDOW, 2 * WINDOW), 0)
    si = lax.broadcasted_iota(jnp.int32, (WINDOW, 2 * WINDOW), 1)
    band = (si > qi) & (si <= qi + WINDOW)
    in_block = si >= WINDOW
    lo = lax.broadcasted_iota(jnp.int32, (WINDOW, LANES), 1) < HEAD_DIM

    def attend(jb):
        q0 = jb * WINDOW
        has_prev = (i > 0) | (jb > 0)
        bias_scr[...] = jnp.where(band & (in_block | has_prev), 0.0, -jnp.inf)

        def head_pair(hp, c):
            j = hp // 2
            c0 = pl.multiple_of(hp * LANES, LANES)
            kd = kd_scr[j, pl.ds(q0, 2 * WINDOW), :]
            vd = vd_scr[j, pl.ds(q0, 2 * WINDOW), :]
            qp = q_ref[pl.ds(q0, WINDOW), pl.ds(c0, LANES)]
            outs = []
            for par in range(2):
                qm = jnp.where(lo if par == 0 else ~lo, qp, jnp.zeros_like(qp))
                s = lax.dot_general(qm, kd, (((1,), (1,)), ((), ())), preferred_element_type=F32)
                p, inv = _sink_softmax(s + bias_scr[...], sinks_ref[2 * hp + par])
                outs.append(jnp.dot(p, vd, preferred_element_type=F32) * inv)
            attn_scr[pl.ds(q0, WINDOW), pl.ds(c0, LANES)] = jnp.where(lo, outs[0], outs[1]).astype(BF16)
            return c

        lax.fori_loop(0, N_HEADS // 2, head_pair, 0, unroll=8)

    def project(jb):
        rows = pl.ds(jb * WINDOW, WINDOW)
        a_out = jnp.dot(attn_scr[rows, :], wo_ref[...], preferred_element_type=F32)
        out_ref[rows, :] = ga_ref[rows, :].astype(F32) * a_out

    for jb in range(tq // WINDOW):
        attend(jb)
        project(jb)


def _attn_prompt(sinks, q, kv, ga, wo, batch, seq, tq):
    nq = seq // tq
    bpt = tq // WINDOW
    row = lambda w: pl.BlockSpec((tq, w), lambda b, i: (b * nq + i, 0))
    prev = pl.BlockSpec((WINDOW, 2 * D_KV),
                        lambda b, i: (jnp.maximum((b * nq + i) * bpt - 1, 0), 0))
    return pl.pallas_call(
        functools.partial(_attn_prompt_kernel, tq=tq),
        grid=(batch, nq),
        in_specs=[pl.BlockSpec(memory_space=pltpu.SMEM), row(D_MODEL), row(2 * D_KV), prev,
                  row(D_MODEL), _const_spec(wo.shape)],
        out_specs=row(D_MODEL),
        out_shape=jax.ShapeDtypeStruct((batch * seq, D_MODEL), F32),
        scratch_shapes=[pltpu.VMEM((N_KV_HEADS, WINDOW + tq, LANES), BF16),
                        pltpu.VMEM((N_KV_HEADS, WINDOW + tq, LANES), BF16),
                        pltpu.VMEM((WINDOW, 2 * WINDOW), F32),
                        pltpu.VMEM((tq, D_MODEL), BF16)],
        compiler_params=_params("arbitrary", "arbitrary"),
        name="attn_prompt",
    )(sinks, q, kv, kv, ga, wo)


def _attn_sample_kernel(sinks_ref, q_ref, kvn_ref, ck_ref, cv_ref, ga_ref, wo_ref,
                        out_ref, ko_ref, vo_ref, attn_scr, *, bb, tpad, dec_seq):
    nk = WINDOW + tpad
    lo3 = lax.broadcasted_iota(jnp.int32, (bb, tpad, LANES), 2) < HEAD_DIM
    row = lax.broadcasted_iota(jnp.int32, (4 * tpad, nk), 0)
    si = lax.broadcasted_iota(jnp.int32, (4 * tpad, nk), 1)
    tq = row % tpad
    valid = (si > tq) & (si <= tq + WINDOW) & (si < WINDOW + dec_seq)
    hrow = lax.broadcasted_iota(jnp.int32, (4 * tpad, 1), 0) // tpad

    keys_first = lambda ref, sl: jnp.stack([ref[b, sl, :].T for b in range(bb)])
    for t in range(2):
        sl = slice(t * LANES, (t + 1) * LANES)
        kk = jnp.concatenate([keys_first(ck_ref, sl), kvn_ref[:, :, sl]], axis=1)
        vv = jnp.concatenate([keys_first(cv_ref, sl),
                              kvn_ref[:, :, D_KV + t * LANES:D_KV + (t + 1) * LANES]], axis=1)
        ko_ref[:, :, sl] = kk[:, dec_seq:dec_seq + WINDOW, :]
        vo_ref[:, :, sl] = vv[:, dec_seq:dec_seq + WINDOW, :]
        kds = _dup_heads(kk)
        vds = _dup_heads(vv)
        for par_kv in range(2):
            j = 2 * t + par_kv
            kd, vd = kds[par_kv], vds[par_kv]
            parts = []
            for r in range(2):
                c0 = j * 2 * LANES + r * LANES
                qp = q_ref[:, :, c0:c0 + LANES]
                parts += [jnp.where(lo3, qp, 0.0), jnp.where(lo3, 0.0, qp)]
            lhs = jnp.concatenate(parts, axis=1).astype(BF16)
            s = jnp.einsum('bqd,bkd->bqk', lhs, kd, preferred_element_type=F32)
            sink = jnp.zeros((4 * tpad, 1), F32)
            for g in range(4):
                sink = jnp.where(hrow == g, sinks_ref[4 * j + g], sink)
            p, inv = _sink_softmax(jnp.where(valid[None], s, -jnp.inf), sink[None])
            o = jnp.einsum('bqk,bkd->bqd', p, vd, preferred_element_type=F32) * inv
            for r in range(2):
                c0 = j * 2 * LANES + r * LANES
                o_even = o[:, (2 * r) * tpad:(2 * r + 1) * tpad, :]
                o_odd = o[:, (2 * r + 1) * tpad:(2 * r + 2) * tpad, :]
                attn_scr[:, :, c0:c0 + LANES] = jnp.where(lo3, o_even, o_odd)

    attn = attn_scr[...].reshape(bb * tpad, D_MODEL).astype(BF16)
    a_out = jnp.dot(attn, wo_ref[...], preferred_element_type=F32)
    out_ref[...] = ga_ref[...].astype(F32) * a_out


def _attn_sample(sinks, q3, kvn3, ck, cv, ga2, wo, bb, dec_seq):
    db, tpad, _ = q3.shape
    blk3 = lambda r, w: pl.BlockSpec((bb, r, w), lambda i: (i, 0, 0))
    row = pl.BlockSpec((bb * tpad, D_MODEL), lambda i: (i, 0))
    return pl.pallas_call(
        functools.partial(_attn_sample_kernel, bb=bb, tpad=tpad, dec_seq=dec_seq),
        grid=(db // bb,),
        in_specs=[pl.BlockSpec(memory_space=pltpu.SMEM), blk3(tpad, D_MODEL), blk3(tpad, 2 * D_KV),
                  blk3(D_KV, WINDOW), blk3(D_KV, WINDOW), row, _const_spec(wo.shape)],
        out_specs=[row, blk3(WINDOW, D_KV), blk3(WINDOW, D_KV)],
        out_shape=[jax.ShapeDtypeStruct((db * tpad, D_MODEL), F32),
                   jax.ShapeDtypeStruct((db, WINDOW, D_KV), F32),
                   jax.ShapeDtypeStruct((db, WINDOW, D_KV), F32)],
        scratch_shapes=[pltpu.VMEM((bb, tpad, D_MODEL), F32)],
        compiler_params=_params("arbitrary"),
        name="attn_sample",
    )(sinks, q3, kvn3, ck, cv, ga2, wo)


def _s5_discretize(lam_re, lam_im, log_dt, b_re, b_im):
    dt = jnp.exp(log_dt)[:, None]
    decay = jnp.exp(lam_re * dt)
    ab_re = decay * jnp.cos(lam_im * dt)
    ab_im = decay * jnp.sin(lam_im * dt)
    nr, ni = ab_re - 1.0, ab_im
    den = lam_re * lam_re + lam_im * lam_im
    f_re = ((nr * lam_re + ni * lam_im) / den)[..., None]
    f_im = ((ni * lam_re - nr * lam_im) / den)[..., None]
    return ab_re, ab_im, f_re * b_re - f_im * b_im, f_re * b_im + f_im * b_re


def _state_split(h):
    h = h.reshape(h.shape[:-1] + (N_PAIRS, 2, 2, SSM_STATE))
    unpair = lambda a: a.reshape(a.shape[:-3] + (SSM_GROUPS, SSM_STATE))
    return unpair(h[..., 0, :, :]), unpair(h[..., 1, :, :])


def _shift_lanes(x, n):
    a, b = x[:, :LANES], x[:, LANES:]
    lane = lax.broadcasted_iota(jnp.int32, a.shape, 1)
    if n == 0:
        return x
    if n >= LANES:
        r = n - LANES
        hi = a if r == 0 else jnp.where(lane < r, 0.0, pltpu.roll(a, r, 1))
        return jnp.concatenate([jnp.zeros_like(a), hi], axis=1)
    ra, rb = pltpu.roll(a, n, 1), pltpu.roll(b, n, 1)
    return jnp.concatenate([jnp.where(lane < n, 0.0, ra), jnp.where(lane < n, ra, rb)], axis=1)


def _dot_nt_split(lhs, rhs):
    nt = lambda a, b: lax.dot_general(a, b, (((1,), (1,)), ((), ())), preferred_element_type=F32)
    l_hi, r_hi = lhs.astype(BF16), rhs.astype(BF16)
    l_lo = (lhs - l_hi.astype(F32)).astype(BF16)
    r_lo = (rhs - r_hi.astype(F32)).astype(BF16)
    return nt(l_hi, r_hi) + nt(l_hi, r_lo) + nt(l_lo, r_hi)


def _s5_prep_kernel(a_ref, c_ref, bt_ref, w_ref, m_ref, ws_ref, wsn_ref, whyt_ref, mk_ref, pw_ref, an_ref,
                    wb_ref, *, n_tok):
    wb_ref[...] = w_ref[...].astype(BF16)
    for p in range(PREP_PAIRS):
        one, two, lanes = pl.ds(p, 1), pl.ds(2 * p, 2), pl.ds(p * LANES, LANES)
        _s5_prep_pair(a_ref.at[one], c_ref.at[one], bt_ref.at[one], m_ref.at[two], ws_ref.at[two],
                      wsn_ref.at[two], whyt_ref.at[two], mk_ref.at[:, :, :, lanes], pw_ref.at[:, :, lanes],
                      an_ref.at[:, :, lanes], n_tok)


def _s5_prep_pair(a_ref, c_ref, bt_ref, m_ref, ws_ref, wsn_ref, whyt_ref, mk_ref, pw_ref, an_ref, n_tok):
    ar, ai = a_ref[0, 0], a_ref[0, 1]
    cr, ci = c_ref[0, 0], c_ref[0, 1]
    br, bi = bt_ref[0, 0], bt_ref[0, 1]
    pr, pi = [jnp.ones_like(ar)], [jnp.zeros_like(ai)]
    for _ in range(CHUNK):
        pr, pi = pr + [pr[-1] * ar - pi[-1] * ai], pi + [pr[-1] * ai + pi[-1] * ar]
    first = lax.broadcasted_iota(jnp.int32, (SSM_CH, LANES), 1) < SSM_STATE
    ca = [(cr * pr[t] - ci * pi[t], -(cr * pi[t] + ci * pr[t])) for t in range(CHUNK + 1)]
    ca_full = jnp.concatenate([jnp.concatenate(ca[t], axis=1) for t in range(CHUNK)], axis=0)
    zero = jnp.zeros((SSM_CH, LANES), F32)
    for e in range(2):
        own = first if e == 0 else ~first
        pick = lambda v: jnp.where(own, v, zero)
        for t in range(CHUNK):
            rows = slice(t * SSM_CH, (t + 1) * SSM_CH)
            whyt_ref[e, rows, :LANES] = pick(ca[t + 1][0]).astype(BF16)
            whyt_ref[e, rows, LANES:] = pick(ca[t + 1][1]).astype(BF16)
            for ref, k in ((ws_ref, CHUNK - 1 - t), (wsn_ref, n_tok - 1 - t)):
                if k >= 0:
                    ref[e, rows, :LANES] = pick(br * pr[k] - bi * pi[k]).astype(BF16)
                    ref[e, rows, LANES:] = pick(br * pi[k] + bi * pr[k]).astype(BF16)
                else:
                    ref[e, rows, :] = jnp.zeros((SSM_CH, PAIR_COLS), BF16)
        k_row = _dot_nt_split(jnp.concatenate([pick(br), pick(bi)], axis=1), ca_full)
        for s in range(CHUNK):
            m_ref[e, s * SSM_CH:(s + 1) * SSM_CH, :] = _shift_lanes(k_row, s * SSM_CH).astype(BF16)
    sub = lax.broadcasted_iota(jnp.int32, (SUBLANES, LANES), 0)
    qr, qi = [pr[CHUNK]], [pi[CHUNK]]
    for _ in range(SUBLANES - 1):
        qr, qi = qr + [qr[-1] * pr[CHUNK] - qi[-1] * pi[CHUNK]], qi + [qr[-1] * pi[CHUNK] + qi[-1] * pr[CHUNK]]
    for part, q in enumerate((qr, qi)):
        rows = jnp.zeros((SUBLANES, LANES), F32)
        for s in range(SUBLANES):
            rows = jnp.where(sub == s, q[s], rows)
        pw_ref[part] = rows
        for k, shift in enumerate((1, 2, 4)):
            mk_ref[k, part] = jnp.where(sub >= shift, q[shift - 1], 0.0)
    an_ref[0] = jnp.broadcast_to(pr[n_tok], (SUBLANES, LANES))
    an_ref[1] = jnp.broadcast_to(pi[n_tok], (SUBLANES, LANES))


def _s5_prep(lam_re, lam_im, log_dt, b_re, b_im, c_re, c_im, n_tok, w_in):
    ab_re, ab_im, bb_re, bb_im = _s5_discretize(lam_re, lam_im, log_dt, b_re, b_im)
    pair = lambda v: v.reshape(N_PAIRS, 2, v.shape[1], SSM_STATE).transpose(0, 2, 1, 3).reshape(
        N_PAIRS, v.shape[1], LANES)
    a = jnp.stack([pair(ab_re[:, None, :]), pair(ab_im[:, None, :])], axis=1)
    c = jnp.stack([pair(c_re), pair(c_im)], axis=1)
    bt = jnp.stack([pair(bb_re.transpose(0, 2, 1)), pair(bb_im.transpose(0, 2, 1))], axis=1)
    blk4 = lambda r: pl.BlockSpec((PREP_PAIRS, 2, r, LANES), lambda q: (q, 0, 0, 0))
    w_spec = pl.BlockSpec((2 * PREP_PAIRS, GROUP_IO, PAIR_COLS), lambda q: (q, 0, 0))
    w_shape = jax.ShapeDtypeStruct((SSM_GROUPS, GROUP_IO, PAIR_COLS), BF16)
    steps = N_PAIRS // PREP_PAIRS
    assert w_in.shape[0] % (steps * BF16_ROWS) == 0
    slab = pl.BlockSpec((w_in.shape[0] // steps, w_in.shape[1]), lambda q: (q, 0))
    return pl.pallas_call(
        functools.partial(_s5_prep_kernel, n_tok=n_tok),
        grid=(steps,),
        in_specs=[blk4(1), blk4(SSM_CH), blk4(SSM_CH), slab],
        out_specs=[w_spec, w_spec, w_spec, w_spec,
                   pl.BlockSpec((3, 2, SUBLANES, PREP_PAIRS * LANES), lambda q: (0, 0, 0, q)),
                   pl.BlockSpec((2, SUBLANES, PREP_PAIRS * LANES), lambda q: (0, 0, q)),
                   pl.BlockSpec((2, SUBLANES, PREP_PAIRS * LANES), lambda q: (0, 0, q)), slab],
        out_shape=[w_shape, w_shape, w_shape, w_shape,
                   jax.ShapeDtypeStruct((3, 2, SUBLANES, HALF_COLS), F32),
                   jax.ShapeDtypeStruct((2, SUBLANES, HALF_COLS), F32),
                   jax.ShapeDtypeStruct((2, SUBLANES, HALF_COLS), F32),
                   jax.ShapeDtypeStruct(w_in.shape, BF16)],
        compiler_params=_params("arbitrary"),
        name="s5_prep",
    )(a, c, bt, w_in)


def _group_io(ref, g):
    return ref.at[:, g * GROUP_IO:(g + 1) * GROUP_IO]


def _s5_emit_tile(u_ref, st_scr, m_ref, whyt_ref, dt_ref, g_ref, j):
    for gl in range(GROUPS_PER_TILE):
        g = GROUPS_PER_TILE * j + gl
        q0 = (gl // 2) * PAIR_COLS
        u_g = _group_io(u_ref, g)[...]
        y = (jnp.dot(u_g, m_ref[g], preferred_element_type=F32)
             + lax.dot_general(st_scr[:, q0:q0 + PAIR_COLS].astype(BF16), whyt_ref[g],
                               (((1,), (1,)), ((), ())), preferred_element_type=F32)
             + dt_ref[g] * u_g.astype(F32))
        _group_io(g_ref, g)[...] = _gelu_tanh(y).astype(BF16)


def _s5_local_states(u_ref, ws_ref, j, q):
    g = GROUPS_PER_TILE * j + 2 * q
    return (jnp.dot(_group_io(u_ref, g)[...], ws_ref[g], preferred_element_type=F32)
            + jnp.dot(_group_io(u_ref, g + 1)[...], ws_ref[g + 1], preferred_element_type=F32))


def _cmul_add(ar, ai, hr, hi, xr, xi):
    return ar * hr - ai * hi + xr, ar * hi + ai * hr + xi


def _s5_chain_kernel(u_ref, m_ref, ws_ref, whyt_ref, mk_ref, pw_ref, dt_ref, h0_ref, g_ref, hT_ref,
                     carry_scr, st_scr, *, rows):
    @pl.when(pl.program_id(1) == 0)
    def _():
        carry_scr[...] = jnp.broadcast_to(h0_ref[0], (SUBLANES, STATE_COLS))

    row0 = lax.broadcasted_iota(jnp.int32, (SUBLANES, LANES), 0) == 0
    last = lambda h: jnp.broadcast_to(h[SUBLANES - 1:, :], (SUBLANES, LANES))

    for j in range(N_LANE_TILES):
        for q in range(PAIRS_PER_TILE):
            st_scr[:, q * PAIR_COLS:(q + 1) * PAIR_COLS] = _s5_local_states(u_ref, ws_ref, j, q)

        def step(r, carry):
            r0 = pl.multiple_of(r * SUBLANES, SUBLANES)
            out = []
            for q in range(PAIRS_PER_TILE):
                re_c = slice(q * PAIR_COLS, q * PAIR_COLS + LANES)
                im_c = slice(q * PAIR_COLS + LANES, (q + 1) * PAIR_COLS)
                hc = slice((j * PAIRS_PER_TILE + q) * LANES, (j * PAIRS_PER_TILE + q + 1) * LANES)
                xr = st_scr[pl.ds(r0, SUBLANES), re_c]
                xi = st_scr[pl.ds(r0, SUBLANES), im_c]
                for k, shift in enumerate((1, 2, 4)):
                    xr, xi = _cmul_add(mk_ref[k, 0, :, hc], mk_ref[k, 1, :, hc],
                                       pltpu.roll(xr, shift, 0), pltpu.roll(xi, shift, 0), xr, xi)
                cr, ci = carry[q]
                hr, hi = _cmul_add(pw_ref[0, :, hc], pw_ref[1, :, hc], cr, ci, xr, xi)
                st_scr[pl.ds(r0, SUBLANES), re_c] = jnp.where(row0, cr, pltpu.roll(hr, 1, 0))
                st_scr[pl.ds(r0, SUBLANES), im_c] = jnp.where(row0, ci, pltpu.roll(hi, 1, 0))
                out.append((last(hr), last(hi)))
            return tuple(out)

        c0 = j * TILE_COLS
        init = tuple((carry_scr[:, c0 + q * PAIR_COLS:c0 + q * PAIR_COLS + LANES],
                      carry_scr[:, c0 + q * PAIR_COLS + LANES:c0 + (q + 1) * PAIR_COLS])
                     for q in range(PAIRS_PER_TILE))
        final = lax.fori_loop(0, rows // SUBLANES, step, init, unroll=True)
        for q in range(PAIRS_PER_TILE):
            carry_scr[:, c0 + q * PAIR_COLS:c0 + q * PAIR_COLS + LANES] = final[q][0]
            carry_scr[:, c0 + q * PAIR_COLS + LANES:c0 + (q + 1) * PAIR_COLS] = final[q][1]

        _s5_emit_tile(u_ref, st_scr, m_ref, whyt_ref, dt_ref, g_ref, j)

    hT_ref[0] = carry_scr[...]


def _s5_chain(u16, m, ws, why, mk, pw, d_tiled, h0, batch, n_rows, rows):
    nblk = n_rows // rows
    row = pl.BlockSpec((rows, CHUNK_LANES), lambda b, i: (b * nblk + i, 0))
    return pl.pallas_call(
        functools.partial(_s5_chain_kernel, rows=rows),
        grid=(batch, nblk),
        in_specs=[row, _const_spec(m.shape), _const_spec(ws.shape), _const_spec(why.shape),
                  _const_spec(mk.shape), _const_spec(pw.shape), _const_spec(d_tiled.shape),
                  pl.BlockSpec((1, 1, STATE_COLS), lambda b, i: (b, 0, 0))],
        out_specs=[row, pl.BlockSpec((1, SUBLANES, STATE_COLS), lambda b, i: (b, 0, 0))],
        out_shape=[jax.ShapeDtypeStruct(u16.shape, BF16),
                   jax.ShapeDtypeStruct((batch, SUBLANES, STATE_COLS), F32)],
        scratch_shapes=[pltpu.VMEM((SUBLANES, STATE_COLS), F32), pltpu.VMEM((rows, TILE_COLS), F32)],
        compiler_params=_params("arbitrary", "arbitrary"),
        name="s5_chain",
    )(u16, m, ws, why, mk, pw, d_tiled, h0)


def _s5_rows_kernel(u_ref, m_ref, ws_ref, whyt_ref, an_ref, dt_ref, h0r_ref, h0i_ref, g_ref, hTr_ref, hTi_ref,
                    st_scr):
    for j in range(N_LANE_TILES):
        for q in range(PAIRS_PER_TILE):
            hc = slice((j * PAIRS_PER_TILE + q) * LANES, (j * PAIRS_PER_TILE + q + 1) * LANES)
            h0r, h0i = h0r_ref[hc, :].T, h0i_ref[hc, :].T
            local = _s5_local_states(u_ref, ws_ref, j, q)
            hr, hi = _cmul_add(an_ref[0, :1, hc], an_ref[1, :1, hc], h0r, h0i, local[:, :LANES], local[:, LANES:])
            hTr_ref[hc, :] = hr.T
            hTi_ref[hc, :] = hi.T
            st_scr[:, q * PAIR_COLS:q * PAIR_COLS + LANES] = h0r
            st_scr[:, q * PAIR_COLS + LANES:(q + 1) * PAIR_COLS] = h0i
        _s5_emit_tile(u_ref, st_scr, m_ref, whyt_ref, dt_ref, g_ref, j)


def _s5_rows(u16, m, ws, why, an, d_tiled, h0_re_t, h0_im_t, rows):
    n = u16.shape[0]
    once = pl.Buffered(1)
    row = pl.BlockSpec((rows, CHUNK_LANES), lambda i: (i, 0), pipeline_mode=once)
    col = pl.BlockSpec((HALF_COLS, rows), lambda i: (0, i), pipeline_mode=once)
    state = jax.ShapeDtypeStruct((HALF_COLS, n), F32)
    return pl.pallas_call(
        _s5_rows_kernel,
        grid=(n // rows,),
        in_specs=[row, _const_spec(m.shape), _const_spec(ws.shape), _const_spec(why.shape),
                  _const_spec(an.shape), _const_spec(d_tiled.shape), col, col],
        out_specs=[row, col, col],
        out_shape=[jax.ShapeDtypeStruct(u16.shape, BF16), state, state],
        scratch_shapes=[pltpu.VMEM((rows, TILE_COLS), F32)],
        compiler_params=_params("arbitrary"),
        name="s5_rows",
    )(u16, m, ws, why, an, d_tiled, h0_re_t, h0_im_t)


def _post_kernel(g_ref, ma_ref, gs_ref, x_ref, permt_ref, wglu_ref, wout_ref, gffn_ref, wup_ref, wdown_ref,
                 gfin_ref, out_ref, *, n_tok):
    g = _load_chunk_rows(g_ref, permt_ref, n_tok)
    glu = jnp.dot(g, wglu_ref[...], preferred_element_type=F32)
    s_out = glu[:, :D_MODEL] * jax.nn.sigmoid(glu[:, D_MODEL:])
    merged = ma_ref[...] + gs_ref[...].astype(F32) * s_out
    x1 = x_ref[...] + jnp.dot(merged.astype(BF16), wout_ref[...], preferred_element_type=F32)
    h2 = _rmsnorm(x1, gffn_ref[...]).astype(BF16)
    x2 = x1
    for c in range(0, D_FF, FF_CHUNK):
        up = jnp.dot(h2, wup_ref[:, c:c + FF_CHUNK], preferred_element_type=F32)
        act = jnp.square(jnp.maximum(up, 0.0)).astype(BF16)
        x2 = x2 + jnp.dot(act, wdown_ref[c:c + FF_CHUNK, :], preferred_element_type=F32)
    out_ref[...] = _rmsnorm(x2, gfin_ref[...])


def _post(g, ma, gs, x, permt, w_glu, w_out, g_ffn, w_up, w_down, g_final, tm, n_tok):
    n = x.shape[0]
    row = pl.BlockSpec((tm, D_MODEL), lambda i: (i, 0))
    g_spec = pl.BlockSpec((tm // n_tok, CHUNK_LANES), lambda i: (i, 0))
    return pl.pallas_call(
        functools.partial(_post_kernel, n_tok=n_tok),
        grid=(n // tm,),
        in_specs=[g_spec, row, row, row, _const_spec(permt.shape), _const_spec(w_glu.shape),
                  _const_spec(w_out.shape), _const_spec((1, D_MODEL)), _const_spec(w_up.shape),
                  _const_spec(w_down.shape), _const_spec((1, D_MODEL))],
        out_specs=row,
        out_shape=jax.ShapeDtypeStruct((n, D_MODEL), F32),
        compiler_params=_params("arbitrary"),
        name="post",
    )(g, ma, gs, x, permt, w_glu, w_out, g_ffn, w_up, w_down, g_final)


def _tile(n, pref):
    t = pref
    while n % t:
        t //= 2
    return t


def kernel(x_prompt, x_sample, cache_k, cache_v, state_ssm_re, state_ssm_im, g_mix, w_in, attn_sinks,
           w_attn_o, ssm_lambda_re, ssm_lambda_im, ssm_log_dt, ssm_b_re, ssm_b_im, ssm_c_re, ssm_c_im,
           ssm_d, w_glu, w_out, g_ffn, w_up, w_down, g_final):
    batch, seq, _ = x_prompt.shape
    db, dec_seq, _ = x_sample.shape
    assert w_in.shape[0] == 1, "one layer"
    assert seq % PERM_ROWS == 0 and dec_seq in (1, 2, 4, 8) and (db * dec_seq) % PERM_ROWS == 0

    vec = lambda v: v.reshape(1, D_MODEL).astype(F32)
    sinks = attn_sinks[0].astype(F32)
    s5p = (ssm_lambda_re[0], ssm_lambda_im[0], ssm_log_dt[0], ssm_b_re[0], ssm_b_im[0], ssm_c_re[0], ssm_c_im[0])
    d_skip = jnp.tile(ssm_d[0].astype(F32).reshape(SSM_GROUPS, 1, SSM_CH), (1, 1, CHUNK))
    perm = _chunk_perm(CHUNK)
    m, ws, ws_n, whyt, mk, pw, an, w_in_b = _s5_prep(*s5p, dec_seq, w_in[0])

    xp = x_prompt.reshape(batch * seq, D_MODEL)
    q, kv, u16, ga, gs, wo_b, w_glu_b, w_out_b, w_up_b, w_down_b = _proj(
        xp, vec(g_mix[0]), w_in_b, perm, _tile(batch * seq, PROJ_ROWS), CHUNK,
        cast=(w_attn_o[0], w_glu[0], w_out[0], w_up[0], w_down[0]))
    post_w = (w_glu_b, w_out_b, vec(g_ffn[0]), w_up_b, w_down_b, vec(g_final))
    ma = _attn_prompt(sinks, q, kv, ga, wo_b, batch, seq, _tile(seq, ATTN_ROWS))
    n_rows = seq // CHUNK
    g16, hT = _s5_chain(u16, m, ws, whyt, mk, pw, d_skip, jnp.zeros((batch, 1, STATE_COLS), F32),
                        batch, n_rows, _tile(n_rows, S5_BLOCK_ROWS))
    y_prompt = _post(g16, ma, gs, xp, perm.T, *post_w, _tile(batch * seq, POST_ROWS), CHUNK).reshape(
        batch, seq, D_MODEL)
    kv_last = kv.reshape(batch, seq, 2 * D_KV)[:, seq - WINDOW:]
    k_prompt = kv_last[..., :D_KV].reshape(1, batch, WINDOW, N_KV_HEADS, HEAD_DIM)
    v_prompt = kv_last[..., D_KV:].reshape(1, batch, WINDOW, N_KV_HEADS, HEAD_DIM)
    hr, hi = _state_split(hT[:, 0])
    ssm_re_prompt, ssm_im_prompt = hr[None], hi[None]

    ns = db * dec_seq
    xs = x_sample.reshape(ns, D_MODEL)
    perm_s = _chunk_perm(dec_seq)
    q, kv, u16, ga, gs = _proj(xs, vec(g_mix[0]), w_in_b, perm_s, _tile(ns, PROJ_ROWS), dec_seq)
    tpad = SUBLANES
    pad3 = lambda v: jnp.pad(v.reshape(db, dec_seq, -1).astype(F32), ((0, 0), (0, tpad - dec_seq), (0, 0)))
    ga_pad = pad3(ga).astype(BF16).reshape(db * tpad, D_MODEL)
    window_last = lambda c: c[0].astype(F32).transpose(0, 2, 3, 1).reshape(db, D_KV, WINDOW)
    ma_pad, k_new, v_new = _attn_sample(
        sinks, pad3(q), pad3(kv), window_last(cache_k), window_last(cache_v), ga_pad, wo_b,
        _tile(db, SAMPLE_SEQS), dec_seq)
    ma = ma_pad.reshape(db, tpad, D_MODEL)[:, :dec_seq].reshape(ns, D_MODEL)
    to_cols = lambda s: s[0].astype(F32).transpose(1, 2, 0).reshape(HALF_COLS, db)
    from_cols = lambda h: h.reshape(SSM_GROUPS, SSM_STATE, db).transpose(2, 0, 1)[None]
    g16, hT_re, hT_im = _s5_rows(u16, m, ws_n, whyt, an, d_skip, to_cols(state_ssm_re), to_cols(state_ssm_im),
                                 _tile(db, LANES))
    y_sample = _post(g16, ma, gs, xs, perm_s.T, *post_w, PERM_ROWS, dec_seq).reshape(db, dec_seq, D_MODEL)
    k_sample = k_new.reshape(1, db, WINDOW, N_KV_HEADS, HEAD_DIM)
    v_sample = v_new.reshape(1, db, WINDOW, N_KV_HEADS, HEAD_DIM)
    ssm_re_sample, ssm_im_sample = from_cols(hT_re), from_cols(hT_im)

    return (y_prompt, y_sample, k_prompt, v_prompt, ssm_re_prompt, ssm_im_prompt,
            k_sample, v_sample, ssm_re_sample, ssm_im_sample)
```
